```python
import math
import jax
import jax.numpy as jnp
from jax import lax
import numpy as np

D_MODEL = 1024
BATCH = 2
SEQ = 8192
DEPTH = 2

N_EVEN = (DEPTH + 1) // 2
N_ODD = DEPTH // 2

HEAD_DIM = 64
NEG = -1e30
BIG = 1e30
NORM_EPS = 1e-6

N_BUCKETS = 32
MAX_DISTANCE = 1024
N_BIAS_HEADS = 8

NSA_HEADS = 8
NSA_KV_HEADS = 2
NSA_GROUP = NSA_HEADS // NSA_KV_HEADS
CMP_LEN = 32
CMP_STRIDE = 16
CMP_HIDDEN = 256
SLC_BLOCK = 64
SLC_TOPN = 16
NSA_WINDOW = 512
NSA_QBLOCK = 128

MOBA_HEADS = 8
MOBA_BLOCK = 256
MOBA_TOPK = 3
MOBA_QBLOCK = 32

MLA_HEADS = 4
MLA_Q_RANK = 256
MLA_KV_RANK = 128
MLA_NOPE = 128
MLA_ROPE = 64
MLA_V = 128
ROPE_THETA = 10000.0
DENSE_QBLOCK = 128

SWA_HEADS = 8
SWA_KV_HEADS = 2
SWA_GROUP = SWA_HEADS // SWA_KV_HEADS
SWA_WINDOW = 128
SWA_QBLOCK = 128

N_EXPERTS = 16
N_GROUPS = 4
EXPERTS_PER_GROUP = N_EXPERTS // N_GROUPS
TOP_K = 2
D_EXPERT = 512
MOE_BLOCK = 256

ADA_INIT = 0.5

EVEN_WIDTHS = (NSA_HEADS * HEAD_DIM,) + (NSA_KV_HEADS * HEAD_DIM,) * 6 + (3 * NSA_HEADS,) + (MOBA_HEADS * HEAD_DIM,) * 3
EVEN_IN = sum(EVEN_WIDTHS)
EVEN_OUT = NSA_HEADS * HEAD_DIM + MOBA_HEADS * HEAD_DIM
ODD_WIDTHS = (MLA_Q_RANK, MLA_KV_RANK, MLA_ROPE, SWA_HEADS * HEAD_DIM, SWA_KV_HEADS * HEAD_DIM, SWA_KV_HEADS * HEAD_DIM)
ODD_IN = sum(ODD_WIDTHS)
ODD_OUT = MLA_HEADS * MLA_V + SWA_HEADS * HEAD_DIM

kernel_name = "hybrid_nsa_moba_mla_swa_grouped_moe"


def rmsnorm(x, g):
    xf = x.astype(jnp.float32)
    y = xf * lax.rsqrt(jnp.mean(xf * xf, axis=-1, keepdims=True) + NORM_EPS)
    return (y * g.astype(jnp.float32)).astype(x.dtype)


def split_cols(x, widths):
    idx, acc = [], 0
    for w in widths[:-1]:
        acc += w
        idx.append(acc)
    return jnp.split(x, idx, axis=-1)


def masked_softmax(s, mask):
    s = jnp.where(mask, s.astype(jnp.float32), NEG)
    p = jax.nn.softmax(s, axis=-1)
    return jnp.where(mask, p, 0.0)


def rel_bucket(dist):
    exact = N_BUCKETS // 2
    d = jnp.maximum(dist, 0)
    log_part = exact + (jnp.log(jnp.maximum(d, 1).astype(jnp.float32) / exact)
                        / math.log(MAX_DISTANCE / exact) * (N_BUCKETS - exact)).astype(jnp.int32)
    return jnp.where(d < exact, d, jnp.minimum(log_part, N_BUCKETS - 1))


def shared_bias(rel_table, dist, n_kv):
    b = jnp.moveaxis(rel_table[rel_bucket(dist)], -1, 0).astype(jnp.float32)
    return b.reshape(n_kv, -1, *dist.shape)


def gathered_bias(rel_table, dist, n_kv):
    table = rel_table.reshape(N_BUCKETS, n_kv, -1)
    kv = jnp.arange(n_kv)[None, :, None, None]
    return jnp.moveaxis(table[rel_bucket(dist), kv], -1, 2).astype(jnp.float32)


def rope_tables(S, dim):
    inv = ROPE_THETA ** (-jnp.arange(0, dim, 2, dtype=jnp.float32) / dim)
    ang = jnp.arange(S, dtype=jnp.float32)[:, None] * inv[None, :]
    return jnp.cos(ang), jnp.sin(ang)


def apply_rope(x, cos, sin):
    half = x.shape[-1] // 2
    x1 = x[..., :half].astype(jnp.float32)
    x2 = x[..., half:].astype(jnp.float32)
    return jnp.concatenate([x1 * cos - x2 * sin, x1 * sin + x2 * cos], axis=-1).astype(x.dtype)


def adaln(c, w, b):
    m = jax.nn.silu(c) @ w + b
    return jnp.split(m[:, None, :], 6, axis=-1)


def nsa_attention(q, kc, vc, ks, vs, kw, vw, gates, pos_k, pos_v, ck_w1, ck_w2, cv_w1, cv_w2, rel_table):
    B, S, _ = q.shape
    Hk, G, dh, QB = NSA_KV_HEADS, NSA_GROUP, HEAD_DIM, NSA_QBLOCK
    scale = dh ** -0.5
    q = q.reshape(B, S, Hk, G, dh).transpose(0, 2, 3, 1, 4)
    to_kv = lambda t: t.reshape(B, S, Hk, dh).transpose(0, 2, 1, 3)
    kc, vc, ks, vs, kw, vw = (to_kv(t) for t in (kc, vc, ks, vs, kw, vw))
    g = jax.nn.sigmoid(gates.astype(jnp.float32)).reshape(B, S, Hk, G, 3).transpose(0, 2, 3, 1, 4)

    n_cmp = (S - CMP_LEN) // CMP_STRIDE + 1
    cmp_start = np.arange(n_cmp, dtype=np.int32) * CMP_STRIDE
    win_idx = cmp_start[:, None] + np.arange(CMP_LEN, dtype=np.int32)[None, :]
    cmp_end = cmp_start + CMP_LEN - 1

    def compress(t, pos, w1, w2):
        blocks = t[:, :, win_idx, :] + pos
        flat = blocks.reshape(B, Hk, n_cmp, CMP_LEN * dh)
        return jax.nn.gelu(flat @ w1) @ w2

    k_cmp = compress(kc, pos_k, ck_w1, ck_w2)
    v_cmp = compress(vc, pos_v, cv_w1, cv_w2)

    n_slc = S // SLC_BLOCK
    slc_lo = np.arange(n_slc, dtype=np.int32) * SLC_BLOCK
    overlap = ((cmp_start[:, None] <= slc_lo[None, :] + SLC_BLOCK - 1)
               & (cmp_end[:, None] >= slc_lo[None, :])).astype(np.float32)
    n_sel = min(SLC_TOPN, n_slc)
    ks_blocks = ks.reshape(B, Hk, n_slc, SLC_BLOCK, dh)
    vs_blocks = vs.reshape(B, Hk, n_slc, SLC_BLOCK, dh)

    pad = ((0, 0), (0, 0), (NSA_WINDOW, 0), (0, 0))
    kw_pad, vw_pad = jnp.pad(kw, pad), jnp.pad(vw, pad)

    nq = S // QB
    q_blocks = jnp.moveaxis(q.reshape(B, Hk, G, nq, QB, dh), 3, 0)
    g_blocks = jnp.moveaxis(g.reshape(B, Hk, G, nq, QB, 3), 3, 0)
    b_idx = jnp.arange(B)[:, None, None, None]
    kv_idx = jnp.arange(Hk)[None, :, None, None]

    def block(args):
        qb, gb, i = args
        start = i * QB
        t = start + jnp.arange(QB)
        s_c = jnp.einsum('bkgqd,bknd->bkgqn', qb, k_cmp).astype(jnp.float32) * scale \
            + shared_bias(rel_table, t[:, None] - cmp_end[None, :], Hk)
        p_c = masked_softmax(s_c, cmp_end[None, :] <= t[:, None])
        o_c = jnp.einsum('bkgqn,bknd->bkgqd', p_c.astype(v_cmp.dtype), v_cmp)
        imp = jnp.einsum('bkgqn,nm->bkqm', p_c, overlap)
        blk_t = (t // SLC_BLOCK)[:, None]
        j = jnp.arange(n_slc)[None, :]
        forced = (j == 0) | (j == blk_t) | (j == blk_t - 1)
        score = jnp.where(forced, BIG, jnp.where(j <= blk_t, imp, NEG))
        sel = lax.top_k(score, n_sel)[1]
        k_sel = ks_blocks[b_idx, kv_idx, sel].reshape(B, Hk, QB, n_sel * SLC_BLOCK, dh)
        v_sel = vs_blocks[b_idx, kv_idx, sel].reshape(B, Hk, QB, n_sel * SLC_BLOCK, dh)
        pos_sel = (sel[..., None] * SLC_BLOCK + jnp.arange(SLC_BLOCK)).reshape(B, Hk, QB, -1)
        dist_s = t[:, None] - pos_sel
        s_s = jnp.einsum('bkgqd,bkqsd->bkgqs', qb, k_sel).astype(jnp.float32) * scale \
            + gathered_bias(rel_table, dist_s, Hk)
        p_s = masked_softmax(s_s, (dist_s >= 0)[:, :, None])
        o_s = jnp.einsum('bkgqs,bkqsd->bkgqd', p_s.astype(v_sel.dtype), v_sel)
        k_w = lax.dynamic_slice_in_dim(kw_pad, start, QB + NSA_WINDOW, axis=2)
        v_w = lax.dynamic_slice_in_dim(vw_pad, start, QB + NSA_WINDOW, axis=2)
        dist_w = t[:, None] - (start - NSA_WINDOW + jnp.arange(QB + NSA_WINDOW))[None, :]
        s_w = jnp.einsum('bkgqd,bksd->bkgqs', qb, k_w).astype(jnp.float32) * scale \
            + shared_bias(rel_table, dist_w, Hk)
        p_w = masked_softmax(s_w, (dist_w >= 0) & (dist_w < NSA_WINDOW))
        o_w = jnp.einsum('bkgqs,bksd->bkgqd', p_w.astype(v_w.dtype), v_w)
        return (gb[..., 0:1] * o_c + gb[..., 1:2] * o_s + gb[..., 2:3] * o_w).astype(qb.dtype)

    o = lax.map(block, (q_blocks, g_blocks, jnp.arange(nq)))
    return o.transpose(1, 0, 4, 2, 3, 5).reshape(B, S, NSA_HEADS * dh)


def moba_attention(q, k, v, rel_table):
    B, S, _ = q.shape
    H, dh, QB = MOBA_HEADS, HEAD_DIM, MOBA_QBLOCK
    scale = dh ** -0.5
    q, k, v = (t.reshape(B, S, H, dh).transpose(0, 2, 1, 3) for t in (q, k, v))
    n_blk = -(-S // MOBA_BLOCK)
    pad = ((0, 0), (0, 0), (0, n_blk * MOBA_BLOCK - S), (0, 0))
    k_pad, v_pad = jnp.pad(k, pad), jnp.pad(v, pad)
    k_blocks = k_pad.reshape(B, H, n_blk, MOBA_BLOCK, dh)
    v_blocks = v_pad.reshape(B, H, n_blk, MOBA_BLOCK, dh)
    k_mean = jnp.mean(k_blocks.astype(jnp.float32), axis=3)
    n_top = min(MOBA_TOPK, n_blk)
    n_s = n_top * MOBA_BLOCK
    nq = S // QB
    q_blocks = jnp.moveaxis(q.reshape(B, H, nq, QB, dh), 2, 0)
    b_idx = jnp.arange(B)[:, None, None, None]
    h_idx = jnp.arange(H)[None, :, None, None]

    def block(args):
        qb, i = args
        start = i * QB
        t = start + jnp.arange(QB)
        own = start // MOBA_BLOCK
        gate = jnp.einsum('bhqd,bhnd->bhqn', qb.astype(jnp.float32), k_mean)
        gate = jnp.where(jnp.arange(n_blk) < own, gate, NEG)
        sel = lax.top_k(gate, n_top)[1]
        k_sel = k_blocks[b_idx, h_idx, sel].reshape(B, H, QB, n_s, dh)
        v_sel = v_blocks[b_idx, h_idx, sel].reshape(B, H, QB, n_s, dh)
        pos_sel = (sel[..., None] * MOBA_BLOCK + jnp.arange(MOBA_BLOCK)).reshape(B, H, QB, n_s)
        mask_sel = pos_sel < own * MOBA_BLOCK
        k_own = lax.dynamic_slice_in_dim(k_pad, own * MOBA_BLOCK, MOBA_BLOCK, axis=2)
        v_own = lax.dynamic_slice_in_dim(v_pad, own * MOBA_BLOCK, MOBA_BLOCK, axis=2)
        dist_own = t[:, None] - (own * MOBA_BLOCK + jnp.arange(MOBA_BLOCK))[None, :]
        s_sel = jnp.einsum('bhqd,bhqsd->bhqs', qb, k_sel).astype(jnp.float32) * scale \
            + gathered_bias(rel_table, t[:, None] - pos_sel, H)[:, :, 0]
        s_own = jnp.einsum('bhqd,bhsd->bhqs', qb, k_own).astype(jnp.float32) * scale \
            + shared_bias(rel_table, dist_own, H)[:, 0]
        s = jnp.concatenate([s_sel, s_own], axis=-1)
        mask = jnp.concatenate([mask_sel, jnp.broadcast_to(dist_own >= 0, s_own.shape)], axis=-1)
        p = masked_softmax(s, mask).astype(v.dtype)
        return jnp.einsum('bhqs,bhqsd->bhqd', p[..., :n_s], v_sel) \
            + jnp.einsum('bhqs,bhsd->bhqd', p[..., n_s:], v_own)

    o = lax.map(block, (q_blocks, jnp.arange(nq)))
    return o.transpose(1, 0, 3, 2, 4).reshape(B, S, H * dh)


def mla_attention(c_q, c_kv, k_rope, q_norm, kv_norm, w_q_up, w_kv_up):
    B, S, _ = c_q.shape
    H, QB = MLA_HEADS, DENSE_QBLOCK
    q = (rmsnorm(c_q, q_norm) @ w_q_up).reshape(B, S, H, MLA_NOPE + MLA_ROPE)
    kv = (rmsnorm(c_kv, kv_norm) @ w_kv_up).reshape(B, S, H, MLA_NOPE + MLA_V)
    cos, sin = rope_tables(S, MLA_ROPE)
    q_nope = q[..., :MLA_NOPE]
    q_rope = apply_rope(q[..., MLA_NOPE:], cos[:, None, :], sin[:, None, :])
    k_rope = apply_rope(k_rope, cos, sin)
    k_nope = kv[..., :MLA_NOPE].transpose(0, 2, 1, 3)
    v = kv[..., MLA_NOPE:].transpose(0, 2, 1, 3)
    scale = (MLA_NOPE + MLA_ROPE) ** -0.5
    nq = S // QB
    qn_b = q_nope.reshape(B, nq, QB, H, MLA_NOPE).transpose(1, 0, 3, 2, 4)
    qr_b = q_rope.reshape(B, nq, QB, H, MLA_ROPE).transpose(1, 0, 3, 2, 4)
    kpos = jnp.arange(S)

    def block(args):
        qn, qr, i = args
        t = i * QB + jnp.arange(QB)
        s = (jnp.einsum('bhqd,bhkd->bhqk', qn, k_nope)
             + jnp.einsum('bhqd,bkd->bhqk', qr, k_rope)).astype(jnp.float32) * scale
        s = jnp.where(kpos[None, :] <= t[:, None], s, NEG)
        p = jax.nn.softmax(s, axis=-1).astype(v.dtype)
        return jnp.einsum('bhqk,bhkd->bhqd', p, v)

    o = lax.map(block, (qn_b, qr_b, jnp.arange(nq)))
    return o.transpose(1, 0, 3, 2, 4).reshape(B, S, H * MLA_V)


def swa_sink_attention(q, k, v, sinks, rel_table):
    B, S, _ = q.shape
    Hk, G, dh, QB, W = SWA_KV_HEADS, SWA_GROUP, HEAD_DIM, SWA_QBLOCK, SWA_WINDOW
    scale = dh ** -0.5
    q = q.reshape(B, S, Hk, G, dh).transpose(0, 2, 3, 1, 4)
    k = k.reshape(B, S, Hk, dh).transpose(0, 2, 1, 3)
    v = v.reshape(B, S, Hk, dh).transpose(0, 2, 1, 3)
    pad = ((0, 0), (0, 0), (W, 0), (0, 0))
    k_pad, v_pad = jnp.pad(k, pad), jnp.pad(v, pad)
    nq = S // QB
    q_blocks = jnp.moveaxis(q.reshape(B, Hk, G, nq, QB, dh), 3, 0)
    sink = sinks.astype(jnp.float32).reshape(Hk, G, 1, 1)

    def block(args):
        qb, i = args
        start = i * QB
        t = start + jnp.arange(QB)
        k_w = lax.dynamic_slice_in_dim(k_pad, start, QB + W, axis=2)
        v_w = lax.dynamic_slice_in_dim(v_pad, start, QB + W, axis=2)
        dist = t[:, None] - (start - W + jnp.arange(QB + W))[None, :]
        mask = (dist >= 0) & (dist < W)
        s = jnp.einsum('bkgqd,bksd->bkgqs', qb, k_w).astype(jnp.float32) * scale \
            + shared_bias(rel_table, dist, Hk)
        s = jnp.where(mask, s, NEG)
        m = jnp.maximum(jnp.max(s, axis=-1, keepdims=True), sink)
        e = jnp.exp(s - m)
        p = e / (jnp.sum(e, axis=-1, keepdims=True) + jnp.exp(sink - m))
        return jnp.einsum('bkgqs,bksd->bkgqd', p.astype(v_w.dtype), v_w)

    o = lax.map(block, (q_blocks, jnp.arange(nq)))
    return o.transpose(1, 0, 4, 2, 3, 5).reshape(B, S, SWA_HEADS * dh)


def even_mixer(h, w_in, w_out, pos_k, pos_v, ck_w1, ck_w2, cv_w1, cv_w2, rel_table):
    q_a, kc, vc, ks, vs, kw, vw, gates, q_b, k_b, v_b = split_cols(h @ w_in, EVEN_WIDTHS)
    o_a = nsa_attention(q_a, kc, vc, ks, vs, kw, vw, gates, pos_k, pos_v, ck_w1, ck_w2, cv_w1, cv_w2, rel_table)
    o_b = moba_attention(q_b, k_b, v_b, rel_table)
    return jnp.concatenate([o_a, o_b.astype(o_a.dtype)], axis=-1) @ w_out


def odd_mixer(h, w_in, w_out, q_norm, kv_norm, w_q_up, w_kv_up, sinks, rel_table):
    c_q, c_kv, k_rope, q_d, k_d, v_d = split_cols(h @ w_in, ODD_WIDTHS)
    o_c = mla_attention(c_q, c_kv, k_rope, q_norm, kv_norm, w_q_up, w_kv_up)
    o_d = swa_sink_attention(q_d, k_d, v_d, sinks, rel_table)
    return jnp.concatenate([o_c, o_d.astype(o_c.dtype)], axis=-1) @ w_out


def grouped_moe(h, router_w, router_b, w_gate, w_up, w_down):
    B, S, D = h.shape
    T = B * S
    E = N_EXPERTS
    xf = h.reshape(T, D)
    aff = jax.nn.sigmoid((xf @ router_w).astype(jnp.float32))
    biased = (aff + router_b.astype(jnp.float32)).reshape(T, N_GROUPS, EXPERTS_PER_GROUP)
    grp_score = jnp.sum(lax.top_k(biased, TOP_K)[0], axis=-1)
    g_sel = jnp.argmax(grp_score, axis=-1)
    tok = jnp.arange(T)
    local = lax.top_k(biased[tok, g_sel], TOP_K)[1]
    expert = g_sel[:, None] * EXPERTS_PER_GROUP + local
    w = aff[tok[:, None], expert]
    w = w / jnp.sum(w, axis=-1, keepdims=True)

    A = T * TOP_K
    flat_e = expert.reshape(A)
    flat_tok = jnp.repeat(tok, TOP_K)
    flat_w = w.reshape(A)
    order = jnp.argsort(flat_e)
    se, stok, sw = flat_e[order], flat_tok[order], flat_w[order]
    counts = jnp.zeros((E,), jnp.int32).at[flat_e].add(1)
    padded = (counts + MOE_BLOCK - 1) // MOE_BLOCK * MOE_BLOCK
    start = jnp.cumsum(counts) - counts
    pend = jnp.cumsum(padded)
    pstart = pend - padded
    dest = pstart[se] + jnp.arange(A) - start[se]
    n_blocks = (A + E * (MOE_BLOCK - 1) + MOE_BLOCK - 1) // MOE_BLOCK
    P = n_blocks * MOE_BLOCK
    tok_buf = jnp.zeros((P,), jnp.int32).at[dest].set(stok)
    w_buf = jnp.zeros((P,), jnp.float32).at[dest].set(sw)
    blk_expert = jnp.minimum(jnp.searchsorted(pend, jnp.arange(n_blocks) * MOE_BLOCK, side='right'), E - 1)
    xs = xf[tok_buf].reshape(n_blocks, MOE_BLOCK, D)

    def expert_block(args):
        xb, e = args
        hid = jax.nn.silu(xb @ w_gate[e]) * (xb @ w_up[e])
        return hid @ w_down[e]

    ys = lax.map(expert_block, (xs, blk_expert)).reshape(P, D)
    out = jnp.zeros((T, D), ys.dtype).at[tok_buf].add(ys * w_buf[:, None].astype(ys.dtype))
    return out.reshape(B, S, D).astype(h.dtype)


def setup_inputs(seed: int = 0) -> dict:
    key = jax.random.key(seed)
    ks = iter(jax.random.split(key, 40))
    D = D_MODEL

    def nrm(shape, std):
        return jax.random.normal(next(ks), shape, jnp.float32) * std

    def gain(shape):
        return 1.0 + nrm(shape, 0.02)

    return {
        "x": nrm((BATCH, SEQ, D), 1.0),
        "c": nrm((BATCH, D), 1.0),
        "rel_table": nrm((N_BUCKETS, N_BIAS_HEADS), 0.5),
        "router_w": nrm((D, N_EXPERTS), D ** -0.5),
        "router_b": nrm((N_EXPERTS,), 0.01),
        "final_norm": gain((D,)),
        "norm_mix": gain((DEPTH, D)),
        "norm_ffn": gain((DEPTH, D)),
        "ada_w": nrm((DEPTH, D, 6 * D), ADA_INIT * D ** -0.5),
        "ada_b": nrm((DEPTH, 6 * D), 0.02),
        "moe_w_gate": nrm((DEPTH, N_EXPERTS, D, D_EXPERT), D ** -0.5),
        "moe_w_up": nrm((DEPTH, N_EXPERTS, D, D_EXPERT), D ** -0.5),
        "moe_w_down": nrm((DEPTH, N_EXPERTS, D_EXPERT, D), D_EXPERT ** -0.5),
        "ev_w_in": nrm((N_EVEN, D, EVEN_IN), D ** -0.5),
        "ev_w_out": nrm((N_EVEN, EVEN_OUT, D), EVEN_OUT ** -0.5),
        "nsa_pos_k": nrm((N_EVEN, CMP_LEN, HEAD_DIM), 0.1),
        "nsa_pos_v": nrm((N_EVEN, CMP_LEN, HEAD_DIM), 0.1),
        "nsa_ck_w1": nrm((N_EVEN, CMP_LEN * HEAD_DIM, CMP_HIDDEN), (CMP_LEN * HEAD_DIM) ** -0.5),
        "nsa_ck_w2": nrm((N_EVEN, CMP_HIDDEN, HEAD_DIM), CMP_HIDDEN ** -0.5),
        "nsa_cv_w1": nrm((N_EVEN, CMP_LEN * HEAD_DIM, CMP_HIDDEN), (CMP_LEN * HEAD_DIM) ** -0.5),
        "nsa_cv_w2": nrm((N_EVEN, CMP_HIDDEN, HEAD_DIM), CMP_HIDDEN ** -0.5),
        "od_w_in": nrm((N_ODD, D, ODD_IN), D ** -0.5),
        "od_w_out": nrm((N_ODD, ODD_OUT, D), ODD_OUT ** -0.5),
        "mla_q_norm": gain((N_ODD, MLA_Q_RANK)),
        "mla_kv_norm": gain((N_ODD, MLA_KV_RANK)),
        "mla_w_q_up": nrm((N_ODD, MLA_Q_RANK, MLA_HEADS * (MLA_NOPE + MLA_ROPE)), MLA_Q_RANK ** -0.5),
        "mla_w_kv_up": nrm((N_ODD, MLA_KV_RANK, MLA_HEADS * (MLA_NOPE + MLA_V)), MLA_KV_RANK ** -0.5),
        "swa_sinks": nrm((N_ODD, SWA_HEADS), 1.0),
    }


def reference(x, c, rel_table, router_w, router_b, final_norm, norm_mix, norm_ffn, ada_w, ada_b,
              moe_w_gate, moe_w_up, moe_w_down, ev_w_in, ev_w_out, nsa_pos_k, nsa_pos_v,
              nsa_ck_w1, nsa_ck_w2, nsa_cv_w1, nsa_cv_w2, od_w_in, od_w_out, mla_q_norm,
              mla_kv_norm, mla_w_q_up, mla_w_kv_up, swa_sinks):
    for layer in range(DEPTH):
        shift_m, scale_m, gate_m, shift_f, scale_f, gate_f = adaln(c, ada_w[layer], ada_b[layer])
        h = rmsnorm(x, norm_mix[layer]) * (1.0 + scale_m) + shift_m
        i = layer // 2
        if layer % 2 == 0:
            mix = even_mixer(h, ev_w_in[i], ev_w_out[i], nsa_pos_k[i], nsa_pos_v[i], nsa_ck_w1[i],
                             nsa_ck_w2[i], nsa_cv_w1[i], nsa_cv_w2[i], rel_table)
        else:
            mix = odd_mixer(h, od_w_in[i], od_w_out[i], mla_q_norm[i], mla_kv_norm[i],
                            mla_w_q_up[i], mla_w_kv_up[i], swa_sinks[i], rel_table)
        x = x + gate_m * mix.astype(x.dtype)
        h = rmsnorm(x, norm_ffn[layer]) * (1.0 + scale_f) + shift_f
        x = x + gate_f * grouped_moe(h, router_w, router_b, moe_w_gate[layer], moe_w_up[layer],
                                     moe_w_down[layer]).astype(x.dtype)
    return rmsnorm(x, final_norm)
```

```python
import functools
import math

import numpy as np
import jax
import jax.numpy as jnp
from jax import lax
from jax.experimental import pallas as pl
from jax.experimental.pallas import tpu as pltpu

F32 = jnp.float32
BF16 = jnp.bfloat16
HI = lax.Precision.HIGHEST

D_MODEL = 1024
HEAD_DIM = 64
NEG = -1e30
BIG = 1e30
M_INIT = -1e9
NORM_EPS = 1e-6

N_BUCKETS = 32
MAX_DISTANCE = 1024
N_BIAS_HEADS = 8

NSA_HEADS = 8
NSA_KV_HEADS = 2
NSA_GROUP = NSA_HEADS // NSA_KV_HEADS
CMP_LEN = 32
CMP_STRIDE = 16
CMP_HIDDEN = 256
SLC_BLOCK = 64
SLC_TOPN = 16
NSA_WINDOW = 512

MOBA_HEADS = 8
MOBA_BLOCK = 256
MOBA_TOPK = 3

MLA_HEADS = 4
MLA_Q_RANK = 256
MLA_KV_RANK = 128
MLA_NOPE = 128
MLA_ROPE = 64
MLA_V = 128
ROPE_THETA = 10000.0

SWA_HEADS = 8
SWA_KV_HEADS = 2
SWA_GROUP = SWA_HEADS // SWA_KV_HEADS
SWA_WINDOW = 128

N_EXPERTS = 16
N_GROUPS = 4
EXPERTS_PER_GROUP = N_EXPERTS // N_GROUPS
D_EXPERT = 512

EVEN_WIDTHS = (NSA_HEADS * HEAD_DIM,) + (NSA_KV_HEADS * HEAD_DIM,) * 6 + (3 * NSA_HEADS,) + (MOBA_HEADS * HEAD_DIM,) * 3
ODD_WIDTHS = (MLA_Q_RANK, MLA_KV_RANK, MLA_ROPE, SWA_HEADS * HEAD_DIM, SWA_KV_HEADS * HEAD_DIM, SWA_KV_HEADS * HEAD_DIM)

LANES = 128
VMEM_LIMIT = 56 * 1024 * 1024


def _params(*sem):
    return pltpu.CompilerParams(dimension_semantics=sem, vmem_limit_bytes=VMEM_LIMIT)


def _dot(a, b, precision=None):
    return lax.dot_general(a, b, (((1,), (0,)), ((), ())), precision=precision, preferred_element_type=F32)


def _dot_nt(a, b, precision=None):
    return lax.dot_general(a, b, (((1,), (1,)), ((), ())), precision=precision, preferred_element_type=F32)


def _iota(shape, dim):
    return lax.broadcasted_iota(jnp.int32, shape, dim)


def _offsets(widths):
    out, acc = [], 0
    for w in widths:
        out.append((acc, acc + w))
        acc += w
    return out


def _bucket_thresholds():
    d = np.arange(0, 4 * MAX_DISTANCE, dtype=np.int64)
    exact = N_BUCKETS // 2
    x = np.maximum(d, 1).astype(np.float32) / np.float32(exact)
    logp = exact + (np.log(x) / np.float32(math.log(MAX_DISTANCE / exact)) * np.float32(N_BUCKETS - exact)).astype(np.int32)
    bucket = np.where(d < exact, d, np.minimum(logp, N_BUCKETS - 1))
    return [int(np.argmax(bucket >= b)) for b in range(1, N_BUCKETS)]


BUCKET_THR = _bucket_thresholds()
FAR_DIST = BUCKET_THR[-1]


def _ada_kernel(c_ref, w_ref, b_ref, o_ref):
    c = c_ref[...]
    o_ref[...] = _dot(c * jax.nn.sigmoid(c), w_ref[...], precision=HI) + b_ref[...]


def ada_all(c, ada_w, ada_b):
    depth, d, n = ada_w.shape
    rows = 8
    cp = jnp.pad(c, ((0, rows - c.shape[0]), (0, 0)))
    tn = 1536
    out = pl.pallas_call(
        _ada_kernel,
        grid=(depth, n // tn),
        in_specs=[pl.BlockSpec((rows, d), lambda l, j: (0, 0)),
                  pl.BlockSpec((None, d, tn), lambda l, j: (l, 0, j)),
                  pl.BlockSpec((None, 1, tn), lambda l, j: (l, 0, j))],
        out_specs=pl.BlockSpec((None, rows, tn), lambda l, j: (l, 0, j)),
        out_shape=jax.ShapeDtypeStruct((depth, rows, n), F32),
        compiler_params=_params("parallel", "parallel"),
        name="ada",
    )(cp, ada_w, ada_b.reshape(depth, 1, n))
    return out[:, :c.shape[0], :]


def _norm_mod(x, g, sc, sh):
    y = x * lax.rsqrt(jnp.mean(x * x, axis=-1, keepdims=True) + NORM_EPS) * g
    return y * (1.0 + sc) + sh


def _nmm_kernel(x_ref, g_ref, sc_ref, sh_ref, w_ref, o_ref):
    h = _norm_mod(x_ref[...], g_ref[...], sc_ref[...], sh_ref[...])
    o_ref[...] = _dot(h.astype(BF16), w_ref[...]).astype(o_ref.dtype)


def norm_mod_matmul(x, g, sc, sh, w, tm=512):
    b, s, d = x.shape
    n = w.shape[1]
    return pl.pallas_call(
        _nmm_kernel,
        grid=(b, s // tm),
        in_specs=[pl.BlockSpec((None, tm, d), lambda bi, i: (bi, i, 0)),
                  pl.BlockSpec((1, d), lambda bi, i: (0, 0)),
                  pl.BlockSpec((None, 1, d), lambda bi, i: (bi, 0, 0)),
                  pl.BlockSpec((None, 1, d), lambda bi, i: (bi, 0, 0)),
                  pl.BlockSpec((d, n), lambda bi, i: (0, 0))],
        out_specs=pl.BlockSpec((None, tm, n), lambda bi, i: (bi, i, 0)),
        out_shape=jax.ShapeDtypeStruct((b, s, n), F32),
        compiler_params=_params("parallel", "parallel"),
        name="norm_mod_matmul",
    )(x, g.reshape(1, d), sc.reshape(b, 1, d), sh.reshape(b, 1, d), w)


def _bias_kernel(tab_ref, o_ref, *, rows, cols, step, cstride, c0):
    sub = 8
    off = pl.program_id(0) * step - c0
    base = _iota((sub, cols), 0) - cstride * _iota((sub, cols), 1)

    def body(r, carry):
        dist = off + r * sub + base
        vs = [jnp.full((sub, cols), tab_ref[0, h], F32) for h in range(N_BIAS_HEADS)]
        for b in range(1, N_BUCKETS):
            ge = dist >= BUCKET_THR[b - 1]
            for h in range(N_BIAS_HEADS):
                vs[h] = jnp.where(ge, tab_ref[b, h], vs[h])
        for h in range(N_BIAS_HEADS):
            o_ref[h, pl.ds(pl.multiple_of(r * sub, sub), sub), :] = vs[h]
        return carry

    lax.fori_loop(0, rows // sub, body, 0)


def bias_bank(rel_table, t):
    nd = -(-(FAR_DIST + t - 1) // t)
    kern = functools.partial(_bias_kernel, rows=t, cols=t, step=t, cstride=1, c0=0)
    bank = pl.pallas_call(
        kern,
        grid=(nd + 1,),
        in_specs=[pl.BlockSpec(memory_space=pltpu.SMEM)],
        out_specs=pl.BlockSpec((N_BIAS_HEADS, None, t, t), lambda i: (0, i, 0, 0)),
        out_shape=jax.ShapeDtypeStruct((N_BIAS_HEADS, nd + 1, t, t), F32),
        compiler_params=_params("parallel"),
        name="bias_bank",
    )(rel_table)
    return bank, nd


def bias_cmp(rel_table, s, nc, tq=256):
    kern = functools.partial(_bias_kernel, rows=tq, cols=nc, step=tq, cstride=CMP_STRIDE, c0=CMP_LEN - 1)
    return pl.pallas_call(
        kern,
        grid=(s // tq,),
        in_specs=[pl.BlockSpec(memory_space=pltpu.SMEM)],
        out_specs=pl.BlockSpec((N_BIAS_HEADS, tq, nc), lambda i: (0, i, 0)),
        out_shape=jax.ShapeDtypeStruct((N_BIAS_HEADS, s, nc), F32),
        compiler_params=_params("parallel"),
        name="bias_cmp",
    )(rel_table)


def _compress_kernel(x_ref, w1_ref, pos_ref, w2_ref, o_ref):
    x = x_ref[...]
    w1 = w1_ref[...]
    half = w1.shape[0]
    hid = w1.shape[1] // 2
    r = _dot(x, w1, precision=HI)
    pos = pos_ref[...]
    pb = _dot(pos[:, :half], w1[:, :hid], precision=HI) + _dot(pos[:, half:], w1[:, hid:], precision=HI)
    nxt = pltpu.roll(r[:, hid:], x.shape[0] - 1, 0)
    pre = r[:, :hid] + nxt + pb[0:1, :]
    o_ref[...] = _dot(jax.nn.gelu(pre), w2_ref[...], precision=HI)


def nsa_compress(kv, w1, pos, w2):
    two, bh, s, dh = kv.shape
    nc = s // CMP_STRIDE
    half = CMP_STRIDE * dh
    hid = w1.shape[2]
    x = kv.reshape(two, bh, nc, half)
    w1cat = jnp.concatenate([w1[:, :half, :], w1[:, half:, :]], axis=2)
    posf = jnp.broadcast_to(pos.reshape(two, 1, CMP_LEN * dh), (two, 8, CMP_LEN * dh))
    return pl.pallas_call(
        _compress_kernel,
        grid=(two, bh),
        in_specs=[pl.BlockSpec((None, None, nc, half), lambda w, i: (w, i, 0, 0)),
                  pl.BlockSpec((None, half, 2 * hid), lambda w, i: (w, 0, 0)),
                  pl.BlockSpec((None, 8, CMP_LEN * dh), lambda w, i: (w, 0, 0)),
                  pl.BlockSpec((None, hid, dh), lambda w, i: (w, 0, 0))],
        out_specs=pl.BlockSpec((None, None, nc, dh), lambda w, i: (w, i, 0, 0)),
        out_shape=jax.ShapeDtypeStruct((two, bh, nc, dh), F32),
        compiler_params=_params("parallel", "parallel"),
        name="nsa_compress",
    )(x, w1cat, posf, w2)


def _topn_mask(score, lane, n):
    lane_f = lane.astype(F32)
    sel = jnp.zeros(score.shape, jnp.bool_)
    for _ in range(n):
        m = jnp.max(score, axis=-1, keepdims=True)
        idx = jnp.min(jnp.where(score == m, lane_f, float(score.shape[-1])), axis=-1, keepdims=True)
        pick = lane_f == idx
        sel = sel | pick
        score = jnp.where(pick, -jnp.inf, score)
    return sel


def _cmp_kernel(q_ref, k_ref, v_ref, b_ref, ov_ref, oc_ref, sel_ref, *, tq, nc, nslc, ntop, group):
    qi = pl.program_id(2)
    t = qi * tq + _iota((tq, 1), 0)
    n = _iota((1, nc), 1)
    mask = (n * CMP_STRIDE + (CMP_LEN - 1) <= t) & (n < nc - 1)
    k = k_ref[...]
    v = v_ref[...].astype(BF16)
    psum = jnp.zeros((tq, nc), F32)
    for g in range(group):
        s = _dot_nt(q_ref[g], k, precision=HI) + b_ref[g]
        s = jnp.where(mask, s, NEG)
        m = jnp.max(s, axis=-1, keepdims=True)
        p = jnp.where(mask, jnp.exp(s - m), 0.0)
        den = jnp.sum(p, axis=-1, keepdims=True)
        p = p * jnp.where(den > 0.0, 1.0 / den, 0.0)
        oc_ref[g] = _dot(p.astype(BF16), v)
        psum = psum + p
    imp = _dot(psum, ov_ref[...], precision=HI)
    j = _iota((tq, nslc), 1)
    blk = jnp.right_shift(t, int(math.log2(SLC_BLOCK)))
    forced = (j == 0) | (j == blk) | (j == blk - 1)
    valid = j <= blk
    score = jnp.where(forced, BIG, jnp.where(valid, imp, NEG))
    sel = _topn_mask(score, j, ntop) & valid
    sel_ref[0] = sel.astype(sel_ref.dtype)


def nsa_cmp_attention(q, k_cmp, v_cmp, bias_c, tq=256):
    b, hk, g, s, dh = q.shape
    nc = k_cmp.shape[2]
    nslc = s // SLC_BLOCK
    ntop = min(SLC_TOPN, nslc)
    cmp_start = np.arange(nc, dtype=np.int64) * CMP_STRIDE
    slc_lo = np.arange(nslc, dtype=np.int64) * SLC_BLOCK
    overlap = ((cmp_start[:, None] <= slc_lo[None, :] + SLC_BLOCK - 1)
               & (cmp_start[:, None] + CMP_LEN - 1 >= slc_lo[None, :])).astype(np.float32)
    kern = functools.partial(_cmp_kernel, tq=tq, nc=nc, nslc=nslc, ntop=ntop, group=g)
    return pl.pallas_call(
        kern,
        grid=(b, hk, s // tq),
        in_specs=[pl.BlockSpec((None, None, g, tq, dh), lambda bi, h, i: (bi, h, 0, i, 0)),
                  pl.BlockSpec((None, None, nc, dh), lambda bi, h, i: (bi, h, 0, 0)),
                  pl.BlockSpec((None, None, nc, dh), lambda bi, h, i: (bi, h, 0, 0)),
                  pl.BlockSpec((g, tq, nc), lambda bi, h, i: (h, i, 0)),
                  pl.BlockSpec((nc, nslc), lambda bi, h, i: (0, 0))],
        out_specs=[pl.BlockSpec((None, None, g, tq, dh), lambda bi, h, i: (bi, h, 0, i, 0)),
                   pl.BlockSpec((None, None, 1, tq, nslc), lambda bi, h, i: (bi, h, 0, i, 0))],
        out_shape=[jax.ShapeDtypeStruct((b, hk, g, s, dh), F32),
                   jax.ShapeDtypeStruct((b, hk, 1, s, nslc), BF16)],
        compiler_params=_params("parallel", "parallel", "parallel"),
        name="nsa_cmp_attention",
    )(q, k_cmp, v_cmp, bias_c, jnp.asarray(overlap))


def _kmean_kernel(k_ref, o_ref, *, nb, blk):
    k = k_ref[...]
    o_ref[...] = jnp.sum(k.reshape(nb, blk, k.shape[-1]), axis=1) * (1.0 / blk)


def _moba_sel_kernel(q_ref, km_ref, sel_ref, *, tq, nb, ntop):
    qi = pl.program_id(2)
    t = qi * tq + _iota((tq, 1), 0)
    own = jnp.right_shift(t, int(math.log2(MOBA_BLOCK)))
    j = _iota((tq, nb), 1)
    gate = _dot_nt(q_ref[...], km_ref[...], precision=HI)
    past = j < own
    sel = _topn_mask(jnp.where(past, gate, NEG), j, ntop) & past
    sel_ref[...] = (sel | (j == own)).astype(sel_ref.dtype)


def moba_select(q, k, tq=256):
    b, h, s, dh = q.shape
    nb = s // MOBA_BLOCK
    ntop = min(MOBA_TOPK, nb)
    kmean = pl.pallas_call(
        functools.partial(_kmean_kernel, nb=nb, blk=MOBA_BLOCK),
        grid=(b, h),
        in_specs=[pl.BlockSpec((None, None, s, dh), lambda bi, hi: (bi, hi, 0, 0))],
        out_specs=pl.BlockSpec((None, None, nb, dh), lambda bi, hi: (bi, hi, 0, 0)),
        out_shape=jax.ShapeDtypeStruct((b, h, nb, dh), F32),
        compiler_params=_params("parallel", "parallel"),
        name="moba_kmean",
    )(k)
    return pl.pallas_call(
        functools.partial(_moba_sel_kernel, tq=tq, nb=nb, ntop=ntop),
        grid=(b, h, s // tq),
        in_specs=[pl.BlockSpec((None, None, tq, dh), lambda bi, hi, i: (bi, hi, i, 0)),
                  pl.BlockSpec((None, None, nb, dh), lambda bi, hi, i: (bi, hi, 0, 0))],
        out_specs=pl.BlockSpec((None, None, tq, nb), lambda bi, hi, i: (bi, hi, i, 0)),
        out_shape=jax.ShapeDtypeStruct((b, h, s, nb), BF16),
        compiler_params=_params("parallel", "parallel", "parallel"),
        name="moba_select",
    )(q, kmean)


def _flash_kernel(qt_ref, kt_ref, *refs, hb, kb, sb, t, nparts, use_bias, use_sel, bpt, blk_shift):
    pos = 0
    q_refs = refs[pos:pos + nparts]; pos += nparts
    k_refs = refs[pos:pos + nparts]; pos += nparts
    v_ref = refs[pos]; pos += 1
    sel_ref = bias_ref = None
    if use_sel:
        sel_ref = refs[pos]; pos += 1
    if use_bias:
        bias_ref = refs[pos]; pos += 1
    o_ref, m_ref, l_ref, acc_ref = refs[pos:pos + 4]

    step = pl.program_id(2)
    qi = qt_ref[step]
    ki = kt_ref[step]

    @pl.when(ki == 0)
    def _():
        m_ref[...] = jnp.full(m_ref.shape, M_INIT, F32)
        l_ref[...] = jnp.zeros(l_ref.shape, F32)
        acc_ref[...] = jnp.zeros(acc_ref.shape, F32)

    causal = (ki * t + _iota((t, t), 1)) <= (qi * t + _iota((t, t), 0))

    def sel_mask(si):
        sel = sel_ref[si]
        nb = sel.shape[-1]
        expand = (_iota((nb, t), 0) == ki * bpt + jnp.right_shift(_iota((nb, t), 1), blk_shift)).astype(BF16)
        return (_dot(sel, expand) > 0.5) & causal

    mask = causal
    if use_sel and sb == 1:
        mask = sel_mask(0)
    for g in range(hb):
        kg = g if kb > 1 else 0
        if use_sel and sb > 1:
            mask = sel_mask(g)
        s = _dot_nt(q_refs[0][g], k_refs[0][kg])
        for p in range(1, nparts):
            s = s + _dot_nt(q_refs[p][g], k_refs[p][kg])
        if use_bias:
            s = s + bias_ref[g]
        s = jnp.where(mask, s, NEG)
        m_prev = m_ref[g]
        m_new = jnp.maximum(m_prev, jnp.max(s, axis=-1, keepdims=True))
        alpha = jnp.exp(m_prev - m_new)
        p = jnp.exp(s - m_new)
        l_ref[g] = alpha * l_ref[g] + jnp.sum(p, axis=-1, keepdims=True)
        acc_ref[g] = alpha * acc_ref[g] + _dot(p.astype(BF16), v_ref[kg])
        m_ref[g] = m_new

    @pl.when(ki == qi)
    def _():
        o_ref[...] = (acc_ref[...] / l_ref[...]).astype(o_ref.dtype)


def flash_attention(q_parts, k_parts, v, sel=None, bias=None, nd=0, t=512, blk=1):
    b, ng, hb, s, _ = q_parts[0].shape
    kb = v.shape[2]
    dv = v.shape[-1]
    t = min(t, s)
    nq = s // t
    pairs = [(i, j) for i in range(nq) for j in range(i + 1)]
    qt = jnp.asarray(np.array([p[0] for p in pairs], np.int32))
    kt = jnp.asarray(np.array([p[1] for p in pairs], np.int32))
    nparts = len(q_parts)
    use_sel, use_bias = sel is not None, bias is not None
    sb = sel.shape[2] if use_sel else 1

    in_specs, args = [], []
    for qp in q_parts:
        in_specs.append(pl.BlockSpec((None, None, hb, t, qp.shape[-1]), lambda bi, n, st, qt, kt: (bi, n, 0, qt[st], 0)))
        args.append(qp)
    for kp in list(k_parts) + [v]:
        in_specs.append(pl.BlockSpec((None, None, kb, t, kp.shape[-1]), lambda bi, n, st, qt, kt: (bi, n, 0, kt[st], 0)))
        args.append(kp)
    if use_sel:
        in_specs.append(pl.BlockSpec((None, None, sb, t, sel.shape[-1]), lambda bi, n, st, qt, kt: (bi, n, 0, qt[st], 0)))
        args.append(sel)
    if use_bias:
        in_specs.append(pl.BlockSpec((hb, None, t, t),
                                     lambda bi, n, st, qt, kt: (n, jnp.minimum(qt[st] - kt[st], nd), 0, 0)))
        args.append(bias)
    kern = functools.partial(_flash_kernel, hb=hb, kb=kb, sb=sb, t=t, nparts=nparts, use_bias=use_bias,
                             use_sel=use_sel, bpt=t // blk, blk_shift=int(math.log2(blk)))
    grid_spec = pltpu.PrefetchScalarGridSpec(
        num_scalar_prefetch=2,
        grid=(b, ng, len(pairs)),
        in_specs=in_specs,
        out_specs=pl.BlockSpec((None, None, hb, t, dv), lambda bi, n, st, qt, kt: (bi, n, 0, qt[st], 0)),
        scratch_shapes=[pltpu.VMEM((hb, t, 1), F32), pltpu.VMEM((hb, t, 1), F32), pltpu.VMEM((hb, t, dv), F32)])
    return pl.pallas_call(
        kern,
        grid_spec=grid_spec,
        out_shape=jax.ShapeDtypeStruct((b, ng, hb, s, dv), F32),
        compiler_params=_params("parallel", "parallel", "arbitrary"),
        name="flash_attention",
    )(qt, kt, *args)


def _window_kernel(*refs, group, t, ntile, window, has_sink):
    q_ref = refs[0]
    k_refs = refs[1:1 + ntile]
    v_refs = refs[1 + ntile:1 + 2 * ntile]
    bias_ref = refs[1 + 2 * ntile]
    sink_ref = refs[2 + 2 * ntile] if has_sink else None
    o_ref = refs[-1]
    kvh = pl.program_id(1)
    diff = _iota((t, t), 0) - _iota((t, t), 1)
    masks = []
    for r in range(ntile):
        dist = (ntile - 1 - r) * t + diff
        masks.append((dist >= 0) & (dist < window))
    for g in range(group):
        q = q_ref[g]
        ss = []
        m = None
        for r in range(ntile):
            s = _dot_nt(q, k_refs[r][...]) + bias_ref[g, ntile - 1 - r]
            s = jnp.where(masks[r], s, NEG)
            ss.append(s)
            mr = jnp.max(s, axis=-1, keepdims=True)
            m = mr if m is None else jnp.maximum(m, mr)
        if has_sink:
            sink = sink_ref[kvh * group + g]
            m = jnp.maximum(m, sink)
        den = jnp.exp(sink - m) if has_sink else 0.0
        acc = None
        for r in range(ntile):
            p = jnp.exp(ss[r] - m)
            den = den + jnp.sum(p, axis=-1, keepdims=True)
            pv = _dot(p.astype(BF16), v_refs[r][...])
            acc = pv if acc is None else acc + pv
        o_ref[g] = acc / den


def window_attention(q, k, v, bank, t, window, sinks=None):
    b, hk, g, s, dh = q.shape
    ntile = window // t + 1
    pad = ((0, 0), (0, 0), (window, 0), (0, 0))
    kp, vp = jnp.pad(k, pad), jnp.pad(v, pad)
    has_sink = sinks is not None
    in_specs = [pl.BlockSpec((None, None, g, t, dh), lambda bi, h, i: (bi, h, 0, i, 0))]
    args = [q]
    for arr in (kp, vp):
        for r in range(ntile):
            in_specs.append(pl.BlockSpec((None, None, t, dh), lambda bi, h, i, r=r: (bi, h, i + r, 0)))
            args.append(arr)
    in_specs.append(pl.BlockSpec((g, ntile, t, t), lambda bi, h, i: (h, 0, 0, 0)))
    args.append(bank)
    if has_sink:
        in_specs.append(pl.BlockSpec(memory_space=pltpu.SMEM))
        args.append(sinks)
    kern = functools.partial(_window_kernel, group=g, t=t, ntile=ntile, window=window, has_sink=has_sink)
    return pl.pallas_call(
        kern,
        grid=(b, hk, s // t),
        in_specs=in_specs,
        out_specs=pl.BlockSpec((None, None, g, t, dh), lambda bi, h, i: (bi, h, 0, i, 0)),
        out_shape=jax.ShapeDtypeStruct((b, hk, g, s, dh), F32),
        compiler_params=_params("parallel", "parallel", "parallel"),
        name="window_attention",
    )(*args)


def _rms(x, g):
    return x * lax.rsqrt(jnp.mean(x * x, axis=-1, keepdims=True) + NORM_EPS) * g


def _mla_up_kernel(cq_ref, ckv_ref, kr_ref, krs_ref, qn_ref, kvn_ref, wq_ref, wkv_ref, cs_ref, sn_ref,
                   qnope_ref, qrope_ref, knope_ref, v_ref, krope_ref, *, scale):
    cs = cs_ref[...]
    sn = sn_ref[...]
    q = _dot(_rms(cq_ref[...], qn_ref[...]).astype(BF16), wq_ref[...]) * scale
    kv = _dot(_rms(ckv_ref[...], kvn_ref[...]).astype(BF16), wkv_ref[...])
    wq_head = MLA_NOPE + 2 * MLA_ROPE
    for h in range(MLA_HEADS):
        base = h * wq_head
        qnope_ref[h] = q[:, base:base + MLA_NOPE].astype(qnope_ref.dtype)
        x = q[:, base + MLA_NOPE:base + MLA_NOPE + MLA_ROPE]
        xs = q[:, base + MLA_NOPE + MLA_ROPE:base + wq_head]
        qrope_ref[h] = (x * cs + xs * sn).astype(qrope_ref.dtype)
        kb = h * (MLA_NOPE + MLA_V)
        knope_ref[h] = kv[:, kb:kb + MLA_NOPE].astype(knope_ref.dtype)
        v_ref[h] = kv[:, kb + MLA_NOPE:kb + MLA_NOPE + MLA_V].astype(v_ref.dtype)
    kr = (kr_ref[...] * cs + krs_ref[...] * sn).astype(krope_ref.dtype)
    for h in range(MLA_HEADS):
        krope_ref[h] = kr


def mla_up(c_q, c_kv, k_rope, k_rope_sw, q_norm, kv_norm, w_q_up, w_kv_up, tm=512):
    b, s, _ = c_q.shape
    half = MLA_ROPE // 2
    inv = ROPE_THETA ** (-np.arange(0, MLA_ROPE, 2, dtype=np.float32) / np.float32(MLA_ROPE))
    ang = np.arange(s, dtype=np.float32)[:, None] * inv[None, :].astype(np.float32)
    cos, sin = np.cos(ang).astype(np.float32), np.sin(ang).astype(np.float32)
    cs = jnp.asarray(np.concatenate([cos, cos], axis=1))
    sn = jnp.asarray(np.concatenate([-sin, sin], axis=1))
    dq = MLA_NOPE + MLA_ROPE
    wq = w_q_up.reshape(MLA_Q_RANK, MLA_HEADS, dq)
    rope_cols = wq[:, :, MLA_NOPE:]
    swapped = jnp.concatenate([rope_cols[:, :, half:], rope_cols[:, :, :half]], axis=2)
    wq_aug = jnp.concatenate([wq, swapped], axis=2).reshape(MLA_Q_RANK, MLA_HEADS * (dq + MLA_ROPE)).astype(BF16)
    wkv = w_kv_up.astype(BF16)
    scale = (MLA_NOPE + MLA_ROPE) ** -0.5
    tm = min(tm, s)
    row = lambda w: pl.BlockSpec((None, tm, w), lambda bi, i: (bi, i, 0))
    full = lambda a: pl.BlockSpec(a.shape, lambda bi, i: (0,) * a.ndim)
    head = lambda w: pl.BlockSpec((None, MLA_HEADS, tm, w), lambda bi, i: (bi, 0, i, 0))
    qn2, kvn2 = q_norm.reshape(1, -1), kv_norm.reshape(1, -1)
    outs = pl.pallas_call(
        functools.partial(_mla_up_kernel, scale=scale),
        grid=(b, s // tm),
        in_specs=[row(MLA_Q_RANK), row(MLA_KV_RANK), row(MLA_ROPE), row(MLA_ROPE), full(qn2), full(kvn2),
                  full(wq_aug), full(wkv),
                  pl.BlockSpec((tm, MLA_ROPE), lambda bi, i: (i, 0)), pl.BlockSpec((tm, MLA_ROPE), lambda bi, i: (i, 0))],
        out_specs=[head(MLA_NOPE), head(MLA_ROPE), head(MLA_NOPE), head(MLA_V), head(MLA_ROPE)],
        out_shape=[jax.ShapeDtypeStruct((b, MLA_HEADS, s, w), BF16)
                   for w in (MLA_NOPE, MLA_ROPE, MLA_NOPE, MLA_V, MLA_ROPE)],
        compiler_params=_params("parallel", "parallel"),
        name="mla_up",
    )(c_q, c_kv, k_rope, k_rope_sw, qn2, kvn2, wq_aug, wkv, cs, sn)
    return outs


def _out_kernel(*refs, gated):
    if gated:
        oc_ref, os_ref, ow_ref, gt_ref, ex_ref, ob_ref, x_ref, gm_ref, w_ref, o_ref = refs
        half = oc_ref.shape[-1]
        ge = _dot(jax.nn.sigmoid(gt_ref[...]), ex_ref[...], precision=HI)
        oa = ge[:, :half] * oc_ref[...] + ge[:, half:2 * half] * os_ref[...] + ge[:, 2 * half:] * ow_ref[...]
    else:
        oa_ref, ob_ref, x_ref, gm_ref, w_ref, o_ref = refs
        half = oa_ref.shape[-1]
        oa = oa_ref[...]
    mix = _dot(oa.astype(BF16), w_ref[:half, :]) + _dot(ob_ref[...].astype(BF16), w_ref[half:, :])
    o_ref[...] = x_ref[...] + gm_ref[...] * mix


def out_project(parts, ob, x, gate_m, w_out, gates=None, tm=512):
    b, s, d = x.shape
    gated = gates is not None
    tm = min(tm, s)
    row = lambda a: pl.BlockSpec((None, tm, a.shape[-1]), lambda bi, i: (bi, i, 0))
    full = lambda a: pl.BlockSpec(a.shape, lambda bi, i: (0,) * a.ndim)
    args, in_specs = [], []
    for p in parts:
        args.append(p); in_specs.append(row(p))
    if gated:
        half = parts[0].shape[-1]
        nh = half // HEAD_DIM
        ex = np.zeros((LANES, 3 * half), np.float32)
        for h in range(nh):
            for br in range(3):
                ex[h * 3 + br, br * half + h * HEAD_DIM: br * half + (h + 1) * HEAD_DIM] = 1.0
        gpad = jnp.pad(gates, ((0, 0), (0, 0), (0, LANES - gates.shape[-1])))
        ex = jnp.asarray(ex)
        args += [gpad, ex]; in_specs += [row(gpad), full(ex)]
    gm = gate_m.reshape(b, 1, d)
    wb = w_out.astype(BF16)
    args += [ob, x, gm, wb]
    in_specs += [row(ob), row(x), pl.BlockSpec((None, 1, d), lambda bi, i: (bi, 0, 0)), full(wb)]
    return pl.pallas_call(
        functools.partial(_out_kernel, gated=gated),
        grid=(b, s // tm),
        in_specs=in_specs,
        out_specs=pl.BlockSpec((None, tm, d), lambda bi, i: (bi, i, 0)),
        out_shape=jax.ShapeDtypeStruct((b, s, d), F32),
        compiler_params=_params("parallel", "parallel"),
        name="out_project",
    )(*args)


def _ffn_pre_kernel(x_ref, g_ref, sc_ref, sh_ref, rw_ref, rb_ref, h_ref, cw_ref):
    h = _norm_mod(x_ref[...], g_ref[...], sc_ref[...], sh_ref[...])
    h_ref[...] = h.astype(h_ref.dtype)
    aff = jax.nn.sigmoid(_dot_nt(rw_ref[...], h, precision=HI))
    biased = aff + rb_ref[...]
    epg = EXPERTS_PER_GROUP
    brow = [biased[e:e + 1, :] for e in range(N_EXPERTS)]
    arow = [aff[e:e + 1, :] for e in range(N_EXPERTS)]
    best = gsel = None
    for gi in range(N_GROUPS):
        a, b_, c, d_ = brow[gi * epg:(gi + 1) * epg]
        hi1, lo1, hi2, lo2 = jnp.maximum(a, b_), jnp.minimum(a, b_), jnp.maximum(c, d_), jnp.minimum(c, d_)
        score = jnp.maximum(hi1, hi2) + jnp.maximum(jnp.minimum(hi1, hi2), jnp.maximum(lo1, lo2))
        if gi == 0:
            best, gsel = score, jnp.zeros(score.shape, jnp.int32)
        else:
            better = score > best
            gsel = jnp.where(better, gi, gsel)
            best = jnp.where(better, score, best)

    def in_group(rows, j):
        v = rows[j]
        for gi in range(1, N_GROUPS):
            v = jnp.where(gsel == gi, rows[gi * epg + j], v)
        return v

    bv = [in_group(brow, j) for j in range(epg)]
    av = [in_group(arow, j) for j in range(epg)]

    def argmax_excluding(skip):
        val = idx = None
        for j in range(epg):
            cand = bv[j] if skip is None else jnp.where(skip == j, -jnp.inf, bv[j])
            if j == 0:
                val, idx = cand, jnp.zeros(cand.shape, jnp.int32)
            else:
                better = cand > val
                idx = jnp.where(better, j, idx)
                val = jnp.where(better, cand, val)
        return idx

    first = argmax_excluding(None)
    second = argmax_excluding(first)

    def pick(rows, idx):
        v = rows[0]
        for j in range(1, epg):
            v = jnp.where(idx == j, rows[j], v)
        return v

    a1, a2 = pick(av, first), pick(av, second)
    tot = a1 + a2
    e1, e2 = gsel * epg + first, gsel * epg + second
    eid = _iota(aff.shape, 0)
    cw_ref[...] = jnp.where(eid == e1, a1 / tot, 0.0) + jnp.where(eid == e2, a2 / tot, 0.0)


def ffn_pre(x, g, sc, sh, router_w, router_b, tm=512):
    b, s, d = x.shape
    e = router_w.shape[1]
    tm = min(tm, s)
    return pl.pallas_call(
        _ffn_pre_kernel,
        grid=(b, s // tm),
        in_specs=[pl.BlockSpec((None, tm, d), lambda bi, i: (bi, i, 0)),
                  pl.BlockSpec((1, d), lambda bi, i: (0, 0)),
                  pl.BlockSpec((None, 1, d), lambda bi, i: (bi, 0, 0)),
                  pl.BlockSpec((None, 1, d), lambda bi, i: (bi, 0, 0)),
                  pl.BlockSpec((e, d), lambda bi, i: (0, 0)),
                  pl.BlockSpec((e, 1), lambda bi, i: (0, 0))],
        out_specs=[pl.BlockSpec((None, tm, d), lambda bi, i: (bi, i, 0)),
                   pl.BlockSpec((None, e, tm), lambda bi, i: (bi, 0, i))],
        out_shape=[jax.ShapeDtypeStruct((b, s, d), BF16), jax.ShapeDtypeStruct((b, e, s), F32)],
        compiler_params=_params("parallel", "parallel"),
        name="ffn_pre",
    )(x, g.reshape(1, d), sc.reshape(b, 1, d), sh.reshape(b, 1, d), router_w.T, router_b.reshape(e, 1))


def _moe_kernel(h_ref, cw_ref, x_ref, gf_ref, wg_ref, wu_ref, wd_ref, o_ref, acc_ref):
    e = pl.program_id(2)

    @pl.when(e == 0)
    def _():
        acc_ref[...] = jnp.zeros(acc_ref.shape, F32)

    h = h_ref[...]
    a = _dot(h, wg_ref[...])
    u = _dot(h, wu_ref[...])
    cw = cw_ref[...]
    c = jnp.sum(jnp.where(_iota(cw.shape, 1) == e, cw, 0.0), axis=-1, keepdims=True)
    hid = (a * jax.nn.sigmoid(a)) * u * c
    acc_ref[...] += _dot(hid.astype(BF16), wd_ref[...])

    @pl.when(e == pl.num_programs(2) - 1)
    def _():
        o_ref[...] = x_ref[...] + gf_ref[...] * acc_ref[...]


def moe_dense(h, cw, x, gate_f, w_gate, w_up, w_down, tm=512):
    b, s, d = x.shape
    ne, _, f = w_gate.shape
    tm = min(tm, s)
    return pl.pallas_call(
        _moe_kernel,
        grid=(b, s // tm, ne),
        in_specs=[pl.BlockSpec((None, tm, d), lambda bi, i, e: (bi, i, 0)),
                  pl.BlockSpec((None, tm, ne), lambda bi, i, e: (bi, i, 0)),
                  pl.BlockSpec((None, tm, d), lambda bi, i, e: (bi, i, 0)),
                  pl.BlockSpec((None, 1, d), lambda bi, i, e: (bi, 0, 0)),
                  pl.BlockSpec((None, d, f), lambda bi, i, e: (e, 0, 0)),
                  pl.BlockSpec((None, d, f), lambda bi, i, e: (e, 0, 0)),
                  pl.BlockSpec((None, f, d), lambda bi, i, e: (e, 0, 0))],
        out_specs=pl.BlockSpec((None, tm, d), lambda bi, i, e: (bi, i, 0)),
        out_shape=jax.ShapeDtypeStruct((b, s, d), F32),
        scratch_shapes=[pltpu.VMEM((tm, d), F32)],
        compiler_params=_params("parallel", "parallel", "arbitrary"),
        name="moe_dense",
    )(h, cw, x, gate_f.reshape(b, 1, d), w_gate, w_up, w_down)


def _final_norm_kernel(x_ref, g_ref, o_ref):
    o_ref[...] = _rms(x_ref[...], g_ref[...])


def final_rmsnorm(x, g, tm=512):
    b, s, d = x.shape
    tm = min(tm, s)
    return pl.pallas_call(
        _final_norm_kernel,
        grid=(b, s // tm),
        in_specs=[pl.BlockSpec((None, tm, d), lambda bi, i: (bi, i, 0)), pl.BlockSpec((1, d), lambda bi, i: (0, 0))],
        out_specs=pl.BlockSpec((None, tm, d), lambda bi, i: (bi, i, 0)),
        out_shape=jax.ShapeDtypeStruct((b, s, d), F32),
        compiler_params=_params("parallel", "parallel"),
        name="final_rmsnorm",
    )(x, g.reshape(1, d))


def _heads(x, nh):
    b, s, w = x.shape
    return x.reshape(b, s, nh, w // nh).transpose(0, 2, 1, 3)


def _merge(x):
    b, ng, hb, s, dh = x.shape
    return x.transpose(0, 3, 1, 2, 4).reshape(b, s, ng * hb * dh)


def even_mixer(proj, x, gate_m, w_out, pos_k, pos_v, ck_w1, ck_w2, cv_w1, cv_w2, rel_table, banks):
    b, s, _ = proj.shape
    off = _offsets(EVEN_WIDTHS)
    q_a, kc, vc, ks, vs, kw, vw, gates, q_b, k_b, v_b = (proj[..., lo:hi] for lo, hi in off)
    hk, g, dh = NSA_KV_HEADS, NSA_GROUP, HEAD_DIM
    scale = dh ** -0.5
    qa = (_heads(q_a, NSA_HEADS) * scale).reshape(b, hk, g, s, dh)
    qa16 = qa.astype(BF16)
    (bank_l, nd_l, t_l), (bank_w, _, t_w) = banks["dense"], banks["nsa_window"]

    nc = s // CMP_STRIDE
    kv = jnp.stack([_heads(kc, hk), _heads(vc, hk)]).reshape(2, b * hk, s, dh)
    cmp = nsa_compress(kv, jnp.stack([ck_w1, cv_w1]), jnp.stack([pos_k, pos_v]), jnp.stack([ck_w2, cv_w2]))
    cmp = cmp.reshape(2, b, hk, nc, dh)
    o_c, sel = nsa_cmp_attention(qa, cmp[0], cmp[1], bias_cmp(rel_table, s, nc, tq=min(256, s)), tq=min(256, s))
    ks5 = _heads(ks, hk).astype(BF16)[:, :, None]
    vs5 = _heads(vs, hk).astype(BF16)[:, :, None]
    o_s = flash_attention([qa16], [ks5], vs5, sel=sel, bias=bank_l, nd=nd_l, t=t_l, blk=SLC_BLOCK)
    o_w = window_attention(qa16, _heads(kw, hk).astype(BF16), _heads(vw, hk).astype(BF16), bank_w, t_w, NSA_WINDOW)

    hb = 4
    qb = _heads(q_b, MOBA_HEADS)
    kb = _heads(k_b, MOBA_HEADS)
    vb = _heads(v_b, MOBA_HEADS)
    selb = moba_select(qb, kb, tq=min(256, s))
    ngb = MOBA_HEADS // hb
    r5 = lambda a: a.reshape(b, ngb, hb, s, a.shape[-1])
    o_b = flash_attention([r5((qb * scale).astype(BF16))], [r5(kb.astype(BF16))], r5(vb.astype(BF16)),
                          sel=r5(selb), bias=bank_l, nd=nd_l, t=t_l, blk=MOBA_BLOCK)

    return out_project([_merge(o_c), _merge(o_s), _merge(o_w)], _merge(o_b), x, gate_m, w_out, gates=gates)


def odd_mixer(proj, x, gate_m, w_out, q_norm, kv_norm, w_q_up, w_kv_up, sinks, banks):
    b, s, _ = proj.shape
    off = _offsets(ODD_WIDTHS + (MLA_ROPE,))
    c_q, c_kv, k_rope, q_d, k_d, v_d, k_rope_sw = (proj[..., lo:hi] for lo, hi in off)
    qn, qr, kn, v, kr = mla_up(c_q, c_kv, k_rope, k_rope_sw, q_norm, kv_norm, w_q_up, w_kv_up)
    hb = 2
    ng = MLA_HEADS // hb
    r5 = lambda a: a.reshape(b, ng, hb, s, a.shape[-1])
    o_c = flash_attention([r5(qn), r5(qr)], [r5(kn), r5(kr)], r5(v), t=banks["dense"][2])

    hk, g, dh = SWA_KV_HEADS, SWA_GROUP, HEAD_DIM
    scale = dh ** -0.5
    qd = (_heads(q_d, SWA_HEADS) * scale).reshape(b, hk, g, s, dh).astype(BF16)
    bank_s, _, t_s = banks["swa"]
    o_d = window_attention(qd, _heads(k_d, hk).astype(BF16), _heads(v_d, hk).astype(BF16), bank_s, t_s, SWA_WINDOW,
                           sinks=sinks)
    return out_project([_merge(o_c)], _merge(o_d), x, gate_m, w_out)


def kernel(x, c, rel_table, router_w, router_b, final_norm, norm_mix, norm_ffn, ada_w, ada_b, moe_w_gate, moe_w_up, moe_w_down, ev_w_in, ev_w_out, nsa_pos_k, nsa_pos_v, nsa_ck_w1, nsa_ck_w2, nsa_cv_w1, nsa_cv_w2, od_w_in, od_w_out, mla_q_norm, mla_kv_norm, mla_w_q_up, mla_w_kv_up, swa_sinks):
    b, s, d = x.shape
    depth = ada_w.shape[0]
    mods = ada_all(c, ada_w, ada_b)
    t_dense = min(512, s)
    bank_l, nd_l = bias_bank(rel_table, t_dense)
    t_w = min(256, s)
    bank_w, nd_w = bias_bank(rel_table, t_w)
    bank_s, nd_s = bias_bank(rel_table, SWA_WINDOW)
    banks = {"dense": (bank_l, nd_l, t_dense), "nsa_window": (bank_w, nd_w, t_w), "swa": (bank_s, nd_s, SWA_WINDOW)}

    for layer in range(depth):
        shift_m, scale_m, gate_m, shift_f, scale_f, gate_f = jnp.split(mods[layer], 6, axis=-1)
        i = layer // 2
        if layer % 2 == 0:
            w_in = ev_w_in[i]
            pad = -w_in.shape[1] % LANES
            proj = norm_mod_matmul(x, norm_mix[layer], scale_m, shift_m, jnp.pad(w_in, ((0, 0), (0, pad))).astype(BF16))
            x = even_mixer(proj, x, gate_m, ev_w_out[i], nsa_pos_k[i], nsa_pos_v[i], nsa_ck_w1[i], nsa_ck_w2[i],
                           nsa_cv_w1[i], nsa_cv_w2[i], rel_table, banks)
        else:
            w_in = od_w_in[i]
            lo = MLA_Q_RANK + MLA_KV_RANK
            half = MLA_ROPE // 2
            sw = jnp.concatenate([w_in[:, lo + half:lo + MLA_ROPE], w_in[:, lo:lo + half]], axis=1)
            proj = norm_mod_matmul(x, norm_mix[layer], scale_m, shift_m, jnp.concatenate([w_in, sw], axis=1).astype(BF16))
            x = odd_mixer(proj, x, gate_m, od_w_out[i], mla_q_norm[i], mla_kv_norm[i], mla_w_q_up[i], mla_w_kv_up[i],
                          swa_sinks[i], banks)
        h, cw = ffn_pre(x, norm_ffn[layer], scale_f, shift_f, router_w, router_b)
        x = moe_dense(h, cw.transpose(0, 2, 1), x, gate_f, moe_w_gate[layer].astype(BF16), moe_w_up[layer].astype(BF16),
                      moe_w_down[layer].astype(BF16))
    return final_rmsnorm(x, final_norm)
```

```python
import functools
import math

import numpy as np
import jax
import jax.numpy as jnp
from jax import lax
from jax.experimental import pallas as pl
from jax.experimental.pallas import tpu as pltpu

F32 = jnp.float32
BF16 = jnp.bfloat16
HI = lax.Precision.HIGHEST

D_MODEL = 1024
HEAD_DIM = 64
NEG = -1e30
BIG = 1e30
M_INIT = -1e9
NORM_EPS = 1e-6

N_BUCKETS = 32
MAX_DISTANCE = 1024
N_BIAS_HEADS = 8

NSA_HEADS = 8
NSA_KV_HEADS = 2
NSA_GROUP = NSA_HEADS // NSA_KV_HEADS
CMP_LEN = 32
CMP_STRIDE = 16
CMP_HIDDEN = 256
SLC_BLOCK = 64
SLC_TOPN = 16
NSA_WINDOW = 512

MOBA_HEADS = 8
MOBA_BLOCK = 256
MOBA_TOPK = 3

MLA_HEADS = 4
MLA_Q_RANK = 256
MLA_KV_RANK = 128
MLA_NOPE = 128
MLA_ROPE = 64
MLA_V = 128
ROPE_THETA = 10000.0

SWA_HEADS = 8
SWA_KV_HEADS = 2
SWA_GROUP = SWA_HEADS // SWA_KV_HEADS
SWA_WINDOW = 128

N_EXPERTS = 16
N_GROUPS = 4
EXPERTS_PER_GROUP = N_EXPERTS // N_GROUPS
D_EXPERT = 512

EVEN_WIDTHS = (NSA_HEADS * HEAD_DIM,) + (NSA_KV_HEADS * HEAD_DIM,) * 6 + (3 * NSA_HEADS,) + (MOBA_HEADS * HEAD_DIM,) * 3
ODD_WIDTHS = (MLA_Q_RANK, MLA_KV_RANK, MLA_ROPE, SWA_HEADS * HEAD_DIM, SWA_KV_HEADS * HEAD_DIM, SWA_KV_HEADS * HEAD_DIM)

LANES = 128
VMEM_LIMIT = 56 * 1024 * 1024


def _params(*sem):
    return pltpu.CompilerParams(dimension_semantics=sem, vmem_limit_bytes=VMEM_LIMIT)


def _dot(a, b, precision=None):
    return lax.dot_general(a, b, (((1,), (0,)), ((), ())), precision=precision, preferred_element_type=F32)


def _dot_nt(a, b, precision=None):
    return lax.dot_general(a, b, (((1,), (1,)), ((), ())), precision=precision, preferred_element_type=F32)


def _iota(shape, dim):
    return lax.broadcasted_iota(jnp.int32, shape, dim)


def _offsets(widths):
    out, acc = [], 0
    for w in widths:
        out.append((acc, acc + w))
        acc += w
    return out


def _bucket_thresholds():
    d = np.arange(0, 4 * MAX_DISTANCE, dtype=np.int64)
    exact = N_BUCKETS // 2
    x = np.maximum(d, 1).astype(np.float32) / np.float32(exact)
    logp = exact + (np.log(x) / np.float32(math.log(MAX_DISTANCE / exact)) * np.float32(N_BUCKETS - exact)).astype(np.int32)
    bucket = np.where(d < exact, d, np.minimum(logp, N_BUCKETS - 1))
    return [int(np.argmax(bucket >= b)) for b in range(1, N_BUCKETS)]


BUCKET_THR = _bucket_thresholds()
FAR_DIST = BUCKET_THR[-1]


def _ada_kernel(c_ref, w_ref, b_ref, o_ref):
    c = c_ref[...]
    o_ref[...] = _dot(c * jax.nn.sigmoid(c), w_ref[...], precision=HI) + b_ref[...]


def ada_all(c, ada_w, ada_b):
    depth, d, n = ada_w.shape
    rows = 8
    cp = jnp.pad(c, ((0, rows - c.shape[0]), (0, 0)))
    tn = 1536
    out = pl.pallas_call(
        _ada_kernel,
        grid=(depth, n // tn),
        in_specs=[pl.BlockSpec((rows, d), lambda l, j: (0, 0)),
                  pl.BlockSpec((None, d, tn), lambda l, j: (l, 0, j)),
                  pl.BlockSpec((None, 1, tn), lambda l, j: (l, 0, j))],
        out_specs=pl.BlockSpec((None, rows, tn), lambda l, j: (l, 0, j)),
        out_shape=jax.ShapeDtypeStruct((depth, rows, n), F32),
        compiler_params=_params("parallel", "parallel"),
        name="ada",
    )(cp, ada_w, ada_b.reshape(depth, 1, n))
    return out[:, :c.shape[0], :]


def _norm_mod(x, g, sc, sh):
    y = x * lax.rsqrt(jnp.mean(x * x, axis=-1, keepdims=True) + NORM_EPS) * g
    return y * (1.0 + sc) + sh


def _nmm_kernel(x_ref, g_ref, sc_ref, sh_ref, w_ref, o_ref):
    h = _norm_mod(x_ref[...], g_ref[...], sc_ref[...], sh_ref[...])
    o_ref[...] = _dot(h.astype(BF16), w_ref[...]).astype(o_ref.dtype)


def norm_mod_matmul(x, g, sc, sh, w, tm=512):
    b, s, d = x.shape
    n = w.shape[1]
    return pl.pallas_call(
        _nmm_kernel,
        grid=(b, s // tm),
        in_specs=[pl.BlockSpec((None, tm, d), lambda bi, i: (bi, i, 0)),
                  pl.BlockSpec((1, d), lambda bi, i: (0, 0)),
                  pl.BlockSpec((None, 1, d), lambda bi, i: (bi, 0, 0)),
                  pl.BlockSpec((None, 1, d), lambda bi, i: (bi, 0, 0)),
                  pl.BlockSpec((d, n), lambda bi, i: (0, 0))],
        out_specs=pl.BlockSpec((None, tm, n), lambda bi, i: (bi, i, 0)),
        out_shape=jax.ShapeDtypeStruct((b, s, n), F32),
        compiler_params=_params("parallel", "parallel"),
        name="norm_mod_matmul",
    )(x, g.reshape(1, d), sc.reshape(b, 1, d), sh.reshape(b, 1, d), w)


def _bias_kernel(tab_ref, o_ref, *, rows, cols, step, cstride, c0, key_major=False):
    sub = 8
    off = pl.program_id(0) * step - c0
    if key_major:
        base = _iota((sub, cols), 1) - _iota((sub, cols), 0)
    else:
        base = _iota((sub, cols), 0) - cstride * _iota((sub, cols), 1)

    def body(r, carry):
        dist = off + base + (-r * sub if key_major else r * sub)
        vs = [jnp.full((sub, cols), tab_ref[0, h], F32) for h in range(N_BIAS_HEADS)]
        for b in range(1, N_BUCKETS):
            ge = dist >= BUCKET_THR[b - 1]
            for h in range(N_BIAS_HEADS):
                vs[h] = jnp.where(ge, tab_ref[b, h], vs[h])
        for h in range(N_BIAS_HEADS):
            v = jnp.where(dist < 0, NEG, vs[h]) if key_major else vs[h]
            o_ref[h, pl.ds(pl.multiple_of(r * sub, sub), sub), :] = v
        return carry

    lax.fori_loop(0, rows // sub, body, 0)


def bias_bank(rel_table, t, key_major=False):
    nd = -(-(FAR_DIST + t - 1) // t)
    kern = functools.partial(_bias_kernel, rows=t, cols=t, step=t, cstride=1, c0=0, key_major=key_major)
    bank = pl.pallas_call(
        kern,
        grid=(nd + 1,),
        in_specs=[pl.BlockSpec(memory_space=pltpu.SMEM)],
        out_specs=pl.BlockSpec((N_BIAS_HEADS, None, t, t), lambda i: (0, i, 0, 0)),
        out_shape=jax.ShapeDtypeStruct((N_BIAS_HEADS, nd + 1, t, t), F32),
        compiler_params=_params("parallel"),
        name="bias_bank",
    )(rel_table)
    return bank, nd


def bias_cmp(rel_table, s, nc, tq=256):
    kern = functools.partial(_bias_kernel, rows=tq, cols=nc, step=tq, cstride=CMP_STRIDE, c0=CMP_LEN - 1)
    return pl.pallas_call(
        kern,
        grid=(s // tq,),
        in_specs=[pl.BlockSpec(memory_space=pltpu.SMEM)],
        out_specs=pl.BlockSpec((N_BIAS_HEADS, tq, nc), lambda i: (0, i, 0)),
        out_shape=jax.ShapeDtypeStruct((N_BIAS_HEADS, s, nc), F32),
        compiler_params=_params("parallel"),
        name="bias_cmp",
    )(rel_table)


def _compress_kernel(x_ref, w1_ref, pos_ref, w2_ref, o_ref):
    x = x_ref[...]
    w1 = w1_ref[...]
    half = w1.shape[0]
    hid = w1.shape[1] // 2
    r = _dot(x, w1, precision=HI)
    pos = pos_ref[...]
    pb = _dot(pos[:, :half], w1[:, :hid], precision=HI) + _dot(pos[:, half:], w1[:, hid:], precision=HI)
    nxt = pltpu.roll(r[:, hid:], x.shape[0] - 1, 0)
    pre = r[:, :hid] + nxt + pb[0:1, :]
    o_ref[...] = _dot(jax.nn.gelu(pre), w2_ref[...], precision=HI)


def nsa_compress(kv, w1, pos, w2):
    two, bh, s, dh = kv.shape
    nc = s // CMP_STRIDE
    half = CMP_STRIDE * dh
    hid = w1.shape[2]
    x = kv.reshape(two, bh, nc, half)
    w1cat = jnp.concatenate([w1[:, :half, :], w1[:, half:, :]], axis=2)
    posf = jnp.broadcast_to(pos.reshape(two, 1, CMP_LEN * dh), (two, 8, CMP_LEN * dh))
    return pl.pallas_call(
        _compress_kernel,
        grid=(two, bh),
        in_specs=[pl.BlockSpec((None, None, nc, half), lambda w, i: (w, i, 0, 0)),
                  pl.BlockSpec((None, half, 2 * hid), lambda w, i: (w, 0, 0)),
                  pl.BlockSpec((None, 8, CMP_LEN * dh), lambda w, i: (w, 0, 0)),
                  pl.BlockSpec((None, hid, dh), lambda w, i: (w, 0, 0))],
        out_specs=pl.BlockSpec((None, None, nc, dh), lambda w, i: (w, i, 0, 0)),
        out_shape=jax.ShapeDtypeStruct((two, bh, nc, dh), F32),
        compiler_params=_params("parallel", "parallel"),
        name="nsa_compress",
    )(x, w1cat, posf, w2)


def _topn_mask(score, index, n):
    idx_f = index.astype(F32)
    sel = jnp.zeros(score.shape, jnp.bool_)
    for _ in range(n):
        m = jnp.max(score, axis=0, keepdims=True)
        first = jnp.min(jnp.where(score == m, idx_f, float(score.shape[0])), axis=0, keepdims=True)
        pick = idx_f == first
        sel = sel | pick
        score = jnp.where(pick, -jnp.inf, score)
    return sel


def _cmp_kernel(q_ref, k_ref, v_ref, b_ref, ov_ref, oc_ref, sel_ref, *, tq, nc, nslc, ntop, group):
    qi = pl.program_id(2)
    t = qi * tq + _iota((tq, 1), 0)
    n = _iota((1, nc), 1)
    mask = (n * CMP_STRIDE + (CMP_LEN - 1) <= t) & (n < nc - 1)
    k = k_ref[...]
    v = v_ref[...].astype(BF16)
    psum = jnp.zeros((tq, nc), F32)
    for g in range(group):
        s = _dot_nt(q_ref[g], k, precision=HI) + b_ref[g]
        s = jnp.where(mask, s, NEG)
        m = jnp.max(s, axis=-1, keepdims=True)
        p = jnp.where(mask, jnp.exp(s - m), 0.0)
        den = jnp.sum(p, axis=-1, keepdims=True)
        p = p * jnp.where(den > 0.0, 1.0 / den, 0.0)
        oc_ref[g] = _dot(p.astype(BF16), v)
        psum = psum + p
    imp = _dot_nt(ov_ref[...], psum, precision=HI)
    j = _iota((nslc, tq), 0)
    blk = jnp.right_shift(qi * tq + _iota((1, tq), 1), int(math.log2(SLC_BLOCK)))
    forced = (j == 0) | (j == blk) | (j == blk - 1)
    valid = j <= blk
    score = jnp.where(forced, BIG, jnp.where(valid, imp, NEG))
    sel = _topn_mask(score, j, ntop) & valid
    sel_ref[0] = jnp.where(sel, 0.0, NEG)


def nsa_cmp_attention(q, k_cmp, v_cmp, bias_c, tq=256):
    b, hk, g, s, dh = q.shape
    nc = k_cmp.shape[2]
    nslc = s // SLC_BLOCK
    ntop = min(SLC_TOPN, nslc)
    cmp_start = np.arange(nc, dtype=np.int64) * CMP_STRIDE
    slc_lo = np.arange(nslc, dtype=np.int64) * SLC_BLOCK
    overlap = ((cmp_start[:, None] <= slc_lo[None, :] + SLC_BLOCK - 1)
               & (cmp_start[:, None] + CMP_LEN - 1 >= slc_lo[None, :])).astype(np.float32)
    kern = functools.partial(_cmp_kernel, tq=tq, nc=nc, nslc=nslc, ntop=ntop, group=g)
    return pl.pallas_call(
        kern,
        grid=(b, hk, s // tq),
        in_specs=[pl.BlockSpec((None, None, g, tq, dh), lambda bi, h, i: (bi, h, 0, i, 0)),
                  pl.BlockSpec((None, None, nc, dh), lambda bi, h, i: (bi, h, 0, 0)),
                  pl.BlockSpec((None, None, nc, dh), lambda bi, h, i: (bi, h, 0, 0)),
                  pl.BlockSpec((g, tq, nc), lambda bi, h, i: (h, i, 0)),
                  pl.BlockSpec((nslc, nc), lambda bi, h, i: (0, 0))],
        out_specs=[pl.BlockSpec((None, None, g, tq, dh), lambda bi, h, i: (bi, h, 0, i, 0)),
                   pl.BlockSpec((None, None, 1, nslc, tq), lambda bi, h, i: (bi, h, 0, 0, i))],
        out_shape=[jax.ShapeDtypeStruct((b, hk, g, s, dh), F32),
                   jax.ShapeDtypeStruct((b, hk, 1, nslc, s), F32)],
        compiler_params=_params("parallel", "parallel", "parallel"),
        name="nsa_cmp_attention",
    )(q, k_cmp, v_cmp, bias_c, jnp.asarray(overlap.T))


def _kmean_kernel(k_ref, o_ref, *, nb, blk):
    k = k_ref[...]
    o_ref[...] = jnp.sum(k.reshape(nb, blk, k.shape[-1]), axis=1) * (1.0 / blk)


def _moba_sel_kernel(q_ref, km_ref, sel_ref, *, tq, nb, ntop):
    qi = pl.program_id(2)
    own = jnp.right_shift(qi * tq + _iota((1, tq), 1), int(math.log2(MOBA_BLOCK)))
    j = _iota((nb, tq), 0)
    gate = _dot_nt(km_ref[...], q_ref[...], precision=HI)
    past = j < own
    sel = _topn_mask(jnp.where(past, gate, NEG), j, ntop) & past
    sel_ref[...] = jnp.where(sel | (j == own), 0.0, NEG)


def moba_select(q, k, tq=256):
    b, h, s, dh = q.shape
    nb = s // MOBA_BLOCK
    ntop = min(MOBA_TOPK, nb)
    kmean = pl.pallas_call(
        functools.partial(_kmean_kernel, nb=nb, blk=MOBA_BLOCK),
        grid=(b, h),
        in_specs=[pl.BlockSpec((None, None, s, dh), lambda bi, hi: (bi, hi, 0, 0))],
        out_specs=pl.BlockSpec((None, None, nb, dh), lambda bi, hi: (bi, hi, 0, 0)),
        out_shape=jax.ShapeDtypeStruct((b, h, nb, dh), F32),
        compiler_params=_params("parallel", "parallel"),
        name="moba_kmean",
    )(k)
    return pl.pallas_call(
        functools.partial(_moba_sel_kernel, tq=tq, nb=nb, ntop=ntop),
        grid=(b, h, s // tq),
        in_specs=[pl.BlockSpec((None, None, tq, dh), lambda bi, hi, i: (bi, hi, i, 0)),
                  pl.BlockSpec((None, None, nb, dh), lambda bi, hi, i: (bi, hi, 0, 0))],
        out_specs=pl.BlockSpec((None, None, nb, tq), lambda bi, hi, i: (bi, hi, 0, i)),
        out_shape=jax.ShapeDtypeStruct((b, h, nb, s), F32),
        compiler_params=_params("parallel", "parallel", "parallel"),
        name="moba_select",
    )(q, kmean)


def _flash_kernel(qt_ref, kt_ref, *refs, hb, kb, sb, bh, t, tq, kc, nparts, use_sel, blk):
    pos = 0
    q_refs = refs[pos:pos + nparts]; pos += nparts
    k_refs = refs[pos:pos + nparts]; pos += nparts
    vt_ref = refs[pos]; pos += 1
    sel_ref = None
    if use_sel:
        sel_ref = refs[pos]; pos += 1
    bias_ref = refs[pos]; pos += 1
    o_ref, m_ref, l_ref, acc_ref = refs[pos:pos + 4]

    step = pl.program_id(2)
    qi = qt_ref[step]
    ki = kt_ref[step]

    @pl.when(ki == 0)
    def _():
        m_ref[...] = jnp.full(m_ref.shape, M_INIT, F32)
        l_ref[...] = jnp.zeros(l_ref.shape, F32)
        acc_ref[...] = jnp.zeros(acc_ref.shape, F32)

    piece = min(kc, blk)
    nchunk = t // kc

    def scores(g, qc):
        kg = g if kb > 1 else 0
        sg = g if sb > 1 else 0
        bg = g if bh > 1 else 0
        ql = slice(qc * tq, (qc + 1) * tq)
        qs = [q_ref[g, ql, :] for q_ref in q_refs]
        chunks, top = [], None
        for c in range(nchunk):
            kr = slice(c * kc, (c + 1) * kc)
            s = _dot_nt(k_refs[0][kg, kr, :], qs[0])
            for p in range(1, nparts):
                s = s + _dot_nt(k_refs[p][kg, kr, :], qs[p])
            s = s + bias_ref[bg, kr, ql]
            if use_sel:
                rows = [sel_ref[sg, pl.ds(ki * (t // blk) + (c * kc + r * piece) // blk, 1), ql]
                        for r in range(kc // piece)]
                add = rows[0] if len(rows) == 1 else jnp.concatenate(
                    [jnp.broadcast_to(row, (piece, tq)) for row in rows], axis=0)
                s = s + add
            chunks.append(s)
            top = s if top is None else jnp.maximum(top, s)
        return chunks, top

    def update(g, qc, chunks, top):
        kg = g if kb > 1 else 0
        ql = slice(qc * tq, (qc + 1) * tq)
        m = m_ref[g, :, ql]
        m_new = jnp.maximum(m, jnp.max(top, axis=0, keepdims=True))
        alpha = jnp.exp(m - m_new)
        psum = pv = None
        for c in range(nchunk):
            p = jnp.exp(chunks[c] - m_new)
            pd = _dot(vt_ref[kg, :, c * kc:(c + 1) * kc], p.astype(BF16))
            psum = p if psum is None else psum + p
            pv = pd if pv is None else pv + pd
        m_ref[g, :, ql] = m_new
        l_ref[g, :, ql] = alpha * l_ref[g, :, ql] + jnp.sum(psum, axis=0, keepdims=True)
        acc_ref[g, :, ql] = alpha * acc_ref[g, :, ql] + pv

    items = [(g, qc) for g in range(hb) for qc in range(t // tq)]
    pending = scores(*items[0])
    for i, item in enumerate(items):
        nxt = scores(*items[i + 1]) if i + 1 < len(items) else None
        update(*item, *pending)
        pending = nxt

    @pl.when(ki == qi)
    def _():
        for g in range(hb):
            o_ref[g] = jnp.transpose(acc_ref[g] / l_ref[g]).astype(o_ref.dtype)


def flash_attention(q_parts, k_parts, v, bias, nd, sel=None, t=512, blk=1, tq=256, kc=128):
    b, ng, hb, s, _ = q_parts[0].shape
    kb = v.shape[2]
    dv = v.shape[-1]
    t = min(t, s)
    tq, kc = min(tq, t), min(kc, t)
    nq = s // t
    pairs = [(i, j) for i in range(nq) for j in range(i + 1)]
    qt = jnp.asarray(np.array([p[0] for p in pairs], np.int32))
    kt = jnp.asarray(np.array([p[1] for p in pairs], np.int32))
    nparts = len(q_parts)
    use_sel = sel is not None
    sb = sel.shape[2] if use_sel else 1
    bh = hb if bias.shape[0] > 1 else 1
    vt = jnp.swapaxes(v, -1, -2)

    in_specs, args = [], []
    for qp in q_parts:
        in_specs.append(pl.BlockSpec((None, None, hb, t, qp.shape[-1]), lambda bi, n, st, qt, kt: (bi, n, 0, qt[st], 0)))
        args.append(qp)
    for kp in k_parts:
        in_specs.append(pl.BlockSpec((None, None, kb, t, kp.shape[-1]), lambda bi, n, st, qt, kt: (bi, n, 0, kt[st], 0)))
        args.append(kp)
    in_specs.append(pl.BlockSpec((None, None, kb, dv, t), lambda bi, n, st, qt, kt: (bi, n, 0, 0, kt[st])))
    args.append(vt)
    if use_sel:
        in_specs.append(pl.BlockSpec((None, None, sb, sel.shape[-2], t), lambda bi, n, st, qt, kt: (bi, n, 0, 0, qt[st])))
        args.append(sel)
    if bh > 1:
        in_specs.append(pl.BlockSpec((bh, None, t, t), lambda bi, n, st, qt, kt: (n, jnp.minimum(qt[st] - kt[st], nd), 0, 0)))
    else:
        in_specs.append(pl.BlockSpec((1, None, t, t), lambda bi, n, st, qt, kt: (0, jnp.minimum(qt[st] - kt[st], nd), 0, 0)))
    args.append(bias)
    kern = functools.partial(_flash_kernel, hb=hb, kb=kb, sb=sb, bh=bh, t=t, tq=tq, kc=kc, nparts=nparts,
                             use_sel=use_sel, blk=blk)
    grid_spec = pltpu.PrefetchScalarGridSpec(
        num_scalar_prefetch=2,
        grid=(b, ng, len(pairs)),
        in_specs=in_specs,
        out_specs=pl.BlockSpec((None, None, hb, t, dv), lambda bi, n, st, qt, kt: (bi, n, 0, qt[st], 0)),
        scratch_shapes=[pltpu.VMEM((hb, 1, t), F32), pltpu.VMEM((hb, 1, t), F32), pltpu.VMEM((hb, dv, t), F32)])
    return pl.pallas_call(
        kern,
        grid_spec=grid_spec,
        out_shape=jax.ShapeDtypeStruct((b, ng, hb, s, dv), F32),
        compiler_params=_params("parallel", "parallel", "arbitrary"),
        name="flash_attention",
    )(qt, kt, *args)


def _window_kernel(*refs, group, t, ntile, window, has_sink):
    q_ref = refs[0]
    k_refs = refs[1:1 + ntile]
    v_refs = refs[1 + ntile:1 + 2 * ntile]
    bias_ref = refs[1 + 2 * ntile]
    sink_ref = refs[2 + 2 * ntile] if has_sink else None
    o_ref = refs[-1]
    kvh = pl.program_id(1)
    diff = _iota((t, t), 0) - _iota((t, t), 1)
    masks = []
    for r in range(ntile):
        dist = (ntile - 1 - r) * t + diff
        masks.append((dist >= 0) & (dist < window))
    for g in range(group):
        q = q_ref[g]
        ss = []
        m = None
        for r in range(ntile):
            s = _dot_nt(q, k_refs[r][...]) + bias_ref[g, ntile - 1 - r]
            s = jnp.where(masks[r], s, NEG)
            ss.append(s)
            mr = jnp.max(s, axis=-1, keepdims=True)
            m = mr if m is None else jnp.maximum(m, mr)
        if has_sink:
            sink = sink_ref[kvh * group + g]
            m = jnp.maximum(m, sink)
        den = jnp.exp(sink - m) if has_sink else 0.0
        acc = None
        for r in range(ntile):
            p = jnp.exp(ss[r] - m)
            den = den + jnp.sum(p, axis=-1, keepdims=True)
            pv = _dot(p.astype(BF16), v_refs[r][...])
            acc = pv if acc is None else acc + pv
        o_ref[g] = acc / den


def window_attention(q, k, v, bank, t, window, sinks=None):
    b, hk, g, s, dh = q.shape
    ntile = window // t + 1
    pad = ((0, 0), (0, 0), (window, 0), (0, 0))
    kp, vp = jnp.pad(k, pad), jnp.pad(v, pad)
    has_sink = sinks is not None
    in_specs = [pl.BlockSpec((None, None, g, t, dh), lambda bi, h, i: (bi, h, 0, i, 0))]
    args = [q]
    for arr in (kp, vp):
        for r in range(ntile):
            in_specs.append(pl.BlockSpec((None, None, t, dh), lambda bi, h, i, r=r: (bi, h, i + r, 0)))
            args.append(arr)
    in_specs.append(pl.BlockSpec((g, ntile, t, t), lambda bi, h, i: (h, 0, 0, 0)))
    args.append(bank)
    if has_sink:
        in_specs.append(pl.BlockSpec(memory_space=pltpu.SMEM))
        args.append(sinks)
    kern = functools.partial(_window_kernel, group=g, t=t, ntile=ntile, window=window, has_sink=has_sink)
    return pl.pallas_call(
        kern,
        grid=(b, hk, s // t),
        in_specs=in_specs,
        out_specs=pl.BlockSpec((None, None, g, t, dh), lambda bi, h, i: (bi, h, 0, i, 0)),
        out_shape=jax.ShapeDtypeStruct((b, hk, g, s, dh), F32),
        compiler_params=_params("parallel", "parallel", "parallel"),
        name="window_attention",
    )(*args)


def _rms(x, g):
    return x * lax.rsqrt(jnp.mean(x * x, axis=-1, keepdims=True) + NORM_EPS) * g


def _mla_up_kernel(cq_ref, ckv_ref, kr_ref, krs_ref, qn_ref, kvn_ref, wq_ref, wkv_ref, cs_ref, sn_ref,
                   qnope_ref, qrope_ref, knope_ref, v_ref, krope_ref, *, scale):
    cs = cs_ref[...]
    sn = sn_ref[...]
    q = _dot(_rms(cq_ref[...], qn_ref[...]).astype(BF16), wq_ref[...]) * scale
    kv = _dot(_rms(ckv_ref[...], kvn_ref[...]).astype(BF16), wkv_ref[...])
    wq_head = MLA_NOPE + 2 * MLA_ROPE
    for h in range(MLA_HEADS):
        base = h * wq_head
        qnope_ref[h] = q[:, base:base + MLA_NOPE].astype(qnope_ref.dtype)
        x = q[:, base + MLA_NOPE:base + MLA_NOPE + MLA_ROPE]
        xs = q[:, base + MLA_NOPE + MLA_ROPE:base + wq_head]
        qrope_ref[h] = (x * cs + xs * sn).astype(qrope_ref.dtype)
        kb = h * (MLA_NOPE + MLA_V)
        knope_ref[h] = kv[:, kb:kb + MLA_NOPE].astype(knope_ref.dtype)
        v_ref[h] = kv[:, kb + MLA_NOPE:kb + MLA_NOPE + MLA_V].astype(v_ref.dtype)
    kr = (kr_ref[...] * cs + krs_ref[...] * sn).astype(krope_ref.dtype)
    for h in range(MLA_HEADS):
        krope_ref[h] = kr


def mla_up(c_q, c_kv, k_rope, k_rope_sw, q_norm, kv_norm, w_q_up, w_kv_up, tm=512):
    b, s, _ = c_q.shape
    half = MLA_ROPE // 2
    inv = ROPE_THETA ** (-np.arange(0, MLA_ROPE, 2, dtype=np.float32) / np.float32(MLA_ROPE))
    ang = np.arange(s, dtype=np.float32)[:, None] * inv[None, :].astype(np.float32)
    cos, sin = np.cos(ang).astype(np.float32), np.sin(ang).astype(np.float32)
    cs = jnp.asarray(np.concatenate([cos, cos], axis=1))
    sn = jnp.asarray(np.concatenate([-sin, sin], axis=1))
    dq = MLA_NOPE + MLA_ROPE
    wq = w_q_up.reshape(MLA_Q_RANK, MLA_HEADS, dq)
    rope_cols = wq[:, :, MLA_NOPE:]
    swapped = jnp.concatenate([rope_cols[:, :, half:], rope_cols[:, :, :half]], axis=2)
    wq_aug = jnp.concatenate([wq, swapped], axis=2).reshape(MLA_Q_RANK, MLA_HEADS * (dq + MLA_ROPE)).astype(BF16)
    wkv = w_kv_up.astype(BF16)
    scale = (MLA_NOPE + MLA_ROPE) ** -0.5
    tm = min(tm, s)
    row = lambda w: pl.BlockSpec((None, tm, w), lambda bi, i: (bi, i, 0))
    full = lambda a: pl.BlockSpec(a.shape, lambda bi, i: (0,) * a.ndim)
    head = lambda w: pl.BlockSpec((None, MLA_HEADS, tm, w), lambda bi, i: (bi, 0, i, 0))
    qn2, kvn2 = q_norm.reshape(1, -1), kv_norm.reshape(1, -1)
    outs = pl.pallas_call(
        functools.partial(_mla_up_kernel, scale=scale),
        grid=(b, s // tm),
        in_specs=[row(MLA_Q_RANK), row(MLA_KV_RANK), row(MLA_ROPE), row(MLA_ROPE), full(qn2), full(kvn2),
                  full(wq_aug), full(wkv),
                  pl.BlockSpec((tm, MLA_ROPE), lambda bi, i: (i, 0)), pl.BlockSpec((tm, MLA_ROPE), lambda bi, i: (i, 0))],
        out_specs=[head(MLA_NOPE), head(MLA_ROPE), head(MLA_NOPE), head(MLA_V), head(MLA_ROPE)],
        out_shape=[jax.ShapeDtypeStruct((b, MLA_HEADS, s, w), BF16)
                   for w in (MLA_NOPE, MLA_ROPE, MLA_NOPE, MLA_V, MLA_ROPE)],
        compiler_params=_params("parallel", "parallel"),
        name="mla_up",
    )(c_q, c_kv, k_rope, k_rope_sw, qn2, kvn2, wq_aug, wkv, cs, sn)
    return outs


def _out_kernel(*refs, gated):
    if gated:
        oc_ref, os_ref, ow_ref, gt_ref, ex_ref, ob_ref, x_ref, gm_ref, w_ref, o_ref = refs
        half = oc_ref.shape[-1]
        ge = _dot(jax.nn.sigmoid(gt_ref[...]), ex_ref[...], precision=HI)
        oa = ge[:, :half] * oc_ref[...] + ge[:, half:2 * half] * os_ref[...] + ge[:, 2 * half:] * ow_ref[...]
    else:
        oa_ref, ob_ref, x_ref, gm_ref, w_ref, o_ref = refs
        half = oa_ref.shape[-1]
        oa = oa_ref[...]
    mix = _dot(oa.astype(BF16), w_ref[:half, :]) + _dot(ob_ref[...].astype(BF16), w_ref[half:, :])
    o_ref[...] = x_ref[...] + gm_ref[...] * mix


def out_project(parts, ob, x, gate_m, w_out, gates=None, tm=512):
    b, s, d = x.shape
    gated = gates is not None
    tm = min(tm, s)
    row = lambda a: pl.BlockSpec((None, tm, a.shape[-1]), lambda bi, i: (bi, i, 0))
    full = lambda a: pl.BlockSpec(a.shape, lambda bi, i: (0,) * a.ndim)
    args, in_specs = [], []
    for p in parts:
        args.append(p); in_specs.append(row(p))
    if gated:
        half = parts[0].shape[-1]
        nh = half // HEAD_DIM
        ex = np.zeros((LANES, 3 * half), np.float32)
        for h in range(nh):
            for br in range(3):
                ex[h * 3 + br, br * half + h * HEAD_DIM: br * half + (h + 1) * HEAD_DIM] = 1.0
        gpad = jnp.pad(gates, ((0, 0), (0, 0), (0, LANES - gates.shape[-1])))
        ex = jnp.asarray(ex)
        args += [gpad, ex]; in_specs += [row(gpad), full(ex)]
    gm = gate_m.reshape(b, 1, d)
    wb = w_out.astype(BF16)
    args += [ob, x, gm, wb]
    in_specs += [row(ob), row(x), pl.BlockSpec((None, 1, d), lambda bi, i: (bi, 0, 0)), full(wb)]
    return pl.pallas_call(
        functools.partial(_out_kernel, gated=gated),
        grid=(b, s // tm),
        in_specs=in_specs,
        out_specs=pl.BlockSpec((None, tm, d), lambda bi, i: (bi, i, 0)),
        out_shape=jax.ShapeDtypeStruct((b, s, d), F32),
        compiler_params=_params("parallel", "parallel"),
        name="out_project",
    )(*args)


def _ffn_pre_kernel(x_ref, g_ref, sc_ref, sh_ref, rw_ref, rb_ref, h_ref, cw_ref):
    h = _norm_mod(x_ref[...], g_ref[...], sc_ref[...], sh_ref[...])
    h_ref[...] = h.astype(h_ref.dtype)
    aff = jax.nn.sigmoid(_dot_nt(rw_ref[...], h, precision=HI))
    biased = aff + rb_ref[...]
    epg = EXPERTS_PER_GROUP
    brow = [biased[e:e + 1, :] for e in range(N_EXPERTS)]
    arow = [aff[e:e + 1, :] for e in range(N_EXPERTS)]
    best = gsel = None
    for gi in range(N_GROUPS):
        a, b_, c, d_ = brow[gi * epg:(gi + 1) * epg]
        hi1, lo1, hi2, lo2 = jnp.maximum(a, b_), jnp.minimum(a, b_), jnp.maximum(c, d_), jnp.minimum(c, d_)
        score = jnp.maximum(hi1, hi2) + jnp.maximum(jnp.minimum(hi1, hi2), jnp.maximum(lo1, lo2))
        if gi == 0:
            best, gsel = score, jnp.zeros(score.shape, jnp.int32)
        else:
            better = score > best
            gsel = jnp.where(better, gi, gsel)
            best = jnp.where(better, score, best)

    def in_group(rows, j):
        v = rows[j]
        for gi in range(1, N_GROUPS):
            v = jnp.where(gsel == gi, rows[gi * epg + j], v)
        return v

    bv = [in_group(brow, j) for j in range(epg)]
    av = [in_group(arow, j) for j in range(epg)]

    def argmax_excluding(skip):
        val = idx = None
        for j in range(epg):
            cand = bv[j] if skip is None else jnp.where(skip == j, -jnp.inf, bv[j])
            if j == 0:
                val, idx = cand, jnp.zeros(cand.shape, jnp.int32)
            else:
                better = cand > val
                idx = jnp.where(better, j, idx)
                val = jnp.where(better, cand, val)
        return idx

    first = argmax_excluding(None)
    second = argmax_excluding(first)

    def pick(rows, idx):
        v = rows[0]
        for j in range(1, epg):
            v = jnp.where(idx == j, rows[j], v)
        return v

    a1, a2 = pick(av, first), pick(av, second)
    tot = a1 + a2
    e1, e2 = gsel * epg + first, gsel * epg + second
    eid = _iota(aff.shape, 0)
    cw_ref[...] = jnp.where(eid == e1, a1 / tot, 0.0) + jnp.where(eid == e2, a2 / tot, 0.0)


def ffn_pre(x, g, sc, sh, router_w, router_b, tm=512):
    b, s, d = x.shape
    e = router_w.shape[1]
    tm = min(tm, s)
    return pl.pallas_call(
        _ffn_pre_kernel,
        grid=(b, s // tm),
        in_specs=[pl.BlockSpec((None, tm, d), lambda bi, i: (bi, i, 0)),
                  pl.BlockSpec((1, d), lambda bi, i: (0, 0)),
                  pl.BlockSpec((None, 1, d), lambda bi, i: (bi, 0, 0)),
                  pl.BlockSpec((None, 1, d), lambda bi, i: (bi, 0, 0)),
                  pl.BlockSpec((e, d), lambda bi, i: (0, 0)),
                  pl.BlockSpec((e, 1), lambda bi, i: (0, 0))],
        out_specs=[pl.BlockSpec((None, tm, d), lambda bi, i: (bi, i, 0)),
                   pl.BlockSpec((None, e, tm), lambda bi, i: (bi, 0, i))],
        out_shape=[jax.ShapeDtypeStruct((b, s, d), BF16), jax.ShapeDtypeStruct((b, e, s), F32)],
        compiler_params=_params("parallel", "parallel"),
        name="ffn_pre",
    )(x, g.reshape(1, d), sc.reshape(b, 1, d), sh.reshape(b, 1, d), router_w.T, router_b.reshape(e, 1))


def _moe_kernel(h_ref, cw_ref, x_ref, gf_ref, wg_ref, wu_ref, wd_ref, o_ref, acc_ref):
    e = pl.program_id(2)

    @pl.when(e == 0)
    def _():
        acc_ref[...] = jnp.zeros(acc_ref.shape, F32)

    h = h_ref[...]
    a = _dot(h, wg_ref[...])
    u = _dot(h, wu_ref[...])
    cw = cw_ref[...]
    c = jnp.sum(jnp.where(_iota(cw.shape, 1) == e, cw, 0.0), axis=-1, keepdims=True)
    hid = (a * jax.nn.sigmoid(a)) * u * c
    acc_ref[...] += _dot(hid.astype(BF16), wd_ref[...])

    @pl.when(e == pl.num_programs(2) - 1)
    def _():
        o_ref[...] = x_ref[...] + gf_ref[...] * acc_ref[...]


def moe_dense(h, cw, x, gate_f, w_gate, w_up, w_down, tm=512):
    b, s, d = x.shape
    ne, _, f = w_gate.shape
    tm = min(tm, s)
    return pl.pallas_call(
        _moe_kernel,
        grid=(b, s // tm, ne),
        in_specs=[pl.BlockSpec((None, tm, d), lambda bi, i, e: (bi, i, 0)),
                  pl.BlockSpec((None, tm, ne), lambda bi, i, e: (bi, i, 0)),
                  pl.BlockSpec((None, tm, d), lambda bi, i, e: (bi, i, 0)),
                  pl.BlockSpec((None, 1, d), lambda bi, i, e: (bi, 0, 0)),
                  pl.BlockSpec((None, d, f), lambda bi, i, e: (e, 0, 0)),
                  pl.BlockSpec((None, d, f), lambda bi, i, e: (e, 0, 0)),
                  pl.BlockSpec((None, f, d), lambda bi, i, e: (e, 0, 0))],
        out_specs=pl.BlockSpec((None, tm, d), lambda bi, i, e: (bi, i, 0)),
        out_shape=jax.ShapeDtypeStruct((b, s, d), F32),
        scratch_shapes=[pltpu.VMEM((tm, d), F32)],
        compiler_params=_params("parallel", "parallel", "arbitrary"),
        name="moe_dense",
    )(h, cw, x, gate_f.reshape(b, 1, d), w_gate, w_up, w_down)


def _final_norm_kernel(x_ref, g_ref, o_ref):
    o_ref[...] = _rms(x_ref[...], g_ref[...])


def final_rmsnorm(x, g, tm=512):
    b, s, d = x.shape
    tm = min(tm, s)
    return pl.pallas_call(
        _final_norm_kernel,
        grid=(b, s // tm),
        in_specs=[pl.BlockSpec((None, tm, d), lambda bi, i: (bi, i, 0)), pl.BlockSpec((1, d), lambda bi, i: (0, 0))],
        out_specs=pl.BlockSpec((None, tm, d), lambda bi, i: (bi, i, 0)),
        out_shape=jax.ShapeDtypeStruct((b, s, d), F32),
        compiler_params=_params("parallel", "parallel"),
        name="final_rmsnorm",
    )(x, g.reshape(1, d))


def _heads(x, nh):
    b, s, w = x.shape
    return x.reshape(b, s, nh, w // nh).transpose(0, 2, 1, 3)


def _merge(x):
    b, ng, hb, s, dh = x.shape
    return x.transpose(0, 3, 1, 2, 4).reshape(b, s, ng * hb * dh)


def even_mixer(proj, x, gate_m, w_out, pos_k, pos_v, ck_w1, ck_w2, cv_w1, cv_w2, rel_table, banks):
    b, s, _ = proj.shape
    off = _offsets(EVEN_WIDTHS)
    q_a, kc, vc, ks, vs, kw, vw, gates, q_b, k_b, v_b = (proj[..., lo:hi] for lo, hi in off)
    hk, g, dh = NSA_KV_HEADS, NSA_GROUP, HEAD_DIM
    scale = dh ** -0.5
    qa = (_heads(q_a, NSA_HEADS) * scale).reshape(b, hk, g, s, dh)
    qa16 = qa.astype(BF16)
    (bank_l, nd_l, t_l), (bank_w, _, t_w) = banks["dense"], banks["nsa_window"]

    nc = s // CMP_STRIDE
    kv = jnp.stack([_heads(kc, hk), _heads(vc, hk)]).reshape(2, b * hk, s, dh)
    cmp = nsa_compress(kv, jnp.stack([ck_w1, cv_w1]), jnp.stack([pos_k, pos_v]), jnp.stack([ck_w2, cv_w2]))
    cmp = cmp.reshape(2, b, hk, nc, dh)
    o_c, sel = nsa_cmp_attention(qa, cmp[0], cmp[1], bias_cmp(rel_table, s, nc, tq=min(256, s)), tq=min(256, s))
    ks5 = _heads(ks, hk).astype(BF16)[:, :, None]
    vs5 = _heads(vs, hk).astype(BF16)[:, :, None]
    o_s = flash_attention([qa16], [ks5], vs5, bank_l, nd_l, sel=sel, t=t_l, blk=SLC_BLOCK)
    o_w = window_attention(qa16, _heads(kw, hk).astype(BF16), _heads(vw, hk).astype(BF16), bank_w, t_w, NSA_WINDOW)

    hb = 4
    qb = _heads(q_b, MOBA_HEADS)
    kb = _heads(k_b, MOBA_HEADS)
    vb = _heads(v_b, MOBA_HEADS)
    selb = moba_select(qb, kb, tq=min(256, s))
    ngb = MOBA_HEADS // hb
    r5 = lambda a: a.reshape(b, ngb, hb, a.shape[-2], a.shape[-1])
    o_b = flash_attention([r5((qb * scale).astype(BF16))], [r5(kb.astype(BF16))], r5(vb.astype(BF16)),
                          bank_l, nd_l, sel=r5(selb), t=t_l, blk=MOBA_BLOCK)

    return out_project([_merge(o_c), _merge(o_s), _merge(o_w)], _merge(o_b), x, gate_m, w_out, gates=gates)


def odd_mixer(proj, x, gate_m, w_out, q_norm, kv_norm, w_q_up, w_kv_up, sinks, banks):
    b, s, _ = proj.shape
    off = _offsets(ODD_WIDTHS + (MLA_ROPE,))
    c_q, c_kv, k_rope, q_d, k_d, v_d, k_rope_sw = (proj[..., lo:hi] for lo, hi in off)
    qn, qr, kn, v, kr = mla_up(c_q, c_kv, k_rope, k_rope_sw, q_norm, kv_norm, w_q_up, w_kv_up)
    hb = 2
    ng = MLA_HEADS // hb
    r5 = lambda a: a.reshape(b, ng, hb, s, a.shape[-1])
    bank_c, nd_c, t_c = banks["causal"]
    o_c = flash_attention([r5(qn), r5(qr)], [r5(kn), r5(kr)], r5(v), bank_c, nd_c, t=t_c)

    hk, g, dh = SWA_KV_HEADS, SWA_GROUP, HEAD_DIM
    scale = dh ** -0.5
    qd = (_heads(q_d, SWA_HEADS) * scale).reshape(b, hk, g, s, dh).astype(BF16)
    bank_s, _, t_s = banks["swa"]
    o_d = window_attention(qd, _heads(k_d, hk).astype(BF16), _heads(v_d, hk).astype(BF16), bank_s, t_s, SWA_WINDOW,
                           sinks=sinks)
    return out_project([_merge(o_c)], _merge(o_d), x, gate_m, w_out)


def kernel(x, c, rel_table, router_w, router_b, final_norm, norm_mix, norm_ffn, ada_w, ada_b, moe_w_gate, moe_w_up, moe_w_down, ev_w_in, ev_w_out, nsa_pos_k, nsa_pos_v, nsa_ck_w1, nsa_ck_w2, nsa_cv_w1, nsa_cv_w2, od_w_in, od_w_out, mla_q_norm, mla_kv_norm, mla_w_q_up, mla_w_kv_up, swa_sinks):
    b, s, d = x.shape
    depth = ada_w.shape[0]
    mods = ada_all(c, ada_w, ada_b)
    t_dense = min(512, s)
    bank_l, nd_l = bias_bank(rel_table, t_dense, key_major=True)
    bank_c = bias_bank(jnp.zeros_like(rel_table), t_dense, key_major=True)[0][:1, :2]
    t_w = min(256, s)
    bank_w, nd_w = bias_bank(rel_table, t_w)
    bank_s, nd_s = bias_bank(rel_table, SWA_WINDOW)
    banks = {"dense": (bank_l, nd_l, t_dense), "causal": (bank_c, 1, t_dense), "nsa_window": (bank_w, nd_w, t_w),
             "swa": (bank_s, nd_s, SWA_WINDOW)}

    for layer in range(depth):
        shift_m, scale_m, gate_m, shift_f, scale_f, gate_f = jnp.split(mods[layer], 6, axis=-1)
        i = layer // 2
        if layer % 2 == 0:
            w_in = ev_w_in[i]
            pad = -w_in.shape[1] % LANES
            proj = norm_mod_matmul(x, norm_mix[layer], scale_m, shift_m, jnp.pad(w_in, ((0, 0), (0, pad))).astype(BF16))
            x = even_mixer(proj, x, gate_m, ev_w_out[i], nsa_pos_k[i], nsa_pos_v[i], nsa_ck_w1[i], nsa_ck_w2[i],
                           nsa_cv_w1[i], nsa_cv_w2[i], rel_table, banks)
        else:
            w_in = od_w_in[i]
            lo = MLA_Q_RANK + MLA_KV_RANK
            half = MLA_ROPE // 2
            sw = jnp.concatenate([w_in[:, lo + half:lo + MLA_ROPE], w_in[:, lo:lo + half]], axis=1)
            proj = norm_mod_matmul(x, norm_mix[layer], scale_m, shift_m, jnp.concatenate([w_in, sw], axis=1).astype(BF16))
            x = odd_mixer(proj, x, gate_m, od_w_out[i], mla_q_norm[i], mla_kv_norm[i], mla_w_q_up[i], mla_w_kv_up[i],
                          swa_sinks[i], banks)
        h, cw = ffn_pre(x, norm_ffn[layer], scale_f, shift_f, router_w, router_b)
        x = moe_dense(h, cw.transpose(0, 2, 1), x, gate_f, moe_w_gate[layer].astype(BF16), moe_w_up[layer].astype(BF16),
                      moe_w_down[layer].astype(BF16))
    return final_rmsnorm(x, final_norm)
```

```python
import functools
import math

import numpy as np
import jax
import jax.numpy as jnp
from jax import lax
from jax.experimental import pallas as pl
from jax.experimental.pallas import tpu as pltpu

F32 = jnp.float32
BF16 = jnp.bfloat16
HI = lax.Precision.HIGHEST

D_MODEL = 1024
HEAD_DIM = 64
NEG = -1e30
BIG = 1e30
M_INIT = -1e9
NORM_EPS = 1e-6
LOG2E = math.log2(math.e)

N_BUCKETS = 32
MAX_DISTANCE = 1024
N_BIAS_HEADS = 8

NSA_HEADS = 8
NSA_KV_HEADS = 2
NSA_GROUP = NSA_HEADS // NSA_KV_HEADS
CMP_LEN = 32
CMP_STRIDE = 16
CMP_HIDDEN = 256
SLC_BLOCK = 64
SLC_TOPN = 16
NSA_WINDOW = 512

MOBA_HEADS = 8
MOBA_BLOCK = 256
MOBA_TOPK = 3

MLA_HEADS = 4
MLA_Q_RANK = 256
MLA_KV_RANK = 128
MLA_NOPE = 128
MLA_ROPE = 64
MLA_V = 128
ROPE_THETA = 10000.0

SWA_HEADS = 8
SWA_KV_HEADS = 2
SWA_GROUP = SWA_HEADS // SWA_KV_HEADS
SWA_WINDOW = 128

N_EXPERTS = 16
N_GROUPS = 4
EXPERTS_PER_GROUP = N_EXPERTS // N_GROUPS
D_EXPERT = 512

EVEN_WIDTHS = (NSA_HEADS * HEAD_DIM,) + (NSA_KV_HEADS * HEAD_DIM,) * 6 + (3 * NSA_HEADS,) + (MOBA_HEADS * HEAD_DIM,) * 3
ODD_WIDTHS = (MLA_Q_RANK, MLA_KV_RANK, MLA_ROPE, SWA_HEADS * HEAD_DIM, SWA_KV_HEADS * HEAD_DIM, SWA_KV_HEADS * HEAD_DIM)

LANES = 128
VMEM_LIMIT = 56 * 1024 * 1024


def _params(*sem):
    return pltpu.CompilerParams(dimension_semantics=sem, vmem_limit_bytes=VMEM_LIMIT)


def _dot(a, b, precision=None):
    return lax.dot_general(a, b, (((1,), (0,)), ((), ())), precision=precision, preferred_element_type=F32)


def _dot_nt(a, b, precision=None):
    return lax.dot_general(a, b, (((1,), (1,)), ((), ())), precision=precision, preferred_element_type=F32)


def _iota(shape, dim):
    return lax.broadcasted_iota(jnp.int32, shape, dim)


def _offsets(widths):
    out, acc = [], 0
    for w in widths:
        out.append((acc, acc + w))
        acc += w
    return out


def _bucket_thresholds():
    d = np.arange(0, 4 * MAX_DISTANCE, dtype=np.int64)
    exact = N_BUCKETS // 2
    x = np.maximum(d, 1).astype(np.float32) / np.float32(exact)
    logp = exact + (np.log(x) / np.float32(math.log(MAX_DISTANCE / exact)) * np.float32(N_BUCKETS - exact)).astype(np.int32)
    bucket = np.where(d < exact, d, np.minimum(logp, N_BUCKETS - 1))
    return [int(np.argmax(bucket >= b)) for b in range(1, N_BUCKETS)]


BUCKET_THR = _bucket_thresholds()
FAR_DIST = BUCKET_THR[-1]


def _ada_kernel(c_ref, w_ref, b_ref, o_ref):
    c = c_ref[...]
    o_ref[...] = _dot(c * jax.nn.sigmoid(c), w_ref[...], precision=HI) + b_ref[...]


def ada_all(c, ada_w, ada_b):
    depth, d, n = ada_w.shape
    rows = 8
    cp = jnp.pad(c, ((0, rows - c.shape[0]), (0, 0)))
    tn = 1536
    out = pl.pallas_call(
        _ada_kernel,
        grid=(depth, n // tn),
        in_specs=[pl.BlockSpec((rows, d), lambda l, j: (0, 0)),
                  pl.BlockSpec((None, d, tn), lambda l, j: (l, 0, j)),
                  pl.BlockSpec((None, 1, tn), lambda l, j: (l, 0, j))],
        out_specs=pl.BlockSpec((None, rows, tn), lambda l, j: (l, 0, j)),
        out_shape=jax.ShapeDtypeStruct((depth, rows, n), F32),
        compiler_params=_params("parallel", "parallel"),
        name="ada",
    )(cp, ada_w, ada_b.reshape(depth, 1, n))
    return out[:, :c.shape[0], :]


def _norm_mod(x, g, sc, sh):
    y = x * lax.rsqrt(jnp.mean(x * x, axis=-1, keepdims=True) + NORM_EPS) * g
    return y * (1.0 + sc) + sh


def _nmm_kernel(x_ref, g_ref, sc_ref, sh_ref, w_ref, o_ref):
    h = _norm_mod(x_ref[...], g_ref[...], sc_ref[...], sh_ref[...])
    o_ref[...] = _dot(h.astype(BF16), w_ref[...]).astype(o_ref.dtype)


def norm_mod_matmul(x, g, sc, sh, w, tm=512):
    b, s, d = x.shape
    n = w.shape[1]
    return pl.pallas_call(
        _nmm_kernel,
        grid=(b, s // tm),
        in_specs=[pl.BlockSpec((None, tm, d), lambda bi, i: (bi, i, 0)),
                  pl.BlockSpec((1, d), lambda bi, i: (0, 0)),
                  pl.BlockSpec((None, 1, d), lambda bi, i: (bi, 0, 0)),
                  pl.BlockSpec((None, 1, d), lambda bi, i: (bi, 0, 0)),
                  pl.BlockSpec((d, n), lambda bi, i: (0, 0))],
        out_specs=pl.BlockSpec((None, tm, n), lambda bi, i: (bi, i, 0)),
        out_shape=jax.ShapeDtypeStruct((b, s, n), F32),
        compiler_params=_params("parallel", "parallel"),
        name="norm_mod_matmul",
    )(x, g.reshape(1, d), sc.reshape(b, 1, d), sh.reshape(b, 1, d), w)


def _bias_kernel(tab_ref, o_ref, *, rows, cols, step, cstride, c0, key_major=False):
    sub = 8
    off = pl.program_id(0) * step - c0
    unit = LOG2E if key_major else 1.0
    if key_major:
        base = _iota((sub, cols), 1) - _iota((sub, cols), 0)
    else:
        base = _iota((sub, cols), 0) - cstride * _iota((sub, cols), 1)

    def body(r, carry):
        dist = off + base + (-r * sub if key_major else r * sub)
        vs = [jnp.full((sub, cols), tab_ref[0, h] * unit, F32) for h in range(N_BIAS_HEADS)]
        for b in range(1, N_BUCKETS):
            ge = dist >= BUCKET_THR[b - 1]
            for h in range(N_BIAS_HEADS):
                vs[h] = jnp.where(ge, tab_ref[b, h] * unit, vs[h])
        for h in range(N_BIAS_HEADS):
            v = jnp.where(dist < 0, NEG, vs[h]) if key_major else vs[h]
            o_ref[h, pl.ds(pl.multiple_of(r * sub, sub), sub), :] = v
        return carry

    lax.fori_loop(0, rows // sub, body, 0)


def bias_bank(rel_table, t, key_major=False):
    nd = -(-(FAR_DIST + t - 1) // t)
    kern = functools.partial(_bias_kernel, rows=t, cols=t, step=t, cstride=1, c0=0, key_major=key_major)
    bank = pl.pallas_call(
        kern,
        grid=(nd,),
        in_specs=[pl.BlockSpec(memory_space=pltpu.SMEM)],
        out_specs=pl.BlockSpec((N_BIAS_HEADS, None, t, t), lambda i: (0, i, 0, 0)),
        out_shape=jax.ShapeDtypeStruct((N_BIAS_HEADS, nd, t, t), F32),
        compiler_params=_params("parallel"),
        name="bias_bank",
    )(rel_table)
    return bank


def bias_cmp(rel_table, s, nc, tq=256):
    kern = functools.partial(_bias_kernel, rows=tq, cols=nc, step=tq, cstride=CMP_STRIDE, c0=CMP_LEN - 1)
    return pl.pallas_call(
        kern,
        grid=(s // tq,),
        in_specs=[pl.BlockSpec(memory_space=pltpu.SMEM)],
        out_specs=pl.BlockSpec((N_BIAS_HEADS, tq, nc), lambda i: (0, i, 0)),
        out_shape=jax.ShapeDtypeStruct((N_BIAS_HEADS, s, nc), F32),
        compiler_params=_params("parallel"),
        name="bias_cmp",
    )(rel_table)


def _compress_kernel(x_ref, w1_ref, pos_ref, w2_ref, o_ref):
    x = x_ref[...]
    w1 = w1_ref[...]
    half = w1.shape[0]
    hid = w1.shape[1] // 2
    r = _dot(x, w1, precision=HI)
    pos = pos_ref[...]
    pb = _dot(pos[:, :half], w1[:, :hid], precision=HI) + _dot(pos[:, half:], w1[:, hid:], precision=HI)
    nxt = pltpu.roll(r[:, hid:], x.shape[0] - 1, 0)
    pre = r[:, :hid] + nxt + pb[0:1, :]
    o_ref[...] = _dot(jax.nn.gelu(pre), w2_ref[...], precision=HI)


def nsa_compress(kv, w1, pos, w2):
    two, bh, s, dh = kv.shape
    nc = s // CMP_STRIDE
    half = CMP_STRIDE * dh
    hid = w1.shape[2]
    x = kv.reshape(two, bh, nc, half)
    w1cat = jnp.concatenate([w1[:, :half, :], w1[:, half:, :]], axis=2)
    posf = jnp.broadcast_to(pos.reshape(two, 1, CMP_LEN * dh), (two, 8, CMP_LEN * dh))
    return pl.pallas_call(
        _compress_kernel,
        grid=(two, bh),
        in_specs=[pl.BlockSpec((None, None, nc, half), lambda w, i: (w, i, 0, 0)),
                  pl.BlockSpec((None, half, 2 * hid), lambda w, i: (w, 0, 0)),
                  pl.BlockSpec((None, 8, CMP_LEN * dh), lambda w, i: (w, 0, 0)),
                  pl.BlockSpec((None, hid, dh), lambda w, i: (w, 0, 0))],
        out_specs=pl.BlockSpec((None, None, nc, dh), lambda w, i: (w, i, 0, 0)),
        out_shape=jax.ShapeDtypeStruct((two, bh, nc, dh), F32),
        compiler_params=_params("parallel", "parallel"),
        name="nsa_compress",
    )(x, w1cat, posf, w2)


def _topn_mask(score, index, n):
    idx_f = index.astype(F32)
    sel = jnp.zeros(score.shape, jnp.bool_)
    for _ in range(n):
        m = jnp.max(score, axis=0, keepdims=True)
        first = jnp.min(jnp.where(score == m, idx_f, float(score.shape[0])), axis=0, keepdims=True)
        pick = idx_f == first
        sel = sel | pick
        score = jnp.where(pick, -jnp.inf, score)
    return sel


def _cmp_kernel(q_ref, k_ref, v_ref, b_ref, ov_ref, oc_ref, sel_ref, *, tq, nc, nslc, ntop, group):
    qi = pl.program_id(2)
    t = qi * tq + _iota((tq, 1), 0)
    n = _iota((1, nc), 1)
    mask = (n * CMP_STRIDE + (CMP_LEN - 1) <= t) & (n < nc - 1)
    k = k_ref[...]
    v = v_ref[...].astype(BF16)
    psum = jnp.zeros((tq, nc), F32)
    for g in range(group):
        s = _dot_nt(q_ref[g], k, precision=HI) + b_ref[g]
        s = jnp.where(mask, s, NEG)
        m = jnp.max(s, axis=-1, keepdims=True)
        p = jnp.where(mask, jnp.exp(s - m), 0.0)
        den = jnp.sum(p, axis=-1, keepdims=True)
        p = p * jnp.where(den > 0.0, 1.0 / den, 0.0)
        oc_ref[g] = _dot(p.astype(BF16), v)
        psum = psum + p
    imp = _dot_nt(ov_ref[...], psum, precision=HI)
    j = _iota((nslc, tq), 0)
    blk = jnp.right_shift(qi * tq + _iota((1, tq), 1), int(math.log2(SLC_BLOCK)))
    forced = (j == 0) | (j == blk) | (j == blk - 1)
    valid = j <= blk
    score = jnp.where(forced, BIG, jnp.where(valid, imp, NEG))
    sel = _topn_mask(score, j, ntop) & valid
    sel_ref[0] = jnp.where(sel, 0.0, NEG)


def nsa_cmp_attention(q, k_cmp, v_cmp, bias_c, tq=256):
    b, hk, g, s, dh = q.shape
    nc = k_cmp.shape[2]
    nslc = s // SLC_BLOCK
    ntop = min(SLC_TOPN, nslc)
    cmp_start = np.arange(nc, dtype=np.int64) * CMP_STRIDE
    slc_lo = np.arange(nslc, dtype=np.int64) * SLC_BLOCK
    overlap = ((cmp_start[:, None] <= slc_lo[None, :] + SLC_BLOCK - 1)
               & (cmp_start[:, None] + CMP_LEN - 1 >= slc_lo[None, :])).astype(np.float32)
    kern = functools.partial(_cmp_kernel, tq=tq, nc=nc, nslc=nslc, ntop=ntop, group=g)
    return pl.pallas_call(
        kern,
        grid=(b, hk, s // tq),
        in_specs=[pl.BlockSpec((None, None, g, tq, dh), lambda bi, h, i: (bi, h, 0, i, 0)),
                  pl.BlockSpec((None, None, nc, dh), lambda bi, h, i: (bi, h, 0, 0)),
                  pl.BlockSpec((None, None, nc, dh), lambda bi, h, i: (bi, h, 0, 0)),
                  pl.BlockSpec((g, tq, nc), lambda bi, h, i: (h, i, 0)),
                  pl.BlockSpec((nslc, nc), lambda bi, h, i: (0, 0))],
        out_specs=[pl.BlockSpec((None, None, g, tq, dh), lambda bi, h, i: (bi, h, 0, i, 0)),
                   pl.BlockSpec((None, None, 1, nslc, tq), lambda bi, h, i: (bi, h, 0, 0, i))],
        out_shape=[jax.ShapeDtypeStruct((b, hk, g, s, dh), F32),
                   jax.ShapeDtypeStruct((b, hk, 1, nslc, s), F32)],
        compiler_params=_params("parallel", "parallel", "parallel"),
        name="nsa_cmp_attention",
    )(q, k_cmp, v_cmp, bias_c, jnp.asarray(overlap.T))


def _kmean_kernel(k_ref, o_ref, *, nb, blk):
    k = k_ref[...]
    o_ref[...] = jnp.sum(k.reshape(nb, blk, k.shape[-1]), axis=1) * (1.0 / blk)


def _moba_sel_kernel(q_ref, km_ref, sel_ref, *, tq, nb, ntop):
    qi = pl.program_id(2)
    own = jnp.right_shift(qi * tq + _iota((1, tq), 1), int(math.log2(MOBA_BLOCK)))
    j = _iota((nb, tq), 0)
    gate = _dot_nt(km_ref[...], q_ref[...], precision=HI)
    past = j < own
    sel = _topn_mask(jnp.where(past, gate, NEG), j, ntop) & past
    sel_ref[...] = jnp.where(sel | (j == own), 0.0, NEG)


def moba_select(q, k, tq=256):
    b, h, s, dh = q.shape
    nb = s // MOBA_BLOCK
    ntop = min(MOBA_TOPK, nb)
    kmean = pl.pallas_call(
        functools.partial(_kmean_kernel, nb=nb, blk=MOBA_BLOCK),
        grid=(b, h),
        in_specs=[pl.BlockSpec((None, None, s, dh), lambda bi, hi: (bi, hi, 0, 0))],
        out_specs=pl.BlockSpec((None, None, nb, dh), lambda bi, hi: (bi, hi, 0, 0)),
        out_shape=jax.ShapeDtypeStruct((b, h, nb, dh), F32),
        compiler_params=_params("parallel", "parallel"),
        name="moba_kmean",
    )(k)
    return pl.pallas_call(
        functools.partial(_moba_sel_kernel, tq=tq, nb=nb, ntop=ntop),
        grid=(b, h, s // tq),
        in_specs=[pl.BlockSpec((None, None, tq, dh), lambda bi, hi, i: (bi, hi, i, 0)),
                  pl.BlockSpec((None, None, nb, dh), lambda bi, hi, i: (bi, hi, 0, 0))],
        out_specs=pl.BlockSpec((None, None, nb, tq), lambda bi, hi, i: (bi, hi, 0, i)),
        out_shape=jax.ShapeDtypeStruct((b, h, nb, s), F32),
        compiler_params=_params("parallel", "parallel", "parallel"),
        name="moba_select",
    )(q, kmean)


AUG = 64


def _flash_kernel(qt_ref, kt_ref, *refs, hb, kb, sb, bh, t, tq, kc, nparts, use_sel, nbt, nd):
    pos = 0
    q_refs = refs[pos:pos + nparts]; pos += nparts
    k_refs = refs[pos:pos + nparts]; pos += nparts
    vt_ref = refs[pos]; pos += 1
    sel_ref = None
    if use_sel:
        sel_ref = refs[pos]; pos += 1
    bias_ref, far_ref, o_ref, m_ref, l_ref, acc_ref = refs[pos:pos + 6]

    head0 = pl.program_id(1) * hb
    step = pl.program_id(2)
    qi = qt_ref[step]
    ki = kt_ref[step]

    @pl.when(ki == 0)
    def _():
        m_ref[...] = jnp.full(m_ref.shape, M_INIT, F32)
        l_ref[...] = jnp.zeros(l_ref.shape, F32)
        acc_ref[...] = jnp.zeros(acc_ref.shape, F32)

    nchunk = t // kc
    nlane = t // tq

    def mask_lanes(si, qc):
        rows = sel_ref[si, qc * tq:(qc + 1) * tq, :]
        shift = lax.rem(AUG + LANES - lax.rem(ki * nbt, LANES), LANES)
        lane = _iota((tq, LANES), 1)
        keep = (lane >= AUG) & (lane < AUG + nbt)
        return jnp.where(keep, pltpu.roll(rows, shift, 1), 0.0).astype(BF16)

    def scores(g, qc, near, masks):
        kg = g if kb > 1 else 0
        bg = g if bh > 1 else 0
        ql = slice(qc * tq, (qc + 1) * tq)
        qs = [q_ref[g, ql, :] for q_ref in q_refs]
        if use_sel:
            qs[0] = jnp.where(_iota((tq, LANES), 1) < AUG, qs[0], masks[(g if sb > 1 else 0, qc)])
        chunks, top = [], None
        for c in range(nchunk):
            kr = slice(c * kc, (c + 1) * kc)
            s = _dot_nt(k_refs[0][kg, kr, :], qs[0])
            for p in range(1, nparts):
                s = s + _dot_nt(k_refs[p][kg, kr, :], qs[p])
            if near:
                s = s + bias_ref[bg, kr, ql]
            chunks.append(s)
            top = s if top is None else jnp.maximum(top, s)
        return chunks, top

    def update(g, qc, near, chunks, top):
        kg = g if kb > 1 else 0
        ql = slice(qc * tq, (qc + 1) * tq)
        shift = 0.0 if near else far_ref[head0 + g]
        m = m_ref[g, :, ql]
        m_new = jnp.maximum(m, jnp.max(top, axis=0, keepdims=True) + shift)
        alpha = jnp.exp2(m - m_new)
        base = m_new - shift
        pv = psum = None
        for c in range(nchunk):
            p = jnp.exp2(chunks[c] - base)
            ps = jnp.sum(p, axis=0, keepdims=True)
            pd = _dot(vt_ref[kg, :, c * kc:(c + 1) * kc], p.astype(BF16))
            pv = pd if pv is None else pv + pd
            psum = ps if psum is None else psum + ps
        m_ref[g, :, ql] = m_new
        l_ref[g, :, ql] = alpha * l_ref[g, :, ql] + psum
        acc_ref[g, :, ql] = alpha * acc_ref[g, :, ql] + pv

    def body(near):
        masks = {}
        if use_sel:
            masks = {(si, qc): mask_lanes(si, qc) for si in range(sb) for qc in range(nlane)}
        items = [(g, qc) for g in range(hb) for qc in range(nlane)]
        ahead = 2
        pending = [scores(*item, near, masks) for item in items[:ahead]]
        for i, item in enumerate(items):
            if i + ahead < len(items):
                pending.append(scores(*items[i + ahead], near, masks))
            update(*item, near, *pending.pop(0))

    @pl.when(qi - ki < nd)
    def _():
        body(True)

    @pl.when(qi - ki >= nd)
    def _():
        body(False)

    @pl.when(ki == qi)
    def _():
        for g in range(hb):
            o_ref[g] = jnp.transpose(acc_ref[g] / l_ref[g]).astype(o_ref.dtype)


def flash_attention(q_parts, k_parts, v, bias, far, sel=None, t=512, blk=1, tq=256, kc=128):
    b, ng, hb, s, _ = q_parts[0].shape
    kb = v.shape[2]
    dv = v.shape[-1]
    t = min(t, s)
    tq, kc = min(tq, t), min(kc, t)
    nq = s // t
    nd = bias.shape[1]
    pairs = [(i, j) for i in range(nq) for j in range(i + 1)]
    qt = jnp.asarray(np.array([p[0] for p in pairs], np.int32))
    kt = jnp.asarray(np.array([p[1] for p in pairs], np.int32))
    nparts = len(q_parts)
    use_sel = sel is not None
    sb = sel.shape[2] if use_sel else 1
    bh = hb if bias.shape[0] > 1 else 1
    nbt = t // blk
    vt = jnp.swapaxes(v, -1, -2)
    q_parts, k_parts = list(q_parts), list(k_parts)
    if use_sel:
        onehot = (np.arange(s)[:, None] % t // blk == np.arange(LANES - AUG)[None, :]).astype(np.float32)
        k0 = k_parts[0]
        k_parts[0] = jnp.concatenate(
            [k0, jnp.broadcast_to(jnp.asarray(onehot, k0.dtype), k0.shape[:-1] + (LANES - AUG,))], axis=-1)
        q_parts[0] = jnp.pad(q_parts[0], ((0, 0),) * 4 + ((0, LANES - AUG),))
        selq = jnp.swapaxes(sel, -1, -2)
        selq = jnp.pad(selq, ((0, 0),) * 4 + ((0, LANES - selq.shape[-1]),))

    in_specs, args = [], []
    for qp in q_parts:
        in_specs.append(pl.BlockSpec((None, None, hb, t, qp.shape[-1]), lambda bi, n, st, qt, kt: (bi, n, 0, qt[st], 0)))
        args.append(qp)
    for kp in k_parts:
        in_specs.append(pl.BlockSpec((None, None, kb, t, kp.shape[-1]), lambda bi, n, st, qt, kt: (bi, n, 0, kt[st], 0)))
        args.append(kp)
    in_specs.append(pl.BlockSpec((None, None, kb, dv, t), lambda bi, n, st, qt, kt: (bi, n, 0, 0, kt[st])))
    args.append(vt)
    if use_sel:
        in_specs.append(pl.BlockSpec((None, None, sb, t, LANES), lambda bi, n, st, qt, kt: (bi, n, 0, qt[st], 0)))
        args.append(selq)
    near_tile = lambda qt, kt, st: jnp.minimum(qt[st] - kt[st], nd - 1)
    if bh > 1:
        in_specs.append(pl.BlockSpec((bh, None, t, t), lambda bi, n, st, qt, kt: (n, near_tile(qt, kt, st), 0, 0)))
    else:
        in_specs.append(pl.BlockSpec((1, None, t, t), lambda bi, n, st, qt, kt: (0, near_tile(qt, kt, st), 0, 0)))
    in_specs.append(pl.BlockSpec(memory_space=pltpu.SMEM))
    args += [bias, far]
    kern = functools.partial(_flash_kernel, hb=hb, kb=kb, sb=sb, bh=bh, t=t, tq=tq, kc=kc, nparts=nparts,
                             use_sel=use_sel, nbt=nbt, nd=nd)
    grid_spec = pltpu.PrefetchScalarGridSpec(
        num_scalar_prefetch=2,
        grid=(b, ng, len(pairs)),
        in_specs=in_specs,
        out_specs=pl.BlockSpec((None, None, hb, t, dv), lambda bi, n, st, qt, kt: (bi, n, 0, qt[st], 0)),
        scratch_shapes=[pltpu.VMEM((hb, 1, t), F32), pltpu.VMEM((hb, 1, t), F32), pltpu.VMEM((hb, dv, t), F32)])
    return pl.pallas_call(
        kern,
        grid_spec=grid_spec,
        out_shape=jax.ShapeDtypeStruct((b, ng, hb, s, dv), F32),
        compiler_params=_params("parallel", "parallel", "arbitrary"),
        name="flash_attention",
    )(qt, kt, *args)


def _window_kernel(*refs, group, t, ntile, window, has_sink):
    q_ref = refs[0]
    k_refs = refs[1:1 + ntile]
    v_refs = refs[1 + ntile:1 + 2 * ntile]
    bias_ref = refs[1 + 2 * ntile]
    sink_ref = refs[2 + 2 * ntile] if has_sink else None
    o_ref = refs[-1]
    kvh = pl.program_id(1)
    diff = _iota((t, t), 0) - _iota((t, t), 1)
    masks = []
    for r in range(ntile):
        dist = (ntile - 1 - r) * t + diff
        masks.append((dist >= 0) & (dist < window))
    for g in range(group):
        q = q_ref[g]
        ss = []
        m = None
        for r in range(ntile):
            s = _dot_nt(q, k_refs[r][...]) + bias_ref[g, ntile - 1 - r]
            s = jnp.where(masks[r], s, NEG)
            ss.append(s)
            mr = jnp.max(s, axis=-1, keepdims=True)
            m = mr if m is None else jnp.maximum(m, mr)
        if has_sink:
            sink = sink_ref[kvh * group + g]
            m = jnp.maximum(m, sink)
        den = jnp.exp(sink - m) if has_sink else 0.0
        acc = None
        for r in range(ntile):
            p = jnp.exp(ss[r] - m)
            den = den + jnp.sum(p, axis=-1, keepdims=True)
            pv = _dot(p.astype(BF16), v_refs[r][...])
            acc = pv if acc is None else acc + pv
        o_ref[g] = acc / den


def window_attention(q, k, v, bank, t, window, sinks=None):
    b, hk, g, s, dh = q.shape
    ntile = window // t + 1
    pad = ((0, 0), (0, 0), (window, 0), (0, 0))
    kp, vp = jnp.pad(k, pad), jnp.pad(v, pad)
    has_sink = sinks is not None
    in_specs = [pl.BlockSpec((None, None, g, t, dh), lambda bi, h, i: (bi, h, 0, i, 0))]
    args = [q]
    for arr in (kp, vp):
        for r in range(ntile):
            in_specs.append(pl.BlockSpec((None, None, t, dh), lambda bi, h, i, r=r: (bi, h, i + r, 0)))
            args.append(arr)
    in_specs.append(pl.BlockSpec((g, ntile, t, t), lambda bi, h, i: (h, 0, 0, 0)))
    args.append(bank)
    if has_sink:
        in_specs.append(pl.BlockSpec(memory_space=pltpu.SMEM))
        args.append(sinks)
    kern = functools.partial(_window_kernel, group=g, t=t, ntile=ntile, window=window, has_sink=has_sink)
    return pl.pallas_call(
        kern,
        grid=(b, hk, s // t),
        in_specs=in_specs,
        out_specs=pl.BlockSpec((None, None, g, t, dh), lambda bi, h, i: (bi, h, 0, i, 0)),
        out_shape=jax.ShapeDtypeStruct((b, hk, g, s, dh), F32),
        compiler_params=_params("parallel", "parallel", "parallel"),
        name="window_attention",
    )(*args)


def _rms(x, g):
    return x * lax.rsqrt(jnp.mean(x * x, axis=-1, keepdims=True) + NORM_EPS) * g


def _mla_up_kernel(cq_ref, ckv_ref, kr_ref, krs_ref, qn_ref, kvn_ref, wq_ref, wkv_ref, cs_ref, sn_ref,
                   qnope_ref, qrope_ref, knope_ref, v_ref, krope_ref, *, scale):
    cs = cs_ref[...]
    sn = sn_ref[...]
    q = _dot(_rms(cq_ref[...], qn_ref[...]).astype(BF16), wq_ref[...]) * scale
    kv = _dot(_rms(ckv_ref[...], kvn_ref[...]).astype(BF16), wkv_ref[...])
    wq_head = MLA_NOPE + 2 * MLA_ROPE
    for h in range(MLA_HEADS):
        base = h * wq_head
        qnope_ref[h] = q[:, base:base + MLA_NOPE].astype(qnope_ref.dtype)
        x = q[:, base + MLA_NOPE:base + MLA_NOPE + MLA_ROPE]
        xs = q[:, base + MLA_NOPE + MLA_ROPE:base + wq_head]
        qrope_ref[h] = (x * cs + xs * sn).astype(qrope_ref.dtype)
        kb = h * (MLA_NOPE + MLA_V)
        knope_ref[h] = kv[:, kb:kb + MLA_NOPE].astype(knope_ref.dtype)
        v_ref[h] = kv[:, kb + MLA_NOPE:kb + MLA_NOPE + MLA_V].astype(v_ref.dtype)
    kr = (kr_ref[...] * cs + krs_ref[...] * sn).astype(krope_ref.dtype)
    for h in range(MLA_HEADS):
        krope_ref[h] = kr


def mla_up(c_q, c_kv, k_rope, k_rope_sw, q_norm, kv_norm, w_q_up, w_kv_up, tm=512):
    b, s, _ = c_q.shape
    half = MLA_ROPE // 2
    inv = ROPE_THETA ** (-np.arange(0, MLA_ROPE, 2, dtype=np.float32) / np.float32(MLA_ROPE))
    ang = np.arange(s, dtype=np.float32)[:, None] * inv[None, :].astype(np.float32)
    cos, sin = np.cos(ang).astype(np.float32), np.sin(ang).astype(np.float32)
    cs = jnp.asarray(np.concatenate([cos, cos], axis=1))
    sn = jnp.asarray(np.concatenate([-sin, sin], axis=1))
    dq = MLA_NOPE + MLA_ROPE
    wq = w_q_up.reshape(MLA_Q_RANK, MLA_HEADS, dq)
    rope_cols = wq[:, :, MLA_NOPE:]
    swapped = jnp.concatenate([rope_cols[:, :, half:], rope_cols[:, :, :half]], axis=2)
    wq_aug = jnp.concatenate([wq, swapped], axis=2).reshape(MLA_Q_RANK, MLA_HEADS * (dq + MLA_ROPE)).astype(BF16)
    wkv = w_kv_up.astype(BF16)
    scale = (MLA_NOPE + MLA_ROPE) ** -0.5 * LOG2E
    tm = min(tm, s)
    row = lambda w: pl.BlockSpec((None, tm, w), lambda bi, i: (bi, i, 0))
    full = lambda a: pl.BlockSpec(a.shape, lambda bi, i: (0,) * a.ndim)
    head = lambda w: pl.BlockSpec((None, MLA_HEADS, tm, w), lambda bi, i: (bi, 0, i, 0))
    qn2, kvn2 = q_norm.reshape(1, -1), kv_norm.reshape(1, -1)
    outs = pl.pallas_call(
        functools.partial(_mla_up_kernel, scale=scale),
        grid=(b, s // tm),
        in_specs=[row(MLA_Q_RANK), row(MLA_KV_RANK), row(MLA_ROPE), row(MLA_ROPE), full(qn2), full(kvn2),
                  full(wq_aug), full(wkv),
                  pl.BlockSpec((tm, MLA_ROPE), lambda bi, i: (i, 0)), pl.BlockSpec((tm, MLA_ROPE), lambda bi, i: (i, 0))],
        out_specs=[head(MLA_NOPE), head(MLA_ROPE), head(MLA_NOPE), head(MLA_V), head(MLA_ROPE)],
        out_shape=[jax.ShapeDtypeStruct((b, MLA_HEADS, s, w), BF16)
                   for w in (MLA_NOPE, MLA_ROPE, MLA_NOPE, MLA_V, MLA_ROPE)],
        compiler_params=_params("parallel", "parallel"),
        name="mla_up",
    )(c_q, c_kv, k_rope, k_rope_sw, qn2, kvn2, wq_aug, wkv, cs, sn)
    return outs


def _out_kernel(*refs, gated):
    if gated:
        oc_ref, os_ref, ow_ref, gt_ref, ex_ref, ob_ref, x_ref, gm_ref, w_ref, o_ref = refs
        half = oc_ref.shape[-1]
        ge = _dot(jax.nn.sigmoid(gt_ref[...]), ex_ref[...], precision=HI)
        oa = ge[:, :half] * oc_ref[...] + ge[:, half:2 * half] * os_ref[...] + ge[:, 2 * half:] * ow_ref[...]
    else:
        oa_ref, ob_ref, x_ref, gm_ref, w_ref, o_ref = refs
        half = oa_ref.shape[-1]
        oa = oa_ref[...]
    mix = _dot(oa.astype(BF16), w_ref[:half, :]) + _dot(ob_ref[...].astype(BF16), w_ref[half:, :])
    o_ref[...] = x_ref[...] + gm_ref[...] * mix


def out_project(parts, ob, x, gate_m, w_out, gates=None, tm=512):
    b, s, d = x.shape
    gated = gates is not None
    tm = min(tm, s)
    row = lambda a: pl.BlockSpec((None, tm, a.shape[-1]), lambda bi, i: (bi, i, 0))
    full = lambda a: pl.BlockSpec(a.shape, lambda bi, i: (0,) * a.ndim)
    args, in_specs = [], []
    for p in parts:
        args.append(p); in_specs.append(row(p))
    if gated:
        half = parts[0].shape[-1]
        nh = half // HEAD_DIM
        ex = np.zeros((LANES, 3 * half), np.float32)
        for h in range(nh):
            for br in range(3):
                ex[h * 3 + br, br * half + h * HEAD_DIM: br * half + (h + 1) * HEAD_DIM] = 1.0
        gpad = jnp.pad(gates, ((0, 0), (0, 0), (0, LANES - gates.shape[-1])))
        ex = jnp.asarray(ex)
        args += [gpad, ex]; in_specs += [row(gpad), full(ex)]
    gm = gate_m.reshape(b, 1, d)
    wb = w_out.astype(BF16)
    args += [ob, x, gm, wb]
    in_specs += [row(ob), row(x), pl.BlockSpec((None, 1, d), lambda bi, i: (bi, 0, 0)), full(wb)]
    return pl.pallas_call(
        functools.partial(_out_kernel, gated=gated),
        grid=(b, s // tm),
        in_specs=in_specs,
        out_specs=pl.BlockSpec((None, tm, d), lambda bi, i: (bi, i, 0)),
        out_shape=jax.ShapeDtypeStruct((b, s, d), F32),
        compiler_params=_params("parallel", "parallel"),
        name="out_project",
    )(*args)


def _ffn_pre_kernel(x_ref, g_ref, sc_ref, sh_ref, rw_ref, rb_ref, h_ref, cw_ref):
    h = _norm_mod(x_ref[...], g_ref[...], sc_ref[...], sh_ref[...])
    h_ref[...] = h.astype(h_ref.dtype)
    aff = jax.nn.sigmoid(_dot_nt(rw_ref[...], h, precision=HI))
    biased = aff + rb_ref[...]
    epg = EXPERTS_PER_GROUP
    brow = [biased[e:e + 1, :] for e in range(N_EXPERTS)]
    arow = [aff[e:e + 1, :] for e in range(N_EXPERTS)]
    best = gsel = None
    for gi in range(N_GROUPS):
        a, b_, c, d_ = brow[gi * epg:(gi + 1) * epg]
        hi1, lo1, hi2, lo2 = jnp.maximum(a, b_), jnp.minimum(a, b_), jnp.maximum(c, d_), jnp.minimum(c, d_)
        score = jnp.maximum(hi1, hi2) + jnp.maximum(jnp.minimum(hi1, hi2), jnp.maximum(lo1, lo2))
        if gi == 0:
            best, gsel = score, jnp.zeros(score.shape, jnp.int32)
        else:
            better = score > best
            gsel = jnp.where(better, gi, gsel)
            best = jnp.where(better, score, best)

    def in_group(rows, j):
        v = rows[j]
        for gi in range(1, N_GROUPS):
            v = jnp.where(gsel == gi, rows[gi * epg + j], v)
        return v

    bv = [in_group(brow, j) for j in range(epg)]
    av = [in_group(arow, j) for j in range(epg)]

    def argmax_excluding(skip):
        val = idx = None
        for j in range(epg):
            cand = bv[j] if skip is None else jnp.where(skip == j, -jnp.inf, bv[j])
            if j == 0:
                val, idx = cand, jnp.zeros(cand.shape, jnp.int32)
            else:
                better = cand > val
                idx = jnp.where(better, j, idx)
                val = jnp.where(better, cand, val)
        return idx

    first = argmax_excluding(None)
    second = argmax_excluding(first)

    def pick(rows, idx):
        v = rows[0]
        for j in range(1, epg):
            v = jnp.where(idx == j, rows[j], v)
        return v

    a1, a2 = pick(av, first), pick(av, second)
    tot = a1 + a2
    e1, e2 = gsel * epg + first, gsel * epg + second
    eid = _iota(aff.shape, 0)
    cw_ref[...] = jnp.where(eid == e1, a1 / tot, 0.0) + jnp.where(eid == e2, a2 / tot, 0.0)


def ffn_pre(x, g, sc, sh, router_w, router_b, tm=512):
    b, s, d = x.shape
    e = router_w.shape[1]
    tm = min(tm, s)
    return pl.pallas_call(
        _ffn_pre_kernel,
        grid=(b, s // tm),
        in_specs=[pl.BlockSpec((None, tm, d), lambda bi, i: (bi, i, 0)),
                  pl.BlockSpec((1, d), lambda bi, i: (0, 0)),
                  pl.BlockSpec((None, 1, d), lambda bi, i: (bi, 0, 0)),
                  pl.BlockSpec((None, 1, d), lambda bi, i: (bi, 0, 0)),
                  pl.BlockSpec((e, d), lambda bi, i: (0, 0)),
                  pl.BlockSpec((e, 1), lambda bi, i: (0, 0))],
        out_specs=[pl.BlockSpec((None, tm, d), lambda bi, i: (bi, i, 0)),
                   pl.BlockSpec((None, e, tm), lambda bi, i: (bi, 0, i))],
        out_shape=[jax.ShapeDtypeStruct((b, s, d), BF16), jax.ShapeDtypeStruct((b, e, s), F32)],
        compiler_params=_params("parallel", "parallel"),
        name="ffn_pre",
    )(x, g.reshape(1, d), sc.reshape(b, 1, d), sh.reshape(b, 1, d), router_w.T, router_b.reshape(e, 1))


def _moe_kernel(h_ref, cw_ref, x_ref, gf_ref, wg_ref, wu_ref, wd_ref, o_ref, acc_ref):
    e = pl.program_id(2)

    @pl.when(e == 0)
    def _():
        acc_ref[...] = jnp.zeros(acc_ref.shape, F32)

    h = h_ref[...]
    a = _dot(h, wg_ref[...])
    u = _dot(h, wu_ref[...])
    cw = cw_ref[...]
    c = jnp.sum(jnp.where(_iota(cw.shape, 1) == e, cw, 0.0), axis=-1, keepdims=True)
    hid = (a * jax.nn.sigmoid(a)) * u * c
    acc_ref[...] += _dot(hid.astype(BF16), wd_ref[...])

    @pl.when(e == pl.num_programs(2) - 1)
    def _():
        o_ref[...] = x_ref[...] + gf_ref[...] * acc_ref[...]


def moe_dense(h, cw, x, gate_f, w_gate, w_up, w_down, tm=512):
    b, s, d = x.shape
    ne, _, f = w_gate.shape
    tm = min(tm, s)
    return pl.pallas_call(
        _moe_kernel,
        grid=(b, s // tm, ne),
        in_specs=[pl.BlockSpec((None, tm, d), lambda bi, i, e: (bi, i, 0)),
                  pl.BlockSpec((None, tm, ne), lambda bi, i, e: (bi, i, 0)),
                  pl.BlockSpec((None, tm, d), lambda bi, i, e: (bi, i, 0)),
                  pl.BlockSpec((None, 1, d), lambda bi, i, e: (bi, 0, 0)),
                  pl.BlockSpec((None, d, f), lambda bi, i, e: (e, 0, 0)),
                  pl.BlockSpec((None, d, f), lambda bi, i, e: (e, 0, 0)),
                  pl.BlockSpec((None, f, d), lambda bi, i, e: (e, 0, 0))],
        out_specs=pl.BlockSpec((None, tm, d), lambda bi, i, e: (bi, i, 0)),
        out_shape=jax.ShapeDtypeStruct((b, s, d), F32),
        scratch_shapes=[pltpu.VMEM((tm, d), F32)],
        compiler_params=_params("parallel", "parallel", "arbitrary"),
        name="moe_dense",
    )(h, cw, x, gate_f.reshape(b, 1, d), w_gate, w_up, w_down)


def _final_norm_kernel(x_ref, g_ref, o_ref):
    o_ref[...] = _rms(x_ref[...], g_ref[...])


def final_rmsnorm(x, g, tm=512):
    b, s, d = x.shape
    tm = min(tm, s)
    return pl.pallas_call(
        _final_norm_kernel,
        grid=(b, s // tm),
        in_specs=[pl.BlockSpec((None, tm, d), lambda bi, i: (bi, i, 0)), pl.BlockSpec((1, d), lambda bi, i: (0, 0))],
        out_specs=pl.BlockSpec((None, tm, d), lambda bi, i: (bi, i, 0)),
        out_shape=jax.ShapeDtypeStruct((b, s, d), F32),
        compiler_params=_params("parallel", "parallel"),
        name="final_rmsnorm",
    )(x, g.reshape(1, d))


def _heads(x, nh):
    b, s, w = x.shape
    return x.reshape(b, s, nh, w // nh).transpose(0, 2, 1, 3)


def _merge(x):
    b, ng, hb, s, dh = x.shape
    return x.transpose(0, 3, 1, 2, 4).reshape(b, s, ng * hb * dh)


def even_mixer(proj, x, gate_m, w_out, pos_k, pos_v, ck_w1, ck_w2, cv_w1, cv_w2, rel_table, banks):
    b, s, _ = proj.shape
    off = _offsets(EVEN_WIDTHS)
    q_a, kc, vc, ks, vs, kw, vw, gates, q_b, k_b, v_b = (proj[..., lo:hi] for lo, hi in off)
    hk, g, dh = NSA_KV_HEADS, NSA_GROUP, HEAD_DIM
    scale = dh ** -0.5
    qa = (_heads(q_a, NSA_HEADS) * scale).reshape(b, hk, g, s, dh)
    qa16 = qa.astype(BF16)
    qa16_log2 = (qa * LOG2E).astype(BF16)
    (bank_l, far_l, t_l), (bank_w, t_w) = banks["dense"], banks["nsa_window"]

    nc = s // CMP_STRIDE
    kv = jnp.stack([_heads(kc, hk), _heads(vc, hk)]).reshape(2, b * hk, s, dh)
    cmp = nsa_compress(kv, jnp.stack([ck_w1, cv_w1]), jnp.stack([pos_k, pos_v]), jnp.stack([ck_w2, cv_w2]))
    cmp = cmp.reshape(2, b, hk, nc, dh)
    o_c, sel = nsa_cmp_attention(qa, cmp[0], cmp[1], bias_cmp(rel_table, s, nc, tq=min(256, s)), tq=min(256, s))
    ks5 = _heads(ks, hk).astype(BF16)[:, :, None]
    vs5 = _heads(vs, hk).astype(BF16)[:, :, None]
    o_s = flash_attention([qa16_log2], [ks5], vs5, bank_l, far_l, sel=sel, t=t_l, blk=SLC_BLOCK)
    o_w = window_attention(qa16, _heads(kw, hk).astype(BF16), _heads(vw, hk).astype(BF16), bank_w, t_w, NSA_WINDOW)

    hb = 4
    qb = _heads(q_b, MOBA_HEADS)
    kb = _heads(k_b, MOBA_HEADS)
    vb = _heads(v_b, MOBA_HEADS)
    selb = moba_select(qb, kb, tq=min(256, s))
    ngb = MOBA_HEADS // hb
    r5 = lambda a: a.reshape(b, ngb, hb, a.shape[-2], a.shape[-1])
    o_b = flash_attention([r5((qb * (scale * LOG2E)).astype(BF16))], [r5(kb.astype(BF16))], r5(vb.astype(BF16)),
                          bank_l, far_l, sel=r5(selb), t=t_l, blk=MOBA_BLOCK)

    return out_project([_merge(o_c), _merge(o_s), _merge(o_w)], _merge(o_b), x, gate_m, w_out, gates=gates)


def odd_mixer(proj, x, gate_m, w_out, q_norm, kv_norm, w_q_up, w_kv_up, sinks, banks):
    b, s, _ = proj.shape
    off = _offsets(ODD_WIDTHS + (MLA_ROPE,))
    c_q, c_kv, k_rope, q_d, k_d, v_d, k_rope_sw = (proj[..., lo:hi] for lo, hi in off)
    qn, qr, kn, v, kr = mla_up(c_q, c_kv, k_rope, k_rope_sw, q_norm, kv_norm, w_q_up, w_kv_up)
    hb = 2
    ng = MLA_HEADS // hb
    r5 = lambda a: a.reshape(b, ng, hb, s, a.shape[-1])
    bank_c, far_c, t_c = banks["causal"]
    q_full = jnp.concatenate([qn, qr], axis=-1)
    k_full = jnp.concatenate([kn, kr], axis=-1)
    o_c = flash_attention([r5(q_full)], [r5(k_full)], r5(v), bank_c, far_c, t=t_c)

    hk, g, dh = SWA_KV_HEADS, SWA_GROUP, HEAD_DIM
    scale = dh ** -0.5
    qd = (_heads(q_d, SWA_HEADS) * scale).reshape(b, hk, g, s, dh).astype(BF16)
    bank_s, t_s = banks["swa"]
    o_d = window_attention(qd, _heads(k_d, hk).astype(BF16), _heads(v_d, hk).astype(BF16), bank_s, t_s, SWA_WINDOW,
                           sinks=sinks)
    return out_project([_merge(o_c)], _merge(o_d), x, gate_m, w_out)


def kernel(x, c, rel_table, router_w, router_b, final_norm, norm_mix, norm_ffn, ada_w, ada_b, moe_w_gate, moe_w_up, moe_w_down, ev_w_in, ev_w_out, nsa_pos_k, nsa_pos_v, nsa_ck_w1, nsa_ck_w2, nsa_cv_w1, nsa_cv_w2, od_w_in, od_w_out, mla_q_norm, mla_kv_norm, mla_w_q_up, mla_w_kv_up, swa_sinks):
    b, s, d = x.shape
    depth = ada_w.shape[0]
    mods = ada_all(c, ada_w, ada_b)
    t_dense = min(512, s)
    bank_l = bias_bank(rel_table, t_dense, key_major=True)
    far_l = rel_table[N_BUCKETS - 1] * LOG2E
    future = np.arange(t_dense)[:, None] > np.arange(t_dense)[None, :]
    bank_c = jnp.asarray(np.where(future, NEG, 0.0).astype(np.float32)[None, None])
    t_w = min(256, s)
    banks = {"dense": (bank_l, far_l, t_dense), "causal": (bank_c, jnp.zeros((N_BIAS_HEADS,), F32), t_dense),
             "nsa_window": (bias_bank(rel_table, t_w), t_w), "swa": (bias_bank(rel_table, SWA_WINDOW), SWA_WINDOW)}

    for layer in range(depth):
        shift_m, scale_m, gate_m, shift_f, scale_f, gate_f = jnp.split(mods[layer], 6, axis=-1)
        i = layer // 2
        if layer % 2 == 0:
            w_in = ev_w_in[i]
            pad = -w_in.shape[1] % LANES
            proj = norm_mod_matmul(x, norm_mix[layer], scale_m, shift_m, jnp.pad(w_in, ((0, 0), (0, pad))).astype(BF16))
            x = even_mixer(proj, x, gate_m, ev_w_out[i], nsa_pos_k[i], nsa_pos_v[i], nsa_ck_w1[i], nsa_ck_w2[i],
                           nsa_cv_w1[i], nsa_cv_w2[i], rel_table, banks)
        else:
            w_in = od_w_in[i]
            lo = MLA_Q_RANK + MLA_KV_RANK
            half = MLA_ROPE // 2
            sw = jnp.concatenate([w_in[:, lo + half:lo + MLA_ROPE], w_in[:, lo:lo + half]], axis=1)
            proj = norm_mod_matmul(x, norm_mix[layer], scale_m, shift_m, jnp.concatenate([w_in, sw], axis=1).astype(BF16))
            x = odd_mixer(proj, x, gate_m, od_w_out[i], mla_q_norm[i], mla_kv_norm[i], mla_w_q_up[i], mla_w_kv_up[i],
                          swa_sinks[i], banks)
        h, cw = ffn_pre(x, norm_ffn[layer], scale_f, shift_f, router_w, router_b)
        x = moe_dense(h, cw.transpose(0, 2, 1), x, gate_f, moe_w_gate[layer].astype(BF16), moe_w_up[layer].astype(BF16),
                      moe_w_down[layer].astype(BF16))
    return final_rmsnorm(x, final_norm)
```

```python
import functools
import math

import numpy as np
import jax
import jax.numpy as jnp
from jax import lax
from jax.experimental import pallas as pl
from jax.experimental.pallas import tpu as pltpu

F32 = jnp.float32
BF16 = jnp.bfloat16
HI = lax.Precision.HIGHEST

D_MODEL = 1024
HEAD_DIM = 64
NEG = -1e30
BIG = 1e30
M_INIT = -1e9
NORM_EPS = 1e-6
LOG2E = math.log2(math.e)

N_BUCKETS = 32
MAX_DISTANCE = 1024
N_BIAS_HEADS = 8

NSA_HEADS = 8
NSA_KV_HEADS = 2
NSA_GROUP = NSA_HEADS // NSA_KV_HEADS
CMP_LEN = 32
CMP_STRIDE = 16
CMP_HIDDEN = 256
SLC_BLOCK = 64
SLC_TOPN = 16
NSA_WINDOW = 512

MOBA_HEADS = 8
MOBA_BLOCK = 256
MOBA_TOPK = 3

MLA_HEADS = 4
MLA_Q_RANK = 256
MLA_KV_RANK = 128
MLA_NOPE = 128
MLA_ROPE = 64
MLA_V = 128
ROPE_THETA = 10000.0

SWA_HEADS = 8
SWA_KV_HEADS = 2
SWA_GROUP = SWA_HEADS // SWA_KV_HEADS
SWA_WINDOW = 128

N_EXPERTS = 16
N_GROUPS = 4
EXPERTS_PER_GROUP = N_EXPERTS // N_GROUPS
D_EXPERT = 512

EVEN_WIDTHS = (NSA_HEADS * HEAD_DIM,) + (NSA_KV_HEADS * HEAD_DIM,) * 6 + (3 * NSA_HEADS,) + (MOBA_HEADS * HEAD_DIM,) * 3
ODD_WIDTHS = (MLA_Q_RANK, MLA_KV_RANK, MLA_ROPE, SWA_HEADS * HEAD_DIM, SWA_KV_HEADS * HEAD_DIM, SWA_KV_HEADS * HEAD_DIM)

LANES = 128
VMEM_LIMIT = 56 * 1024 * 1024


def _params(*sem):
    return pltpu.CompilerParams(dimension_semantics=sem, vmem_limit_bytes=VMEM_LIMIT)


def _dot(a, b, precision=None):
    return lax.dot_general(a, b, (((1,), (0,)), ((), ())), precision=precision, preferred_element_type=F32)


def _dot_nt(a, b, precision=None):
    return lax.dot_general(a, b, (((1,), (1,)), ((), ())), precision=precision, preferred_element_type=F32)


def _iota(shape, dim):
    return lax.broadcasted_iota(jnp.int32, shape, dim)


def _offsets(widths):
    out, acc = [], 0
    for w in widths:
        out.append((acc, acc + w))
        acc += w
    return out


def _bucket_thresholds():
    d = np.arange(0, 4 * MAX_DISTANCE, dtype=np.int64)
    exact = N_BUCKETS // 2
    x = np.maximum(d, 1).astype(np.float32) / np.float32(exact)
    logp = exact + (np.log(x) / np.float32(math.log(MAX_DISTANCE / exact)) * np.float32(N_BUCKETS - exact)).astype(np.int32)
    bucket = np.where(d < exact, d, np.minimum(logp, N_BUCKETS - 1))
    return [int(np.argmax(bucket >= b)) for b in range(1, N_BUCKETS)]


BUCKET_THR = _bucket_thresholds()
FAR_DIST = BUCKET_THR[-1]


def _ada_kernel(c_ref, w_ref, b_ref, o_ref):
    c = c_ref[...]
    o_ref[...] = _dot(c * jax.nn.sigmoid(c), w_ref[...], precision=HI) + b_ref[...]


def ada_all(c, ada_w, ada_b):
    depth, d, n = ada_w.shape
    rows = 8
    cp = jnp.pad(c, ((0, rows - c.shape[0]), (0, 0)))
    tn = 1536
    out = pl.pallas_call(
        _ada_kernel,
        grid=(depth, n // tn),
        in_specs=[pl.BlockSpec((rows, d), lambda l, j: (0, 0)),
                  pl.BlockSpec((None, d, tn), lambda l, j: (l, 0, j)),
                  pl.BlockSpec((None, 1, tn), lambda l, j: (l, 0, j))],
        out_specs=pl.BlockSpec((None, rows, tn), lambda l, j: (l, 0, j)),
        out_shape=jax.ShapeDtypeStruct((depth, rows, n), F32),
        compiler_params=_params("parallel", "parallel"),
        name="ada",
    )(cp, ada_w, ada_b.reshape(depth, 1, n))
    return out[:, :c.shape[0], :]


def _norm_mod(x, g, sc, sh):
    y = x * lax.rsqrt(jnp.mean(x * x, axis=-1, keepdims=True) + NORM_EPS) * g
    return y * (1.0 + sc) + sh


def _nmm_kernel(x_ref, g_ref, sc_ref, sh_ref, w_ref, o_ref):
    h = _norm_mod(x_ref[...], g_ref[...], sc_ref[...], sh_ref[...])
    o_ref[...] = _dot(h.astype(BF16), w_ref[...]).astype(o_ref.dtype)


def norm_mod_matmul(x, g, sc, sh, w, tm=512):
    b, s, d = x.shape
    n = w.shape[1]
    return pl.pallas_call(
        _nmm_kernel,
        grid=(b, s // tm),
        in_specs=[pl.BlockSpec((None, tm, d), lambda bi, i: (bi, i, 0)),
                  pl.BlockSpec((1, d), lambda bi, i: (0, 0)),
                  pl.BlockSpec((None, 1, d), lambda bi, i: (bi, 0, 0)),
                  pl.BlockSpec((None, 1, d), lambda bi, i: (bi, 0, 0)),
                  pl.BlockSpec((d, n), lambda bi, i: (0, 0))],
        out_specs=pl.BlockSpec((None, tm, n), lambda bi, i: (bi, i, 0)),
        out_shape=jax.ShapeDtypeStruct((b, s, n), F32),
        compiler_params=_params("parallel", "parallel"),
        name="norm_mod_matmul",
    )(x, g.reshape(1, d), sc.reshape(b, 1, d), sh.reshape(b, 1, d), w)


def _bias_kernel(tab_ref, o_ref, *, rows, cols, step, cstride, c0, key_major=False, window=None):
    sub = 8
    off = pl.program_id(0) * step - c0
    unit = LOG2E if key_major else 1.0
    if key_major:
        base = _iota((sub, cols), 1) - _iota((sub, cols), 0)
    else:
        base = _iota((sub, cols), 0) - cstride * _iota((sub, cols), 1)

    def body(r, carry):
        dist = off + base + (-r * sub if key_major else r * sub)
        vs = [jnp.full((sub, cols), tab_ref[0, h] * unit, F32) for h in range(N_BIAS_HEADS)]
        for b in range(1, N_BUCKETS):
            ge = dist >= BUCKET_THR[b - 1]
            for h in range(N_BIAS_HEADS):
                vs[h] = jnp.where(ge, tab_ref[b, h] * unit, vs[h])
        masked = dist < 0 if window is None else (dist < 0) | (dist >= window)
        for h in range(N_BIAS_HEADS):
            v = jnp.where(masked, NEG, vs[h]) if key_major else vs[h]
            o_ref[h, pl.ds(pl.multiple_of(r * sub, sub), sub), :] = v
        return carry

    lax.fori_loop(0, rows // sub, body, 0)


def bias_bank(rel_table, t, window=None):
    nd = -(-(FAR_DIST + t - 1) // t) if window is None else window // t + 1
    kern = functools.partial(_bias_kernel, rows=t, cols=t, step=t, cstride=1, c0=0, key_major=True, window=window)
    bank = pl.pallas_call(
        kern,
        grid=(nd,),
        in_specs=[pl.BlockSpec(memory_space=pltpu.SMEM)],
        out_specs=pl.BlockSpec((N_BIAS_HEADS, None, t, t), lambda i: (0, i, 0, 0)),
        out_shape=jax.ShapeDtypeStruct((N_BIAS_HEADS, nd, t, t), F32),
        compiler_params=_params("parallel"),
        name="bias_bank",
    )(rel_table)
    return bank


def bias_cmp(rel_table, s, nc, tq=256):
    kern = functools.partial(_bias_kernel, rows=tq, cols=nc, step=tq, cstride=CMP_STRIDE, c0=CMP_LEN - 1)
    return pl.pallas_call(
        kern,
        grid=(s // tq,),
        in_specs=[pl.BlockSpec(memory_space=pltpu.SMEM)],
        out_specs=pl.BlockSpec((N_BIAS_HEADS, tq, nc), lambda i: (0, i, 0)),
        out_shape=jax.ShapeDtypeStruct((N_BIAS_HEADS, s, nc), F32),
        compiler_params=_params("parallel"),
        name="bias_cmp",
    )(rel_table)


def _compress_kernel(x_ref, w1_ref, pos_ref, w2_ref, o_ref):
    x = x_ref[...]
    w1 = w1_ref[...]
    half = w1.shape[0]
    hid = w1.shape[1] // 2
    r = _dot(x, w1, precision=HI)
    pos = pos_ref[...]
    pb = _dot(pos[:, :half], w1[:, :hid], precision=HI) + _dot(pos[:, half:], w1[:, hid:], precision=HI)
    nxt = pltpu.roll(r[:, hid:], x.shape[0] - 1, 0)
    pre = r[:, :hid] + nxt + pb[0:1, :]
    o_ref[...] = _dot(jax.nn.gelu(pre), w2_ref[...], precision=HI)


def nsa_compress(kv, w1, pos, w2):
    two, bh, s, dh = kv.shape
    nc = s // CMP_STRIDE
    half = CMP_STRIDE * dh
    hid = w1.shape[2]
    x = kv.reshape(two, bh, nc, half)
    w1cat = jnp.concatenate([w1[:, :half, :], w1[:, half:, :]], axis=2)
    posf = jnp.broadcast_to(pos.reshape(two, 1, CMP_LEN * dh), (two, 8, CMP_LEN * dh))
    return pl.pallas_call(
        _compress_kernel,
        grid=(two, bh),
        in_specs=[pl.BlockSpec((None, None, nc, half), lambda w, i: (w, i, 0, 0)),
                  pl.BlockSpec((None, half, 2 * hid), lambda w, i: (w, 0, 0)),
                  pl.BlockSpec((None, 8, CMP_LEN * dh), lambda w, i: (w, 0, 0)),
                  pl.BlockSpec((None, hid, dh), lambda w, i: (w, 0, 0))],
        out_specs=pl.BlockSpec((None, None, nc, dh), lambda w, i: (w, i, 0, 0)),
        out_shape=jax.ShapeDtypeStruct((two, bh, nc, dh), F32),
        compiler_params=_params("parallel", "parallel"),
        name="nsa_compress",
    )(x, w1cat, posf, w2)


def _topn_mask(score, index, n):
    idx_f = index.astype(F32)
    sel = jnp.zeros(score.shape, jnp.bool_)
    for _ in range(n):
        m = jnp.max(score, axis=0, keepdims=True)
        first = jnp.min(jnp.where(score == m, idx_f, float(score.shape[0])), axis=0, keepdims=True)
        pick = idx_f == first
        sel = sel | pick
        score = jnp.where(pick, -jnp.inf, score)
    return sel


def _cmp_kernel(q_ref, k_ref, v_ref, b_ref, ov_ref, oc_ref, sel_ref, *, tq, nc, nslc, ntop, group):
    qi = pl.program_id(2)
    t = qi * tq + _iota((tq, 1), 0)
    n = _iota((1, nc), 1)
    mask = (n * CMP_STRIDE + (CMP_LEN - 1) <= t) & (n < nc - 1)
    k = k_ref[...]
    v = v_ref[...].astype(BF16)
    psum = jnp.zeros((tq, nc), F32)
    for g in range(group):
        s = _dot_nt(q_ref[g], k, precision=HI) + b_ref[g]
        s = jnp.where(mask, s, NEG)
        m = jnp.max(s, axis=-1, keepdims=True)
        p = jnp.where(mask, jnp.exp(s - m), 0.0)
        den = jnp.sum(p, axis=-1, keepdims=True)
        p = p * jnp.where(den > 0.0, 1.0 / den, 0.0)
        oc_ref[g] = _dot(p.astype(BF16), v)
        psum = psum + p
    imp = _dot_nt(ov_ref[...], psum, precision=HI)
    j = _iota((nslc, tq), 0)
    blk = jnp.right_shift(qi * tq + _iota((1, tq), 1), int(math.log2(SLC_BLOCK)))
    forced = (j == 0) | (j == blk) | (j == blk - 1)
    valid = j <= blk
    score = jnp.where(forced, BIG, jnp.where(valid, imp, NEG))
    sel = _topn_mask(score, j, ntop) & valid
    sel_ref[0] = jnp.where(sel, 0.0, NEG)


def nsa_cmp_attention(q, k_cmp, v_cmp, bias_c, tq=256):
    b, hk, g, s, dh = q.shape
    nc = k_cmp.shape[2]
    nslc = s // SLC_BLOCK
    ntop = min(SLC_TOPN, nslc)
    cmp_start = np.arange(nc, dtype=np.int64) * CMP_STRIDE
    slc_lo = np.arange(nslc, dtype=np.int64) * SLC_BLOCK
    overlap = ((cmp_start[:, None] <= slc_lo[None, :] + SLC_BLOCK - 1)
               & (cmp_start[:, None] + CMP_LEN - 1 >= slc_lo[None, :])).astype(np.float32)
    kern = functools.partial(_cmp_kernel, tq=tq, nc=nc, nslc=nslc, ntop=ntop, group=g)
    return pl.pallas_call(
        kern,
        grid=(b, hk, s // tq),
        in_specs=[pl.BlockSpec((None, None, g, tq, dh), lambda bi, h, i: (bi, h, 0, i, 0)),
                  pl.BlockSpec((None, None, nc, dh), lambda bi, h, i: (bi, h, 0, 0)),
                  pl.BlockSpec((None, None, nc, dh), lambda bi, h, i: (bi, h, 0, 0)),
                  pl.BlockSpec((g, tq, nc), lambda bi, h, i: (h, i, 0)),
                  pl.BlockSpec((nslc, nc), lambda bi, h, i: (0, 0))],
        out_specs=[pl.BlockSpec((None, None, g, tq, dh), lambda bi, h, i: (bi, h, 0, i, 0)),
                   pl.BlockSpec((None, None, 1, nslc, tq), lambda bi, h, i: (bi, h, 0, 0, i))],
        out_shape=[jax.ShapeDtypeStruct((b, hk, g, s, dh), F32),
                   jax.ShapeDtypeStruct((b, hk, 1, nslc, s), F32)],
        compiler_params=_params("parallel", "parallel", "parallel"),
        name="nsa_cmp_attention",
    )(q, k_cmp, v_cmp, bias_c, jnp.asarray(overlap.T))


def _kmean_kernel(k_ref, o_ref, *, nb, blk):
    k = k_ref[...]
    o_ref[...] = jnp.sum(k.reshape(nb, blk, k.shape[-1]), axis=1) * (1.0 / blk)


def _moba_sel_kernel(q_ref, km_ref, sel_ref, *, tq, nb, ntop):
    qi = pl.program_id(2)
    own = jnp.right_shift(qi * tq + _iota((1, tq), 1), int(math.log2(MOBA_BLOCK)))
    j = _iota((nb, tq), 0)
    gate = _dot_nt(km_ref[...], q_ref[...], precision=HI)
    past = j < own
    sel = _topn_mask(jnp.where(past, gate, NEG), j, ntop) & past
    sel_ref[...] = jnp.where(sel | (j == own), 0.0, NEG)


def moba_select(q, k, tq=256):
    b, h, s, dh = q.shape
    nb = s // MOBA_BLOCK
    ntop = min(MOBA_TOPK, nb)
    kmean = pl.pallas_call(
        functools.partial(_kmean_kernel, nb=nb, blk=MOBA_BLOCK),
        grid=(b, h),
        in_specs=[pl.BlockSpec((None, None, s, dh), lambda bi, hi: (bi, hi, 0, 0))],
        out_specs=pl.BlockSpec((None, None, nb, dh), lambda bi, hi: (bi, hi, 0, 0)),
        out_shape=jax.ShapeDtypeStruct((b, h, nb, dh), F32),
        compiler_params=_params("parallel", "parallel"),
        name="moba_kmean",
    )(k)
    return pl.pallas_call(
        functools.partial(_moba_sel_kernel, tq=tq, nb=nb, ntop=ntop),
        grid=(b, h, s // tq),
        in_specs=[pl.BlockSpec((None, None, tq, dh), lambda bi, hi, i: (bi, hi, i, 0)),
                  pl.BlockSpec((None, None, nb, dh), lambda bi, hi, i: (bi, hi, 0, 0))],
        out_specs=pl.BlockSpec((None, None, nb, tq), lambda bi, hi, i: (bi, hi, 0, i)),
        out_shape=jax.ShapeDtypeStruct((b, h, nb, s), F32),
        compiler_params=_params("parallel", "parallel", "parallel"),
        name="moba_select",
    )(q, kmean)


AUG = 64


def _flash_kernel(qt_ref, kt_ref, *refs, hb, kb, sb, bh, t, tq, kc, nparts, use_sel, nbt, nd):
    pos = 0
    q_refs = refs[pos:pos + nparts]; pos += nparts
    k_refs = refs[pos:pos + nparts]; pos += nparts
    vt_ref = refs[pos]; pos += 1
    sel_ref = None
    if use_sel:
        sel_ref = refs[pos]; pos += 1
    bias_ref, far_ref, o_ref, m_ref, l_ref, acc_ref = refs[pos:pos + 6]

    head0 = pl.program_id(1) * hb
    step = pl.program_id(2)
    qi = qt_ref[step]
    ki = kt_ref[step]

    @pl.when(ki == 0)
    def _():
        m_ref[...] = jnp.full(m_ref.shape, M_INIT, F32)
        l_ref[...] = jnp.zeros(l_ref.shape, F32)
        acc_ref[...] = jnp.zeros(acc_ref.shape, F32)

    nchunk = t // kc
    nlane = t // tq

    def mask_lanes(si, qc):
        rows = sel_ref[si, qc * tq:(qc + 1) * tq, :]
        shift = lax.rem(AUG + LANES - lax.rem(ki * nbt, LANES), LANES)
        lane = _iota((tq, LANES), 1)
        keep = (lane >= AUG) & (lane < AUG + nbt)
        return jnp.where(keep, pltpu.roll(rows, shift, 1), 0.0).astype(BF16)

    def scores(g, qc, near, masks):
        kg = g if kb > 1 else 0
        bg = g if bh > 1 else 0
        ql = slice(qc * tq, (qc + 1) * tq)
        qs = [q_ref[g, ql, :] for q_ref in q_refs]
        if use_sel:
            qs[0] = jnp.where(_iota((tq, LANES), 1) < AUG, qs[0], masks[(g if sb > 1 else 0, qc)])
        chunks, top = [], None
        for c in range(nchunk):
            kr = slice(c * kc, (c + 1) * kc)
            s = _dot_nt(k_refs[0][kg, kr, :], qs[0])
            for p in range(1, nparts):
                s = s + _dot_nt(k_refs[p][kg, kr, :], qs[p])
            if near:
                s = s + bias_ref[bg, kr, ql]
            chunks.append(s)
            top = s if top is None else jnp.maximum(top, s)
        return chunks, top

    def update(g, qc, near, chunks, top):
        kg = g if kb > 1 else 0
        ql = slice(qc * tq, (qc + 1) * tq)
        shift = 0.0 if near else far_ref[head0 + g]
        m = m_ref[g, :, ql]
        m_new = jnp.maximum(m, jnp.max(top, axis=0, keepdims=True) + shift)
        alpha = jnp.exp2(m - m_new)
        base = m_new - shift
        pv = psum = None
        for c in range(nchunk):
            p = jnp.exp2(chunks[c] - base)
            ps = jnp.sum(p, axis=0, keepdims=True)
            pd = _dot(vt_ref[kg, :, c * kc:(c + 1) * kc], p.astype(BF16))
            pv = pd if pv is None else pv + pd
            psum = ps if psum is None else psum + ps
        m_ref[g, :, ql] = m_new
        l_ref[g, :, ql] = alpha * l_ref[g, :, ql] + psum
        acc_ref[g, :, ql] = alpha * acc_ref[g, :, ql] + pv

    def body(near):
        masks = {}
        if use_sel:
            masks = {(si, qc): mask_lanes(si, qc) for si in range(sb) for qc in range(nlane)}
        items = [(g, qc) for g in range(hb) for qc in range(nlane)]
        ahead = 2
        pending = [scores(*item, near, masks) for item in items[:ahead]]
        for i, item in enumerate(items):
            if i + ahead < len(items):
                pending.append(scores(*items[i + ahead], near, masks))
            update(*item, near, *pending.pop(0))

    @pl.when(qi - ki < nd)
    def _():
        body(True)

    @pl.when(qi - ki >= nd)
    def _():
        body(False)

    @pl.when(ki == qi)
    def _():
        for g in range(hb):
            o_ref[g] = jnp.transpose(acc_ref[g] / l_ref[g]).astype(o_ref.dtype)


def flash_attention(q_parts, k_parts, v, bias, far, sel=None, t=512, blk=1, tq=256, kc=128):
    b, ng, hb, s, _ = q_parts[0].shape
    kb = v.shape[2]
    dv = v.shape[-1]
    t = min(t, s)
    tq, kc = min(tq, t), min(kc, t)
    nq = s // t
    nd = bias.shape[1]
    pairs = [(i, j) for i in range(nq) for j in range(i + 1)]
    qt = jnp.asarray(np.array([p[0] for p in pairs], np.int32))
    kt = jnp.asarray(np.array([p[1] for p in pairs], np.int32))
    nparts = len(q_parts)
    use_sel = sel is not None
    sb = sel.shape[2] if use_sel else 1
    bh = hb if bias.shape[0] > 1 else 1
    nbt = t // blk
    vt = jnp.swapaxes(v, -1, -2)
    q_parts, k_parts = list(q_parts), list(k_parts)
    if use_sel:
        onehot = (np.arange(s)[:, None] % t // blk == np.arange(LANES - AUG)[None, :]).astype(np.float32)
        k0 = k_parts[0]
        k_parts[0] = jnp.concatenate(
            [k0, jnp.broadcast_to(jnp.asarray(onehot, k0.dtype), k0.shape[:-1] + (LANES - AUG,))], axis=-1)
        q_parts[0] = jnp.pad(q_parts[0], ((0, 0),) * 4 + ((0, LANES - AUG),))
        selq = jnp.swapaxes(sel, -1, -2)
        selq = jnp.pad(selq, ((0, 0),) * 4 + ((0, LANES - selq.shape[-1]),))

    in_specs, args = [], []
    for qp in q_parts:
        in_specs.append(pl.BlockSpec((None, None, hb, t, qp.shape[-1]), lambda bi, n, st, qt, kt: (bi, n, 0, qt[st], 0)))
        args.append(qp)
    for kp in k_parts:
        in_specs.append(pl.BlockSpec((None, None, kb, t, kp.shape[-1]), lambda bi, n, st, qt, kt: (bi, n, 0, kt[st], 0)))
        args.append(kp)
    in_specs.append(pl.BlockSpec((None, None, kb, dv, t), lambda bi, n, st, qt, kt: (bi, n, 0, 0, kt[st])))
    args.append(vt)
    if use_sel:
        in_specs.append(pl.BlockSpec((None, None, sb, t, LANES), lambda bi, n, st, qt, kt: (bi, n, 0, qt[st], 0)))
        args.append(selq)
    near_tile = lambda qt, kt, st: jnp.minimum(qt[st] - kt[st], nd - 1)
    if bh > 1:
        in_specs.append(pl.BlockSpec((bh, None, t, t), lambda bi, n, st, qt, kt: (n, near_tile(qt, kt, st), 0, 0)))
    else:
        in_specs.append(pl.BlockSpec((1, None, t, t), lambda bi, n, st, qt, kt: (0, near_tile(qt, kt, st), 0, 0)))
    in_specs.append(pl.BlockSpec(memory_space=pltpu.SMEM))
    args += [bias, far]
    kern = functools.partial(_flash_kernel, hb=hb, kb=kb, sb=sb, bh=bh, t=t, tq=tq, kc=kc, nparts=nparts,
                             use_sel=use_sel, nbt=nbt, nd=nd)
    grid_spec = pltpu.PrefetchScalarGridSpec(
        num_scalar_prefetch=2,
        grid=(b, ng, len(pairs)),
        in_specs=in_specs,
        out_specs=pl.BlockSpec((None, None, hb, t, dv), lambda bi, n, st, qt, kt: (bi, n, 0, qt[st], 0)),
        scratch_shapes=[pltpu.VMEM((hb, 1, t), F32), pltpu.VMEM((hb, 1, t), F32), pltpu.VMEM((hb, dv, t), F32)])
    return pl.pallas_call(
        kern,
        grid_spec=grid_spec,
        out_shape=jax.ShapeDtypeStruct((b, ng, hb, s, dv), F32),
        compiler_params=_params("parallel", "parallel", "arbitrary"),
        name="flash_attention",
    )(qt, kt, *args)


def _window_kernel(*refs, group, t, ntile, has_sink):
    q_ref = refs[0]
    k_refs = refs[1:1 + ntile]
    vt_refs = refs[1 + ntile:1 + 2 * ntile]
    bias_ref = refs[1 + 2 * ntile]
    sink_ref = refs[2 + 2 * ntile] if has_sink else None
    o_ref = refs[-1]
    kvh = pl.program_id(1)

    def scores(g):
        q = q_ref[g]
        ss = [_dot_nt(k_refs[r][...], q) + bias_ref[g, ntile - 1 - r] for r in range(ntile)]
        top = ss[0]
        for s in ss[1:]:
            top = jnp.maximum(top, s)
        return ss, top

    def finish(g, ss, top):
        m = jnp.max(top, axis=0, keepdims=True)
        den = None
        if has_sink:
            sink = sink_ref[kvh * group + g] * LOG2E
            m = jnp.maximum(m, sink)
            den = jnp.exp2(sink - m)
        acc = None
        for r in range(ntile):
            p = jnp.exp2(ss[r] - m)
            ps = jnp.sum(p, axis=0, keepdims=True)
            pv = _dot(vt_refs[r][...], p.astype(BF16))
            den = ps if den is None else den + ps
            acc = pv if acc is None else acc + pv
        o_ref[g] = jnp.transpose(acc / den)

    pending = scores(0)
    for g in range(group):
        nxt = scores(g + 1) if g + 1 < group else None
        finish(g, *pending)
        pending = nxt


def window_attention(q, k, v, bank, t, window, sinks=None):
    b, hk, g, s, dh = q.shape
    ntile = window // t + 1
    kp = jnp.pad(k, ((0, 0), (0, 0), (window, 0), (0, 0)))
    vtp = jnp.pad(jnp.swapaxes(v, -1, -2), ((0, 0), (0, 0), (0, 0), (window, 0)))
    has_sink = sinks is not None
    in_specs = [pl.BlockSpec((None, None, g, t, dh), lambda bi, h, i: (bi, h, 0, i, 0))]
    args = [q]
    for r in range(ntile):
        in_specs.append(pl.BlockSpec((None, None, t, dh), lambda bi, h, i, r=r: (bi, h, i + r, 0)))
        args.append(kp)
    for r in range(ntile):
        in_specs.append(pl.BlockSpec((None, None, dh, t), lambda bi, h, i, r=r: (bi, h, 0, i + r)))
        args.append(vtp)
    in_specs.append(pl.BlockSpec((g, ntile, t, t), lambda bi, h, i: (h, 0, 0, 0)))
    args.append(bank)
    if has_sink:
        in_specs.append(pl.BlockSpec(memory_space=pltpu.SMEM))
        args.append(sinks)
    kern = functools.partial(_window_kernel, group=g, t=t, ntile=ntile, has_sink=has_sink)
    return pl.pallas_call(
        kern,
        grid=(b, hk, s // t),
        in_specs=in_specs,
        out_specs=pl.BlockSpec((None, None, g, t, dh), lambda bi, h, i: (bi, h, 0, i, 0)),
        out_shape=jax.ShapeDtypeStruct((b, hk, g, s, dh), F32),
        compiler_params=_params("parallel", "parallel", "parallel"),
        name="window_attention",
    )(*args)


def _rms(x, g):
    return x * lax.rsqrt(jnp.mean(x * x, axis=-1, keepdims=True) + NORM_EPS) * g


def _mla_up_kernel(cq_ref, ckv_ref, kr_ref, krs_ref, qn_ref, kvn_ref, wq_ref, wkv_ref, cs_ref, sn_ref,
                   qnope_ref, qrope_ref, knope_ref, v_ref, krope_ref, *, scale):
    cs = cs_ref[...]
    sn = sn_ref[...]
    q = _dot(_rms(cq_ref[...], qn_ref[...]).astype(BF16), wq_ref[...]) * scale
    kv = _dot(_rms(ckv_ref[...], kvn_ref[...]).astype(BF16), wkv_ref[...])
    wq_head = MLA_NOPE + 2 * MLA_ROPE
    for h in range(MLA_HEADS):
        base = h * wq_head
        qnope_ref[h] = q[:, base:base + MLA_NOPE].astype(qnope_ref.dtype)
        x = q[:, base + MLA_NOPE:base + MLA_NOPE + MLA_ROPE]
        xs = q[:, base + MLA_NOPE + MLA_ROPE:base + wq_head]
        qrope_ref[h] = (x * cs + xs * sn).astype(qrope_ref.dtype)
        kb = h * (MLA_NOPE + MLA_V)
        knope_ref[h] = kv[:, kb:kb + MLA_NOPE].astype(knope_ref.dtype)
        v_ref[h] = kv[:, kb + MLA_NOPE:kb + MLA_NOPE + MLA_V].astype(v_ref.dtype)
    kr = (kr_ref[...] * cs + krs_ref[...] * sn).astype(krope_ref.dtype)
    for h in range(MLA_HEADS):
        krope_ref[h] = kr


def mla_up(c_q, c_kv, k_rope, k_rope_sw, q_norm, kv_norm, w_q_up, w_kv_up, tm=512):
    b, s, _ = c_q.shape
    half = MLA_ROPE // 2
    inv = ROPE_THETA ** (-np.arange(0, MLA_ROPE, 2, dtype=np.float32) / np.float32(MLA_ROPE))
    ang = np.arange(s, dtype=np.float32)[:, None] * inv[None, :].astype(np.float32)
    cos, sin = np.cos(ang).astype(np.float32), np.sin(ang).astype(np.float32)
    cs = jnp.asarray(np.concatenate([cos, cos], axis=1))
    sn = jnp.asarray(np.concatenate([-sin, sin], axis=1))
    dq = MLA_NOPE + MLA_ROPE
    wq = w_q_up.reshape(MLA_Q_RANK, MLA_HEADS, dq)
    rope_cols = wq[:, :, MLA_NOPE:]
    swapped = jnp.concatenate([rope_cols[:, :, half:], rope_cols[:, :, :half]], axis=2)
    wq_aug = jnp.concatenate([wq, swapped], axis=2).reshape(MLA_Q_RANK, MLA_HEADS * (dq + MLA_ROPE)).astype(BF16)
    wkv = w_kv_up.astype(BF16)
    scale = (MLA_NOPE + MLA_ROPE) ** -0.5 * LOG2E
    tm = min(tm, s)
    row = lambda w: pl.BlockSpec((None, tm, w), lambda bi, i: (bi, i, 0))
    full = lambda a: pl.BlockSpec(a.shape, lambda bi, i: (0,) * a.ndim)
    head = lambda w: pl.BlockSpec((None, MLA_HEADS, tm, w), lambda bi, i: (bi, 0, i, 0))
    qn2, kvn2 = q_norm.reshape(1, -1), kv_norm.reshape(1, -1)
    outs = pl.pallas_call(
        functools.partial(_mla_up_kernel, scale=scale),
        grid=(b, s // tm),
        in_specs=[row(MLA_Q_RANK), row(MLA_KV_RANK), row(MLA_ROPE), row(MLA_ROPE), full(qn2), full(kvn2),
                  full(wq_aug), full(wkv),
                  pl.BlockSpec((tm, MLA_ROPE), lambda bi, i: (i, 0)), pl.BlockSpec((tm, MLA_ROPE), lambda bi, i: (i, 0))],
        out_specs=[head(MLA_NOPE), head(MLA_ROPE), head(MLA_NOPE), head(MLA_V), head(MLA_ROPE)],
        out_shape=[jax.ShapeDtypeStruct((b, MLA_HEADS, s, w), BF16)
                   for w in (MLA_NOPE, MLA_ROPE, MLA_NOPE, MLA_V, MLA_ROPE)],
        compiler_params=_params("parallel", "parallel"),
        name="mla_up",
    )(c_q, c_kv, k_rope, k_rope_sw, qn2, kvn2, wq_aug, wkv, cs, sn)
    return outs


def _out_kernel(*refs, gated):
    if gated:
        oc_ref, os_ref, ow_ref, gt_ref, ex_ref, ob_ref, x_ref, gm_ref, w_ref, o_ref = refs
        half = oc_ref.shape[-1]
        ge = _dot(jax.nn.sigmoid(gt_ref[...]), ex_ref[...], precision=HI)
        oa = ge[:, :half] * oc_ref[...] + ge[:, half:2 * half] * os_ref[...] + ge[:, 2 * half:] * ow_ref[...]
    else:
        oa_ref, ob_ref, x_ref, gm_ref, w_ref, o_ref = refs
        half = oa_ref.shape[-1]
        oa = oa_ref[...]
    mix = _dot(oa.astype(BF16), w_ref[:half, :]) + _dot(ob_ref[...].astype(BF16), w_ref[half:, :])
    o_ref[...] = x_ref[...] + gm_ref[...] * mix


def out_project(parts, ob, x, gate_m, w_out, gates=None, tm=512):
    b, s, d = x.shape
    gated = gates is not None
    tm = min(tm, s)
    row = lambda a: pl.BlockSpec((None, tm, a.shape[-1]), lambda bi, i: (bi, i, 0))
    full = lambda a: pl.BlockSpec(a.shape, lambda bi, i: (0,) * a.ndim)
    args, in_specs = [], []
    for p in parts:
        args.append(p); in_specs.append(row(p))
    if gated:
        half = parts[0].shape[-1]
        nh = half // HEAD_DIM
        ex = np.zeros((LANES, 3 * half), np.float32)
        for h in range(nh):
            for br in range(3):
                ex[h * 3 + br, br * half + h * HEAD_DIM: br * half + (h + 1) * HEAD_DIM] = 1.0
        gpad = jnp.pad(gates, ((0, 0), (0, 0), (0, LANES - gates.shape[-1])))
        ex = jnp.asarray(ex)
        args += [gpad, ex]; in_specs += [row(gpad), full(ex)]
    gm = gate_m.reshape(b, 1, d)
    wb = w_out.astype(BF16)
    args += [ob, x, gm, wb]
    in_specs += [row(ob), row(x), pl.BlockSpec((None, 1, d), lambda bi, i: (bi, 0, 0)), full(wb)]
    return pl.pallas_call(
        functools.partial(_out_kernel, gated=gated),
        grid=(b, s // tm),
        in_specs=in_specs,
        out_specs=pl.BlockSpec((None, tm, d), lambda bi, i: (bi, i, 0)),
        out_shape=jax.ShapeDtypeStruct((b, s, d), F32),
        compiler_params=_params("parallel", "parallel"),
        name="out_project",
    )(*args)


ROW_EXTRA = LANES


def _ffn_pre_kernel(x_ref, g_ref, sc_ref, sh_ref, rw_ref, rb_ref, row_ref, grp_ref):
    x = x_ref[...]
    tm, d = x.shape
    h = _norm_mod(x, g_ref[...], sc_ref[...], sh_ref[...])
    row_ref[:, :d] = h
    row_ref[:, d:2 * d] = x
    aff = jax.nn.sigmoid(_dot_nt(rw_ref[...], h, precision=HI))
    biased = aff + rb_ref[...]
    epg = EXPERTS_PER_GROUP
    brow = [biased[e:e + 1, :] for e in range(N_EXPERTS)]
    arow = [aff[e:e + 1, :] for e in range(N_EXPERTS)]
    best = gsel = None
    for gi in range(N_GROUPS):
        a, b_, c, d_ = brow[gi * epg:(gi + 1) * epg]
        hi1, lo1, hi2, lo2 = jnp.maximum(a, b_), jnp.minimum(a, b_), jnp.maximum(c, d_), jnp.minimum(c, d_)
        score = jnp.maximum(hi1, hi2) + jnp.maximum(jnp.minimum(hi1, hi2), jnp.maximum(lo1, lo2))
        if gi == 0:
            best, gsel = score, jnp.zeros(score.shape, jnp.int32)
        else:
            better = score > best
            gsel = jnp.where(better, gi, gsel)
            best = jnp.where(better, score, best)

    def in_group(rows, j):
        v = rows[j]
        for gi in range(1, N_GROUPS):
            v = jnp.where(gsel == gi, rows[gi * epg + j], v)
        return v

    bv = [in_group(brow, j) for j in range(epg)]
    av = [in_group(arow, j) for j in range(epg)]

    def argmax_excluding(skip):
        val = idx = None
        for j in range(epg):
            cand = bv[j] if skip is None else jnp.where(skip == j, -jnp.inf, bv[j])
            if j == 0:
                val, idx = cand, jnp.zeros(cand.shape, jnp.int32)
            else:
                better = cand > val
                idx = jnp.where(better, j, idx)
                val = jnp.where(better, cand, val)
        return idx

    first = argmax_excluding(None)
    second = argmax_excluding(first)

    def pick(rows, idx):
        v = rows[0]
        for j in range(1, epg):
            v = jnp.where(idx == j, rows[j], v)
        return v

    a1, a2 = pick(av, first), pick(av, second)
    tot = a1 + a2
    sub = _iota((ROW_EXTRA, tm), 0)
    extra = (jnp.where(sub == first, a1 / tot, 0.0) + jnp.where(sub == second, a2 / tot, 0.0)
             + jnp.where(sub == epg, pl.program_id(0).astype(F32), 0.0))
    row_ref[:, 2 * d:] = jnp.transpose(extra)
    grp_ref[...] = gsel


def ffn_pre(x, g, sc, sh, router_w, router_b, tm=512):
    b, s, d = x.shape
    e = router_w.shape[1]
    tm = min(tm, s)
    width = 2 * d + ROW_EXTRA
    return pl.pallas_call(
        _ffn_pre_kernel,
        grid=(b, s // tm),
        in_specs=[pl.BlockSpec((None, tm, d), lambda bi, i: (bi, i, 0)),
                  pl.BlockSpec((1, d), lambda bi, i: (0, 0)),
                  pl.BlockSpec((None, 1, d), lambda bi, i: (bi, 0, 0)),
                  pl.BlockSpec((None, 1, d), lambda bi, i: (bi, 0, 0)),
                  pl.BlockSpec((e, d), lambda bi, i: (0, 0)),
                  pl.BlockSpec((e, 1), lambda bi, i: (0, 0))],
        out_specs=[pl.BlockSpec((None, tm, width), lambda bi, i: (bi, i, 0)),
                   pl.BlockSpec((None, 1, tm), lambda bi, i: (bi, 0, i))],
        out_shape=[jax.ShapeDtypeStruct((b, s, width), F32), jax.ShapeDtypeStruct((b, 1, s), jnp.int32)],
        compiler_params=_params("parallel", "parallel"),
        name="ffn_pre",
    )(x, g.reshape(1, d), sc.reshape(b, 1, d), sh.reshape(b, 1, d), router_w.T, router_b.reshape(e, 1))


def _plan_kernel(grp_ref, rank_ref, cnt_ref, carry_ref, *, tb):
    @pl.when(pl.program_id(0) == 0)
    def _():
        carry_ref[...] = jnp.zeros(carry_ref.shape, F32)

    onehot = (_iota((8, tb), 0) == grp_ref[...]).astype(F32)
    upper = (_iota((tb, tb), 0) <= _iota((tb, tb), 1)).astype(BF16)
    cum = _dot(onehot.astype(BF16), upper)
    carry = carry_ref[...]
    before = carry[:, :1] + cum - onehot
    rank_ref[...] = jnp.sum(onehot * before, axis=0, keepdims=True).astype(jnp.int32)
    total = carry + cum[:, tb - 1:tb]
    carry_ref[...] = total
    cnt_ref[...] = total


def _permute_rows_kernel(pos_ref, src_ref, *rest, chunk, scatter):
    dst_ref, sem = rest[-2], rest[-1]
    base = pl.program_id(0) * chunk

    def copy(i):
        p = pos_ref[0, i]
        if scatter:
            return pltpu.make_async_copy(src_ref.at[pl.ds(base + i, 1)], dst_ref.at[pl.ds(p, 1)], sem)
        return pltpu.make_async_copy(src_ref.at[pl.ds(p, 1)], dst_ref.at[pl.ds(base + i, 1)], sem)

    def start(i, carry):
        copy(i).start()
        return carry

    lax.fori_loop(0, chunk, start, 0, unroll=8)
    pltpu.make_async_copy(src_ref.at[pl.ds(0, chunk)], dst_ref.at[pl.ds(0, chunk)], sem).wait()


def permute_rows(src, pos, n_out, scatter, chunk=2048):
    n_idx = pos.shape[0]
    w = src.shape[1]
    chunk = min(chunk, n_idx)
    args = [pos.reshape(n_idx // chunk, 1, chunk), src]
    in_specs = [pl.BlockSpec((None, 1, chunk), lambda i: (i, 0, 0), memory_space=pltpu.SMEM),
                pl.BlockSpec(memory_space=pl.ANY)]
    aliases = {}
    if scatter:
        args.append(jnp.zeros((n_out, w), src.dtype))
        in_specs.append(pl.BlockSpec(memory_space=pl.ANY))
        aliases = {2: 0}
    return pl.pallas_call(
        functools.partial(_permute_rows_kernel, chunk=chunk, scatter=scatter),
        grid=(n_idx // chunk,),
        in_specs=in_specs,
        out_specs=pl.BlockSpec(memory_space=pl.ANY),
        out_shape=jax.ShapeDtypeStruct((n_out, w), src.dtype),
        scratch_shapes=[pltpu.SemaphoreType.DMA(())],
        input_output_aliases=aliases,
        compiler_params=pltpu.CompilerParams(dimension_semantics=("arbitrary",), has_side_effects=True),
        name="permute_rows",
    )(*args)


def _moe_group_kernel(tg_ref, row_ref, gf_ref, wg_ref, wu_ref, wd_ref, o_ref, acc_ref, *, d):
    j = pl.program_id(1)

    @pl.when(j == 0)
    def _():
        acc_ref[...] = jnp.zeros(acc_ref.shape, F32)

    h = row_ref[:, :d].astype(BF16)
    extra = row_ref[:, 2 * d:]
    a = _dot(h, wg_ref[...].astype(BF16))
    u = _dot(h, wu_ref[...].astype(BF16))
    c = jnp.sum(jnp.where(_iota(extra.shape, 1) == j, extra, 0.0), axis=-1, keepdims=True)
    hid = (a * jax.nn.sigmoid(a)) * u * c
    acc_ref[...] += _dot(hid.astype(BF16), wd_ref[...].astype(BF16))

    @pl.when(j == pl.num_programs(1) - 1)
    def _():
        batch = extra[:, EXPERTS_PER_GROUP:EXPERTS_PER_GROUP + 1]
        gate = jnp.zeros(acc_ref.shape, F32)
        for bi in range(gf_ref.shape[0]):
            gate = jnp.where(batch == float(bi), gf_ref[bi:bi + 1, :], gate)
        o_ref[...] = row_ref[:, d:2 * d] + gate * acc_ref[...]


def moe_grouped(rows, grp, gate_f, w_gate, w_up, w_down, layer, tm=512):
    b, s, width = rows.shape
    d = (width - ROW_EXTRA) // 2
    f = w_gate.shape[-1]
    t = b * s
    tb = min(512, t)
    rank, cnt = pl.pallas_call(
        functools.partial(_plan_kernel, tb=tb),
        grid=(t // tb,),
        in_specs=[pl.BlockSpec((1, tb), lambda i: (0, i))],
        out_specs=[pl.BlockSpec((1, tb), lambda i: (0, i)), pl.BlockSpec((8, LANES), lambda i: (0, 0))],
        out_shape=[jax.ShapeDtypeStruct((1, t), jnp.int32), jax.ShapeDtypeStruct((8, LANES), F32)],
        scratch_shapes=[pltpu.VMEM((8, LANES), F32)],
        compiler_params=_params("arbitrary"),
        name="moe_plan",
    )(grp.reshape(1, t))
    grp_flat = grp.reshape(t)
    counts = cnt[:N_GROUPS, 0].astype(jnp.int32)
    padded = (counts + tm - 1) // tm * tm
    pend = jnp.cumsum(padded)
    pos = (pend - padded)[grp_flat] + rank.reshape(t)
    n_tiles = t // tm + N_GROUPS
    tile_group = jnp.minimum(jnp.searchsorted(pend, jnp.arange(n_tiles, dtype=jnp.int32) * tm, side="right"),
                             N_GROUPS - 1).astype(jnp.int32)

    sorted_rows = permute_rows(rows.reshape(t, width), pos, n_tiles * tm, scatter=True)
    epg = EXPERTS_PER_GROUP
    grid_spec = pltpu.PrefetchScalarGridSpec(
        num_scalar_prefetch=1,
        grid=(n_tiles, epg),
        in_specs=[pl.BlockSpec((tm, width), lambda i, j, tg: (i, 0)),
                  pl.BlockSpec(gate_f.shape, lambda i, j, tg: (0, 0)),
                  pl.BlockSpec((None, None, d, f), lambda i, j, tg: (layer, tg[i] * epg + j, 0, 0)),
                  pl.BlockSpec((None, None, d, f), lambda i, j, tg: (layer, tg[i] * epg + j, 0, 0)),
                  pl.BlockSpec((None, None, f, d), lambda i, j, tg: (layer, tg[i] * epg + j, 0, 0))],
        out_specs=pl.BlockSpec((tm, d), lambda i, j, tg: (i, 0)),
        scratch_shapes=[pltpu.VMEM((tm, d), F32)])
    out_sorted = pl.pallas_call(
        functools.partial(_moe_group_kernel, d=d),
        grid_spec=grid_spec,
        out_shape=jax.ShapeDtypeStruct((n_tiles * tm, d), F32),
        compiler_params=_params("parallel", "arbitrary"),
        name="moe_grouped",
    )(tile_group, sorted_rows, gate_f, w_gate, w_up, w_down)
    return permute_rows(out_sorted, pos, t, scatter=False).reshape(b, s, d)


def _final_norm_kernel(x_ref, g_ref, o_ref):
    o_ref[...] = _rms(x_ref[...], g_ref[...])


def final_rmsnorm(x, g, tm=512):
    b, s, d = x.shape
    tm = min(tm, s)
    return pl.pallas_call(
        _final_norm_kernel,
        grid=(b, s // tm),
        in_specs=[pl.BlockSpec((None, tm, d), lambda bi, i: (bi, i, 0)), pl.BlockSpec((1, d), lambda bi, i: (0, 0))],
        out_specs=pl.BlockSpec((None, tm, d), lambda bi, i: (bi, i, 0)),
        out_shape=jax.ShapeDtypeStruct((b, s, d), F32),
        compiler_params=_params("parallel", "parallel"),
        name="final_rmsnorm",
    )(x, g.reshape(1, d))


def _heads(x, nh):
    b, s, w = x.shape
    return x.reshape(b, s, nh, w // nh).transpose(0, 2, 1, 3)


def _merge(x):
    b, ng, hb, s, dh = x.shape
    return x.transpose(0, 3, 1, 2, 4).reshape(b, s, ng * hb * dh)


def even_mixer(proj, x, gate_m, w_out, pos_k, pos_v, ck_w1, ck_w2, cv_w1, cv_w2, rel_table, banks):
    b, s, _ = proj.shape
    off = _offsets(EVEN_WIDTHS)
    q_a, kc, vc, ks, vs, kw, vw, gates, q_b, k_b, v_b = (proj[..., lo:hi] for lo, hi in off)
    hk, g, dh = NSA_KV_HEADS, NSA_GROUP, HEAD_DIM
    scale = dh ** -0.5
    qa = (_heads(q_a, NSA_HEADS) * scale).reshape(b, hk, g, s, dh)
    qa16_log2 = (qa * LOG2E).astype(BF16)
    (bank_l, far_l, t_l), (bank_w, t_w) = banks["dense"], banks["nsa_window"]

    nc = s // CMP_STRIDE
    kv = jnp.stack([_heads(kc, hk), _heads(vc, hk)]).reshape(2, b * hk, s, dh)
    cmp = nsa_compress(kv, jnp.stack([ck_w1, cv_w1]), jnp.stack([pos_k, pos_v]), jnp.stack([ck_w2, cv_w2]))
    cmp = cmp.reshape(2, b, hk, nc, dh)
    o_c, sel = nsa_cmp_attention(qa, cmp[0], cmp[1], bias_cmp(rel_table, s, nc, tq=min(256, s)), tq=min(256, s))
    ks5 = _heads(ks, hk).astype(BF16)[:, :, None]
    vs5 = _heads(vs, hk).astype(BF16)[:, :, None]
    o_s = flash_attention([qa16_log2], [ks5], vs5, bank_l, far_l, sel=sel, t=t_l, blk=SLC_BLOCK)
    o_w = window_attention(qa16_log2, _heads(kw, hk).astype(BF16), _heads(vw, hk).astype(BF16), bank_w, t_w, NSA_WINDOW)

    hb = 4
    qb = _heads(q_b, MOBA_HEADS)
    kb = _heads(k_b, MOBA_HEADS)
    vb = _heads(v_b, MOBA_HEADS)
    selb = moba_select(qb, kb, tq=min(2048, s))
    ngb = MOBA_HEADS // hb
    r5 = lambda a: a.reshape(b, ngb, hb, a.shape[-2], a.shape[-1])
    o_b = flash_attention([r5((qb * (scale * LOG2E)).astype(BF16))], [r5(kb.astype(BF16))], r5(vb.astype(BF16)),
                          bank_l, far_l, sel=r5(selb), t=t_l, blk=MOBA_BLOCK)

    return out_project([_merge(o_c), _merge(o_s), _merge(o_w)], _merge(o_b), x, gate_m, w_out, gates=gates)


def odd_mixer(proj, x, gate_m, w_out, q_norm, kv_norm, w_q_up, w_kv_up, sinks, banks):
    b, s, _ = proj.shape
    off = _offsets(ODD_WIDTHS + (MLA_ROPE,))
    c_q, c_kv, k_rope, q_d, k_d, v_d, k_rope_sw = (proj[..., lo:hi] for lo, hi in off)
    qn, qr, kn, v, kr = mla_up(c_q, c_kv, k_rope, k_rope_sw, q_norm, kv_norm, w_q_up, w_kv_up)
    hb = 2
    ng = MLA_HEADS // hb
    r5 = lambda a: a.reshape(b, ng, hb, s, a.shape[-1])
    bank_c, far_c, t_c = banks["causal"]
    q_full = jnp.concatenate([qn, qr], axis=-1)
    k_full = jnp.concatenate([kn, kr], axis=-1)
    o_c = flash_attention([r5(q_full)], [r5(k_full)], r5(v), bank_c, far_c, t=t_c)

    hk, g, dh = SWA_KV_HEADS, SWA_GROUP, HEAD_DIM
    scale = dh ** -0.5
    qd = (_heads(q_d, SWA_HEADS) * (scale * LOG2E)).reshape(b, hk, g, s, dh).astype(BF16)
    bank_s, t_s = banks["swa"]
    o_d = window_attention(qd, _heads(k_d, hk).astype(BF16), _heads(v_d, hk).astype(BF16), bank_s, t_s, SWA_WINDOW,
                           sinks=sinks)
    return out_project([_merge(o_c)], _merge(o_d), x, gate_m, w_out)


def kernel(x, c, rel_table, router_w, router_b, final_norm, norm_mix, norm_ffn, ada_w, ada_b, moe_w_gate, moe_w_up, moe_w_down, ev_w_in, ev_w_out, nsa_pos_k, nsa_pos_v, nsa_ck_w1, nsa_ck_w2, nsa_cv_w1, nsa_cv_w2, od_w_in, od_w_out, mla_q_norm, mla_kv_norm, mla_w_q_up, mla_w_kv_up, swa_sinks):
    b, s, d = x.shape
    depth = ada_w.shape[0]
    mods = ada_all(c, ada_w, ada_b)
    t_dense = min(512, s)
    bank_l = bias_bank(rel_table, t_dense)
    far_l = rel_table[N_BUCKETS - 1] * LOG2E
    future = np.arange(t_dense)[:, None] > np.arange(t_dense)[None, :]
    bank_c = jnp.asarray(np.where(future, NEG, 0.0).astype(np.float32)[None, None])
    t_w = min(256, s)
    banks = {"dense": (bank_l, far_l, t_dense), "causal": (bank_c, jnp.zeros((N_BIAS_HEADS,), F32), t_dense),
             "nsa_window": (bias_bank(rel_table, t_w, window=NSA_WINDOW), t_w),
             "swa": (bias_bank(rel_table, SWA_WINDOW, window=SWA_WINDOW), SWA_WINDOW)}

    for layer in range(depth):
        shift_m, scale_m, gate_m, shift_f, scale_f, gate_f = jnp.split(mods[layer], 6, axis=-1)
        i = layer // 2
        if layer % 2 == 0:
            w_in = ev_w_in[i]
            pad = -w_in.shape[1] % LANES
            proj = norm_mod_matmul(x, norm_mix[layer], scale_m, shift_m, jnp.pad(w_in, ((0, 0), (0, pad))).astype(BF16))
            x = even_mixer(proj, x, gate_m, ev_w_out[i], nsa_pos_k[i], nsa_pos_v[i], nsa_ck_w1[i], nsa_ck_w2[i],
                           nsa_cv_w1[i], nsa_cv_w2[i], rel_table, banks)
        else:
            w_in = od_w_in[i]
            lo = MLA_Q_RANK + MLA_KV_RANK
            half = MLA_ROPE // 2
            sw = jnp.concatenate([w_in[:, lo + half:lo + MLA_ROPE], w_in[:, lo:lo + half]], axis=1)
            proj = norm_mod_matmul(x, norm_mix[layer], scale_m, shift_m, jnp.concatenate([w_in, sw], axis=1).astype(BF16))
            x = odd_mixer(proj, x, gate_m, od_w_out[i], mla_q_norm[i], mla_kv_norm[i], mla_w_q_up[i], mla_w_kv_up[i],
                          swa_sinks[i], banks)
        rows, grp = ffn_pre(x, norm_ffn[layer], scale_f, shift_f, router_w, router_b)
        x = moe_grouped(rows, grp, gate_f, moe_w_gate, moe_w_up, moe_w_down, layer)
    return final_rmsnorm(x, final_norm)
```

```python
import functools
import math

import numpy as np
import jax
import jax.numpy as jnp
from jax import lax
from jax.experimental import pallas as pl
from jax.experimental.pallas import tpu as pltpu

F32 = jnp.float32
BF16 = jnp.bfloat16
HI = lax.Precision.HIGHEST

D_MODEL = 1024
HEAD_DIM = 64
NEG = -1e30
BIG = 1e30
M_INIT = -1e9
NORM_EPS = 1e-6
LOG2E = math.log2(math.e)

N_BUCKETS = 32
MAX_DISTANCE = 1024
N_BIAS_HEADS = 8

NSA_HEADS = 8
NSA_KV_HEADS = 2
NSA_GROUP = NSA_HEADS // NSA_KV_HEADS
CMP_LEN = 32
CMP_STRIDE = 16
CMP_HIDDEN = 256
SLC_BLOCK = 64
SLC_TOPN = 16
NSA_WINDOW = 512

MOBA_HEADS = 8
MOBA_BLOCK = 256
MOBA_TOPK = 3

MLA_HEADS = 4
MLA_Q_RANK = 256
MLA_KV_RANK = 128
MLA_NOPE = 128
MLA_ROPE = 64
MLA_V = 128
ROPE_THETA = 10000.0

SWA_HEADS = 8
SWA_KV_HEADS = 2
SWA_GROUP = SWA_HEADS // SWA_KV_HEADS
SWA_WINDOW = 128

N_EXPERTS = 16
N_GROUPS = 4
EXPERTS_PER_GROUP = N_EXPERTS // N_GROUPS
D_EXPERT = 512

EVEN_WIDTHS = (NSA_HEADS * HEAD_DIM,) + (NSA_KV_HEADS * HEAD_DIM,) * 6 + (3 * NSA_HEADS,) + (MOBA_HEADS * HEAD_DIM,) * 3
ODD_WIDTHS = (MLA_Q_RANK, MLA_KV_RANK, MLA_ROPE, SWA_HEADS * HEAD_DIM, SWA_KV_HEADS * HEAD_DIM, SWA_KV_HEADS * HEAD_DIM)

LANES = 128
VMEM_LIMIT = 56 * 1024 * 1024


def _params(*sem):
    return pltpu.CompilerParams(dimension_semantics=sem, vmem_limit_bytes=VMEM_LIMIT)


def _dot(a, b, precision=None):
    return lax.dot_general(a, b, (((1,), (0,)), ((), ())), precision=precision, preferred_element_type=F32)


def _dot_nt(a, b, precision=None):
    return lax.dot_general(a, b, (((1,), (1,)), ((), ())), precision=precision, preferred_element_type=F32)


def _iota(shape, dim):
    return lax.broadcasted_iota(jnp.int32, shape, dim)


def _offsets(widths):
    out, acc = [], 0
    for w in widths:
        out.append((acc, acc + w))
        acc += w
    return out


def _bucket_thresholds():
    d = np.arange(0, 4 * MAX_DISTANCE, dtype=np.int64)
    exact = N_BUCKETS // 2
    x = np.maximum(d, 1).astype(np.float32) / np.float32(exact)
    logp = exact + (np.log(x) / np.float32(math.log(MAX_DISTANCE / exact)) * np.float32(N_BUCKETS - exact)).astype(np.int32)
    bucket = np.where(d < exact, d, np.minimum(logp, N_BUCKETS - 1))
    return [int(np.argmax(bucket >= b)) for b in range(1, N_BUCKETS)]


BUCKET_THR = _bucket_thresholds()
FAR_DIST = BUCKET_THR[-1]


def _ada_kernel(c_ref, w_ref, b_ref, o_ref):
    c = c_ref[...]
    o_ref[...] = _dot(c * jax.nn.sigmoid(c), w_ref[...], precision=HI) + b_ref[...]


def ada_all(c, ada_w, ada_b):
    depth, d, n = ada_w.shape
    rows = 8
    cp = jnp.pad(c, ((0, rows - c.shape[0]), (0, 0)))
    tn = 1536
    out = pl.pallas_call(
        _ada_kernel,
        grid=(depth, n // tn),
        in_specs=[pl.BlockSpec((rows, d), lambda l, j: (0, 0)),
                  pl.BlockSpec((None, d, tn), lambda l, j: (l, 0, j)),
                  pl.BlockSpec((None, 1, tn), lambda l, j: (l, 0, j))],
        out_specs=pl.BlockSpec((None, rows, tn), lambda l, j: (l, 0, j)),
        out_shape=jax.ShapeDtypeStruct((depth, rows, n), F32),
        compiler_params=_params("parallel", "parallel"),
        name="ada",
    )(cp, ada_w, ada_b.reshape(depth, 1, n))
    return out[:, :c.shape[0], :]


def _norm_mod(x, g, sc, sh):
    y = x * lax.rsqrt(jnp.mean(x * x, axis=-1, keepdims=True) + NORM_EPS) * g
    return y * (1.0 + sc) + sh


def _nmm_kernel(x_ref, g_ref, sc_ref, sh_ref, w_ref, o_ref):
    h = _norm_mod(x_ref[...], g_ref[...], sc_ref[...], sh_ref[...])
    o_ref[...] = _dot(h.astype(BF16), w_ref[...]).astype(o_ref.dtype)


def norm_mod_matmul(x, g, sc, sh, w, tm=512):
    b, s, d = x.shape
    n = w.shape[1]
    return pl.pallas_call(
        _nmm_kernel,
        grid=(b, s // tm),
        in_specs=[pl.BlockSpec((None, tm, d), lambda bi, i: (bi, i, 0)),
                  pl.BlockSpec((1, d), lambda bi, i: (0, 0)),
                  pl.BlockSpec((None, 1, d), lambda bi, i: (bi, 0, 0)),
                  pl.BlockSpec((None, 1, d), lambda bi, i: (bi, 0, 0)),
                  pl.BlockSpec((d, n), lambda bi, i: (0, 0))],
        out_specs=pl.BlockSpec((None, tm, n), lambda bi, i: (bi, i, 0)),
        out_shape=jax.ShapeDtypeStruct((b, s, n), F32),
        compiler_params=_params("parallel", "parallel"),
        name="norm_mod_matmul",
    )(x, g.reshape(1, d), sc.reshape(b, 1, d), sh.reshape(b, 1, d), w)


def _bias_kernel(tab_ref, o_ref, *, rows, cols, step, cstride, c0, key_major=False, window=None):
    sub = 8
    off = pl.program_id(0) * step - c0
    unit = LOG2E if key_major else 1.0
    if key_major:
        base = _iota((sub, cols), 1) - _iota((sub, cols), 0)
    else:
        base = _iota((sub, cols), 0) - cstride * _iota((sub, cols), 1)

    def body(r, carry):
        dist = off + base + (-r * sub if key_major else r * sub)
        vs = [jnp.full((sub, cols), tab_ref[0, h] * unit, F32) for h in range(N_BIAS_HEADS)]
        for b in range(1, N_BUCKETS):
            ge = dist >= BUCKET_THR[b - 1]
            for h in range(N_BIAS_HEADS):
                vs[h] = jnp.where(ge, tab_ref[b, h] * unit, vs[h])
        masked = dist < 0 if window is None else (dist < 0) | (dist >= window)
        for h in range(N_BIAS_HEADS):
            v = jnp.where(masked, NEG, vs[h]) if key_major else vs[h]
            o_ref[h, pl.ds(pl.multiple_of(r * sub, sub), sub), :] = v
        return carry

    lax.fori_loop(0, rows // sub, body, 0)


def bias_bank(rel_table, t, window=None):
    nd = -(-(FAR_DIST + t - 1) // t) if window is None else window // t + 1
    kern = functools.partial(_bias_kernel, rows=t, cols=t, step=t, cstride=1, c0=0, key_major=True, window=window)
    bank = pl.pallas_call(
        kern,
        grid=(nd,),
        in_specs=[pl.BlockSpec(memory_space=pltpu.SMEM)],
        out_specs=pl.BlockSpec((N_BIAS_HEADS, None, t, t), lambda i: (0, i, 0, 0)),
        out_shape=jax.ShapeDtypeStruct((N_BIAS_HEADS, nd, t, t), F32),
        compiler_params=_params("parallel"),
        name="bias_bank",
    )(rel_table)
    return bank


def bias_cmp(rel_table, s, nc, tq=256):
    kern = functools.partial(_bias_kernel, rows=tq, cols=nc, step=tq, cstride=CMP_STRIDE, c0=CMP_LEN - 1)
    return pl.pallas_call(
        kern,
        grid=(s // tq,),
        in_specs=[pl.BlockSpec(memory_space=pltpu.SMEM)],
        out_specs=pl.BlockSpec((N_BIAS_HEADS, tq, nc), lambda i: (0, i, 0)),
        out_shape=jax.ShapeDtypeStruct((N_BIAS_HEADS, s, nc), F32),
        compiler_params=_params("parallel"),
        name="bias_cmp",
    )(rel_table)


def _compress_kernel(x_ref, w1_ref, pos_ref, w2_ref, o_ref):
    x = x_ref[...]
    w1 = w1_ref[...]
    half = w1.shape[0]
    hid = w1.shape[1] // 2
    r = _dot(x, w1, precision=HI)
    pos = pos_ref[...]
    pb = _dot(pos[:, :half], w1[:, :hid], precision=HI) + _dot(pos[:, half:], w1[:, hid:], precision=HI)
    nxt = pltpu.roll(r[:, hid:], x.shape[0] - 1, 0)
    pre = r[:, :hid] + nxt + pb[0:1, :]
    o_ref[...] = _dot(jax.nn.gelu(pre), w2_ref[...], precision=HI)


def nsa_compress(kv, w1, pos, w2):
    two, bh, s, dh = kv.shape
    nc = s // CMP_STRIDE
    half = CMP_STRIDE * dh
    hid = w1.shape[2]
    x = kv.reshape(two, bh, nc, half)
    w1cat = jnp.concatenate([w1[:, :half, :], w1[:, half:, :]], axis=2)
    posf = jnp.broadcast_to(pos.reshape(two, 1, CMP_LEN * dh), (two, 8, CMP_LEN * dh))
    return pl.pallas_call(
        _compress_kernel,
        grid=(two, bh),
        in_specs=[pl.BlockSpec((None, None, nc, half), lambda w, i: (w, i, 0, 0)),
                  pl.BlockSpec((None, half, 2 * hid), lambda w, i: (w, 0, 0)),
                  pl.BlockSpec((None, 8, CMP_LEN * dh), lambda w, i: (w, 0, 0)),
                  pl.BlockSpec((None, hid, dh), lambda w, i: (w, 0, 0))],
        out_specs=pl.BlockSpec((None, None, nc, dh), lambda w, i: (w, i, 0, 0)),
        out_shape=jax.ShapeDtypeStruct((two, bh, nc, dh), F32),
        compiler_params=_params("parallel", "parallel"),
        name="nsa_compress",
    )(x, w1cat, posf, w2)


def _topn_mask(score, index, n):
    idx_f = index.astype(F32)
    sel = jnp.zeros(score.shape, jnp.bool_)
    for _ in range(n):
        m = jnp.max(score, axis=0, keepdims=True)
        first = jnp.min(jnp.where(score == m, idx_f, float(score.shape[0])), axis=0, keepdims=True)
        pick = idx_f == first
        sel = sel | pick
        score = jnp.where(pick, -jnp.inf, score)
    return sel


def _cmp_kernel(q_ref, k_ref, v_ref, b_ref, ov_ref, oc_ref, sel_ref, *, tq, nc, nslc, ntop, group):
    qi = pl.program_id(2)
    t = qi * tq + _iota((tq, 1), 0)
    n = _iota((1, nc), 1)
    mask = (n * CMP_STRIDE + (CMP_LEN - 1) <= t) & (n < nc - 1)
    k = k_ref[...]
    v = v_ref[...].astype(BF16)
    psum = jnp.zeros((tq, nc), F32)
    for g in range(group):
        s = _dot_nt(q_ref[g], k, precision=HI) + b_ref[g]
        s = jnp.where(mask, s, NEG)
        m = jnp.max(s, axis=-1, keepdims=True)
        p = jnp.where(mask, jnp.exp(s - m), 0.0)
        den = jnp.sum(p, axis=-1, keepdims=True)
        p = p * jnp.where(den > 0.0, 1.0 / den, 0.0)
        oc_ref[g] = _dot(p.astype(BF16), v)
        psum = psum + p
    imp = _dot_nt(ov_ref[...], psum, precision=HI)
    j = _iota((nslc, tq), 0)
    blk = jnp.right_shift(qi * tq + _iota((1, tq), 1), int(math.log2(SLC_BLOCK)))
    forced = (j == 0) | (j == blk) | (j == blk - 1)
    valid = j <= blk
    score = jnp.where(forced, BIG, jnp.where(valid, imp, NEG))
    sel = _topn_mask(score, j, ntop) & valid
    sel_ref[0] = jnp.where(sel, 0.0, NEG)


def nsa_cmp_attention(q, k_cmp, v_cmp, bias_c, tq=256):
    b, hk, g, s, dh = q.shape
    nc = k_cmp.shape[2]
    nslc = s // SLC_BLOCK
    ntop = min(SLC_TOPN, nslc)
    cmp_start = np.arange(nc, dtype=np.int64) * CMP_STRIDE
    slc_lo = np.arange(nslc, dtype=np.int64) * SLC_BLOCK
    overlap = ((cmp_start[:, None] <= slc_lo[None, :] + SLC_BLOCK - 1)
               & (cmp_start[:, None] + CMP_LEN - 1 >= slc_lo[None, :])).astype(np.float32)
    kern = functools.partial(_cmp_kernel, tq=tq, nc=nc, nslc=nslc, ntop=ntop, group=g)
    return pl.pallas_call(
        kern,
        grid=(b, hk, s // tq),
        in_specs=[pl.BlockSpec((None, None, g, tq, dh), lambda bi, h, i: (bi, h, 0, i, 0)),
                  pl.BlockSpec((None, None, nc, dh), lambda bi, h, i: (bi, h, 0, 0)),
                  pl.BlockSpec((None, None, nc, dh), lambda bi, h, i: (bi, h, 0, 0)),
                  pl.BlockSpec((g, tq, nc), lambda bi, h, i: (h, i, 0)),
                  pl.BlockSpec((nslc, nc), lambda bi, h, i: (0, 0))],
        out_specs=[pl.BlockSpec((None, None, g, tq, dh), lambda bi, h, i: (bi, h, 0, i, 0)),
                   pl.BlockSpec((None, None, 1, nslc, tq), lambda bi, h, i: (bi, h, 0, 0, i))],
        out_shape=[jax.ShapeDtypeStruct((b, hk, g, s, dh), F32),
                   jax.ShapeDtypeStruct((b, hk, 1, nslc, s), F32)],
        compiler_params=_params("parallel", "parallel", "parallel"),
        name="nsa_cmp_attention",
    )(q, k_cmp, v_cmp, bias_c, jnp.asarray(overlap.T))


def _kmean_kernel(k_ref, o_ref, *, nb, blk):
    k = k_ref[...]
    o_ref[...] = jnp.sum(k.reshape(nb, blk, k.shape[-1]), axis=1) * (1.0 / blk)


def _moba_sel_kernel(q_ref, km_ref, sel_ref, *, tq, nb, ntop):
    qi = pl.program_id(2)
    own = jnp.right_shift(qi * tq + _iota((1, tq), 1), int(math.log2(MOBA_BLOCK)))
    j = _iota((nb, tq), 0)
    gate = _dot_nt(km_ref[...], q_ref[...], precision=HI)
    past = j < own
    sel = _topn_mask(jnp.where(past, gate, NEG), j, ntop) & past
    sel_ref[...] = jnp.where(sel | (j == own), 0.0, NEG)


def moba_select(q, k, tq=256):
    b, h, s, dh = q.shape
    nb = s // MOBA_BLOCK
    ntop = min(MOBA_TOPK, nb)
    kmean = pl.pallas_call(
        functools.partial(_kmean_kernel, nb=nb, blk=MOBA_BLOCK),
        grid=(b, h),
        in_specs=[pl.BlockSpec((None, None, s, dh), lambda bi, hi: (bi, hi, 0, 0))],
        out_specs=pl.BlockSpec((None, None, nb, dh), lambda bi, hi: (bi, hi, 0, 0)),
        out_shape=jax.ShapeDtypeStruct((b, h, nb, dh), F32),
        compiler_params=_params("parallel", "parallel"),
        name="moba_kmean",
    )(k)
    return pl.pallas_call(
        functools.partial(_moba_sel_kernel, tq=tq, nb=nb, ntop=ntop),
        grid=(b, h, s // tq),
        in_specs=[pl.BlockSpec((None, None, tq, dh), lambda bi, hi, i: (bi, hi, i, 0)),
                  pl.BlockSpec((None, None, nb, dh), lambda bi, hi, i: (bi, hi, 0, 0))],
        out_specs=pl.BlockSpec((None, None, nb, tq), lambda bi, hi, i: (bi, hi, 0, i)),
        out_shape=jax.ShapeDtypeStruct((b, h, nb, s), F32),
        compiler_params=_params("parallel", "parallel", "parallel"),
        name="moba_select",
    )(q, kmean)


AUG = 64


def _flash_kernel(qt_ref, kt_ref, *refs, hb, kb, sb, bh, t, tq, kc, nparts, use_sel, nbt, nd):
    pos = 0
    q_refs = refs[pos:pos + nparts]; pos += nparts
    k_refs = refs[pos:pos + nparts]; pos += nparts
    vt_ref = refs[pos]; pos += 1
    sel_ref = None
    if use_sel:
        sel_ref = refs[pos]; pos += 1
    bias_ref, far_ref, o_ref, m_ref, l_ref, acc_ref = refs[pos:pos + 6]

    head0 = pl.program_id(1) * hb
    step = pl.program_id(2)
    qi = qt_ref[step]
    ki = kt_ref[step]

    @pl.when(ki == 0)
    def _():
        m_ref[...] = jnp.full(m_ref.shape, M_INIT, F32)
        l_ref[...] = jnp.zeros(l_ref.shape, F32)
        acc_ref[...] = jnp.zeros(acc_ref.shape, F32)

    nchunk = t // kc
    nlane = t // tq

    def mask_lanes(si, qc):
        rows = sel_ref[si, qc * tq:(qc + 1) * tq, :]
        shift = lax.rem(AUG + LANES - lax.rem(ki * nbt, LANES), LANES)
        lane = _iota((tq, LANES), 1)
        keep = (lane >= AUG) & (lane < AUG + nbt)
        return jnp.where(keep, pltpu.roll(rows, shift, 1), 0.0).astype(BF16)

    def scores(g, qc, near, masks):
        kg = g if kb > 1 else 0
        bg = g if bh > 1 else 0
        ql = slice(qc * tq, (qc + 1) * tq)
        qs = [q_ref[g, ql, :] for q_ref in q_refs]
        if use_sel:
            qs[0] = jnp.where(_iota((tq, LANES), 1) < AUG, qs[0], masks[(g if sb > 1 else 0, qc)])
        chunks, top = [], None
        for c in range(nchunk):
            kr = slice(c * kc, (c + 1) * kc)
            s = _dot_nt(k_refs[0][kg, kr, :], qs[0])
            for p in range(1, nparts):
                s = s + _dot_nt(k_refs[p][kg, kr, :], qs[p])
            if near:
                s = s + bias_ref[bg, kr, ql]
            chunks.append(s)
            top = s if top is None else jnp.maximum(top, s)
        return chunks, top

    def update(g, qc, near, chunks, top):
        kg = g if kb > 1 else 0
        ql = slice(qc * tq, (qc + 1) * tq)
        shift = 0.0 if near else far_ref[head0 + g]
        m = m_ref[g, :, ql]
        m_new = jnp.maximum(m, jnp.max(top, axis=0, keepdims=True) + shift)
        alpha = jnp.exp2(m - m_new)
        base = m_new - shift
        pv = psum = None
        for c in range(nchunk):
            p = jnp.exp2(chunks[c] - base)
            ps = jnp.sum(p, axis=0, keepdims=True)
            pd = _dot(vt_ref[kg, :, c * kc:(c + 1) * kc], p.astype(BF16))
            pv = pd if pv is None else pv + pd
            psum = ps if psum is None else psum + ps
        m_ref[g, :, ql] = m_new
        l_ref[g, :, ql] = alpha * l_ref[g, :, ql] + psum
        acc_ref[g, :, ql] = alpha * acc_ref[g, :, ql] + pv

    def body(near):
        masks = {}
        if use_sel:
            masks = {(si, qc): mask_lanes(si, qc) for si in range(sb) for qc in range(nlane)}
        items = [(g, qc) for g in range(hb) for qc in range(nlane)]
        ahead = 2
        pending = [scores(*item, near, masks) for item in items[:ahead]]
        for i, item in enumerate(items):
            if i + ahead < len(items):
                pending.append(scores(*items[i + ahead], near, masks))
            update(*item, near, *pending.pop(0))

    @pl.when(qi - ki < nd)
    def _():
        body(True)

    @pl.when(qi - ki >= nd)
    def _():
        body(False)

    @pl.when(ki == qi)
    def _():
        for g in range(hb):
            o_ref[g] = jnp.transpose(acc_ref[g] / l_ref[g]).astype(o_ref.dtype)


def flash_attention(q_parts, k_parts, v, bias, far, sel=None, t=512, blk=1, tq=256, kc=128):
    b, ng, hb, s, _ = q_parts[0].shape
    kb = v.shape[2]
    dv = v.shape[-1]
    t = min(t, s)
    tq, kc = min(tq, t), min(kc, t)
    nq = s // t
    nd = bias.shape[1]
    pairs = [(i, j) for i in range(nq) for j in range(i + 1)]
    qt = jnp.asarray(np.array([p[0] for p in pairs], np.int32))
    kt = jnp.asarray(np.array([p[1] for p in pairs], np.int32))
    nparts = len(q_parts)
    use_sel = sel is not None
    sb = sel.shape[2] if use_sel else 1
    bh = hb if bias.shape[0] > 1 else 1
    nbt = t // blk
    vt = jnp.swapaxes(v, -1, -2)
    q_parts, k_parts = list(q_parts), list(k_parts)
    if use_sel:
        onehot = (np.arange(s)[:, None] % t // blk == np.arange(LANES - AUG)[None, :]).astype(np.float32)
        k0 = k_parts[0]
        k_parts[0] = jnp.concatenate(
            [k0, jnp.broadcast_to(jnp.asarray(onehot, k0.dtype), k0.shape[:-1] + (LANES - AUG,))], axis=-1)
        q_parts[0] = jnp.pad(q_parts[0], ((0, 0),) * 4 + ((0, LANES - AUG),))
        selq = jnp.swapaxes(sel, -1, -2)
        selq = jnp.pad(selq, ((0, 0),) * 4 + ((0, LANES - selq.shape[-1]),))

    in_specs, args = [], []
    for qp in q_parts:
        in_specs.append(pl.BlockSpec((None, None, hb, t, qp.shape[-1]), lambda bi, n, st, qt, kt: (bi, n, 0, qt[st], 0)))
        args.append(qp)
    for kp in k_parts:
        in_specs.append(pl.BlockSpec((None, None, kb, t, kp.shape[-1]), lambda bi, n, st, qt, kt: (bi, n, 0, kt[st], 0)))
        args.append(kp)
    in_specs.append(pl.BlockSpec((None, None, kb, dv, t), lambda bi, n, st, qt, kt: (bi, n, 0, 0, kt[st])))
    args.append(vt)
    if use_sel:
        in_specs.append(pl.BlockSpec((None, None, sb, t, LANES), lambda bi, n, st, qt, kt: (bi, n, 0, qt[st], 0)))
        args.append(selq)
    near_tile = lambda qt, kt, st: jnp.minimum(qt[st] - kt[st], nd - 1)
    if bh > 1:
        in_specs.append(pl.BlockSpec((bh, None, t, t), lambda bi, n, st, qt, kt: (n, near_tile(qt, kt, st), 0, 0)))
    else:
        in_specs.append(pl.BlockSpec((1, None, t, t), lambda bi, n, st, qt, kt: (0, near_tile(qt, kt, st), 0, 0)))
    in_specs.append(pl.BlockSpec(memory_space=pltpu.SMEM))
    args += [bias, far]
    kern = functools.partial(_flash_kernel, hb=hb, kb=kb, sb=sb, bh=bh, t=t, tq=tq, kc=kc, nparts=nparts,
                             use_sel=use_sel, nbt=nbt, nd=nd)
    grid_spec = pltpu.PrefetchScalarGridSpec(
        num_scalar_prefetch=2,
        grid=(b, ng, len(pairs)),
        in_specs=in_specs,
        out_specs=pl.BlockSpec((None, None, hb, t, dv), lambda bi, n, st, qt, kt: (bi, n, 0, qt[st], 0)),
        scratch_shapes=[pltpu.VMEM((hb, 1, t), F32), pltpu.VMEM((hb, 1, t), F32), pltpu.VMEM((hb, dv, t), F32)])
    return pl.pallas_call(
        kern,
        grid_spec=grid_spec,
        out_shape=jax.ShapeDtypeStruct((b, ng, hb, s, dv), F32),
        compiler_params=_params("parallel", "parallel", "arbitrary"),
        name="flash_attention",
    )(qt, kt, *args)


def _window_kernel(*refs, group, t, ntile, has_sink):
    q_ref = refs[0]
    k_refs = refs[1:1 + ntile]
    vt_refs = refs[1 + ntile:1 + 2 * ntile]
    bias_ref = refs[1 + 2 * ntile]
    sink_ref = refs[2 + 2 * ntile] if has_sink else None
    o_ref = refs[-1]
    kvh = pl.program_id(1)

    def scores(g):
        q = q_ref[g]
        ss = [_dot_nt(k_refs[r][...], q) + bias_ref[g, ntile - 1 - r] for r in range(ntile)]
        top = ss[0]
        for s in ss[1:]:
            top = jnp.maximum(top, s)
        return ss, top

    def finish(g, ss, top):
        m = jnp.max(top, axis=0, keepdims=True)
        den = None
        if has_sink:
            sink = sink_ref[kvh * group + g] * LOG2E
            m = jnp.maximum(m, sink)
            den = jnp.exp2(sink - m)
        acc = None
        for r in range(ntile):
            p = jnp.exp2(ss[r] - m)
            ps = jnp.sum(p, axis=0, keepdims=True)
            pv = _dot(vt_refs[r][...], p.astype(BF16))
            den = ps if den is None else den + ps
            acc = pv if acc is None else acc + pv
        o_ref[g] = jnp.transpose(acc / den)

    pending = scores(0)
    for g in range(group):
        nxt = scores(g + 1) if g + 1 < group else None
        finish(g, *pending)
        pending = nxt


def window_attention(q, k, v, bank, t, window, sinks=None):
    b, hk, g, s, dh = q.shape
    ntile = window // t + 1
    kp = jnp.pad(k, ((0, 0), (0, 0), (window, 0), (0, 0)))
    vtp = jnp.pad(jnp.swapaxes(v, -1, -2), ((0, 0), (0, 0), (0, 0), (window, 0)))
    has_sink = sinks is not None
    in_specs = [pl.BlockSpec((None, None, g, t, dh), lambda bi, h, i: (bi, h, 0, i, 0))]
    args = [q]
    for r in range(ntile):
        in_specs.append(pl.BlockSpec((None, None, t, dh), lambda bi, h, i, r=r: (bi, h, i + r, 0)))
        args.append(kp)
    for r in range(ntile):
        in_specs.append(pl.BlockSpec((None, None, dh, t), lambda bi, h, i, r=r: (bi, h, 0, i + r)))
        args.append(vtp)
    in_specs.append(pl.BlockSpec((g, ntile, t, t), lambda bi, h, i: (h, 0, 0, 0)))
    args.append(bank)
    if has_sink:
        in_specs.append(pl.BlockSpec(memory_space=pltpu.SMEM))
        args.append(sinks)
    kern = functools.partial(_window_kernel, group=g, t=t, ntile=ntile, has_sink=has_sink)
    return pl.pallas_call(
        kern,
        grid=(b, hk, s // t),
        in_specs=in_specs,
        out_specs=pl.BlockSpec((None, None, g, t, dh), lambda bi, h, i: (bi, h, 0, i, 0)),
        out_shape=jax.ShapeDtypeStruct((b, hk, g, s, dh), F32),
        compiler_params=_params("parallel", "parallel", "parallel"),
        name="window_attention",
    )(*args)


def _rms(x, g):
    return x * lax.rsqrt(jnp.mean(x * x, axis=-1, keepdims=True) + NORM_EPS) * g


def _mla_up_kernel(cq_ref, ckv_ref, kr_ref, krs_ref, qn_ref, kvn_ref, wq_ref, wkv_ref, cs_ref, sn_ref,
                   qnope_ref, qrope_ref, knope_ref, v_ref, krope_ref, *, scale):
    cs = cs_ref[...]
    sn = sn_ref[...]
    q = _dot(_rms(cq_ref[...], qn_ref[...]).astype(BF16), wq_ref[...]) * scale
    kv = _dot(_rms(ckv_ref[...], kvn_ref[...]).astype(BF16), wkv_ref[...])
    wq_head = MLA_NOPE + 2 * MLA_ROPE
    for h in range(MLA_HEADS):
        base = h * wq_head
        qnope_ref[h] = q[:, base:base + MLA_NOPE].astype(qnope_ref.dtype)
        x = q[:, base + MLA_NOPE:base + MLA_NOPE + MLA_ROPE]
        xs = q[:, base + MLA_NOPE + MLA_ROPE:base + wq_head]
        qrope_ref[h] = (x * cs + xs * sn).astype(qrope_ref.dtype)
        kb = h * (MLA_NOPE + MLA_V)
        knope_ref[h] = kv[:, kb:kb + MLA_NOPE].astype(knope_ref.dtype)
        v_ref[h] = kv[:, kb + MLA_NOPE:kb + MLA_NOPE + MLA_V].astype(v_ref.dtype)
    kr = (kr_ref[...] * cs + krs_ref[...] * sn).astype(krope_ref.dtype)
    for h in range(MLA_HEADS):
        krope_ref[h] = kr


def mla_up(c_q, c_kv, k_rope, k_rope_sw, q_norm, kv_norm, w_q_up, w_kv_up, tm=512):
    b, s, _ = c_q.shape
    half = MLA_ROPE // 2
    inv = ROPE_THETA ** (-np.arange(0, MLA_ROPE, 2, dtype=np.float32) / np.float32(MLA_ROPE))
    ang = np.arange(s, dtype=np.float32)[:, None] * inv[None, :].astype(np.float32)
    cos, sin = np.cos(ang).astype(np.float32), np.sin(ang).astype(np.float32)
    cs = jnp.asarray(np.concatenate([cos, cos], axis=1))
    sn = jnp.asarray(np.concatenate([-sin, sin], axis=1))
    dq = MLA_NOPE + MLA_ROPE
    wq = w_q_up.reshape(MLA_Q_RANK, MLA_HEADS, dq)
    rope_cols = wq[:, :, MLA_NOPE:]
    swapped = jnp.concatenate([rope_cols[:, :, half:], rope_cols[:, :, :half]], axis=2)
    wq_aug = jnp.concatenate([wq, swapped], axis=2).reshape(MLA_Q_RANK, MLA_HEADS * (dq + MLA_ROPE)).astype(BF16)
    wkv = w_kv_up.astype(BF16)
    scale = (MLA_NOPE + MLA_ROPE) ** -0.5 * LOG2E
    tm = min(tm, s)
    row = lambda w: pl.BlockSpec((None, tm, w), lambda bi, i: (bi, i, 0))
    full = lambda a: pl.BlockSpec(a.shape, lambda bi, i: (0,) * a.ndim)
    head = lambda w: pl.BlockSpec((None, MLA_HEADS, tm, w), lambda bi, i: (bi, 0, i, 0))
    qn2, kvn2 = q_norm.reshape(1, -1), kv_norm.reshape(1, -1)
    outs = pl.pallas_call(
        functools.partial(_mla_up_kernel, scale=scale),
        grid=(b, s // tm),
        in_specs=[row(MLA_Q_RANK), row(MLA_KV_RANK), row(MLA_ROPE), row(MLA_ROPE), full(qn2), full(kvn2),
                  full(wq_aug), full(wkv),
                  pl.BlockSpec((tm, MLA_ROPE), lambda bi, i: (i, 0)), pl.BlockSpec((tm, MLA_ROPE), lambda bi, i: (i, 0))],
        out_specs=[head(MLA_NOPE), head(MLA_ROPE), head(MLA_NOPE), head(MLA_V), head(MLA_ROPE)],
        out_shape=[jax.ShapeDtypeStruct((b, MLA_HEADS, s, w), BF16)
                   for w in (MLA_NOPE, MLA_ROPE, MLA_NOPE, MLA_V, MLA_ROPE)],
        compiler_params=_params("parallel", "parallel"),
        name="mla_up",
    )(c_q, c_kv, k_rope, k_rope_sw, qn2, kvn2, wq_aug, wkv, cs, sn)
    return outs


def _out_kernel(*refs, gated):
    if gated:
        oc_ref, os_ref, ow_ref, gt_ref, ex_ref, ob_ref, x_ref, gm_ref, w_ref, o_ref = refs
        half = oc_ref.shape[-1]
        ge = _dot(jax.nn.sigmoid(gt_ref[...]), ex_ref[...], precision=HI)
        oa = ge[:, :half] * oc_ref[...] + ge[:, half:2 * half] * os_ref[...] + ge[:, 2 * half:] * ow_ref[...]
    else:
        oa_ref, ob_ref, x_ref, gm_ref, w_ref, o_ref = refs
        half = oa_ref.shape[-1]
        oa = oa_ref[...]
    mix = _dot(oa.astype(BF16), w_ref[:half, :]) + _dot(ob_ref[...].astype(BF16), w_ref[half:, :])
    o_ref[...] = x_ref[...] + gm_ref[...] * mix


def out_project(parts, ob, x, gate_m, w_out, gates=None, tm=512):
    b, s, d = x.shape
    gated = gates is not None
    tm = min(tm, s)
    row = lambda a: pl.BlockSpec((None, tm, a.shape[-1]), lambda bi, i: (bi, i, 0))
    full = lambda a: pl.BlockSpec(a.shape, lambda bi, i: (0,) * a.ndim)
    args, in_specs = [], []
    for p in parts:
        args.append(p); in_specs.append(row(p))
    if gated:
        half = parts[0].shape[-1]
        nh = half // HEAD_DIM
        ex = np.zeros((LANES, 3 * half), np.float32)
        for h in range(nh):
            for br in range(3):
                ex[h * 3 + br, br * half + h * HEAD_DIM: br * half + (h + 1) * HEAD_DIM] = 1.0
        gpad = jnp.pad(gates, ((0, 0), (0, 0), (0, LANES - gates.shape[-1])))
        ex = jnp.asarray(ex)
        args += [gpad, ex]; in_specs += [row(gpad), full(ex)]
    gm = gate_m.reshape(b, 1, d)
    wb = w_out.astype(BF16)
    args += [ob, x, gm, wb]
    in_specs += [row(ob), row(x), pl.BlockSpec((None, 1, d), lambda bi, i: (bi, 0, 0)), full(wb)]
    return pl.pallas_call(
        functools.partial(_out_kernel, gated=gated),
        grid=(b, s // tm),
        in_specs=in_specs,
        out_specs=pl.BlockSpec((None, tm, d), lambda bi, i: (bi, i, 0)),
        out_shape=jax.ShapeDtypeStruct((b, s, d), F32),
        compiler_params=_params("parallel", "parallel"),
        name="out_project",
    )(*args)


ROW_SUB = 24


def _store_lane_blocks(ref, first, val):
    n = val.shape[1] // LANES
    ref[:, first:first + n, :] = val.reshape(val.shape[0], n, LANES)


def _load_lane_blocks(ref, first, n):
    blk = ref[:, first:first + n, :]
    return blk.reshape(blk.shape[0], n * LANES)


def _ffn_pre_kernel(x_ref, g_ref, sc_ref, sh_ref, rw_ref, rb_ref, row_ref, grp_ref):
    x = x_ref[...]
    tm, d = x.shape
    nblk = d // LANES
    h = _norm_mod(x, g_ref[...], sc_ref[...], sh_ref[...])
    _store_lane_blocks(row_ref, 0, h)
    _store_lane_blocks(row_ref, nblk, x)
    row_ref[:, 2 * nblk + 1:, :] = jnp.zeros((tm, ROW_SUB - 2 * nblk - 1, LANES), F32)
    aff = jax.nn.sigmoid(_dot_nt(rw_ref[...], h, precision=HI))
    biased = aff + rb_ref[...]
    epg = EXPERTS_PER_GROUP
    brow = [biased[e:e + 1, :] for e in range(N_EXPERTS)]
    arow = [aff[e:e + 1, :] for e in range(N_EXPERTS)]
    best = gsel = None
    for gi in range(N_GROUPS):
        a, b_, c, d_ = brow[gi * epg:(gi + 1) * epg]
        hi1, lo1, hi2, lo2 = jnp.maximum(a, b_), jnp.minimum(a, b_), jnp.maximum(c, d_), jnp.minimum(c, d_)
        score = jnp.maximum(hi1, hi2) + jnp.maximum(jnp.minimum(hi1, hi2), jnp.maximum(lo1, lo2))
        if gi == 0:
            best, gsel = score, jnp.zeros(score.shape, jnp.int32)
        else:
            better = score > best
            gsel = jnp.where(better, gi, gsel)
            best = jnp.where(better, score, best)

    def in_group(rows, j):
        v = rows[j]
        for gi in range(1, N_GROUPS):
            v = jnp.where(gsel == gi, rows[gi * epg + j], v)
        return v

    bv = [in_group(brow, j) for j in range(epg)]
    av = [in_group(arow, j) for j in range(epg)]

    def argmax_excluding(skip):
        val = idx = None
        for j in range(epg):
            cand = bv[j] if skip is None else jnp.where(skip == j, -jnp.inf, bv[j])
            if j == 0:
                val, idx = cand, jnp.zeros(cand.shape, jnp.int32)
            else:
                better = cand > val
                idx = jnp.where(better, j, idx)
                val = jnp.where(better, cand, val)
        return idx

    first = argmax_excluding(None)
    second = argmax_excluding(first)

    def pick(rows, idx):
        v = rows[0]
        for j in range(1, epg):
            v = jnp.where(idx == j, rows[j], v)
        return v

    a1, a2 = pick(av, first), pick(av, second)
    tot = a1 + a2
    sub = _iota((LANES, tm), 0)
    extra = (jnp.where(sub == first, a1 / tot, 0.0) + jnp.where(sub == second, a2 / tot, 0.0)
             + jnp.where(sub == epg, pl.program_id(0).astype(F32), 0.0))
    row_ref[:, 2 * nblk, :] = jnp.transpose(extra)
    grp_ref[...] = gsel


def ffn_pre(x, g, sc, sh, router_w, router_b, tm=512):
    b, s, d = x.shape
    e = router_w.shape[1]
    tm = min(tm, s)
    return pl.pallas_call(
        _ffn_pre_kernel,
        grid=(b, s // tm),
        in_specs=[pl.BlockSpec((None, tm, d), lambda bi, i: (bi, i, 0)),
                  pl.BlockSpec((1, d), lambda bi, i: (0, 0)),
                  pl.BlockSpec((None, 1, d), lambda bi, i: (bi, 0, 0)),
                  pl.BlockSpec((None, 1, d), lambda bi, i: (bi, 0, 0)),
                  pl.BlockSpec((e, d), lambda bi, i: (0, 0)),
                  pl.BlockSpec((e, 1), lambda bi, i: (0, 0))],
        out_specs=[pl.BlockSpec((None, tm, ROW_SUB, LANES), lambda bi, i: (bi, i, 0, 0)),
                   pl.BlockSpec((None, 1, tm), lambda bi, i: (bi, 0, i))],
        out_shape=[jax.ShapeDtypeStruct((b, s, ROW_SUB, LANES), F32), jax.ShapeDtypeStruct((b, 1, s), jnp.int32)],
        compiler_params=_params("parallel", "parallel"),
        name="ffn_pre",
    )(x, g.reshape(1, d), sc.reshape(b, 1, d), sh.reshape(b, 1, d), router_w.T, router_b.reshape(e, 1))


def _plan_kernel(grp_ref, rank_ref, cnt_ref, carry_ref, *, tb):
    @pl.when(pl.program_id(0) == 0)
    def _():
        carry_ref[...] = jnp.zeros(carry_ref.shape, F32)

    onehot = (_iota((8, tb), 0) == grp_ref[...]).astype(F32)
    upper = (_iota((tb, tb), 0) <= _iota((tb, tb), 1)).astype(BF16)
    cum = _dot(onehot.astype(BF16), upper)
    carry = carry_ref[...]
    before = carry[:, :1] + cum - onehot
    rank_ref[...] = jnp.sum(onehot * before, axis=0, keepdims=True).astype(jnp.int32)
    total = carry + cum[:, tb - 1:tb]
    carry_ref[...] = total
    cnt_ref[...] = total


def _permute_rows_kernel(pos_ref, src_ref, *rest, chunk, scatter):
    dst_ref, sem = rest[-2], rest[-1]
    base = pl.program_id(0) * chunk

    def copy(i):
        p = pos_ref[0, i]
        if scatter:
            return pltpu.make_async_copy(src_ref.at[pl.ds(base + i, 1)], dst_ref.at[pl.ds(p, 1)], sem)
        return pltpu.make_async_copy(src_ref.at[pl.ds(p, 1)], dst_ref.at[pl.ds(base + i, 1)], sem)

    def start(i, carry):
        copy(i).start()
        return carry

    lax.fori_loop(0, chunk, start, 0, unroll=8)
    pltpu.make_async_copy(src_ref.at[pl.ds(0, chunk)], dst_ref.at[pl.ds(0, chunk)], sem).wait()


def permute_rows(src, pos, n_out, scatter, chunk=2048):
    n_idx = pos.shape[0]
    row = src.shape[1:]
    chunk = min(chunk, n_idx)
    args = [pos.reshape(n_idx // chunk, 1, chunk), src]
    in_specs = [pl.BlockSpec((None, 1, chunk), lambda i: (i, 0, 0), memory_space=pltpu.SMEM),
                pl.BlockSpec(memory_space=pl.ANY)]
    aliases = {}
    if scatter:
        args.append(jnp.zeros((n_out,) + row, src.dtype))
        in_specs.append(pl.BlockSpec(memory_space=pl.ANY))
        aliases = {2: 0}
    return pl.pallas_call(
        functools.partial(_permute_rows_kernel, chunk=chunk, scatter=scatter),
        grid=(n_idx // chunk,),
        in_specs=in_specs,
        out_specs=pl.BlockSpec(memory_space=pl.ANY),
        out_shape=jax.ShapeDtypeStruct((n_out,) + row, src.dtype),
        scratch_shapes=[pltpu.SemaphoreType.DMA(())],
        input_output_aliases=aliases,
        compiler_params=pltpu.CompilerParams(dimension_semantics=("arbitrary",), has_side_effects=True),
        name="permute_rows",
    )(*args)


def _moe_group_kernel(tg_ref, row_ref, gf_ref, wg_ref, wu_ref, wd_ref, o_ref, acc_ref, *, d):
    j = pl.program_id(1)

    @pl.when(j == 0)
    def _():
        acc_ref[...] = jnp.zeros(acc_ref.shape, F32)

    nblk = d // LANES
    h = _load_lane_blocks(row_ref, 0, nblk).astype(BF16)
    extra = row_ref[:, 2 * nblk, :]
    a = _dot(h, wg_ref[...].astype(BF16))
    u = _dot(h, wu_ref[...].astype(BF16))
    c = jnp.sum(jnp.where(_iota(extra.shape, 1) == j, extra, 0.0), axis=-1, keepdims=True)
    hid = (a * jax.nn.sigmoid(a)) * u * c
    acc_ref[...] += _dot(hid.astype(BF16), wd_ref[...].astype(BF16))

    @pl.when(j == pl.num_programs(1) - 1)
    def _():
        batch = extra[:, EXPERTS_PER_GROUP:EXPERTS_PER_GROUP + 1]
        gate = jnp.zeros(acc_ref.shape, F32)
        for bi in range(gf_ref.shape[0]):
            gate = jnp.where(batch == float(bi), gf_ref[bi:bi + 1, :], gate)
        _store_lane_blocks(o_ref, 0, _load_lane_blocks(row_ref, nblk, nblk) + gate * acc_ref[...])


def moe_grouped(rows, grp, gate_f, w_gate, w_up, w_down, layer, tm=512):
    b, s = rows.shape[:2]
    d = w_gate.shape[-2]
    f = w_gate.shape[-1]
    t = b * s
    tb = min(512, t)
    rank, cnt = pl.pallas_call(
        functools.partial(_plan_kernel, tb=tb),
        grid=(t // tb,),
        in_specs=[pl.BlockSpec((1, tb), lambda i: (0, i))],
        out_specs=[pl.BlockSpec((1, tb), lambda i: (0, i)), pl.BlockSpec((8, LANES), lambda i: (0, 0))],
        out_shape=[jax.ShapeDtypeStruct((1, t), jnp.int32), jax.ShapeDtypeStruct((8, LANES), F32)],
        scratch_shapes=[pltpu.VMEM((8, LANES), F32)],
        compiler_params=_params("arbitrary"),
        name="moe_plan",
    )(grp.reshape(1, t))
    grp_flat = grp.reshape(t)
    counts = cnt[:N_GROUPS, 0].astype(jnp.int32)
    padded = (counts + tm - 1) // tm * tm
    pend = jnp.cumsum(padded)
    pos = (pend - padded)[grp_flat] + rank.reshape(t)
    n_tiles = t // tm + N_GROUPS
    tile_group = jnp.minimum(jnp.searchsorted(pend, jnp.arange(n_tiles, dtype=jnp.int32) * tm, side="right"),
                             N_GROUPS - 1).astype(jnp.int32)

    sorted_rows = permute_rows(rows.reshape(t, ROW_SUB, LANES), pos, n_tiles * tm, scatter=True)
    epg = EXPERTS_PER_GROUP
    grid_spec = pltpu.PrefetchScalarGridSpec(
        num_scalar_prefetch=1,
        grid=(n_tiles, epg),
        in_specs=[pl.BlockSpec((tm, ROW_SUB, LANES), lambda i, j, tg: (i, 0, 0)),
                  pl.BlockSpec(gate_f.shape, lambda i, j, tg: (0, 0)),
                  pl.BlockSpec((None, None, d, f), lambda i, j, tg: (layer, tg[i] * epg + j, 0, 0)),
                  pl.BlockSpec((None, None, d, f), lambda i, j, tg: (layer, tg[i] * epg + j, 0, 0)),
                  pl.BlockSpec((None, None, f, d), lambda i, j, tg: (layer, tg[i] * epg + j, 0, 0))],
        out_specs=pl.BlockSpec((tm, d // LANES, LANES), lambda i, j, tg: (i, 0, 0)),
        scratch_shapes=[pltpu.VMEM((tm, d), F32)])
    out_sorted = pl.pallas_call(
        functools.partial(_moe_group_kernel, d=d),
        grid_spec=grid_spec,
        out_shape=jax.ShapeDtypeStruct((n_tiles * tm, d // LANES, LANES), F32),
        compiler_params=_params("parallel", "arbitrary"),
        name="moe_grouped",
    )(tile_group, sorted_rows, gate_f, w_gate, w_up, w_down)
    return permute_rows(out_sorted, pos, t, scatter=False).reshape(b, s, d)


def _final_norm_kernel(x_ref, g_ref, o_ref):
    o_ref[...] = _rms(x_ref[...], g_ref[...])


def final_rmsnorm(x, g, tm=512):
    b, s, d = x.shape
    tm = min(tm, s)
    return pl.pallas_call(
        _final_norm_kernel,
        grid=(b, s // tm),
        in_specs=[pl.BlockSpec((None, tm, d), lambda bi, i: (bi, i, 0)), pl.BlockSpec((1, d), lambda bi, i: (0, 0))],
        out_specs=pl.BlockSpec((None, tm, d), lambda bi, i: (bi, i, 0)),
        out_shape=jax.ShapeDtypeStruct((b, s, d), F32),
        compiler_params=_params("parallel", "parallel"),
        name="final_rmsnorm",
    )(x, g.reshape(1, d))


def _heads(x, nh):
    b, s, w = x.shape
    return x.reshape(b, s, nh, w // nh).transpose(0, 2, 1, 3)


def _merge(x):
    b, ng, hb, s, dh = x.shape
    return x.transpose(0, 3, 1, 2, 4).reshape(b, s, ng * hb * dh)


def even_mixer(proj, x, gate_m, w_out, pos_k, pos_v, ck_w1, ck_w2, cv_w1, cv_w2, rel_table, banks):
    b, s, _ = proj.shape
    off = _offsets(EVEN_WIDTHS)
    q_a, kc, vc, ks, vs, kw, vw, gates, q_b, k_b, v_b = (proj[..., lo:hi] for lo, hi in off)
    hk, g, dh = NSA_KV_HEADS, NSA_GROUP, HEAD_DIM
    scale = dh ** -0.5
    qa = (_heads(q_a, NSA_HEADS) * scale).reshape(b, hk, g, s, dh)
    qa16_log2 = (qa * LOG2E).astype(BF16)
    (bank_l, far_l, t_l), (bank_w, t_w) = banks["dense"], banks["nsa_window"]

    nc = s // CMP_STRIDE
    kv = jnp.stack([_heads(kc, hk), _heads(vc, hk)]).reshape(2, b * hk, s, dh)
    cmp = nsa_compress(kv, jnp.stack([ck_w1, cv_w1]), jnp.stack([pos_k, pos_v]), jnp.stack([ck_w2, cv_w2]))
    cmp = cmp.reshape(2, b, hk, nc, dh)
    o_c, sel = nsa_cmp_attention(qa, cmp[0], cmp[1], bias_cmp(rel_table, s, nc, tq=min(256, s)), tq=min(256, s))
    ks5 = _heads(ks, hk).astype(BF16)[:, :, None]
    vs5 = _heads(vs, hk).astype(BF16)[:, :, None]
    o_s = flash_attention([qa16_log2], [ks5], vs5, bank_l, far_l, sel=sel, t=t_l, blk=SLC_BLOCK)
    o_w = window_attention(qa16_log2, _heads(kw, hk).astype(BF16), _heads(vw, hk).astype(BF16), bank_w, t_w, NSA_WINDOW)

    hb = 4
    qb = _heads(q_b, MOBA_HEADS)
    kb = _heads(k_b, MOBA_HEADS)
    vb = _heads(v_b, MOBA_HEADS)
    selb = moba_select(qb, kb, tq=min(2048, s))
    ngb = MOBA_HEADS // hb
    r5 = lambda a: a.reshape(b, ngb, hb, a.shape[-2], a.shape[-1])
    o_b = flash_attention([r5((qb * (scale * LOG2E)).astype(BF16))], [r5(kb.astype(BF16))], r5(vb.astype(BF16)),
                          bank_l, far_l, sel=r5(selb), t=t_l, blk=MOBA_BLOCK)

    return out_project([_merge(o_c), _merge(o_s), _merge(o_w)], _merge(o_b), x, gate_m, w_out, gates=gates)


def odd_mixer(proj, x, gate_m, w_out, q_norm, kv_norm, w_q_up, w_kv_up, sinks, banks):
    b, s, _ = proj.shape
    off = _offsets(ODD_WIDTHS + (MLA_ROPE,))
    c_q, c_kv, k_rope, q_d, k_d, v_d, k_rope_sw = (proj[..., lo:hi] for lo, hi in off)
    qn, qr, kn, v, kr = mla_up(c_q, c_kv, k_rope, k_rope_sw, q_norm, kv_norm, w_q_up, w_kv_up)
    hb = 2
    ng = MLA_HEADS // hb
    r5 = lambda a: a.reshape(b, ng, hb, s, a.shape[-1])
    bank_c, far_c, t_c = banks["causal"]
    q_full = jnp.concatenate([qn, qr], axis=-1)
    k_full = jnp.concatenate([kn, kr], axis=-1)
    o_c = flash_attention([r5(q_full)], [r5(k_full)], r5(v), bank_c, far_c, t=t_c)

    hk, g, dh = SWA_KV_HEADS, SWA_GROUP, HEAD_DIM
    scale = dh ** -0.5
    qd = (_heads(q_d, SWA_HEADS) * (scale * LOG2E)).reshape(b, hk, g, s, dh).astype(BF16)
    bank_s, t_s = banks["swa"]
    o_d = window_attention(qd, _heads(k_d, hk).astype(BF16), _heads(v_d, hk).astype(BF16), bank_s, t_s, SWA_WINDOW,
                           sinks=sinks)
    return out_project([_merge(o_c)], _merge(o_d), x, gate_m, w_out)


def kernel(x, c, rel_table, router_w, router_b, final_norm, norm_mix, norm_ffn, ada_w, ada_b, moe_w_gate, moe_w_up, moe_w_down, ev_w_in, ev_w_out, nsa_pos_k, nsa_pos_v, nsa_ck_w1, nsa_ck_w2, nsa_cv_w1, nsa_cv_w2, od_w_in, od_w_out, mla_q_norm, mla_kv_norm, mla_w_q_up, mla_w_kv_up, swa_sinks):
    b, s, d = x.shape
    depth = ada_w.shape[0]
    mods = ada_all(c, ada_w, ada_b)
    t_dense = min(512, s)
    bank_l = bias_bank(rel_table, t_dense)
    far_l = rel_table[N_BUCKETS - 1] * LOG2E
    future = np.arange(t_dense)[:, None] > np.arange(t_dense)[None, :]
    bank_c = jnp.asarray(np.where(future, NEG, 0.0).astype(np.float32)[None, None])
    t_w = min(256, s)
    banks = {"dense": (bank_l, far_l, t_dense), "causal": (bank_c, jnp.zeros((N_BIAS_HEADS,), F32), t_dense),
             "nsa_window": (bias_bank(rel_table, t_w, window=NSA_WINDOW), t_w),
             "swa": (bias_bank(rel_table, SWA_WINDOW, window=SWA_WINDOW), SWA_WINDOW)}

    for layer in range(depth):
        shift_m, scale_m, gate_m, shift_f, scale_f, gate_f = jnp.split(mods[layer], 6, axis=-1)
        i = layer // 2
        if layer % 2 == 0:
            w_in = ev_w_in[i]
            pad = -w_in.shape[1] % LANES
            proj = norm_mod_matmul(x, norm_mix[layer], scale_m, shift_m, jnp.pad(w_in, ((0, 0), (0, pad))).astype(BF16))
            x = even_mixer(proj, x, gate_m, ev_w_out[i], nsa_pos_k[i], nsa_pos_v[i], nsa_ck_w1[i], nsa_ck_w2[i],
                           nsa_cv_w1[i], nsa_cv_w2[i], rel_table, banks)
        else:
            w_in = od_w_in[i]
            lo = MLA_Q_RANK + MLA_KV_RANK
            half = MLA_ROPE // 2
            sw = jnp.concatenate([w_in[:, lo + half:lo + MLA_ROPE], w_in[:, lo:lo + half]], axis=1)
            proj = norm_mod_matmul(x, norm_mix[layer], scale_m, shift_m, jnp.concatenate([w_in, sw], axis=1).astype(BF16))
            x = odd_mixer(proj, x, gate_m, od_w_out[i], mla_q_norm[i], mla_kv_norm[i], mla_w_q_up[i], mla_w_kv_up[i],
                          swa_sinks[i], banks)
        rows, grp = ffn_pre(x, norm_ffn[layer], scale_f, shift_f, router_w, router_b)
        x = moe_grouped(rows, grp, gate_f, moe_w_gate, moe_w_up, moe_w_down, layer)
    return final_rmsnorm(x, final_norm)
```

```python
import functools
import math

import numpy as np
import jax
import jax.numpy as jnp
from jax import lax
from jax.experimental import pallas as pl
from jax.experimental.pallas import tpu as pltpu

F32 = jnp.float32
BF16 = jnp.bfloat16
HI = lax.Precision.HIGHEST

D_MODEL = 1024
HEAD_DIM = 64
NEG = -1e30
BIG = 1e30
M_INIT = -1e9
NORM_EPS = 1e-6
LOG2E = math.log2(math.e)

N_BUCKETS = 32
MAX_DISTANCE = 1024
N_BIAS_HEADS = 8

NSA_HEADS = 8
NSA_KV_HEADS = 2
NSA_GROUP = NSA_HEADS // NSA_KV_HEADS
CMP_LEN = 32
CMP_STRIDE = 16
CMP_HIDDEN = 256
SLC_BLOCK = 64
SLC_TOPN = 16
NSA_WINDOW = 512

MOBA_HEADS = 8
MOBA_BLOCK = 256
MOBA_TOPK = 3

MLA_HEADS = 4
MLA_Q_RANK = 256
MLA_KV_RANK = 128
MLA_NOPE = 128
MLA_ROPE = 64
MLA_V = 128
ROPE_THETA = 10000.0

SWA_HEADS = 8
SWA_KV_HEADS = 2
SWA_GROUP = SWA_HEADS // SWA_KV_HEADS
SWA_WINDOW = 128

N_EXPERTS = 16
N_GROUPS = 4
EXPERTS_PER_GROUP = N_EXPERTS // N_GROUPS
D_EXPERT = 512

EVEN_WIDTHS = (NSA_HEADS * HEAD_DIM,) + (NSA_KV_HEADS * HEAD_DIM,) * 6 + (3 * NSA_HEADS,) + (MOBA_HEADS * HEAD_DIM,) * 3
ODD_WIDTHS = (MLA_Q_RANK, MLA_KV_RANK, MLA_ROPE, SWA_HEADS * HEAD_DIM, SWA_KV_HEADS * HEAD_DIM, SWA_KV_HEADS * HEAD_DIM)

LANES = 128
VMEM_LIMIT = 56 * 1024 * 1024


def _params(*sem):
    return pltpu.CompilerParams(dimension_semantics=sem, vmem_limit_bytes=VMEM_LIMIT)


def _dot(a, b, precision=None):
    return lax.dot_general(a, b, (((1,), (0,)), ((), ())), precision=precision, preferred_element_type=F32)


def _dot_nt(a, b, precision=None):
    return lax.dot_general(a, b, (((1,), (1,)), ((), ())), precision=precision, preferred_element_type=F32)


def _iota(shape, dim):
    return lax.broadcasted_iota(jnp.int32, shape, dim)


def _offsets(widths):
    out, acc = [], 0
    for w in widths:
        out.append((acc, acc + w))
        acc += w
    return out


def _bucket_thresholds():
    d = np.arange(0, 4 * MAX_DISTANCE, dtype=np.int64)
    exact = N_BUCKETS // 2
    x = np.maximum(d, 1).astype(np.float32) / np.float32(exact)
    logp = exact + (np.log(x) / np.float32(math.log(MAX_DISTANCE / exact)) * np.float32(N_BUCKETS - exact)).astype(np.int32)
    bucket = np.where(d < exact, d, np.minimum(logp, N_BUCKETS - 1))
    return [int(np.argmax(bucket >= b)) for b in range(1, N_BUCKETS)]


BUCKET_THR = _bucket_thresholds()
FAR_DIST = BUCKET_THR[-1]


def _ada_kernel(c_ref, w_ref, b_ref, o_ref):
    c = c_ref[...]
    o_ref[...] = _dot(c * jax.nn.sigmoid(c), w_ref[...], precision=HI) + b_ref[...]


def ada_all(c, ada_w, ada_b):
    depth, d, n = ada_w.shape
    rows = 8
    cp = jnp.pad(c, ((0, rows - c.shape[0]), (0, 0)))
    tn = 1536
    out = pl.pallas_call(
        _ada_kernel,
        grid=(depth, n // tn),
        in_specs=[pl.BlockSpec((rows, d), lambda l, j: (0, 0)),
                  pl.BlockSpec((None, d, tn), lambda l, j: (l, 0, j)),
                  pl.BlockSpec((None, 1, tn), lambda l, j: (l, 0, j))],
        out_specs=pl.BlockSpec((None, rows, tn), lambda l, j: (l, 0, j)),
        out_shape=jax.ShapeDtypeStruct((depth, rows, n), F32),
        compiler_params=_params("parallel", "parallel"),
        name="ada",
    )(cp, ada_w, ada_b.reshape(depth, 1, n))
    return out[:, :c.shape[0], :]


def _norm_mod(x, g, sc, sh):
    y = x * lax.rsqrt(jnp.mean(x * x, axis=-1, keepdims=True) + NORM_EPS) * g
    return y * (1.0 + sc) + sh


def _nmm_kernel(x_ref, g_ref, sc_ref, sh_ref, w_ref, o_ref):
    h = _norm_mod(x_ref[...], g_ref[...], sc_ref[...], sh_ref[...])
    o_ref[...] = _dot(h.astype(BF16), w_ref[...]).astype(o_ref.dtype)


def norm_mod_matmul(x, g, sc, sh, w, tm=512):
    b, s, d = x.shape
    n = w.shape[1]
    return pl.pallas_call(
        _nmm_kernel,
        grid=(b, s // tm),
        in_specs=[pl.BlockSpec((None, tm, d), lambda bi, i: (bi, i, 0)),
                  pl.BlockSpec((1, d), lambda bi, i: (0, 0)),
                  pl.BlockSpec((None, 1, d), lambda bi, i: (bi, 0, 0)),
                  pl.BlockSpec((None, 1, d), lambda bi, i: (bi, 0, 0)),
                  pl.BlockSpec((d, n), lambda bi, i: (0, 0))],
        out_specs=pl.BlockSpec((None, tm, n), lambda bi, i: (bi, i, 0)),
        out_shape=jax.ShapeDtypeStruct((b, s, n), F32),
        compiler_params=_params("parallel", "parallel"),
        name="norm_mod_matmul",
    )(x, g.reshape(1, d), sc.reshape(b, 1, d), sh.reshape(b, 1, d), w)


def _bias_kernel(tab_ref, o_ref, *, rows, cols, step, cstride, c0, key_major=False, window=None):
    sub = 8
    off = pl.program_id(0) * step - c0
    unit = LOG2E if key_major else 1.0
    if key_major:
        base = _iota((sub, cols), 1) - _iota((sub, cols), 0)
    else:
        base = _iota((sub, cols), 0) - cstride * _iota((sub, cols), 1)

    def body(r, carry):
        dist = off + base + (-r * sub if key_major else r * sub)
        vs = [jnp.full((sub, cols), tab_ref[0, h] * unit, F32) for h in range(N_BIAS_HEADS)]
        for b in range(1, N_BUCKETS):
            ge = dist >= BUCKET_THR[b - 1]
            for h in range(N_BIAS_HEADS):
                vs[h] = jnp.where(ge, tab_ref[b, h] * unit, vs[h])
        masked = dist < 0 if window is None else (dist < 0) | (dist >= window)
        for h in range(N_BIAS_HEADS):
            v = jnp.where(masked, NEG, vs[h]) if key_major else vs[h]
            o_ref[h, pl.ds(pl.multiple_of(r * sub, sub), sub), :] = v
        return carry

    lax.fori_loop(0, rows // sub, body, 0)


def bias_bank(rel_table, t, window=None):
    nd = -(-(FAR_DIST + t - 1) // t) if window is None else window // t + 1
    kern = functools.partial(_bias_kernel, rows=t, cols=t, step=t, cstride=1, c0=0, key_major=True, window=window)
    bank = pl.pallas_call(
        kern,
        grid=(nd,),
        in_specs=[pl.BlockSpec(memory_space=pltpu.SMEM)],
        out_specs=pl.BlockSpec((N_BIAS_HEADS, None, t, t), lambda i: (0, i, 0, 0)),
        out_shape=jax.ShapeDtypeStruct((N_BIAS_HEADS, nd, t, t), F32),
        compiler_params=_params("parallel"),
        name="bias_bank",
    )(rel_table)
    return bank


def _bias_cmp_kernel(tab_ref, o_ref, *, rows, cols):
    sub = 8
    width = min(LANES, cols)
    t0 = pl.program_id(0) * rows
    base = _iota((sub, width), 0) - CMP_STRIDE * _iota((sub, width), 1)

    def body(r, carry):
        rs = pl.ds(pl.multiple_of(r * sub, sub), sub)
        for c in range(cols // width):
            cs = slice(c * width, (c + 1) * width)
            off = t0 + r * sub - (CMP_LEN - 1) - CMP_STRIDE * width * c
            lo = off - CMP_STRIDE * (width - 1)
            hi = off + sub - 1

            @pl.when((hi >= 0) & (lo < FAR_DIST))
            def _():
                dist = off + base
                vs = [jnp.full((sub, width), tab_ref[0, h], F32) for h in range(N_BIAS_HEADS)]
                for b in range(1, N_BUCKETS):
                    ge = dist >= BUCKET_THR[b - 1]
                    for h in range(N_BIAS_HEADS):
                        vs[h] = jnp.where(ge, tab_ref[b, h], vs[h])
                for h in range(N_BIAS_HEADS):
                    o_ref[h, rs, cs] = vs[h]

            @pl.when(lo >= FAR_DIST)
            def _():
                for h in range(N_BIAS_HEADS):
                    o_ref[h, rs, cs] = jnp.full((sub, width), tab_ref[N_BUCKETS - 1, h], F32)

            @pl.when(hi < 0)
            def _():
                for h in range(N_BIAS_HEADS):
                    o_ref[h, rs, cs] = jnp.full((sub, width), tab_ref[0, h], F32)
        return carry

    lax.fori_loop(0, rows // sub, body, 0)


def bias_cmp(rel_table, s, nc, tq=256):
    kern = functools.partial(_bias_cmp_kernel, rows=tq, cols=nc)
    return pl.pallas_call(
        kern,
        grid=(s // tq,),
        in_specs=[pl.BlockSpec(memory_space=pltpu.SMEM)],
        out_specs=pl.BlockSpec((N_BIAS_HEADS, tq, nc), lambda i: (0, i, 0)),
        out_shape=jax.ShapeDtypeStruct((N_BIAS_HEADS, s, nc), F32),
        compiler_params=_params("parallel"),
        name="bias_cmp",
    )(rel_table)


def _compress_kernel(x_ref, w1_ref, pos_ref, w2_ref, o_ref):
    x = x_ref[...]
    w1 = w1_ref[...]
    half = w1.shape[0]
    hid = w1.shape[1] // 2
    r = _dot(x, w1, precision=HI)
    pos = pos_ref[...]
    pb = _dot(pos[:, :half], w1[:, :hid], precision=HI) + _dot(pos[:, half:], w1[:, hid:], precision=HI)
    nxt = pltpu.roll(r[:, hid:], x.shape[0] - 1, 0)
    pre = r[:, :hid] + nxt + pb[0:1, :]
    o_ref[...] = _dot(jax.nn.gelu(pre), w2_ref[...], precision=HI)


def nsa_compress(kv, w1, pos, w2):
    two, bh, s, dh = kv.shape
    nc = s // CMP_STRIDE
    half = CMP_STRIDE * dh
    hid = w1.shape[2]
    x = kv.reshape(two, bh, nc, half)
    w1cat = jnp.concatenate([w1[:, :half, :], w1[:, half:, :]], axis=2)
    posf = jnp.broadcast_to(pos.reshape(two, 1, CMP_LEN * dh), (two, 8, CMP_LEN * dh))
    return pl.pallas_call(
        _compress_kernel,
        grid=(two, bh),
        in_specs=[pl.BlockSpec((None, None, nc, half), lambda w, i: (w, i, 0, 0)),
                  pl.BlockSpec((None, half, 2 * hid), lambda w, i: (w, 0, 0)),
                  pl.BlockSpec((None, 8, CMP_LEN * dh), lambda w, i: (w, 0, 0)),
                  pl.BlockSpec((None, hid, dh), lambda w, i: (w, 0, 0))],
        out_specs=pl.BlockSpec((None, None, nc, dh), lambda w, i: (w, i, 0, 0)),
        out_shape=jax.ShapeDtypeStruct((two, bh, nc, dh), F32),
        compiler_params=_params("parallel", "parallel"),
        name="nsa_compress",
    )(x, w1cat, posf, w2)


def _topn_mask(score, index, n):
    idx_f = index.astype(F32)
    sel = jnp.zeros(score.shape, jnp.bool_)
    for _ in range(n):
        m = jnp.max(score, axis=0, keepdims=True)
        first = jnp.min(jnp.where(score == m, idx_f, float(score.shape[0])), axis=0, keepdims=True)
        pick = idx_f == first
        sel = sel | pick
        score = jnp.where(pick, -jnp.inf, score)
    return sel


def _cmp_kernel(q_ref, k_ref, v_ref, b_ref, ov_ref, oc_ref, sel_ref, *, tq, nc, nslc, ntop, group):
    qi = pl.program_id(2)
    t = qi * tq + _iota((tq, 1), 0)
    n = _iota((1, nc), 1)
    mask = (n * CMP_STRIDE + (CMP_LEN - 1) <= t) & (n < nc - 1)
    k = k_ref[...]
    v = v_ref[...].astype(BF16)
    dh = k.shape[-1]
    psum = jnp.zeros((tq, nc), F32)
    scores = lambda g: _dot_nt(q_ref[g], k, precision=HI) + b_ref[g]
    s_next = scores(0)
    for g in range(group):
        s, s_next = s_next, (scores(g + 1) if g + 1 < group else None)
        s = jnp.where(mask, s, NEG)
        m = jnp.max(s, axis=-1, keepdims=True)
        p = jnp.where(mask, jnp.exp(s - m), 0.0)
        den = jnp.sum(p, axis=-1, keepdims=True)
        p = p * jnp.where(den > 0.0, 1.0 / den, 0.0)
        oc_ref[:, g * dh:(g + 1) * dh] = _dot(p.astype(BF16), v)
        psum = psum + p
    imp = _dot_nt(ov_ref[...], psum, precision=HI)
    j = _iota((nslc, tq), 0)
    blk = jnp.right_shift(qi * tq + _iota((1, tq), 1), int(math.log2(SLC_BLOCK)))
    forced = (j == 0) | (j == blk) | (j == blk - 1)
    valid = j <= blk
    score = jnp.where(forced, BIG, jnp.where(valid, imp, NEG))
    sel = _topn_mask(score, j, ntop) & valid
    sel_ref[0] = jnp.where(sel, 0.0, NEG)


def nsa_cmp_attention(q, k_cmp, v_cmp, bias_c, tq=256):
    b, hk, g, s, dh = q.shape
    nc = k_cmp.shape[2]
    nslc = s // SLC_BLOCK
    ntop = min(SLC_TOPN, nslc)
    cmp_start = np.arange(nc, dtype=np.int64) * CMP_STRIDE
    slc_lo = np.arange(nslc, dtype=np.int64) * SLC_BLOCK
    overlap = ((cmp_start[:, None] <= slc_lo[None, :] + SLC_BLOCK - 1)
               & (cmp_start[:, None] + CMP_LEN - 1 >= slc_lo[None, :])).astype(np.float32)
    kern = functools.partial(_cmp_kernel, tq=tq, nc=nc, nslc=nslc, ntop=ntop, group=g)
    return pl.pallas_call(
        kern,
        grid=(b, hk, s // tq),
        in_specs=[pl.BlockSpec((None, None, g, tq, dh), lambda bi, h, i: (bi, h, 0, i, 0)),
                  pl.BlockSpec((None, None, nc, dh), lambda bi, h, i: (bi, h, 0, 0)),
                  pl.BlockSpec((None, None, nc, dh), lambda bi, h, i: (bi, h, 0, 0)),
                  pl.BlockSpec((g, tq, nc), lambda bi, h, i: (h, i, 0)),
                  pl.BlockSpec((nslc, nc), lambda bi, h, i: (0, 0))],
        out_specs=[pl.BlockSpec((None, tq, g * dh), lambda bi, h, i: (bi, i, h)),
                   pl.BlockSpec((None, None, 1, nslc, tq), lambda bi, h, i: (bi, h, 0, 0, i))],
        out_shape=[jax.ShapeDtypeStruct((b, s, hk * g * dh), F32),
                   jax.ShapeDtypeStruct((b, hk, 1, nslc, s), F32)],
        compiler_params=_params("parallel", "parallel", "parallel"),
        name="nsa_cmp_attention",
    )(q, k_cmp, v_cmp, bias_c, jnp.asarray(overlap.T))


def _kmean_kernel(k_ref, o_ref, *, nb, blk):
    k = k_ref[...]
    o_ref[...] = jnp.sum(k.reshape(nb, blk, k.shape[-1]), axis=1) * (1.0 / blk)


def _moba_sel_kernel(q_ref, km_ref, sel_ref, *, tq, nb, ntop):
    qi = pl.program_id(2)
    own = jnp.right_shift(qi * tq + _iota((1, tq), 1), int(math.log2(MOBA_BLOCK)))
    j = _iota((nb, tq), 0)
    gate = _dot_nt(km_ref[...], q_ref[...], precision=HI)
    past = j < own
    sel = _topn_mask(jnp.where(past, gate, NEG), j, ntop) & past
    sel_ref[...] = jnp.where(sel | (j == own), 0.0, NEG)


def moba_select(q, k, tq=256):
    b, h, s, dh = q.shape
    nb = s // MOBA_BLOCK
    ntop = min(MOBA_TOPK, nb)
    kmean = pl.pallas_call(
        functools.partial(_kmean_kernel, nb=nb, blk=MOBA_BLOCK),
        grid=(b, h),
        in_specs=[pl.BlockSpec((None, None, s, dh), lambda bi, hi: (bi, hi, 0, 0))],
        out_specs=pl.BlockSpec((None, None, nb, dh), lambda bi, hi: (bi, hi, 0, 0)),
        out_shape=jax.ShapeDtypeStruct((b, h, nb, dh), F32),
        compiler_params=_params("parallel", "parallel"),
        name="moba_kmean",
    )(k)
    return pl.pallas_call(
        functools.partial(_moba_sel_kernel, tq=tq, nb=nb, ntop=ntop),
        grid=(b, h, s // tq),
        in_specs=[pl.BlockSpec((None, None, tq, dh), lambda bi, hi, i: (bi, hi, i, 0)),
                  pl.BlockSpec((None, None, nb, dh), lambda bi, hi, i: (bi, hi, 0, 0))],
        out_specs=pl.BlockSpec((None, None, nb, tq), lambda bi, hi, i: (bi, hi, 0, i)),
        out_shape=jax.ShapeDtypeStruct((b, h, nb, s), F32),
        compiler_params=_params("parallel", "parallel", "parallel"),
        name="moba_select",
    )(q, kmean)


AUG = 64


def _flash_kernel(qt_ref, kt_ref, *refs, hb, kb, sb, bh, t, tq, kc, nparts, use_sel, nbt, nd):
    pos = 0
    q_refs = refs[pos:pos + nparts]; pos += nparts
    k_refs = refs[pos:pos + nparts]; pos += nparts
    vt_ref = refs[pos]; pos += 1
    sel_ref = None
    if use_sel:
        sel_ref = refs[pos]; pos += 1
    bias_ref, far_ref, o_ref, m_ref, l_ref, acc_ref = refs[pos:pos + 6]

    head0 = pl.program_id(1) * hb
    step = pl.program_id(2)
    qi = qt_ref[step]
    ki = kt_ref[step]

    @pl.when(ki == 0)
    def _():
        m_ref[...] = jnp.full(m_ref.shape, M_INIT, F32)
        l_ref[...] = jnp.zeros(l_ref.shape, F32)
        acc_ref[...] = jnp.zeros(acc_ref.shape, F32)

    nchunk = t // kc
    nlane = t // tq

    def mask_lanes(si, qc):
        rows = sel_ref[si, qc * tq:(qc + 1) * tq, :]
        shift = lax.rem(AUG + LANES - lax.rem(ki * nbt, LANES), LANES)
        lane = _iota((tq, LANES), 1)
        keep = (lane >= AUG) & (lane < AUG + nbt)
        return jnp.where(keep, pltpu.roll(rows, shift, 1), 0.0).astype(BF16)

    def scores(g, qc, near, masks):
        kg = g if kb > 1 else 0
        bg = g if bh > 1 else 0
        ql = slice(qc * tq, (qc + 1) * tq)
        qs = [q_ref[g, ql, :] for q_ref in q_refs]
        if use_sel:
            qs[0] = jnp.where(_iota((tq, LANES), 1) < AUG, qs[0], masks[(g if sb > 1 else 0, qc)])
        chunks, top = [], None
        for c in range(nchunk):
            kr = slice(c * kc, (c + 1) * kc)
            s = _dot_nt(k_refs[0][kg, kr, :], qs[0])
            for p in range(1, nparts):
                s = s + _dot_nt(k_refs[p][kg, kr, :], qs[p])
            if near:
                s = s + bias_ref[bg, kr, ql]
            chunks.append(s)
            top = s if top is None else jnp.maximum(top, s)
        return chunks, top

    def update(g, qc, near, chunks, top):
        kg = g if kb > 1 else 0
        ql = slice(qc * tq, (qc + 1) * tq)
        shift = 0.0 if near else far_ref[head0 + g]
        m = m_ref[g, :, ql]
        m_new = jnp.maximum(m, jnp.max(top, axis=0, keepdims=True) + shift)
        alpha = jnp.exp2(m - m_new)
        base = m_new - shift
        pv = psum = None
        for c in range(nchunk):
            p = jnp.exp2(chunks[c] - base)
            ps = jnp.sum(p, axis=0, keepdims=True)
            pd = _dot(vt_ref[kg, :, c * kc:(c + 1) * kc], p.astype(BF16))
            pv = pd if pv is None else pv + pd
            psum = ps if psum is None else psum + ps
        m_ref[g, :, ql] = m_new
        l_ref[g, :, ql] = alpha * l_ref[g, :, ql] + psum
        acc_ref[g, :, ql] = alpha * acc_ref[g, :, ql] + pv

    def body(near):
        masks = {}
        if use_sel:
            masks = {(si, qc): mask_lanes(si, qc) for si in range(sb) for qc in range(nlane)}
        items = [(g, qc) for g in range(hb) for qc in range(nlane)]
        ahead = 2
        pending = [scores(*item, near, masks) for item in items[:ahead]]
        for i, item in enumerate(items):
            if i + ahead < len(items):
                pending.append(scores(*items[i + ahead], near, masks))
            update(*item, near, *pending.pop(0))

    @pl.when(qi - ki < nd)
    def _():
        body(True)

    @pl.when(qi - ki >= nd)
    def _():
        body(False)

    @pl.when(ki == qi)
    def _():
        for g in range(hb):
            dv = acc_ref.shape[1]
            o_ref[:, g * dv:(g + 1) * dv] = jnp.transpose(acc_ref[g] / l_ref[g]).astype(o_ref.dtype)


def flash_attention(q_parts, k_parts, v, bias, far, sel=None, t=512, blk=1, tq=256, kc=128):
    b, ng, hb, s, _ = q_parts[0].shape
    kb = v.shape[2]
    dv = v.shape[-1]
    t = min(t, s)
    tq, kc = min(tq, t), min(kc, t)
    nq = s // t
    nd = bias.shape[1]
    pairs = [(i, j) for i in range(nq) for j in range(i + 1)]
    qt = jnp.asarray(np.array([p[0] for p in pairs], np.int32))
    kt = jnp.asarray(np.array([p[1] for p in pairs], np.int32))
    nparts = len(q_parts)
    use_sel = sel is not None
    sb = sel.shape[2] if use_sel else 1
    bh = hb if bias.shape[0] > 1 else 1
    nbt = t // blk
    vt = jnp.swapaxes(v, -1, -2)
    q_parts, k_parts = list(q_parts), list(k_parts)
    if use_sel:
        onehot = (np.arange(s)[:, None] % t // blk == np.arange(LANES - AUG)[None, :]).astype(np.float32)
        k0 = k_parts[0]
        k_parts[0] = jnp.concatenate(
            [k0, jnp.broadcast_to(jnp.asarray(onehot, k0.dtype), k0.shape[:-1] + (LANES - AUG,))], axis=-1)
        q_parts[0] = jnp.pad(q_parts[0], ((0, 0),) * 4 + ((0, LANES - AUG),))
        selq = jnp.swapaxes(sel, -1, -2)
        selq = jnp.pad(selq, ((0, 0),) * 4 + ((0, LANES - selq.shape[-1]),))

    in_specs, args = [], []
    for qp in q_parts:
        in_specs.append(pl.BlockSpec((None, None, hb, t, qp.shape[-1]), lambda bi, n, st, qt, kt: (bi, n, 0, qt[st], 0)))
        args.append(qp)
    for kp in k_parts:
        in_specs.append(pl.BlockSpec((None, None, kb, t, kp.shape[-1]), lambda bi, n, st, qt, kt: (bi, n, 0, kt[st], 0)))
        args.append(kp)
    in_specs.append(pl.BlockSpec((None, None, kb, dv, t), lambda bi, n, st, qt, kt: (bi, n, 0, 0, kt[st])))
    args.append(vt)
    if use_sel:
        in_specs.append(pl.BlockSpec((None, None, sb, t, LANES), lambda bi, n, st, qt, kt: (bi, n, 0, qt[st], 0)))
        args.append(selq)
    near_tile = lambda qt, kt, st: jnp.minimum(qt[st] - kt[st], nd - 1)
    if bh > 1:
        in_specs.append(pl.BlockSpec((bh, None, t, t), lambda bi, n, st, qt, kt: (n, near_tile(qt, kt, st), 0, 0)))
    else:
        in_specs.append(pl.BlockSpec((1, None, t, t), lambda bi, n, st, qt, kt: (0, near_tile(qt, kt, st), 0, 0)))
    in_specs.append(pl.BlockSpec(memory_space=pltpu.SMEM))
    args += [bias, far]
    kern = functools.partial(_flash_kernel, hb=hb, kb=kb, sb=sb, bh=bh, t=t, tq=tq, kc=kc, nparts=nparts,
                             use_sel=use_sel, nbt=nbt, nd=nd)
    grid_spec = pltpu.PrefetchScalarGridSpec(
        num_scalar_prefetch=2,
        grid=(b, ng, len(pairs)),
        in_specs=in_specs,
        out_specs=pl.BlockSpec((None, t, hb * dv), lambda bi, n, st, qt, kt: (bi, qt[st], n)),
        scratch_shapes=[pltpu.VMEM((hb, 1, t), F32), pltpu.VMEM((hb, 1, t), F32), pltpu.VMEM((hb, dv, t), F32)])
    return pl.pallas_call(
        kern,
        grid_spec=grid_spec,
        out_shape=jax.ShapeDtypeStruct((b, s, ng * hb * dv), F32),
        compiler_params=_params("parallel", "parallel", "arbitrary"),
        name="flash_attention",
    )(qt, kt, *args)


def _window_kernel(*refs, group, t, ntile, has_sink):
    q_ref = refs[0]
    k_refs = refs[1:1 + ntile]
    vt_refs = refs[1 + ntile:1 + 2 * ntile]
    bias_ref = refs[1 + 2 * ntile]
    sink_ref = refs[2 + 2 * ntile] if has_sink else None
    o_ref = refs[-1]
    kvh = pl.program_id(1)

    def scores(g):
        q = q_ref[g]
        ss = [_dot_nt(k_refs[r][...], q) + bias_ref[g, ntile - 1 - r] for r in range(ntile)]
        top = ss[0]
        for s in ss[1:]:
            top = jnp.maximum(top, s)
        return ss, top

    def finish(g, ss, top):
        m = jnp.max(top, axis=0, keepdims=True)
        den = None
        if has_sink:
            sink = sink_ref[kvh * group + g] * LOG2E
            m = jnp.maximum(m, sink)
            den = jnp.exp2(sink - m)
        acc = None
        for r in range(ntile):
            p = jnp.exp2(ss[r] - m)
            ps = jnp.sum(p, axis=0, keepdims=True)
            pv = _dot(vt_refs[r][...], p.astype(BF16))
            den = ps if den is None else den + ps
            acc = pv if acc is None else acc + pv
        dh = acc.shape[0]
        o_ref[:, g * dh:(g + 1) * dh] = jnp.transpose(acc / den)

    pending = scores(0)
    for g in range(group):
        nxt = scores(g + 1) if g + 1 < group else None
        finish(g, *pending)
        pending = nxt


def window_attention(q, k, v, bank, t, window, sinks=None):
    b, hk, g, s, dh = q.shape
    ntile = window // t + 1
    kp = jnp.pad(k, ((0, 0), (0, 0), (window, 0), (0, 0)))
    vtp = jnp.pad(jnp.swapaxes(v, -1, -2), ((0, 0), (0, 0), (0, 0), (window, 0)))
    has_sink = sinks is not None
    in_specs = [pl.BlockSpec((None, None, g, t, dh), lambda bi, h, i: (bi, h, 0, i, 0))]
    args = [q]
    for r in range(ntile):
        in_specs.append(pl.BlockSpec((None, None, t, dh), lambda bi, h, i, r=r: (bi, h, i + r, 0)))
        args.append(kp)
    for r in range(ntile):
        in_specs.append(pl.BlockSpec((None, None, dh, t), lambda bi, h, i, r=r: (bi, h, 0, i + r)))
        args.append(vtp)
    in_specs.append(pl.BlockSpec((g, ntile, t, t), lambda bi, h, i: (h, 0, 0, 0)))
    args.append(bank)
    if has_sink:
        in_specs.append(pl.BlockSpec(memory_space=pltpu.SMEM))
        args.append(sinks)
    kern = functools.partial(_window_kernel, group=g, t=t, ntile=ntile, has_sink=has_sink)
    return pl.pallas_call(
        kern,
        grid=(b, hk, s // t),
        in_specs=in_specs,
        out_specs=pl.BlockSpec((None, t, g * dh), lambda bi, h, i: (bi, i, h)),
        out_shape=jax.ShapeDtypeStruct((b, s, hk * g * dh), F32),
        compiler_params=_params("parallel", "parallel", "parallel"),
        name="window_attention",
    )(*args)


def _rms(x, g):
    return x * lax.rsqrt(jnp.mean(x * x, axis=-1, keepdims=True) + NORM_EPS) * g


def _mla_up_kernel(cq_ref, ckv_ref, kr_ref, krs_ref, qn_ref, kvn_ref, wq_ref, wkv_ref, cs_ref, sn_ref,
                   qnope_ref, qrope_ref, knope_ref, v_ref, krope_ref, *, scale):
    cs = cs_ref[...]
    sn = sn_ref[...]
    q = _dot(_rms(cq_ref[...], qn_ref[...]).astype(BF16), wq_ref[...]) * scale
    kv = _dot(_rms(ckv_ref[...], kvn_ref[...]).astype(BF16), wkv_ref[...])
    wq_head = MLA_NOPE + 2 * MLA_ROPE
    for h in range(MLA_HEADS):
        base = h * wq_head
        qnope_ref[h] = q[:, base:base + MLA_NOPE].astype(qnope_ref.dtype)
        x = q[:, base + MLA_NOPE:base + MLA_NOPE + MLA_ROPE]
        xs = q[:, base + MLA_NOPE + MLA_ROPE:base + wq_head]
        qrope_ref[h] = (x * cs + xs * sn).astype(qrope_ref.dtype)
        kb = h * (MLA_NOPE + MLA_V)
        knope_ref[h] = kv[:, kb:kb + MLA_NOPE].astype(knope_ref.dtype)
        v_ref[h] = kv[:, kb + MLA_NOPE:kb + MLA_NOPE + MLA_V].astype(v_ref.dtype)
    kr = (kr_ref[...] * cs + krs_ref[...] * sn).astype(krope_ref.dtype)
    for h in range(MLA_HEADS):
        krope_ref[h] = kr


def mla_up(c_q, c_kv, k_rope, k_rope_sw, q_norm, kv_norm, w_q_up, w_kv_up, tm=512):
    b, s, _ = c_q.shape
    half = MLA_ROPE // 2
    inv = ROPE_THETA ** (-np.arange(0, MLA_ROPE, 2, dtype=np.float32) / np.float32(MLA_ROPE))
    ang = np.arange(s, dtype=np.float32)[:, None] * inv[None, :].astype(np.float32)
    cos, sin = np.cos(ang).astype(np.float32), np.sin(ang).astype(np.float32)
    cs = jnp.asarray(np.concatenate([cos, cos], axis=1))
    sn = jnp.asarray(np.concatenate([-sin, sin], axis=1))
    dq = MLA_NOPE + MLA_ROPE
    wq = w_q_up.reshape(MLA_Q_RANK, MLA_HEADS, dq)
    rope_cols = wq[:, :, MLA_NOPE:]
    swapped = jnp.concatenate([rope_cols[:, :, half:], rope_cols[:, :, :half]], axis=2)
    wq_aug = jnp.concatenate([wq, swapped], axis=2).reshape(MLA_Q_RANK, MLA_HEADS * (dq + MLA_ROPE)).astype(BF16)
    wkv = w_kv_up.astype(BF16)
    scale = (MLA_NOPE + MLA_ROPE) ** -0.5 * LOG2E
    tm = min(tm, s)
    row = lambda w: pl.BlockSpec((None, tm, w), lambda bi, i: (bi, i, 0))
    full = lambda a: pl.BlockSpec(a.shape, lambda bi, i: (0,) * a.ndim)
    head = lambda w: pl.BlockSpec((None, MLA_HEADS, tm, w), lambda bi, i: (bi, 0, i, 0))
    qn2, kvn2 = q_norm.reshape(1, -1), kv_norm.reshape(1, -1)
    outs = pl.pallas_call(
        functools.partial(_mla_up_kernel, scale=scale),
        grid=(b, s // tm),
        in_specs=[row(MLA_Q_RANK), row(MLA_KV_RANK), row(MLA_ROPE), row(MLA_ROPE), full(qn2), full(kvn2),
                  full(wq_aug), full(wkv),
                  pl.BlockSpec((tm, MLA_ROPE), lambda bi, i: (i, 0)), pl.BlockSpec((tm, MLA_ROPE), lambda bi, i: (i, 0))],
        out_specs=[head(MLA_NOPE), head(MLA_ROPE), head(MLA_NOPE), head(MLA_V), head(MLA_ROPE)],
        out_shape=[jax.ShapeDtypeStruct((b, MLA_HEADS, s, w), BF16)
                   for w in (MLA_NOPE, MLA_ROPE, MLA_NOPE, MLA_V, MLA_ROPE)],
        compiler_params=_params("parallel", "parallel"),
        name="mla_up",
    )(c_q, c_kv, k_rope, k_rope_sw, qn2, kvn2, wq_aug, wkv, cs, sn)
    return outs


def _out_kernel(*refs, gated):
    if gated:
        oc_ref, os_ref, ow_ref, gt_ref, ex_ref, ob_ref, x_ref, gm_ref, w_ref, o_ref = refs
        half = oc_ref.shape[-1]
        ge = _dot(jax.nn.sigmoid(gt_ref[...]), ex_ref[...], precision=HI)
        oa = ge[:, :half] * oc_ref[...] + ge[:, half:2 * half] * os_ref[...] + ge[:, 2 * half:] * ow_ref[...]
    else:
        oa_ref, ob_ref, x_ref, gm_ref, w_ref, o_ref = refs
        half = oa_ref.shape[-1]
        oa = oa_ref[...]
    mix = _dot(oa.astype(BF16), w_ref[:half, :]) + _dot(ob_ref[...].astype(BF16), w_ref[half:, :])
    o_ref[...] = x_ref[...] + gm_ref[...] * mix


def out_project(parts, ob, x, gate_m, w_out, gates=None, tm=512):
    b, s, d = x.shape
    gated = gates is not None
    tm = min(tm, s)
    row = lambda a: pl.BlockSpec((None, tm, a.shape[-1]), lambda bi, i: (bi, i, 0))
    full = lambda a: pl.BlockSpec(a.shape, lambda bi, i: (0,) * a.ndim)
    args, in_specs = [], []
    for p in parts:
        args.append(p); in_specs.append(row(p))
    if gated:
        half = parts[0].shape[-1]
        nh = half // HEAD_DIM
        ex = np.zeros((LANES, 3 * half), np.float32)
        for h in range(nh):
            for br in range(3):
                ex[h * 3 + br, br * half + h * HEAD_DIM: br * half + (h + 1) * HEAD_DIM] = 1.0
        gpad = jnp.pad(gates, ((0, 0), (0, 0), (0, LANES - gates.shape[-1])))
        ex = jnp.asarray(ex)
        args += [gpad, ex]; in_specs += [row(gpad), full(ex)]
    gm = gate_m.reshape(b, 1, d)
    wb = w_out.astype(BF16)
    args += [ob, x, gm, wb]
    in_specs += [row(ob), row(x), pl.BlockSpec((None, 1, d), lambda bi, i: (bi, 0, 0)), full(wb)]
    return pl.pallas_call(
        functools.partial(_out_kernel, gated=gated),
        grid=(b, s // tm),
        in_specs=in_specs,
        out_specs=pl.BlockSpec((None, tm, d), lambda bi, i: (bi, i, 0)),
        out_shape=jax.ShapeDtypeStruct((b, s, d), F32),
        compiler_params=_params("parallel", "parallel"),
        name="out_project",
    )(*args)


def _ffn_pre_kernel(x_ref, g_ref, sc_ref, sh_ref, rw_ref, rb_ref, h_ref, cw_ref):
    h = _norm_mod(x_ref[...], g_ref[...], sc_ref[...], sh_ref[...])
    h_ref[...] = h.astype(h_ref.dtype)
    aff = jax.nn.sigmoid(_dot_nt(rw_ref[...], h, precision=HI))
    biased = aff + rb_ref[...]
    epg = EXPERTS_PER_GROUP
    brow = [biased[e:e + 1, :] for e in range(N_EXPERTS)]
    arow = [aff[e:e + 1, :] for e in range(N_EXPERTS)]
    best = gsel = None
    for gi in range(N_GROUPS):
        a, b_, c, d_ = brow[gi * epg:(gi + 1) * epg]
        hi1, lo1, hi2, lo2 = jnp.maximum(a, b_), jnp.minimum(a, b_), jnp.maximum(c, d_), jnp.minimum(c, d_)
        score = jnp.maximum(hi1, hi2) + jnp.maximum(jnp.minimum(hi1, hi2), jnp.maximum(lo1, lo2))
        if gi == 0:
            best, gsel = score, jnp.zeros(score.shape, jnp.int32)
        else:
            better = score > best
            gsel = jnp.where(better, gi, gsel)
            best = jnp.where(better, score, best)

    def in_group(rows, j):
        v = rows[j]
        for gi in range(1, N_GROUPS):
            v = jnp.where(gsel == gi, rows[gi * epg + j], v)
        return v

    bv = [in_group(brow, j) for j in range(epg)]
    av = [in_group(arow, j) for j in range(epg)]

    def argmax_excluding(skip):
        val = idx = None
        for j in range(epg):
            cand = bv[j] if skip is None else jnp.where(skip == j, -jnp.inf, bv[j])
            if j == 0:
                val, idx = cand, jnp.zeros(cand.shape, jnp.int32)
            else:
                better = cand > val
                idx = jnp.where(better, j, idx)
                val = jnp.where(better, cand, val)
        return idx

    first = argmax_excluding(None)
    second = argmax_excluding(first)

    def pick(rows, idx):
        v = rows[0]
        for j in range(1, epg):
            v = jnp.where(idx == j, rows[j], v)
        return v

    a1, a2 = pick(av, first), pick(av, second)
    tot = a1 + a2
    e1, e2 = gsel * epg + first, gsel * epg + second
    eid = _iota(aff.shape, 0)
    cw_ref[...] = jnp.where(eid == e1, a1 / tot, 0.0) + jnp.where(eid == e2, a2 / tot, 0.0)


def ffn_pre(x, g, sc, sh, router_w, router_b, tm=512):
    b, s, d = x.shape
    e = router_w.shape[1]
    tm = min(tm, s)
    return pl.pallas_call(
        _ffn_pre_kernel,
        grid=(b, s // tm),
        in_specs=[pl.BlockSpec((None, tm, d), lambda bi, i: (bi, i, 0)),
                  pl.BlockSpec((1, d), lambda bi, i: (0, 0)),
                  pl.BlockSpec((None, 1, d), lambda bi, i: (bi, 0, 0)),
                  pl.BlockSpec((None, 1, d), lambda bi, i: (bi, 0, 0)),
                  pl.BlockSpec((e, d), lambda bi, i: (0, 0)),
                  pl.BlockSpec((e, 1), lambda bi, i: (0, 0))],
        out_specs=[pl.BlockSpec((None, tm, d), lambda bi, i: (bi, i, 0)),
                   pl.BlockSpec((None, e, tm), lambda bi, i: (bi, 0, i))],
        out_shape=[jax.ShapeDtypeStruct((b, s, d), BF16), jax.ShapeDtypeStruct((b, e, s), F32)],
        compiler_params=_params("parallel", "parallel"),
        name="ffn_pre",
    )(x, g.reshape(1, d), sc.reshape(b, 1, d), sh.reshape(b, 1, d), router_w.T, router_b.reshape(e, 1))


def _moe_kernel(h_ref, cw_ref, x_ref, gf_ref, wg_ref, wu_ref, wd_ref, o_ref, acc_ref):
    e = pl.program_id(2)

    @pl.when(e == 0)
    def _():
        acc_ref[...] = jnp.zeros(acc_ref.shape, F32)

    h = h_ref[...]
    a = _dot(h, wg_ref[...].astype(BF16))
    u = _dot(h, wu_ref[...].astype(BF16))
    cw = cw_ref[...]
    c = jnp.sum(jnp.where(_iota(cw.shape, 1) == e, cw, 0.0), axis=-1, keepdims=True)
    hid = (a * jax.nn.sigmoid(a)) * u * c
    acc_ref[...] += _dot(hid.astype(BF16), wd_ref[...].astype(BF16))

    @pl.when(e == pl.num_programs(2) - 1)
    def _():
        o_ref[...] = x_ref[...] + gf_ref[...] * acc_ref[...]


def moe_dense(h, cw, x, gate_f, w_gate, w_up, w_down, layer, tm=1024):
    b, s, d = x.shape
    _, ne, _, f = w_gate.shape
    tm = min(tm, s)
    return pl.pallas_call(
        _moe_kernel,
        grid=(b, s // tm, ne),
        in_specs=[pl.BlockSpec((None, tm, d), lambda bi, i, e: (bi, i, 0)),
                  pl.BlockSpec((None, tm, ne), lambda bi, i, e: (bi, i, 0)),
                  pl.BlockSpec((None, tm, d), lambda bi, i, e: (bi, i, 0)),
                  pl.BlockSpec((None, 1, d), lambda bi, i, e: (bi, 0, 0)),
                  pl.BlockSpec((None, None, d, f), lambda bi, i, e: (layer, e, 0, 0)),
                  pl.BlockSpec((None, None, d, f), lambda bi, i, e: (layer, e, 0, 0)),
                  pl.BlockSpec((None, None, f, d), lambda bi, i, e: (layer, e, 0, 0))],
        out_specs=pl.BlockSpec((None, tm, d), lambda bi, i, e: (bi, i, 0)),
        out_shape=jax.ShapeDtypeStruct((b, s, d), F32),
        scratch_shapes=[pltpu.VMEM((tm, d), F32)],
        compiler_params=_params("parallel", "parallel", "arbitrary"),
        name="moe_dense",
    )(h, cw, x, gate_f.reshape(b, 1, d), w_gate, w_up, w_down)


def _final_norm_kernel(x_ref, g_ref, o_ref):
    o_ref[...] = _rms(x_ref[...], g_ref[...])


def final_rmsnorm(x, g, tm=512):
    b, s, d = x.shape
    tm = min(tm, s)
    return pl.pallas_call(
        _final_norm_kernel,
        grid=(b, s // tm),
        in_specs=[pl.BlockSpec((None, tm, d), lambda bi, i: (bi, i, 0)), pl.BlockSpec((1, d), lambda bi, i: (0, 0))],
        out_specs=pl.BlockSpec((None, tm, d), lambda bi, i: (bi, i, 0)),
        out_shape=jax.ShapeDtypeStruct((b, s, d), F32),
        compiler_params=_params("parallel", "parallel"),
        name="final_rmsnorm",
    )(x, g.reshape(1, d))


def _heads(x, nh):
    b, s, w = x.shape
    return x.reshape(b, s, nh, w // nh).transpose(0, 2, 1, 3)


def even_mixer(proj, x, gate_m, w_out, pos_k, pos_v, ck_w1, ck_w2, cv_w1, cv_w2, rel_table, banks):
    b, s, _ = proj.shape
    off = _offsets(EVEN_WIDTHS)
    q_a, kc, vc, ks, vs, kw, vw, gates, q_b, k_b, v_b = (proj[..., lo:hi] for lo, hi in off)
    hk, g, dh = NSA_KV_HEADS, NSA_GROUP, HEAD_DIM
    scale = dh ** -0.5
    qa = (_heads(q_a, NSA_HEADS) * scale).reshape(b, hk, g, s, dh)
    qa16_log2 = (qa * LOG2E).astype(BF16)
    (bank_l, far_l, t_l), (bank_w, t_w) = banks["dense"], banks["nsa_window"]

    nc = s // CMP_STRIDE
    kv = jnp.stack([_heads(kc, hk), _heads(vc, hk)]).reshape(2, b * hk, s, dh)
    cmp = nsa_compress(kv, jnp.stack([ck_w1, cv_w1]), jnp.stack([pos_k, pos_v]), jnp.stack([ck_w2, cv_w2]))
    cmp = cmp.reshape(2, b, hk, nc, dh)
    o_c, sel = nsa_cmp_attention(qa, cmp[0], cmp[1], bias_cmp(rel_table, s, nc, tq=min(256, s)), tq=min(256, s))
    ks5 = _heads(ks, hk).astype(BF16)[:, :, None]
    vs5 = _heads(vs, hk).astype(BF16)[:, :, None]
    o_s = flash_attention([qa16_log2], [ks5], vs5, bank_l, far_l, sel=sel, t=t_l, blk=SLC_BLOCK)
    o_w = window_attention(qa16_log2, _heads(kw, hk).astype(BF16), _heads(vw, hk).astype(BF16), bank_w, t_w, NSA_WINDOW)

    hb = 4
    qb = _heads(q_b, MOBA_HEADS)
    kb = _heads(k_b, MOBA_HEADS)
    vb = _heads(v_b, MOBA_HEADS)
    selb = moba_select(qb, kb, tq=min(2048, s))
    ngb = MOBA_HEADS // hb
    r5 = lambda a: a.reshape(b, ngb, hb, a.shape[-2], a.shape[-1])
    o_b = flash_attention([r5((qb * (scale * LOG2E)).astype(BF16))], [r5(kb.astype(BF16))], r5(vb.astype(BF16)),
                          bank_l, far_l, sel=r5(selb), t=t_l, blk=MOBA_BLOCK)

    return out_project([o_c, o_s, o_w], o_b, x, gate_m, w_out, gates=gates)


def odd_mixer(proj, x, gate_m, w_out, q_norm, kv_norm, w_q_up, w_kv_up, sinks, banks):
    b, s, _ = proj.shape
    off = _offsets(ODD_WIDTHS + (MLA_ROPE,))
    c_q, c_kv, k_rope, q_d, k_d, v_d, k_rope_sw = (proj[..., lo:hi] for lo, hi in off)
    qn, qr, kn, v, kr = mla_up(c_q, c_kv, k_rope, k_rope_sw, q_norm, kv_norm, w_q_up, w_kv_up)
    hb = MLA_HEADS
    r5 = lambda a: a.reshape(b, 1, hb, s, a.shape[-1])
    bank_c, far_c, t_c = banks["causal"]
    lane_pad = jnp.zeros(qn.shape[:-1] + (-(MLA_NOPE + MLA_ROPE) % LANES,), qn.dtype)
    q_full = jnp.concatenate([qn, qr, lane_pad], axis=-1)
    k_full = jnp.concatenate([kn, kr, lane_pad], axis=-1)
    o_c = flash_attention([r5(q_full)], [r5(k_full)], r5(v), bank_c, far_c, t=t_c)

    hk, g, dh = SWA_KV_HEADS, SWA_GROUP, HEAD_DIM
    scale = dh ** -0.5
    qd = (_heads(q_d, SWA_HEADS) * (scale * LOG2E)).reshape(b, hk, g, s, dh).astype(BF16)
    bank_s, t_s = banks["swa"]
    o_d = window_attention(qd, _heads(k_d, hk).astype(BF16), _heads(v_d, hk).astype(BF16), bank_s, t_s, SWA_WINDOW,
                           sinks=sinks)
    return out_project([o_c], o_d, x, gate_m, w_out)


def kernel(x, c, rel_table, router_w, router_b, final_norm, norm_mix, norm_ffn, ada_w, ada_b, moe_w_gate, moe_w_up, moe_w_down, ev_w_in, ev_w_out, nsa_pos_k, nsa_pos_v, nsa_ck_w1, nsa_ck_w2, nsa_cv_w1, nsa_cv_w2, od_w_in, od_w_out, mla_q_norm, mla_kv_norm, mla_w_q_up, mla_w_kv_up, swa_sinks):
    b, s, d = x.shape
    depth = ada_w.shape[0]
    mods = ada_all(c, ada_w, ada_b)
    t_dense = min(512, s)
    bank_l = bias_bank(rel_table, t_dense)
    far_l = rel_table[N_BUCKETS - 1] * LOG2E
    future = np.arange(t_dense)[:, None] > np.arange(t_dense)[None, :]
    bank_c = jnp.asarray(np.where(future, NEG, 0.0).astype(np.float32)[None, None])
    t_w = min(256, s)
    banks = {"dense": (bank_l, far_l, t_dense), "causal": (bank_c, jnp.zeros((N_BIAS_HEADS,), F32), t_dense),
             "nsa_window": (bias_bank(rel_table, t_w, window=NSA_WINDOW), t_w),
             "swa": (bias_bank(rel_table, SWA_WINDOW, window=SWA_WINDOW), SWA_WINDOW)}

    for layer in range(depth):
        shift_m, scale_m, gate_m, shift_f, scale_f, gate_f = jnp.split(mods[layer], 6, axis=-1)
        i = layer // 2
        if layer % 2 == 0:
            w_in = ev_w_in[i]
            pad = -w_in.shape[1] % LANES
            proj = norm_mod_matmul(x, norm_mix[layer], scale_m, shift_m, jnp.pad(w_in, ((0, 0), (0, pad))).astype(BF16))
            x = even_mixer(proj, x, gate_m, ev_w_out[i], nsa_pos_k[i], nsa_pos_v[i], nsa_ck_w1[i], nsa_ck_w2[i],
                           nsa_cv_w1[i], nsa_cv_w2[i], rel_table, banks)
        else:
            w_in = od_w_in[i]
            lo = MLA_Q_RANK + MLA_KV_RANK
            half = MLA_ROPE // 2
            sw = jnp.concatenate([w_in[:, lo + half:lo + MLA_ROPE], w_in[:, lo:lo + half]], axis=1)
            proj = norm_mod_matmul(x, norm_mix[layer], scale_m, shift_m, jnp.concatenate([w_in, sw], axis=1).astype(BF16))
            x = odd_mixer(proj, x, gate_m, od_w_out[i], mla_q_norm[i], mla_kv_norm[i], mla_w_q_up[i], mla_w_kv_up[i],
                          swa_sinks[i], banks)
        h, cw = ffn_pre(x, norm_ffn[layer], scale_f, shift_f, router_w, router_b)
        x = moe_dense(h, cw.transpose(0, 2, 1), x, gate_f, moe_w_gate, moe_w_up, moe_w_down, layer)
    return final_rmsnorm(x, final_norm)
```

```python
import functools
import math

import numpy as np
import jax
import jax.numpy as jnp
from jax import lax
from jax.experimental import pallas as pl
from jax.experimental.pallas import tpu as pltpu

F32 = jnp.float32
BF16 = jnp.bfloat16
HI = lax.Precision.HIGHEST

D_MODEL = 1024
HEAD_DIM = 64
NEG = -1e30
BIG = 1e30
M_INIT = -1e9
NORM_EPS = 1e-6
LOG2E = math.log2(math.e)

N_BUCKETS = 32
MAX_DISTANCE = 1024
N_BIAS_HEADS = 8

NSA_HEADS = 8
NSA_KV_HEADS = 2
NSA_GROUP = NSA_HEADS // NSA_KV_HEADS
CMP_LEN = 32
CMP_STRIDE = 16
CMP_HIDDEN = 256
SLC_BLOCK = 64
SLC_TOPN = 16
NSA_WINDOW = 512

MOBA_HEADS = 8
MOBA_BLOCK = 256
MOBA_TOPK = 3

MLA_HEADS = 4
MLA_Q_RANK = 256
MLA_KV_RANK = 128
MLA_NOPE = 128
MLA_ROPE = 64
MLA_V = 128
ROPE_THETA = 10000.0

SWA_HEADS = 8
SWA_KV_HEADS = 2
SWA_GROUP = SWA_HEADS // SWA_KV_HEADS
SWA_WINDOW = 128

N_EXPERTS = 16
N_GROUPS = 4
EXPERTS_PER_GROUP = N_EXPERTS // N_GROUPS
D_EXPERT = 512

EVEN_WIDTHS = (NSA_HEADS * HEAD_DIM,) + (NSA_KV_HEADS * HEAD_DIM,) * 6 + (3 * NSA_HEADS,) + (MOBA_HEADS * HEAD_DIM,) * 3
ODD_WIDTHS = (MLA_Q_RANK, MLA_KV_RANK, MLA_ROPE, SWA_HEADS * HEAD_DIM, SWA_KV_HEADS * HEAD_DIM, SWA_KV_HEADS * HEAD_DIM)

LANES = 128
VMEM_LIMIT = 56 * 1024 * 1024


def _params(*sem):
    return pltpu.CompilerParams(dimension_semantics=sem, vmem_limit_bytes=VMEM_LIMIT)


def _dot(a, b, precision=None):
    return lax.dot_general(a, b, (((1,), (0,)), ((), ())), precision=precision, preferred_element_type=F32)


def _dot_nt(a, b, precision=None):
    return lax.dot_general(a, b, (((1,), (1,)), ((), ())), precision=precision, preferred_element_type=F32)


def _iota(shape, dim):
    return lax.broadcasted_iota(jnp.int32, shape, dim)


def _offsets(widths):
    out, acc = [], 0
    for w in widths:
        out.append((acc, acc + w))
        acc += w
    return out


def _bucket_thresholds():
    d = np.arange(0, 4 * MAX_DISTANCE, dtype=np.int64)
    exact = N_BUCKETS // 2
    x = np.maximum(d, 1).astype(np.float32) / np.float32(exact)
    logp = exact + (np.log(x) / np.float32(math.log(MAX_DISTANCE / exact)) * np.float32(N_BUCKETS - exact)).astype(np.int32)
    bucket = np.where(d < exact, d, np.minimum(logp, N_BUCKETS - 1))
    return [int(np.argmax(bucket >= b)) for b in range(1, N_BUCKETS)]


BUCKET_THR = _bucket_thresholds()
FAR_DIST = BUCKET_THR[-1]


def _ada_kernel(c_ref, w_ref, b_ref, o_ref):
    c = c_ref[...]
    o_ref[...] = _dot(c * jax.nn.sigmoid(c), w_ref[...], precision=HI) + b_ref[...]


def ada_all(c, ada_w, ada_b):
    depth, d, n = ada_w.shape
    rows = 8
    cp = jnp.pad(c, ((0, rows - c.shape[0]), (0, 0)))
    tn = 1536
    out = pl.pallas_call(
        _ada_kernel,
        grid=(depth, n // tn),
        in_specs=[pl.BlockSpec((rows, d), lambda l, j: (0, 0)),
                  pl.BlockSpec((None, d, tn), lambda l, j: (l, 0, j)),
                  pl.BlockSpec((None, 1, tn), lambda l, j: (l, 0, j))],
        out_specs=pl.BlockSpec((None, rows, tn), lambda l, j: (l, 0, j)),
        out_shape=jax.ShapeDtypeStruct((depth, rows, n), F32),
        compiler_params=_params("parallel", "parallel"),
        name="ada",
    )(cp, ada_w, ada_b.reshape(depth, 1, n))
    return out[:, :c.shape[0], :]


def _norm_mod(x, g, sc, sh):
    y = x * lax.rsqrt(jnp.mean(x * x, axis=-1, keepdims=True) + NORM_EPS) * g
    return y * (1.0 + sc) + sh


def _nmm_kernel(x_ref, g_ref, sc_ref, sh_ref, w_ref, *o_refs, outs):
    h = _norm_mod(x_ref[...], g_ref[...], sc_ref[...], sh_ref[...])
    res = _dot(h.astype(BF16), w_ref[...])
    for o_ref, (lo, hi, scale, _) in zip(o_refs, outs):
        part = res[:, lo:hi]
        o_ref[...] = (part if scale == 1.0 else part * scale).astype(o_ref.dtype)


def norm_mod_matmul(x, g, sc, sh, w, outs, tm=512):
    b, s, d = x.shape
    n = w.shape[1]
    tm = min(tm, s)
    return pl.pallas_call(
        functools.partial(_nmm_kernel, outs=outs),
        grid=(b, s // tm),
        in_specs=[pl.BlockSpec((None, tm, d), lambda bi, i: (bi, i, 0)),
                  pl.BlockSpec((1, d), lambda bi, i: (0, 0)),
                  pl.BlockSpec((None, 1, d), lambda bi, i: (bi, 0, 0)),
                  pl.BlockSpec((None, 1, d), lambda bi, i: (bi, 0, 0)),
                  pl.BlockSpec((d, n), lambda bi, i: (0, 0))],
        out_specs=[pl.BlockSpec((None, tm, hi - lo), lambda bi, i: (bi, i, 0)) for lo, hi, _, _ in outs],
        out_shape=[jax.ShapeDtypeStruct((b, s, hi - lo), dt) for lo, hi, _, dt in outs],
        compiler_params=_params("parallel", "parallel"),
        name="norm_mod_matmul",
    )(x, g.reshape(1, d), sc.reshape(b, 1, d), sh.reshape(b, 1, d), w)


def _bias_kernel(tab_ref, o_ref, *, rows, cols, step, cstride, c0, key_major=False, window=None):
    sub = 8
    off = pl.program_id(0) * step - c0
    unit = LOG2E if key_major else 1.0
    if key_major:
        base = _iota((sub, cols), 1) - _iota((sub, cols), 0)
    else:
        base = _iota((sub, cols), 0) - cstride * _iota((sub, cols), 1)

    def body(r, carry):
        dist = off + base + (-r * sub if key_major else r * sub)
        vs = [jnp.full((sub, cols), tab_ref[0, h] * unit, F32) for h in range(N_BIAS_HEADS)]
        for b in range(1, N_BUCKETS):
            ge = dist >= BUCKET_THR[b - 1]
            for h in range(N_BIAS_HEADS):
                vs[h] = jnp.where(ge, tab_ref[b, h] * unit, vs[h])
        masked = dist < 0 if window is None else (dist < 0) | (dist >= window)
        for h in range(N_BIAS_HEADS):
            v = jnp.where(masked, NEG, vs[h]) if key_major else vs[h]
            o_ref[h, pl.ds(pl.multiple_of(r * sub, sub), sub), :] = v
        return carry

    lax.fori_loop(0, rows // sub, body, 0)


def bias_bank(rel_table, t, window=None):
    nd = -(-(FAR_DIST + t - 1) // t) if window is None else window // t + 1
    kern = functools.partial(_bias_kernel, rows=t, cols=t, step=t, cstride=1, c0=0, key_major=True, window=window)
    bank = pl.pallas_call(
        kern,
        grid=(nd,),
        in_specs=[pl.BlockSpec(memory_space=pltpu.SMEM)],
        out_specs=pl.BlockSpec((N_BIAS_HEADS, None, t, t), lambda i: (0, i, 0, 0)),
        out_shape=jax.ShapeDtypeStruct((N_BIAS_HEADS, nd, t, t), F32),
        compiler_params=_params("parallel"),
        name="bias_bank",
    )(rel_table)
    return bank


def _bias_cmp_kernel(tab_ref, o_ref, *, rows, cols):
    sub = 8
    width = min(LANES, cols)
    t0 = pl.program_id(0) * rows
    base = _iota((sub, width), 0) - CMP_STRIDE * _iota((sub, width), 1)

    def body(r, carry):
        rs = pl.ds(pl.multiple_of(r * sub, sub), sub)
        for c in range(cols // width):
            cs = slice(c * width, (c + 1) * width)
            off = t0 + r * sub - (CMP_LEN - 1) - CMP_STRIDE * width * c
            lo = off - CMP_STRIDE * (width - 1)
            hi = off + sub - 1

            @pl.when((hi >= 0) & (lo < FAR_DIST))
            def _():
                dist = off + base
                vs = [jnp.full((sub, width), tab_ref[0, h], F32) for h in range(N_BIAS_HEADS)]
                for b in range(1, N_BUCKETS):
                    ge = dist >= BUCKET_THR[b - 1]
                    for h in range(N_BIAS_HEADS):
                        vs[h] = jnp.where(ge, tab_ref[b, h], vs[h])
                for h in range(N_BIAS_HEADS):
                    o_ref[h, rs, cs] = vs[h]

            @pl.when(lo >= FAR_DIST)
            def _():
                for h in range(N_BIAS_HEADS):
                    o_ref[h, rs, cs] = jnp.full((sub, width), tab_ref[N_BUCKETS - 1, h], F32)

            @pl.when(hi < 0)
            def _():
                for h in range(N_BIAS_HEADS):
                    o_ref[h, rs, cs] = jnp.full((sub, width), tab_ref[0, h], F32)
        return carry

    lax.fori_loop(0, rows // sub, body, 0)


def bias_cmp(rel_table, s, nc, tq=256):
    kern = functools.partial(_bias_cmp_kernel, rows=tq, cols=nc)
    return pl.pallas_call(
        kern,
        grid=(s // tq,),
        in_specs=[pl.BlockSpec(memory_space=pltpu.SMEM)],
        out_specs=pl.BlockSpec((N_BIAS_HEADS, tq, nc), lambda i: (0, i, 0)),
        out_shape=jax.ShapeDtypeStruct((N_BIAS_HEADS, s, nc), F32),
        compiler_params=_params("parallel"),
        name="bias_cmp",
    )(rel_table)


def _compress_kernel(x_ref, w1_ref, pos_ref, w2_ref, o_ref):
    x = x_ref[...]
    w1 = w1_ref[...]
    half = w1.shape[0]
    hid = w1.shape[1] // 2
    r = _dot(x, w1, precision=HI)
    pos = pos_ref[...]
    pb = _dot(pos[:, :half], w1[:, :hid], precision=HI) + _dot(pos[:, half:], w1[:, hid:], precision=HI)
    nxt = pltpu.roll(r[:, hid:], x.shape[0] - 1, 0)
    pre = r[:, :hid] + nxt + pb[0:1, :]
    o_ref[...] = _dot(jax.nn.gelu(pre), w2_ref[...], precision=HI)


def nsa_compress(kv, w1, pos, w2):
    two, bh, s, dh = kv.shape
    nc = s // CMP_STRIDE
    half = CMP_STRIDE * dh
    hid = w1.shape[2]
    x = kv.reshape(two, bh, nc, half)
    w1cat = jnp.concatenate([w1[:, :half, :], w1[:, half:, :]], axis=2)
    posf = jnp.broadcast_to(pos.reshape(two, 1, CMP_LEN * dh), (two, 8, CMP_LEN * dh))
    return pl.pallas_call(
        _compress_kernel,
        grid=(two, bh),
        in_specs=[pl.BlockSpec((None, None, nc, half), lambda w, i: (w, i, 0, 0)),
                  pl.BlockSpec((None, half, 2 * hid), lambda w, i: (w, 0, 0)),
                  pl.BlockSpec((None, 8, CMP_LEN * dh), lambda w, i: (w, 0, 0)),
                  pl.BlockSpec((None, hid, dh), lambda w, i: (w, 0, 0))],
        out_specs=pl.BlockSpec((None, None, nc, dh), lambda w, i: (w, i, 0, 0)),
        out_shape=jax.ShapeDtypeStruct((two, bh, nc, dh), F32),
        compiler_params=_params("parallel", "parallel"),
        name="nsa_compress",
    )(x, w1cat, posf, w2)


def _topn_mask(score, index, n):
    idx_f = index.astype(F32)
    sel = jnp.zeros(score.shape, jnp.bool_)
    for _ in range(n):
        m = jnp.max(score, axis=0, keepdims=True)
        first = jnp.min(jnp.where(score == m, idx_f, float(score.shape[0])), axis=0, keepdims=True)
        pick = idx_f == first
        sel = sel | pick
        score = jnp.where(pick, -jnp.inf, score)
    return sel


def _cmp_kernel(q_ref, k_ref, v_ref, b_ref, ov_ref, oc_ref, sel_ref, *, tq, nc, nslc, ntop, group):
    qi = pl.program_id(2)
    t = qi * tq + _iota((tq, 1), 0)
    n = _iota((1, nc), 1)
    mask = (n * CMP_STRIDE + (CMP_LEN - 1) <= t) & (n < nc - 1)
    k = k_ref[...]
    v = v_ref[...].astype(BF16)
    dh = k.shape[-1]
    psum = jnp.zeros((tq, nc), F32)
    scores = lambda g: _dot_nt(q_ref[g], k, precision=HI) + b_ref[g]
    s_next = scores(0)
    for g in range(group):
        s, s_next = s_next, (scores(g + 1) if g + 1 < group else None)
        s = jnp.where(mask, s, NEG)
        m = jnp.max(s, axis=-1, keepdims=True)
        p = jnp.where(mask, jnp.exp(s - m), 0.0)
        den = jnp.sum(p, axis=-1, keepdims=True)
        p = p * jnp.where(den > 0.0, 1.0 / den, 0.0)
        oc_ref[:, g * dh:(g + 1) * dh] = _dot(p.astype(BF16), v)
        psum = psum + p
    imp = _dot_nt(ov_ref[...], psum, precision=HI)
    j = _iota((nslc, tq), 0)
    blk = jnp.right_shift(qi * tq + _iota((1, tq), 1), int(math.log2(SLC_BLOCK)))
    forced = (j == 0) | (j == blk) | (j == blk - 1)
    valid = j <= blk
    score = jnp.where(forced, BIG, jnp.where(valid, imp, NEG))
    sel = _topn_mask(score, j, ntop) & valid
    sel_ref[0] = jnp.where(sel, 0.0, NEG)


def nsa_cmp_attention(q, k_cmp, v_cmp, bias_c, tq=256):
    b, hk, g, s, dh = q.shape
    nc = k_cmp.shape[2]
    nslc = s // SLC_BLOCK
    ntop = min(SLC_TOPN, nslc)
    cmp_start = np.arange(nc, dtype=np.int64) * CMP_STRIDE
    slc_lo = np.arange(nslc, dtype=np.int64) * SLC_BLOCK
    overlap = ((cmp_start[:, None] <= slc_lo[None, :] + SLC_BLOCK - 1)
               & (cmp_start[:, None] + CMP_LEN - 1 >= slc_lo[None, :])).astype(np.float32)
    kern = functools.partial(_cmp_kernel, tq=tq, nc=nc, nslc=nslc, ntop=ntop, group=g)
    return pl.pallas_call(
        kern,
        grid=(b, hk, s // tq),
        in_specs=[pl.BlockSpec((None, None, g, tq, dh), lambda bi, h, i: (bi, h, 0, i, 0)),
                  pl.BlockSpec((None, None, nc, dh), lambda bi, h, i: (bi, h, 0, 0)),
                  pl.BlockSpec((None, None, nc, dh), lambda bi, h, i: (bi, h, 0, 0)),
                  pl.BlockSpec((g, tq, nc), lambda bi, h, i: (h, i, 0)),
                  pl.BlockSpec((nslc, nc), lambda bi, h, i: (0, 0))],
        out_specs=[pl.BlockSpec((None, tq, g * dh), lambda bi, h, i: (bi, i, h)),
                   pl.BlockSpec((None, None, 1, nslc, tq), lambda bi, h, i: (bi, h, 0, 0, i))],
        out_shape=[jax.ShapeDtypeStruct((b, s, hk * g * dh), F32),
                   jax.ShapeDtypeStruct((b, hk, 1, nslc, s), F32)],
        compiler_params=_params("parallel", "parallel", "parallel"),
        name="nsa_cmp_attention",
    )(q, k_cmp, v_cmp, bias_c, jnp.asarray(overlap.T))


def _kmean_kernel(k_ref, o_ref, *, nb, blk):
    k = k_ref[...]
    o_ref[...] = jnp.sum(k.reshape(nb, blk, k.shape[-1]), axis=1) * (1.0 / blk)


def _moba_sel_kernel(q_ref, km_ref, sel_ref, *, tq, nb, ntop):
    qi = pl.program_id(2)
    own = jnp.right_shift(qi * tq + _iota((1, tq), 1), int(math.log2(MOBA_BLOCK)))
    j = _iota((nb, tq), 0)
    gate = _dot_nt(km_ref[...], q_ref[...], precision=HI)
    past = j < own
    sel = _topn_mask(jnp.where(past, gate, NEG), j, ntop) & past
    sel_ref[...] = jnp.where(sel | (j == own), 0.0, NEG)


def moba_select(q, k, tq=256):
    b, h, s, dh = q.shape
    nb = s // MOBA_BLOCK
    ntop = min(MOBA_TOPK, nb)
    kmean = pl.pallas_call(
        functools.partial(_kmean_kernel, nb=nb, blk=MOBA_BLOCK),
        grid=(b, h),
        in_specs=[pl.BlockSpec((None, None, s, dh), lambda bi, hi: (bi, hi, 0, 0))],
        out_specs=pl.BlockSpec((None, None, nb, dh), lambda bi, hi: (bi, hi, 0, 0)),
        out_shape=jax.ShapeDtypeStruct((b, h, nb, dh), F32),
        compiler_params=_params("parallel", "parallel"),
        name="moba_kmean",
    )(k)
    return pl.pallas_call(
        functools.partial(_moba_sel_kernel, tq=tq, nb=nb, ntop=ntop),
        grid=(b, h, s // tq),
        in_specs=[pl.BlockSpec((None, None, tq, dh), lambda bi, hi, i: (bi, hi, i, 0)),
                  pl.BlockSpec((None, None, nb, dh), lambda bi, hi, i: (bi, hi, 0, 0))],
        out_specs=pl.BlockSpec((None, None, nb, tq), lambda bi, hi, i: (bi, hi, 0, i)),
        out_shape=jax.ShapeDtypeStruct((b, h, nb, s), F32),
        compiler_params=_params("parallel", "parallel", "parallel"),
        name="moba_select",
    )(q, kmean)


AUG = 64


def _flash_kernel(qt_ref, kt_ref, *refs, hb, kb, sb, bh, t, tq, kc, nparts, use_sel, nbt, nd):
    pos = 0
    q_refs = refs[pos:pos + nparts]; pos += nparts
    k_refs = refs[pos:pos + nparts]; pos += nparts
    vt_ref = refs[pos]; pos += 1
    sel_ref = None
    if use_sel:
        sel_ref = refs[pos]; pos += 1
    bias_ref, far_ref, o_ref, m_ref, l_ref, acc_ref = refs[pos:pos + 6]

    head0 = pl.program_id(1) * hb
    step = pl.program_id(2)
    qi = qt_ref[step]
    ki = kt_ref[step]

    @pl.when(ki == 0)
    def _():
        m_ref[...] = jnp.full(m_ref.shape, M_INIT, F32)
        l_ref[...] = jnp.zeros(l_ref.shape, F32)
        acc_ref[...] = jnp.zeros(acc_ref.shape, F32)

    nchunk = t // kc
    nlane = t // tq

    def mask_lanes(si, qc):
        rows = sel_ref[si, qc * tq:(qc + 1) * tq, :]
        shift = lax.rem(AUG + LANES - lax.rem(ki * nbt, LANES), LANES)
        lane = _iota((tq, LANES), 1)
        keep = (lane >= AUG) & (lane < AUG + nbt)
        return jnp.where(keep, pltpu.roll(rows, shift, 1), 0.0).astype(BF16)

    def scores(g, qc, near, masks):
        kg = g if kb > 1 else 0
        bg = g if bh > 1 else 0
        ql = slice(qc * tq, (qc + 1) * tq)
        qs = [q_ref[g, ql, :] for q_ref in q_refs]
        if use_sel:
            qs[0] = jnp.where(_iota((tq, LANES), 1) < AUG, qs[0], masks[(g if sb > 1 else 0, qc)])
        chunks, top = [], None
        for c in range(nchunk):
            kr = slice(c * kc, (c + 1) * kc)
            s = _dot_nt(k_refs[0][kg, kr, :], qs[0])
            for p in range(1, nparts):
                s = s + _dot_nt(k_refs[p][kg, kr, :], qs[p])
            if near:
                s = s + bias_ref[bg, kr, ql]
            chunks.append(s)
            top = s if top is None else jnp.maximum(top, s)
        return chunks, top

    def update(g, qc, near, chunks, top):
        kg = g if kb > 1 else 0
        ql = slice(qc * tq, (qc + 1) * tq)
        shift = 0.0 if near else far_ref[head0 + g]
        m = m_ref[g, :, ql]
        m_new = jnp.maximum(m, jnp.max(top, axis=0, keepdims=True) + shift)
        alpha = jnp.exp2(m - m_new)
        base = m_new - shift
        pv = psum = None
        for c in range(nchunk):
            p = jnp.exp2(chunks[c] - base)
            ps = jnp.sum(p, axis=0, keepdims=True)
            pd = _dot(vt_ref[kg, :, c * kc:(c + 1) * kc], p.astype(BF16))
            pv = pd if pv is None else pv + pd
            psum = ps if psum is None else psum + ps
        m_ref[g, :, ql] = m_new
        l_ref[g, :, ql] = alpha * l_ref[g, :, ql] + psum
        acc_ref[g, :, ql] = alpha * acc_ref[g, :, ql] + pv

    def body(near):
        masks = {}
        if use_sel:
            masks = {(si, qc): mask_lanes(si, qc) for si in range(sb) for qc in range(nlane)}
        items = [(g, qc) for g in range(hb) for qc in range(nlane)]
        ahead = 2
        pending = [scores(*item, near, masks) for item in items[:ahead]]
        for i, item in enumerate(items):
            if i + ahead < len(items):
                pending.append(scores(*items[i + ahead], near, masks))
            update(*item, near, *pending.pop(0))

    @pl.when(qi - ki < nd)
    def _():
        body(True)

    @pl.when(qi - ki >= nd)
    def _():
        body(False)

    @pl.when(ki == qi)
    def _():
        for g in range(hb):
            dv = acc_ref.shape[1]
            o_ref[:, g * dv:(g + 1) * dv] = jnp.transpose(acc_ref[g] / l_ref[g]).astype(o_ref.dtype)


def flash_attention(q_parts, k_parts, v, bias, far, sel=None, t=512, blk=1, tq=256, kc=128):
    b, ng, hb, s, _ = q_parts[0].shape
    kb = v.shape[2]
    dv = v.shape[-1]
    t = min(t, s)
    tq, kc = min(tq, t), min(kc, t)
    nq = s // t
    nd = bias.shape[1]
    pairs = [(i, j) for i in range(nq) for j in range(i + 1)]
    qt = jnp.asarray(np.array([p[0] for p in pairs], np.int32))
    kt = jnp.asarray(np.array([p[1] for p in pairs], np.int32))
    nparts = len(q_parts)
    use_sel = sel is not None
    sb = sel.shape[2] if use_sel else 1
    bh = hb if bias.shape[0] > 1 else 1
    nbt = t // blk
    vt = jnp.swapaxes(v, -1, -2)
    q_parts, k_parts = list(q_parts), list(k_parts)
    if use_sel:
        onehot = (np.arange(s)[:, None] % t // blk == np.arange(LANES - AUG)[None, :]).astype(np.float32)
        k0 = k_parts[0]
        k_parts[0] = jnp.concatenate(
            [k0, jnp.broadcast_to(jnp.asarray(onehot, k0.dtype), k0.shape[:-1] + (LANES - AUG,))], axis=-1)
        q_parts[0] = jnp.pad(q_parts[0], ((0, 0),) * 4 + ((0, LANES - AUG),))
        selq = jnp.swapaxes(sel, -1, -2)
        selq = jnp.pad(selq, ((0, 0),) * 4 + ((0, LANES - selq.shape[-1]),))

    in_specs, args = [], []
    for qp in q_parts:
        in_specs.append(pl.BlockSpec((None, None, hb, t, qp.shape[-1]), lambda bi, n, st, qt, kt: (bi, n, 0, qt[st], 0)))
        args.append(qp)
    for kp in k_parts:
        in_specs.append(pl.BlockSpec((None, None, kb, t, kp.shape[-1]), lambda bi, n, st, qt, kt: (bi, n, 0, kt[st], 0)))
        args.append(kp)
    in_specs.append(pl.BlockSpec((None, None, kb, dv, t), lambda bi, n, st, qt, kt: (bi, n, 0, 0, kt[st])))
    args.append(vt)
    if use_sel:
        in_specs.append(pl.BlockSpec((None, None, sb, t, LANES), lambda bi, n, st, qt, kt: (bi, n, 0, qt[st], 0)))
        args.append(selq)
    near_tile = lambda qt, kt, st: jnp.minimum(qt[st] - kt[st], nd - 1)
    if bh > 1:
        in_specs.append(pl.BlockSpec((bh, None, t, t), lambda bi, n, st, qt, kt: (n, near_tile(qt, kt, st), 0, 0)))
    else:
        in_specs.append(pl.BlockSpec((1, None, t, t), lambda bi, n, st, qt, kt: (0, near_tile(qt, kt, st), 0, 0)))
    in_specs.append(pl.BlockSpec(memory_space=pltpu.SMEM))
    args += [bias, far]
    kern = functools.partial(_flash_kernel, hb=hb, kb=kb, sb=sb, bh=bh, t=t, tq=tq, kc=kc, nparts=nparts,
                             use_sel=use_sel, nbt=nbt, nd=nd)
    grid_spec = pltpu.PrefetchScalarGridSpec(
        num_scalar_prefetch=2,
        grid=(b, ng, len(pairs)),
        in_specs=in_specs,
        out_specs=pl.BlockSpec((None, t, hb * dv), lambda bi, n, st, qt, kt: (bi, qt[st], n)),
        scratch_shapes=[pltpu.VMEM((hb, 1, t), F32), pltpu.VMEM((hb, 1, t), F32), pltpu.VMEM((hb, dv, t), F32)])
    return pl.pallas_call(
        kern,
        grid_spec=grid_spec,
        out_shape=jax.ShapeDtypeStruct((b, s, ng * hb * dv), F32),
        compiler_params=_params("parallel", "parallel", "arbitrary"),
        name="flash_attention",
    )(qt, kt, *args)


def _window_kernel(*refs, group, t, ntile, has_sink):
    q_ref = refs[0]
    k_refs = refs[1:1 + ntile]
    vt_refs = refs[1 + ntile:1 + 2 * ntile]
    bias_ref = refs[1 + 2 * ntile]
    sink_ref = refs[2 + 2 * ntile] if has_sink else None
    o_ref = refs[-1]
    kvh = pl.program_id(1)

    def scores(g):
        q = q_ref[g]
        ss = [_dot_nt(k_refs[r][...], q) + bias_ref[g, ntile - 1 - r] for r in range(ntile)]
        top = ss[0]
        for s in ss[1:]:
            top = jnp.maximum(top, s)
        return ss, top

    def finish(g, ss, top):
        m = jnp.max(top, axis=0, keepdims=True)
        den = None
        if has_sink:
            sink = sink_ref[kvh * group + g] * LOG2E
            m = jnp.maximum(m, sink)
            den = jnp.exp2(sink - m)
        acc = None
        for r in range(ntile):
            p = jnp.exp2(ss[r] - m)
            ps = jnp.sum(p, axis=0, keepdims=True)
            pv = _dot(vt_refs[r][...], p.astype(BF16))
            den = ps if den is None else den + ps
            acc = pv if acc is None else acc + pv
        dh = acc.shape[0]
        o_ref[:, g * dh:(g + 1) * dh] = jnp.transpose(acc / den)

    pending = scores(0)
    for g in range(group):
        nxt = scores(g + 1) if g + 1 < group else None
        finish(g, *pending)
        pending = nxt


def window_attention(q, k, v, bank, t, window, sinks=None):
    b, hk, g, s, dh = q.shape
    ntile = window // t + 1
    kp = jnp.pad(k, ((0, 0), (0, 0), (window, 0), (0, 0)))
    vtp = jnp.pad(jnp.swapaxes(v, -1, -2), ((0, 0), (0, 0), (0, 0), (window, 0)))
    has_sink = sinks is not None
    in_specs = [pl.BlockSpec((None, None, g, t, dh), lambda bi, h, i: (bi, h, 0, i, 0))]
    args = [q]
    for r in range(ntile):
        in_specs.append(pl.BlockSpec((None, None, t, dh), lambda bi, h, i, r=r: (bi, h, i + r, 0)))
        args.append(kp)
    for r in range(ntile):
        in_specs.append(pl.BlockSpec((None, None, dh, t), lambda bi, h, i, r=r: (bi, h, 0, i + r)))
        args.append(vtp)
    in_specs.append(pl.BlockSpec((g, ntile, t, t), lambda bi, h, i: (h, 0, 0, 0)))
    args.append(bank)
    if has_sink:
        in_specs.append(pl.BlockSpec(memory_space=pltpu.SMEM))
        args.append(sinks)
    kern = functools.partial(_window_kernel, group=g, t=t, ntile=ntile, has_sink=has_sink)
    return pl.pallas_call(
        kern,
        grid=(b, hk, s // t),
        in_specs=in_specs,
        out_specs=pl.BlockSpec((None, t, g * dh), lambda bi, h, i: (bi, i, h)),
        out_shape=jax.ShapeDtypeStruct((b, s, hk * g * dh), F32),
        compiler_params=_params("parallel", "parallel", "parallel"),
        name="window_attention",
    )(*args)


def _rms(x, g):
    return x * lax.rsqrt(jnp.mean(x * x, axis=-1, keepdims=True) + NORM_EPS) * g


def _mla_up_kernel(cq_ref, ckv_ref, kr_ref, krs_ref, qn_ref, kvn_ref, wq_ref, wkv_ref, cs_ref, sn_ref,
                   qnope_ref, qrope_ref, knope_ref, v_ref, krope_ref, *, scale):
    cs = cs_ref[...]
    sn = sn_ref[...]
    q = _dot(_rms(cq_ref[...], qn_ref[...]).astype(BF16), wq_ref[...]) * scale
    kv = _dot(_rms(ckv_ref[...], kvn_ref[...]).astype(BF16), wkv_ref[...])
    wq_head = MLA_NOPE + 2 * MLA_ROPE
    for h in range(MLA_HEADS):
        base = h * wq_head
        qnope_ref[h] = q[:, base:base + MLA_NOPE].astype(qnope_ref.dtype)
        x = q[:, base + MLA_NOPE:base + MLA_NOPE + MLA_ROPE]
        xs = q[:, base + MLA_NOPE + MLA_ROPE:base + wq_head]
        qrope_ref[h] = (x * cs + xs * sn).astype(qrope_ref.dtype)
        kb = h * (MLA_NOPE + MLA_V)
        knope_ref[h] = kv[:, kb:kb + MLA_NOPE].astype(knope_ref.dtype)
        v_ref[h] = kv[:, kb + MLA_NOPE:kb + MLA_NOPE + MLA_V].astype(v_ref.dtype)
    kr = (kr_ref[...] * cs + krs_ref[...] * sn).astype(krope_ref.dtype)
    for h in range(MLA_HEADS):
        krope_ref[h] = kr


def mla_up(c_q, c_kv, k_rope, k_rope_sw, q_norm, kv_norm, w_q_up, w_kv_up, tm=512):
    b, s, _ = c_q.shape
    half = MLA_ROPE // 2
    inv = ROPE_THETA ** (-np.arange(0, MLA_ROPE, 2, dtype=np.float32) / np.float32(MLA_ROPE))
    ang = np.arange(s, dtype=np.float32)[:, None] * inv[None, :].astype(np.float32)
    cos, sin = np.cos(ang).astype(np.float32), np.sin(ang).astype(np.float32)
    cs = jnp.asarray(np.concatenate([cos, cos], axis=1))
    sn = jnp.asarray(np.concatenate([-sin, sin], axis=1))
    dq = MLA_NOPE + MLA_ROPE
    wq = w_q_up.reshape(MLA_Q_RANK, MLA_HEADS, dq)
    rope_cols = wq[:, :, MLA_NOPE:]
    swapped = jnp.concatenate([rope_cols[:, :, half:], rope_cols[:, :, :half]], axis=2)
    wq_aug = jnp.concatenate([wq, swapped], axis=2).reshape(MLA_Q_RANK, MLA_HEADS * (dq + MLA_ROPE)).astype(BF16)
    wkv = w_kv_up.astype(BF16)
    scale = (MLA_NOPE + MLA_ROPE) ** -0.5 * LOG2E
    tm = min(tm, s)
    row = lambda w: pl.BlockSpec((None, tm, w), lambda bi, i: (bi, i, 0))
    full = lambda a: pl.BlockSpec(a.shape, lambda bi, i: (0,) * a.ndim)
    head = lambda w: pl.BlockSpec((None, MLA_HEADS, tm, w), lambda bi, i: (bi, 0, i, 0))
    qn2, kvn2 = q_norm.reshape(1, -1), kv_norm.reshape(1, -1)
    outs = pl.pallas_call(
        functools.partial(_mla_up_kernel, scale=scale),
        grid=(b, s // tm),
        in_specs=[row(MLA_Q_RANK), row(MLA_KV_RANK), row(MLA_ROPE), row(MLA_ROPE), full(qn2), full(kvn2),
                  full(wq_aug), full(wkv),
                  pl.BlockSpec((tm, MLA_ROPE), lambda bi, i: (i, 0)), pl.BlockSpec((tm, MLA_ROPE), lambda bi, i: (i, 0))],
        out_specs=[head(MLA_NOPE), head(MLA_ROPE), head(MLA_NOPE), head(MLA_V), head(MLA_ROPE)],
        out_shape=[jax.ShapeDtypeStruct((b, MLA_HEADS, s, w), BF16)
                   for w in (MLA_NOPE, MLA_ROPE, MLA_NOPE, MLA_V, MLA_ROPE)],
        compiler_params=_params("parallel", "parallel"),
        name="mla_up",
    )(c_q, c_kv, k_rope, k_rope_sw, qn2, kvn2, wq_aug, wkv, cs, sn)
    return outs


def _out_kernel(*refs, gated):
    if gated:
        oc_ref, os_ref, ow_ref, gt_ref, ex_ref, ob_ref, x_ref, gm_ref, w_ref, o_ref = refs
        half = oc_ref.shape[-1]
        ge = _dot(jax.nn.sigmoid(gt_ref[...]), ex_ref[...], precision=HI)
        oa = ge[:, :half] * oc_ref[...] + ge[:, half:2 * half] * os_ref[...] + ge[:, 2 * half:] * ow_ref[...]
    else:
        oa_ref, ob_ref, x_ref, gm_ref, w_ref, o_ref = refs
        half = oa_ref.shape[-1]
        oa = oa_ref[...]
    mix = _dot(oa.astype(BF16), w_ref[:half, :]) + _dot(ob_ref[...].astype(BF16), w_ref[half:, :])
    o_ref[...] = x_ref[...] + gm_ref[...] * mix


def out_project(parts, ob, x, gate_m, w_out, gates=None, tm=512):
    b, s, d = x.shape
    gated = gates is not None
    tm = min(tm, s)
    row = lambda a: pl.BlockSpec((None, tm, a.shape[-1]), lambda bi, i: (bi, i, 0))
    full = lambda a: pl.BlockSpec(a.shape, lambda bi, i: (0,) * a.ndim)
    args, in_specs = [], []
    for p in parts:
        args.append(p); in_specs.append(row(p))
    if gated:
        half = parts[0].shape[-1]
        nh = half // HEAD_DIM
        ex = np.zeros((LANES, 3 * half), np.float32)
        for h in range(nh):
            for br in range(3):
                ex[h * 3 + br, br * half + h * HEAD_DIM: br * half + (h + 1) * HEAD_DIM] = 1.0
        gpad = jnp.pad(gates, ((0, 0), (0, 0), (0, LANES - gates.shape[-1])))
        ex = jnp.asarray(ex)
        args += [gpad, ex]; in_specs += [row(gpad), full(ex)]
    gm = gate_m.reshape(b, 1, d)
    wb = w_out.astype(BF16)
    args += [ob, x, gm, wb]
    in_specs += [row(ob), row(x), pl.BlockSpec((None, 1, d), lambda bi, i: (bi, 0, 0)), full(wb)]
    return pl.pallas_call(
        functools.partial(_out_kernel, gated=gated),
        grid=(b, s // tm),
        in_specs=in_specs,
        out_specs=pl.BlockSpec((None, tm, d), lambda bi, i: (bi, i, 0)),
        out_shape=jax.ShapeDtypeStruct((b, s, d), F32),
        compiler_params=_params("parallel", "parallel"),
        name="out_project",
    )(*args)


def _ffn_pre_kernel(x_ref, g_ref, sc_ref, sh_ref, rw_ref, rb_ref, h_ref, cw_ref):
    h = _norm_mod(x_ref[...], g_ref[...], sc_ref[...], sh_ref[...])
    h_ref[...] = h.astype(h_ref.dtype)
    aff = jax.nn.sigmoid(_dot_nt(rw_ref[...], h, precision=HI))
    biased = aff + rb_ref[...]
    epg = EXPERTS_PER_GROUP
    brow = [biased[e:e + 1, :] for e in range(N_EXPERTS)]
    arow = [aff[e:e + 1, :] for e in range(N_EXPERTS)]
    best = gsel = None
    for gi in range(N_GROUPS):
        a, b_, c, d_ = brow[gi * epg:(gi + 1) * epg]
        hi1, lo1, hi2, lo2 = jnp.maximum(a, b_), jnp.minimum(a, b_), jnp.maximum(c, d_), jnp.minimum(c, d_)
        score = jnp.maximum(hi1, hi2) + jnp.maximum(jnp.minimum(hi1, hi2), jnp.maximum(lo1, lo2))
        if gi == 0:
            best, gsel = score, jnp.zeros(score.shape, jnp.int32)
        else:
            better = score > best
            gsel = jnp.where(better, gi, gsel)
            best = jnp.where(better, score, best)

    def in_group(rows, j):
        v = rows[j]
        for gi in range(1, N_GROUPS):
            v = jnp.where(gsel == gi, rows[gi * epg + j], v)
        return v

    bv = [in_group(brow, j) for j in range(epg)]
    av = [in_group(arow, j) for j in range(epg)]

    def argmax_excluding(skip):
        val = idx = None
        for j in range(epg):
            cand = bv[j] if skip is None else jnp.where(skip == j, -jnp.inf, bv[j])
            if j == 0:
                val, idx = cand, jnp.zeros(cand.shape, jnp.int32)
            else:
                better = cand > val
                idx = jnp.where(better, j, idx)
                val = jnp.where(better, cand, val)
        return idx

    first = argmax_excluding(None)
    second = argmax_excluding(first)

    def pick(rows, idx):
        v = rows[0]
        for j in range(1, epg):
            v = jnp.where(idx == j, rows[j], v)
        return v

    a1, a2 = pick(av, first), pick(av, second)
    tot = a1 + a2
    e1, e2 = gsel * epg + first, gsel * epg + second
    eid = _iota(aff.shape, 0)
    cw_ref[...] = jnp.where(eid == e1, a1 / tot, 0.0) + jnp.where(eid == e2, a2 / tot, 0.0)


def ffn_pre(x, g, sc, sh, router_w, router_b, tm=512):
    b, s, d = x.shape
    e = router_w.shape[1]
    tm = min(tm, s)
    return pl.pallas_call(
        _ffn_pre_kernel,
        grid=(b, s // tm),
        in_specs=[pl.BlockSpec((None, tm, d), lambda bi, i: (bi, i, 0)),
                  pl.BlockSpec((1, d), lambda bi, i: (0, 0)),
                  pl.BlockSpec((None, 1, d), lambda bi, i: (bi, 0, 0)),
                  pl.BlockSpec((None, 1, d), lambda bi, i: (bi, 0, 0)),
                  pl.BlockSpec((e, d), lambda bi, i: (0, 0)),
                  pl.BlockSpec((e, 1), lambda bi, i: (0, 0))],
        out_specs=[pl.BlockSpec((None, tm, d), lambda bi, i: (bi, i, 0)),
                   pl.BlockSpec((None, e, tm), lambda bi, i: (bi, 0, i))],
        out_shape=[jax.ShapeDtypeStruct((b, s, d), BF16), jax.ShapeDtypeStruct((b, e, s), F32)],
        compiler_params=_params("parallel", "parallel"),
        name="ffn_pre",
    )(x, g.reshape(1, d), sc.reshape(b, 1, d), sh.reshape(b, 1, d), router_w.T, router_b.reshape(e, 1))


def _moe_kernel(h_ref, cw_ref, x_ref, gf_ref, wg_ref, wu_ref, wd_ref, o_ref, acc_ref):
    e = pl.program_id(2)

    @pl.when(e == 0)
    def _():
        acc_ref[...] = jnp.zeros(acc_ref.shape, F32)

    h = h_ref[...]
    a = _dot(h, wg_ref[...].astype(BF16))
    u = _dot(h, wu_ref[...].astype(BF16))
    cw = cw_ref[...]
    c = jnp.sum(jnp.where(_iota(cw.shape, 1) == e, cw, 0.0), axis=-1, keepdims=True)
    hid = (a * jax.nn.sigmoid(a)) * u * c
    acc_ref[...] += _dot(hid.astype(BF16), wd_ref[...].astype(BF16))

    @pl.when(e == pl.num_programs(2) - 1)
    def _():
        o_ref[...] = x_ref[...] + gf_ref[...] * acc_ref[...]


def moe_dense(h, cw, x, gate_f, w_gate, w_up, w_down, layer, tm=1024):
    b, s, d = x.shape
    _, ne, _, f = w_gate.shape
    tm = min(tm, s)
    return pl.pallas_call(
        _moe_kernel,
        grid=(b, s // tm, ne),
        in_specs=[pl.BlockSpec((None, tm, d), lambda bi, i, e: (bi, i, 0)),
                  pl.BlockSpec((None, tm, ne), lambda bi, i, e: (bi, i, 0)),
                  pl.BlockSpec((None, tm, d), lambda bi, i, e: (bi, i, 0)),
                  pl.BlockSpec((None, 1, d), lambda bi, i, e: (bi, 0, 0)),
                  pl.BlockSpec((None, None, d, f), lambda bi, i, e: (layer, e, 0, 0)),
                  pl.BlockSpec((None, None, d, f), lambda bi, i, e: (layer, e, 0, 0)),
                  pl.BlockSpec((None, None, f, d), lambda bi, i, e: (layer, e, 0, 0))],
        out_specs=pl.BlockSpec((None, tm, d), lambda bi, i, e: (bi, i, 0)),
        out_shape=jax.ShapeDtypeStruct((b, s, d), F32),
        scratch_shapes=[pltpu.VMEM((tm, d), F32)],
        compiler_params=_params("parallel", "parallel", "arbitrary"),
        name="moe_dense",
    )(h, cw, x, gate_f.reshape(b, 1, d), w_gate, w_up, w_down)


def _final_norm_kernel(x_ref, g_ref, o_ref):
    o_ref[...] = _rms(x_ref[...], g_ref[...])


def final_rmsnorm(x, g, tm=512):
    b, s, d = x.shape
    tm = min(tm, s)
    return pl.pallas_call(
        _final_norm_kernel,
        grid=(b, s // tm),
        in_specs=[pl.BlockSpec((None, tm, d), lambda bi, i: (bi, i, 0)), pl.BlockSpec((1, d), lambda bi, i: (0, 0))],
        out_specs=pl.BlockSpec((None, tm, d), lambda bi, i: (bi, i, 0)),
        out_shape=jax.ShapeDtypeStruct((b, s, d), F32),
        compiler_params=_params("parallel", "parallel"),
        name="final_rmsnorm",
    )(x, g.reshape(1, d))


def _heads(x, nh):
    b, s, w = x.shape
    return x.reshape(b, s, nh, w // nh).transpose(0, 2, 1, 3)


ATTN_SCALE = HEAD_DIM ** -0.5


def even_projection(x, g, sc, sh, w_in):
    off = _offsets(EVEN_WIDTHS)
    order = list(range(7)) + [8, 9, 10, 7]
    cols = jnp.concatenate([w_in[:, off[j][0]:off[j][1]] for j in order], axis=1)
    w = jnp.pad(cols, ((0, 0), (0, -cols.shape[1] % LANES))).astype(BF16)
    qw, kvw, hw = NSA_HEADS * HEAD_DIM, NSA_KV_HEADS * HEAD_DIM, MOBA_HEADS * HEAD_DIM
    p = qw + 6 * kvw
    outs = [(0, qw, ATTN_SCALE, F32),
            (0, qw, ATTN_SCALE * LOG2E, BF16),
            (qw, qw + 2 * kvw, 1.0, F32),
            (qw + 2 * kvw, p, 1.0, BF16),
            (p, p + hw, 1.0, F32),
            (p, p + hw, ATTN_SCALE * LOG2E, BF16),
            (p + hw, p + 2 * hw, 1.0, F32),
            (p + hw, p + 3 * hw, 1.0, BF16),
            (p + 3 * hw, p + 3 * hw + LANES, 1.0, F32)]
    return norm_mod_matmul(x, g, sc, sh, w, outs)


def even_mixer(proj, x, gate_m, w_out, pos_k, pos_v, ck_w1, ck_w2, cv_w1, cv_w2, rel_table, banks):
    qa_f32, qa_log2, kcvc, kvsw, qb_f32, qb_log2, kb_f32, kbvb, gates = proj
    b, s, _ = qa_f32.shape
    hk, g, dh = NSA_KV_HEADS, NSA_GROUP, HEAD_DIM
    qa = _heads(qa_f32, NSA_HEADS).reshape(b, hk, g, s, dh)
    qa16_log2 = _heads(qa_log2, NSA_HEADS).reshape(b, hk, g, s, dh)
    (bank_l, far_l, t_l), (bank_w, t_w) = banks["dense"], banks["nsa_window"]

    nc = s // CMP_STRIDE
    kv = _heads(kcvc, 2 * hk).reshape(b, 2, hk, s, dh).transpose(1, 0, 2, 3, 4)
    cmp = nsa_compress(kv.reshape(2, b * hk, s, dh), jnp.stack([ck_w1, cv_w1]), jnp.stack([pos_k, pos_v]),
                       jnp.stack([ck_w2, cv_w2]))
    cmp = cmp.reshape(2, b, hk, nc, dh)
    o_c, sel = nsa_cmp_attention(qa, cmp[0], cmp[1], bias_cmp(rel_table, s, nc, tq=min(256, s)), tq=min(256, s))
    ks, vs, kw, vw = (_heads(kvsw[..., j * hk * dh:(j + 1) * hk * dh], hk) for j in range(4))
    o_s = flash_attention([qa16_log2], [ks[:, :, None]], vs[:, :, None], bank_l, far_l, sel=sel, t=t_l, blk=SLC_BLOCK)
    o_w = window_attention(qa16_log2, kw, vw, bank_w, t_w, NSA_WINDOW)

    hb = 4
    hw = MOBA_HEADS * dh
    selb = moba_select(_heads(qb_f32, MOBA_HEADS), _heads(kb_f32, MOBA_HEADS), tq=min(2048, s))
    ngb = MOBA_HEADS // hb
    r5 = lambda a: a.reshape(b, ngb, hb, a.shape[-2], a.shape[-1])
    o_b = flash_attention([r5(_heads(qb_log2, MOBA_HEADS))], [r5(_heads(kbvb[..., :hw], MOBA_HEADS))],
                          r5(_heads(kbvb[..., hw:], MOBA_HEADS)), bank_l, far_l, sel=r5(selb), t=t_l, blk=MOBA_BLOCK)

    return out_project([o_c, o_s, o_w], o_b, x, gate_m, w_out, gates=gates)


def odd_projection(x, g, sc, sh, w_in):
    off = _offsets(ODD_WIDTHS)
    half = MLA_ROPE // 2
    rope_lo = off[2][0]
    swapped = jnp.concatenate([w_in[:, rope_lo + half:rope_lo + MLA_ROPE], w_in[:, rope_lo:rope_lo + half]], axis=1)
    w = jnp.concatenate([w_in[:, :off[2][1]], swapped, w_in[:, off[3][0]:]], axis=1).astype(BF16)
    lat = MLA_Q_RANK + MLA_KV_RANK + 2 * MLA_ROPE
    qw, kvw = SWA_HEADS * HEAD_DIM, SWA_KV_HEADS * HEAD_DIM
    outs = [(0, lat, 1.0, F32),
            (lat, lat + qw, ATTN_SCALE * LOG2E, BF16),
            (lat + qw, lat + qw + 2 * kvw, 1.0, BF16)]
    return norm_mod_matmul(x, g, sc, sh, w, outs)


def odd_mixer(proj, x, gate_m, w_out, q_norm, kv_norm, w_q_up, w_kv_up, sinks, banks):
    latents, qd_log2, kdvd = proj
    b, s, _ = latents.shape
    lo = 0
    parts = []
    for wdt in (MLA_Q_RANK, MLA_KV_RANK, MLA_ROPE, MLA_ROPE):
        parts.append(latents[..., lo:lo + wdt])
        lo += wdt
    c_q, c_kv, k_rope, k_rope_sw = parts
    qn, qr, kn, v, kr = mla_up(c_q, c_kv, k_rope, k_rope_sw, q_norm, kv_norm, w_q_up, w_kv_up)
    hb = MLA_HEADS
    r5 = lambda a: a.reshape(b, 1, hb, s, a.shape[-1])
    bank_c, far_c, t_c = banks["causal"]
    lane_pad = jnp.zeros(qn.shape[:-1] + (-(MLA_NOPE + MLA_ROPE) % LANES,), qn.dtype)
    q_full = jnp.concatenate([qn, qr, lane_pad], axis=-1)
    k_full = jnp.concatenate([kn, kr, lane_pad], axis=-1)
    o_c = flash_attention([r5(q_full)], [r5(k_full)], r5(v), bank_c, far_c, t=t_c, kc=t_c)

    hk, g, dh = SWA_KV_HEADS, SWA_GROUP, HEAD_DIM
    qd = _heads(qd_log2, SWA_HEADS).reshape(b, hk, g, s, dh)
    bank_s, t_s = banks["swa"]
    o_d = window_attention(qd, _heads(kdvd[..., :hk * dh], hk), _heads(kdvd[..., hk * dh:], hk), bank_s, t_s,
                           SWA_WINDOW, sinks=sinks)
    return out_project([o_c], o_d, x, gate_m, w_out)


def kernel(x, c, rel_table, router_w, router_b, final_norm, norm_mix, norm_ffn, ada_w, ada_b, moe_w_gate, moe_w_up, moe_w_down, ev_w_in, ev_w_out, nsa_pos_k, nsa_pos_v, nsa_ck_w1, nsa_ck_w2, nsa_cv_w1, nsa_cv_w2, od_w_in, od_w_out, mla_q_norm, mla_kv_norm, mla_w_q_up, mla_w_kv_up, swa_sinks):
    b, s, d = x.shape
    depth = ada_w.shape[0]
    mods = ada_all(c, ada_w, ada_b)
    t_dense = min(512, s)
    bank_l = bias_bank(rel_table, t_dense)
    far_l = rel_table[N_BUCKETS - 1] * LOG2E
    future = np.arange(t_dense)[:, None] > np.arange(t_dense)[None, :]
    bank_c = jnp.asarray(np.where(future, NEG, 0.0).astype(np.float32)[None, None])
    t_w = min(256, s)
    banks = {"dense": (bank_l, far_l, t_dense), "causal": (bank_c, jnp.zeros((N_BIAS_HEADS,), F32), t_dense),
             "nsa_window": (bias_bank(rel_table, t_w, window=NSA_WINDOW), t_w),
             "swa": (bias_bank(rel_table, SWA_WINDOW, window=SWA_WINDOW), SWA_WINDOW)}

    for layer in range(depth):
        shift_m, scale_m, gate_m, shift_f, scale_f, gate_f = jnp.split(mods[layer], 6, axis=-1)
        i = layer // 2
        if layer % 2 == 0:
            proj = even_projection(x, norm_mix[layer], scale_m, shift_m, ev_w_in[i])
            x = even_mixer(proj, x, gate_m, ev_w_out[i], nsa_pos_k[i], nsa_pos_v[i], nsa_ck_w1[i], nsa_ck_w2[i],
                           nsa_cv_w1[i], nsa_cv_w2[i], rel_table, banks)
        else:
            proj = odd_projection(x, norm_mix[layer], scale_m, shift_m, od_w_in[i])
            x = odd_mixer(proj, x, gate_m, od_w_out[i], mla_q_norm[i], mla_kv_norm[i], mla_w_q_up[i], mla_w_kv_up[i],
                          swa_sinks[i], banks)
        h, cw = ffn_pre(x, norm_ffn[layer], scale_f, shift_f, router_w, router_b)
        x = moe_dense(h, cw.transpose(0, 2, 1), x, gate_f, moe_w_gate, moe_w_up, moe_w_down, layer)
    return final_rmsnorm(x, final_norm)
```

```python
import functools
import math

import numpy as np
import jax
import jax.numpy as jnp
from jax import lax
from jax.experimental import pallas as pl
from jax.experimental.pallas import tpu as pltpu

F32 = jnp.float32
BF16 = jnp.bfloat16
HI = lax.Precision.HIGHEST

D_MODEL = 1024
HEAD_DIM = 64
NEG = -1e30
BIG = 1e30
M_INIT = -1e9
NORM_EPS = 1e-6
LOG2E = math.log2(math.e)

N_BUCKETS = 32
MAX_DISTANCE = 1024
N_BIAS_HEADS = 8

NSA_HEADS = 8
NSA_KV_HEADS = 2
NSA_GROUP = NSA_HEADS // NSA_KV_HEADS
CMP_LEN = 32
CMP_STRIDE = 16
CMP_HIDDEN = 256
SLC_BLOCK = 64
SLC_TOPN = 16
NSA_WINDOW = 512

MOBA_HEADS = 8
MOBA_BLOCK = 256
MOBA_TOPK = 3

MLA_HEADS = 4
MLA_Q_RANK = 256
MLA_KV_RANK = 128
MLA_NOPE = 128
MLA_ROPE = 64
MLA_V = 128
ROPE_THETA = 10000.0

SWA_HEADS = 8
SWA_KV_HEADS = 2
SWA_GROUP = SWA_HEADS // SWA_KV_HEADS
SWA_WINDOW = 128

N_EXPERTS = 16
N_GROUPS = 4
EXPERTS_PER_GROUP = N_EXPERTS // N_GROUPS
D_EXPERT = 512

EVEN_WIDTHS = (NSA_HEADS * HEAD_DIM,) + (NSA_KV_HEADS * HEAD_DIM,) * 6 + (3 * NSA_HEADS,) + (MOBA_HEADS * HEAD_DIM,) * 3
ODD_WIDTHS = (MLA_Q_RANK, MLA_KV_RANK, MLA_ROPE, SWA_HEADS * HEAD_DIM, SWA_KV_HEADS * HEAD_DIM, SWA_KV_HEADS * HEAD_DIM)

LANES = 128
VMEM_LIMIT = 56 * 1024 * 1024


def _params(*sem):
    return pltpu.CompilerParams(dimension_semantics=sem, vmem_limit_bytes=VMEM_LIMIT)


def _dot(a, b, precision=None):
    return lax.dot_general(a, b, (((1,), (0,)), ((), ())), precision=precision, preferred_element_type=F32)


def _dot_nt(a, b, precision=None):
    return lax.dot_general(a, b, (((1,), (1,)), ((), ())), precision=precision, preferred_element_type=F32)


def _iota(shape, dim):
    return lax.broadcasted_iota(jnp.int32, shape, dim)


def _offsets(widths):
    out, acc = [], 0
    for w in widths:
        out.append((acc, acc + w))
        acc += w
    return out


def _bucket_thresholds():
    d = np.arange(0, 4 * MAX_DISTANCE, dtype=np.int64)
    exact = N_BUCKETS // 2
    x = np.maximum(d, 1).astype(np.float32) / np.float32(exact)
    logp = exact + (np.log(x) / np.float32(math.log(MAX_DISTANCE / exact)) * np.float32(N_BUCKETS - exact)).astype(np.int32)
    bucket = np.where(d < exact, d, np.minimum(logp, N_BUCKETS - 1))
    return [int(np.argmax(bucket >= b)) for b in range(1, N_BUCKETS)]


BUCKET_THR = _bucket_thresholds()
FAR_DIST = BUCKET_THR[-1]


def _ada_kernel(c_ref, w_ref, b_ref, o_ref):
    c = c_ref[...]
    o_ref[...] = _dot(c * jax.nn.sigmoid(c), w_ref[...], precision=HI) + b_ref[...]


def ada_all(c, ada_w, ada_b):
    depth, d, n = ada_w.shape
    rows = 8
    cp = jnp.pad(c, ((0, rows - c.shape[0]), (0, 0)))
    tn = 1536
    out = pl.pallas_call(
        _ada_kernel,
        grid=(depth, n // tn),
        in_specs=[pl.BlockSpec((rows, d), lambda l, j: (0, 0)),
                  pl.BlockSpec((None, d, tn), lambda l, j: (l, 0, j)),
                  pl.BlockSpec((None, 1, tn), lambda l, j: (l, 0, j))],
        out_specs=pl.BlockSpec((None, rows, tn), lambda l, j: (l, 0, j)),
        out_shape=jax.ShapeDtypeStruct((depth, rows, n), F32),
        compiler_params=_params("parallel", "parallel"),
        name="ada",
    )(cp, ada_w, ada_b.reshape(depth, 1, n))
    return out[:, :c.shape[0], :]


def _norm_mod(x, g, sc, sh):
    y = x * lax.rsqrt(jnp.mean(x * x, axis=-1, keepdims=True) + NORM_EPS) * g
    return y * (1.0 + sc) + sh


def _nmm_kernel(x_ref, g_ref, sc_ref, sh_ref, w_ref, *o_refs, outs):
    h = _norm_mod(x_ref[...], g_ref[...], sc_ref[...], sh_ref[...])
    res = _dot(h.astype(BF16), w_ref[...])
    for o_ref, (lo, hi, scale, _) in zip(o_refs, outs):
        part = res[:, lo:hi]
        o_ref[...] = (part if scale == 1.0 else part * scale).astype(o_ref.dtype)


def norm_mod_matmul(x, g, sc, sh, w, outs, tm=512):
    b, s, d = x.shape
    n = w.shape[1]
    tm = min(tm, s)
    return pl.pallas_call(
        functools.partial(_nmm_kernel, outs=outs),
        grid=(b, s // tm),
        in_specs=[pl.BlockSpec((None, tm, d), lambda bi, i: (bi, i, 0)),
                  pl.BlockSpec((1, d), lambda bi, i: (0, 0)),
                  pl.BlockSpec((None, 1, d), lambda bi, i: (bi, 0, 0)),
                  pl.BlockSpec((None, 1, d), lambda bi, i: (bi, 0, 0)),
                  pl.BlockSpec((d, n), lambda bi, i: (0, 0))],
        out_specs=[pl.BlockSpec((None, tm, hi - lo), lambda bi, i: (bi, i, 0)) for lo, hi, _, _ in outs],
        out_shape=[jax.ShapeDtypeStruct((b, s, hi - lo), dt) for lo, hi, _, dt in outs],
        compiler_params=_params("parallel", "parallel"),
        name="norm_mod_matmul",
    )(x, g.reshape(1, d), sc.reshape(b, 1, d), sh.reshape(b, 1, d), w)


def _bias_kernel(tab_ref, o_ref, *, rows, cols, step, cstride, c0, key_major=False, window=None):
    sub = 8
    off = pl.program_id(0) * step - c0
    unit = LOG2E if key_major else 1.0
    if key_major:
        base = _iota((sub, cols), 1) - _iota((sub, cols), 0)
    else:
        base = _iota((sub, cols), 0) - cstride * _iota((sub, cols), 1)

    def body(r, carry):
        dist = off + base + (-r * sub if key_major else r * sub)
        vs = [jnp.full((sub, cols), tab_ref[0, h] * unit, F32) for h in range(N_BIAS_HEADS)]
        for b in range(1, N_BUCKETS):
            ge = dist >= BUCKET_THR[b - 1]
            for h in range(N_BIAS_HEADS):
                vs[h] = jnp.where(ge, tab_ref[b, h] * unit, vs[h])
        masked = dist < 0 if window is None else (dist < 0) | (dist >= window)
        for h in range(N_BIAS_HEADS):
            v = jnp.where(masked, NEG, vs[h]) if key_major else vs[h]
            o_ref[h, pl.ds(pl.multiple_of(r * sub, sub), sub), :] = v
        return carry

    lax.fori_loop(0, rows // sub, body, 0)


def bias_bank(rel_table, t, window=None):
    nd = -(-(FAR_DIST + t - 1) // t) if window is None else window // t + 1
    kern = functools.partial(_bias_kernel, rows=t, cols=t, step=t, cstride=1, c0=0, key_major=True, window=window)
    bank = pl.pallas_call(
        kern,
        grid=(nd,),
        in_specs=[pl.BlockSpec(memory_space=pltpu.SMEM)],
        out_specs=pl.BlockSpec((N_BIAS_HEADS, None, t, t), lambda i: (0, i, 0, 0)),
        out_shape=jax.ShapeDtypeStruct((N_BIAS_HEADS, nd, t, t), F32),
        compiler_params=_params("parallel"),
        name="bias_bank",
    )(rel_table)
    return bank


def _bias_cmp_kernel(tab_ref, o_ref, *, rows, cols):
    sub = 8
    width = min(LANES, cols)
    t0 = pl.program_id(0) * rows
    base = _iota((sub, width), 0) - CMP_STRIDE * _iota((sub, width), 1)

    def body(r, carry):
        rs = pl.ds(pl.multiple_of(r * sub, sub), sub)
        for c in range(cols // width):
            cs = slice(c * width, (c + 1) * width)
            off = t0 + r * sub - (CMP_LEN - 1) - CMP_STRIDE * width * c
            lo = off - CMP_STRIDE * (width - 1)
            hi = off + sub - 1

            @pl.when((hi >= 0) & (lo < FAR_DIST))
            def _():
                dist = off + base
                vs = [jnp.full((sub, width), tab_ref[0, h], F32) for h in range(N_BIAS_HEADS)]
                for b in range(1, N_BUCKETS):
                    ge = dist >= BUCKET_THR[b - 1]
                    for h in range(N_BIAS_HEADS):
                        vs[h] = jnp.where(ge, tab_ref[b, h], vs[h])
                for h in range(N_BIAS_HEADS):
                    o_ref[h, rs, cs] = vs[h]

            @pl.when(lo >= FAR_DIST)
            def _():
                for h in range(N_BIAS_HEADS):
                    o_ref[h, rs, cs] = jnp.full((sub, width), tab_ref[N_BUCKETS - 1, h], F32)

            @pl.when(hi < 0)
            def _():
                for h in range(N_BIAS_HEADS):
                    o_ref[h, rs, cs] = jnp.full((sub, width), tab_ref[0, h], F32)
        return carry

    lax.fori_loop(0, rows // sub, body, 0)


def bias_cmp(rel_table, s, nc, tq=256):
    kern = functools.partial(_bias_cmp_kernel, rows=tq, cols=nc)
    return pl.pallas_call(
        kern,
        grid=(s // tq,),
        in_specs=[pl.BlockSpec(memory_space=pltpu.SMEM)],
        out_specs=pl.BlockSpec((N_BIAS_HEADS, tq, nc), lambda i: (0, i, 0)),
        out_shape=jax.ShapeDtypeStruct((N_BIAS_HEADS, s, nc), F32),
        compiler_params=_params("parallel"),
        name="bias_cmp",
    )(rel_table)


def _compress_kernel(x_ref, w1_ref, pos_ref, w2_ref, o_ref):
    x = x_ref[...]
    w1 = w1_ref[...]
    half = w1.shape[0]
    hid = w1.shape[1] // 2
    r = _dot(x, w1, precision=HI)
    pos = pos_ref[...]
    pb = _dot(pos[:, :half], w1[:, :hid], precision=HI) + _dot(pos[:, half:], w1[:, hid:], precision=HI)
    nxt = pltpu.roll(r[:, hid:], x.shape[0] - 1, 0)
    pre = r[:, :hid] + nxt + pb[0:1, :]
    o_ref[...] = _dot(jax.nn.gelu(pre), w2_ref[...], precision=HI)


def nsa_compress(kv, w1, pos, w2):
    two, bh, s, dh = kv.shape
    nc = s // CMP_STRIDE
    half = CMP_STRIDE * dh
    hid = w1.shape[2]
    x = kv.reshape(two, bh, nc, half)
    w1cat = jnp.concatenate([w1[:, :half, :], w1[:, half:, :]], axis=2)
    posf = jnp.broadcast_to(pos.reshape(two, 1, CMP_LEN * dh), (two, 8, CMP_LEN * dh))
    return pl.pallas_call(
        _compress_kernel,
        grid=(two, bh),
        in_specs=[pl.BlockSpec((None, None, nc, half), lambda w, i: (w, i, 0, 0)),
                  pl.BlockSpec((None, half, 2 * hid), lambda w, i: (w, 0, 0)),
                  pl.BlockSpec((None, 8, CMP_LEN * dh), lambda w, i: (w, 0, 0)),
                  pl.BlockSpec((None, hid, dh), lambda w, i: (w, 0, 0))],
        out_specs=pl.BlockSpec((None, None, nc, dh), lambda w, i: (w, i, 0, 0)),
        out_shape=jax.ShapeDtypeStruct((two, bh, nc, dh), F32),
        compiler_params=_params("parallel", "parallel"),
        name="nsa_compress",
    )(x, w1cat, posf, w2)


def _topn_mask(score, index, n):
    idx_f = index.astype(F32)
    sel = jnp.zeros(score.shape, jnp.bool_)
    for _ in range(n):
        m = jnp.max(score, axis=0, keepdims=True)
        first = jnp.min(jnp.where(score == m, idx_f, float(score.shape[0])), axis=0, keepdims=True)
        pick = idx_f == first
        sel = sel | pick
        score = jnp.where(pick, -jnp.inf, score)
    return sel


def _cmp_kernel(q_ref, k_ref, v_ref, b_ref, ov_ref, oc_ref, sel_ref, *, tq, nc, nslc, ntop, group):
    qi = pl.program_id(2)
    t = qi * tq + _iota((tq, 1), 0)
    n = _iota((1, nc), 1)
    mask = (n * CMP_STRIDE + (CMP_LEN - 1) <= t) & (n < nc - 1)
    k = k_ref[...]
    v = v_ref[...].astype(BF16)
    dh = k.shape[-1]
    psum = jnp.zeros((tq, nc), F32)
    scores = lambda g: _dot_nt(q_ref[g], k, precision=HI) + b_ref[g]
    s_next = scores(0)
    for g in range(group):
        s, s_next = s_next, (scores(g + 1) if g + 1 < group else None)
        s = jnp.where(mask, s, NEG)
        m = jnp.max(s, axis=-1, keepdims=True)
        p = jnp.where(mask, jnp.exp(s - m), 0.0)
        den = jnp.sum(p, axis=-1, keepdims=True)
        p = p * jnp.where(den > 0.0, 1.0 / den, 0.0)
        oc_ref[:, g * dh:(g + 1) * dh] = _dot(p.astype(BF16), v)
        psum = psum + p
    imp = _dot_nt(ov_ref[...], psum, precision=HI)
    j = _iota((nslc, tq), 0)
    blk = jnp.right_shift(qi * tq + _iota((1, tq), 1), int(math.log2(SLC_BLOCK)))
    forced = (j == 0) | (j == blk) | (j == blk - 1)
    valid = j <= blk
    score = jnp.where(forced, BIG, jnp.where(valid, imp, NEG))
    sel = _topn_mask(score, j, ntop) & valid
    sel_ref[0] = jnp.where(sel, 0.0, NEG)


def nsa_cmp_attention(q, k_cmp, v_cmp, bias_c, tq=256):
    b, hk, g, s, dh = q.shape
    nc = k_cmp.shape[2]
    nslc = s // SLC_BLOCK
    ntop = min(SLC_TOPN, nslc)
    cmp_start = np.arange(nc, dtype=np.int64) * CMP_STRIDE
    slc_lo = np.arange(nslc, dtype=np.int64) * SLC_BLOCK
    overlap = ((cmp_start[:, None] <= slc_lo[None, :] + SLC_BLOCK - 1)
               & (cmp_start[:, None] + CMP_LEN - 1 >= slc_lo[None, :])).astype(np.float32)
    kern = functools.partial(_cmp_kernel, tq=tq, nc=nc, nslc=nslc, ntop=ntop, group=g)
    return pl.pallas_call(
        kern,
        grid=(b, hk, s // tq),
        in_specs=[pl.BlockSpec((None, None, g, tq, dh), lambda bi, h, i: (bi, h, 0, i, 0)),
                  pl.BlockSpec((None, None, nc, dh), lambda bi, h, i: (bi, h, 0, 0)),
                  pl.BlockSpec((None, None, nc, dh), lambda bi, h, i: (bi, h, 0, 0)),
                  pl.BlockSpec((g, tq, nc), lambda bi, h, i: (h, i, 0)),
                  pl.BlockSpec((nslc, nc), lambda bi, h, i: (0, 0))],
        out_specs=[pl.BlockSpec((None, tq, g * dh), lambda bi, h, i: (bi, i, h)),
                   pl.BlockSpec((None, None, 1, nslc, tq), lambda bi, h, i: (bi, h, 0, 0, i))],
        out_shape=[jax.ShapeDtypeStruct((b, s, hk * g * dh), F32),
                   jax.ShapeDtypeStruct((b, hk, 1, nslc, s), F32)],
        compiler_params=_params("parallel", "parallel", "parallel"),
        name="nsa_cmp_attention",
    )(q, k_cmp, v_cmp, bias_c, jnp.asarray(overlap.T))


def _kmean_kernel(k_ref, o_ref, *, nb, blk):
    k = k_ref[...]
    o_ref[...] = jnp.sum(k.reshape(nb, blk, k.shape[-1]), axis=1) * (1.0 / blk)


def _moba_sel_kernel(q_ref, km_ref, sel_ref, *, tq, nb, ntop):
    qi = pl.program_id(2)
    own = jnp.right_shift(qi * tq + _iota((1, tq), 1), int(math.log2(MOBA_BLOCK)))
    j = _iota((nb, tq), 0)
    gate = _dot_nt(km_ref[...], q_ref[...], precision=HI)
    past = j < own
    sel = _topn_mask(jnp.where(past, gate, NEG), j, ntop) & past
    sel_ref[...] = jnp.where(sel | (j == own), 0.0, NEG)


def moba_select(q, k, tq=256):
    b, h, s, dh = q.shape
    nb = s // MOBA_BLOCK
    ntop = min(MOBA_TOPK, nb)
    kmean = pl.pallas_call(
        functools.partial(_kmean_kernel, nb=nb, blk=MOBA_BLOCK),
        grid=(b, h),
        in_specs=[pl.BlockSpec((None, None, s, dh), lambda bi, hi: (bi, hi, 0, 0))],
        out_specs=pl.BlockSpec((None, None, nb, dh), lambda bi, hi: (bi, hi, 0, 0)),
        out_shape=jax.ShapeDtypeStruct((b, h, nb, dh), F32),
        compiler_params=_params("parallel", "parallel"),
        name="moba_kmean",
    )(k)
    return pl.pallas_call(
        functools.partial(_moba_sel_kernel, tq=tq, nb=nb, ntop=ntop),
        grid=(b, h, s // tq),
        in_specs=[pl.BlockSpec((None, None, tq, dh), lambda bi, hi, i: (bi, hi, i, 0)),
                  pl.BlockSpec((None, None, nb, dh), lambda bi, hi, i: (bi, hi, 0, 0))],
        out_specs=pl.BlockSpec((None, None, nb, tq), lambda bi, hi, i: (bi, hi, 0, i)),
        out_shape=jax.ShapeDtypeStruct((b, h, nb, s), F32),
        compiler_params=_params("parallel", "parallel", "parallel"),
        name="moba_select",
    )(q, kmean)


AUG = 64


def _flash_kernel(qt_ref, kt_ref, *refs, hb, kb, sb, bh, t, tq, kc, nparts, use_sel, nbt, nd):
    pos = 0
    q_refs = refs[pos:pos + nparts]; pos += nparts
    k_refs = refs[pos:pos + nparts]; pos += nparts
    vt_ref = refs[pos]; pos += 1
    sel_ref = None
    if use_sel:
        sel_ref = refs[pos]; pos += 1
    bias_ref, far_ref, o_ref, m_ref, l_ref, acc_ref = refs[pos:pos + 6]

    head0 = pl.program_id(1) * hb
    step = pl.program_id(2)
    qi = qt_ref[step]
    ki = kt_ref[step]

    @pl.when(ki == 0)
    def _():
        m_ref[...] = jnp.full(m_ref.shape, M_INIT, F32)
        l_ref[...] = jnp.zeros(l_ref.shape, F32)
        acc_ref[...] = jnp.zeros(acc_ref.shape, F32)

    nchunk = t // kc
    nlane = t // tq

    def mask_lanes(si, qc):
        rows = sel_ref[si, qc * tq:(qc + 1) * tq, :]
        shift = lax.rem(AUG + LANES - lax.rem(ki * nbt, LANES), LANES)
        lane = _iota((tq, LANES), 1)
        keep = (lane >= AUG) & (lane < AUG + nbt)
        return jnp.where(keep, pltpu.roll(rows, shift, 1), 0.0).astype(BF16)

    def scores(g, qc, near, masks):
        kg = g * kb // hb
        bg = g if bh > 1 else 0
        ql = slice(qc * tq, (qc + 1) * tq)
        qs = [q_ref[g, ql, :] for q_ref in q_refs]
        if use_sel:
            qs[0] = jnp.where(_iota((tq, LANES), 1) < AUG, qs[0], masks[(g * sb // hb, qc)])
        chunks, top = [], None
        for c in range(nchunk):
            kr = slice(c * kc, (c + 1) * kc)
            s = _dot_nt(k_refs[0][kg, kr, :], qs[0])
            for p in range(1, nparts):
                s = s + _dot_nt(k_refs[p][kg, kr, :], qs[p])
            if near:
                s = s + bias_ref[bg, kr, ql]
            chunks.append(s)
            top = s if top is None else jnp.maximum(top, s)
        return chunks, top

    def update(g, qc, near, chunks, top):
        kg = g * kb // hb
        ql = slice(qc * tq, (qc + 1) * tq)
        shift = 0.0 if near else far_ref[head0 + g]
        m = m_ref[g, :, ql]
        m_new = jnp.maximum(m, jnp.max(top, axis=0, keepdims=True) + shift)
        alpha = jnp.exp2(m - m_new)
        base = m_new - shift
        pv = psum = None
        for c in range(nchunk):
            p = jnp.exp2(chunks[c] - base)
            ps = jnp.sum(p, axis=0, keepdims=True)
            pd = _dot(vt_ref[kg, :, c * kc:(c + 1) * kc], p.astype(BF16))
            pv = pd if pv is None else pv + pd
            psum = ps if psum is None else psum + ps
        m_ref[g, :, ql] = m_new
        l_ref[g, :, ql] = alpha * l_ref[g, :, ql] + psum
        acc_ref[g, :, ql] = alpha * acc_ref[g, :, ql] + pv

    def body(near):
        masks = {}
        if use_sel:
            masks = {(si, qc): mask_lanes(si, qc) for si in range(sb) for qc in range(nlane)}
        items = [(g, qc) for g in range(hb) for qc in range(nlane)]
        ahead = 2
        pending = [scores(*item, near, masks) for item in items[:ahead]]
        for i, item in enumerate(items):
            if i + ahead < len(items):
                pending.append(scores(*items[i + ahead], near, masks))
            update(*item, near, *pending.pop(0))

    @pl.when(qi - ki < nd)
    def _():
        body(True)

    @pl.when(qi - ki >= nd)
    def _():
        body(False)

    @pl.when(ki == qi)
    def _():
        for g in range(hb):
            dv = acc_ref.shape[1]
            o_ref[:, g * dv:(g + 1) * dv] = jnp.transpose(acc_ref[g] / l_ref[g]).astype(o_ref.dtype)


def flash_attention(q_parts, k_parts, v, bias, far, sel=None, t=512, blk=1, tq=256, kc=512):
    b, ng, hb, s, _ = q_parts[0].shape
    kb = v.shape[2]
    dv = v.shape[-1]
    t = min(t, s)
    tq, kc = min(tq, t), min(kc, t)
    nq = s // t
    nd = bias.shape[1]
    pairs = [(i, j) for i in range(nq) for j in range(i + 1)]
    qt = jnp.asarray(np.array([p[0] for p in pairs], np.int32))
    kt = jnp.asarray(np.array([p[1] for p in pairs], np.int32))
    nparts = len(q_parts)
    use_sel = sel is not None
    sb = sel.shape[2] if use_sel else 1
    bh = hb if bias.shape[0] > 1 else 1
    nbt = t // blk
    vt = jnp.swapaxes(v, -1, -2)
    q_parts, k_parts = list(q_parts), list(k_parts)
    if use_sel:
        onehot = (np.arange(s)[:, None] % t // blk == np.arange(LANES - AUG)[None, :]).astype(np.float32)
        k0 = k_parts[0]
        k_parts[0] = jnp.concatenate(
            [k0, jnp.broadcast_to(jnp.asarray(onehot, k0.dtype), k0.shape[:-1] + (LANES - AUG,))], axis=-1)
        q_parts[0] = jnp.pad(q_parts[0], ((0, 0),) * 4 + ((0, LANES - AUG),))
        selq = jnp.swapaxes(sel, -1, -2)
        selq = jnp.pad(selq, ((0, 0),) * 4 + ((0, LANES - selq.shape[-1]),))

    in_specs, args = [], []
    for qp in q_parts:
        in_specs.append(pl.BlockSpec((None, None, hb, t, qp.shape[-1]), lambda bi, n, st, qt, kt: (bi, n, 0, qt[st], 0)))
        args.append(qp)
    for kp in k_parts:
        in_specs.append(pl.BlockSpec((None, None, kb, t, kp.shape[-1]), lambda bi, n, st, qt, kt: (bi, n, 0, kt[st], 0)))
        args.append(kp)
    in_specs.append(pl.BlockSpec((None, None, kb, dv, t), lambda bi, n, st, qt, kt: (bi, n, 0, 0, kt[st])))
    args.append(vt)
    if use_sel:
        in_specs.append(pl.BlockSpec((None, None, sb, t, LANES), lambda bi, n, st, qt, kt: (bi, n, 0, qt[st], 0)))
        args.append(selq)
    near_tile = lambda qt, kt, st: jnp.minimum(qt[st] - kt[st], nd - 1)
    if bh > 1:
        in_specs.append(pl.BlockSpec((bh, None, t, t), lambda bi, n, st, qt, kt: (n, near_tile(qt, kt, st), 0, 0)))
    else:
        in_specs.append(pl.BlockSpec((1, None, t, t), lambda bi, n, st, qt, kt: (0, near_tile(qt, kt, st), 0, 0)))
    in_specs.append(pl.BlockSpec(memory_space=pltpu.SMEM))
    args += [bias, far]
    kern = functools.partial(_flash_kernel, hb=hb, kb=kb, sb=sb, bh=bh, t=t, tq=tq, kc=kc, nparts=nparts,
                             use_sel=use_sel, nbt=nbt, nd=nd)
    grid_spec = pltpu.PrefetchScalarGridSpec(
        num_scalar_prefetch=2,
        grid=(b, ng, len(pairs)),
        in_specs=in_specs,
        out_specs=pl.BlockSpec((None, t, hb * dv), lambda bi, n, st, qt, kt: (bi, qt[st], n)),
        scratch_shapes=[pltpu.VMEM((hb, 1, t), F32), pltpu.VMEM((hb, 1, t), F32), pltpu.VMEM((hb, dv, t), F32)])
    return pl.pallas_call(
        kern,
        grid_spec=grid_spec,
        out_shape=jax.ShapeDtypeStruct((b, s, ng * hb * dv), F32),
        compiler_params=_params("parallel", "parallel", "arbitrary"),
        name="flash_attention",
    )(qt, kt, *args)


def _window_kernel(*refs, group, t, ntile, has_sink):
    q_ref = refs[0]
    k_refs = refs[1:1 + ntile]
    vt_refs = refs[1 + ntile:1 + 2 * ntile]
    bias_ref = refs[1 + 2 * ntile]
    sink_ref = refs[2 + 2 * ntile] if has_sink else None
    o_ref = refs[-1]
    kvh = pl.program_id(1)

    def scores(g):
        q = q_ref[g]
        ss = [_dot_nt(k_refs[r][...], q) + bias_ref[g, ntile - 1 - r] for r in range(ntile)]
        top = ss[0]
        for s in ss[1:]:
            top = jnp.maximum(top, s)
        return ss, top

    def finish(g, ss, top):
        m = jnp.max(top, axis=0, keepdims=True)
        den = None
        if has_sink:
            sink = sink_ref[kvh * group + g] * LOG2E
            m = jnp.maximum(m, sink)
            den = jnp.exp2(sink - m)
        acc = None
        for r in range(ntile):
            p = jnp.exp2(ss[r] - m)
            ps = jnp.sum(p, axis=0, keepdims=True)
            pv = _dot(vt_refs[r][...], p.astype(BF16))
            den = ps if den is None else den + ps
            acc = pv if acc is None else acc + pv
        dh = acc.shape[0]
        o_ref[:, g * dh:(g + 1) * dh] = jnp.transpose(acc / den)

    pending = scores(0)
    for g in range(group):
        nxt = scores(g + 1) if g + 1 < group else None
        finish(g, *pending)
        pending = nxt


def window_attention(q, k, v, bank, t, window, sinks=None):
    b, hk, g, s, dh = q.shape
    ntile = window // t + 1
    kp = jnp.pad(k, ((0, 0), (0, 0), (window, 0), (0, 0)))
    vtp = jnp.pad(jnp.swapaxes(v, -1, -2), ((0, 0), (0, 0), (0, 0), (window, 0)))
    has_sink = sinks is not None
    in_specs = [pl.BlockSpec((None, None, g, t, dh), lambda bi, h, i: (bi, h, 0, i, 0))]
    args = [q]
    for r in range(ntile):
        in_specs.append(pl.BlockSpec((None, None, t, dh), lambda bi, h, i, r=r: (bi, h, i + r, 0)))
        args.append(kp)
    for r in range(ntile):
        in_specs.append(pl.BlockSpec((None, None, dh, t), lambda bi, h, i, r=r: (bi, h, 0, i + r)))
        args.append(vtp)
    in_specs.append(pl.BlockSpec((g, ntile, t, t), lambda bi, h, i: (h, 0, 0, 0)))
    args.append(bank)
    if has_sink:
        in_specs.append(pl.BlockSpec(memory_space=pltpu.SMEM))
        args.append(sinks)
    kern = functools.partial(_window_kernel, group=g, t=t, ntile=ntile, has_sink=has_sink)
    return pl.pallas_call(
        kern,
        grid=(b, hk, s // t),
        in_specs=in_specs,
        out_specs=pl.BlockSpec((None, t, g * dh), lambda bi, h, i: (bi, i, h)),
        out_shape=jax.ShapeDtypeStruct((b, s, hk * g * dh), F32),
        compiler_params=_params("parallel", "parallel", "parallel"),
        name="window_attention",
    )(*args)


def _rms(x, g):
    return x * lax.rsqrt(jnp.mean(x * x, axis=-1, keepdims=True) + NORM_EPS) * g


def _mla_up_kernel(cq_ref, ckv_ref, kr_ref, krs_ref, qn_ref, kvn_ref, wq_ref, wkv_ref, cs_ref, sn_ref,
                   qnope_ref, qrope_ref, knope_ref, v_ref, krope_ref, *, scale):
    cs = cs_ref[...]
    sn = sn_ref[...]
    q = _dot(_rms(cq_ref[...], qn_ref[...]).astype(BF16), wq_ref[...]) * scale
    kv = _dot(_rms(ckv_ref[...], kvn_ref[...]).astype(BF16), wkv_ref[...])
    wq_head = MLA_NOPE + 2 * MLA_ROPE
    for h in range(MLA_HEADS):
        base = h * wq_head
        qnope_ref[h] = q[:, base:base + MLA_NOPE].astype(qnope_ref.dtype)
        x = q[:, base + MLA_NOPE:base + MLA_NOPE + MLA_ROPE]
        xs = q[:, base + MLA_NOPE + MLA_ROPE:base + wq_head]
        qrope_ref[h] = (x * cs + xs * sn).astype(qrope_ref.dtype)
        kb = h * (MLA_NOPE + MLA_V)
        knope_ref[h] = kv[:, kb:kb + MLA_NOPE].astype(knope_ref.dtype)
        v_ref[h] = kv[:, kb + MLA_NOPE:kb + MLA_NOPE + MLA_V].astype(v_ref.dtype)
    kr = (kr_ref[...] * cs + krs_ref[...] * sn).astype(krope_ref.dtype)
    for h in range(MLA_HEADS):
        krope_ref[h] = kr


def mla_up(c_q, c_kv, k_rope, k_rope_sw, q_norm, kv_norm, w_q_up, w_kv_up, tm=512):
    b, s, _ = c_q.shape
    half = MLA_ROPE // 2
    inv = ROPE_THETA ** (-np.arange(0, MLA_ROPE, 2, dtype=np.float32) / np.float32(MLA_ROPE))
    ang = np.arange(s, dtype=np.float32)[:, None] * inv[None, :].astype(np.float32)
    cos, sin = np.cos(ang).astype(np.float32), np.sin(ang).astype(np.float32)
    cs = jnp.asarray(np.concatenate([cos, cos], axis=1))
    sn = jnp.asarray(np.concatenate([-sin, sin], axis=1))
    dq = MLA_NOPE + MLA_ROPE
    wq = w_q_up.reshape(MLA_Q_RANK, MLA_HEADS, dq)
    rope_cols = wq[:, :, MLA_NOPE:]
    swapped = jnp.concatenate([rope_cols[:, :, half:], rope_cols[:, :, :half]], axis=2)
    wq_aug = jnp.concatenate([wq, swapped], axis=2).reshape(MLA_Q_RANK, MLA_HEADS * (dq + MLA_ROPE)).astype(BF16)
    wkv = w_kv_up.astype(BF16)
    scale = (MLA_NOPE + MLA_ROPE) ** -0.5 * LOG2E
    tm = min(tm, s)
    row = lambda w: pl.BlockSpec((None, tm, w), lambda bi, i: (bi, i, 0))
    full = lambda a: pl.BlockSpec(a.shape, lambda bi, i: (0,) * a.ndim)
    head = lambda w: pl.BlockSpec((None, MLA_HEADS, tm, w), lambda bi, i: (bi, 0, i, 0))
    qn2, kvn2 = q_norm.reshape(1, -1), kv_norm.reshape(1, -1)
    outs = pl.pallas_call(
        functools.partial(_mla_up_kernel, scale=scale),
        grid=(b, s // tm),
        in_specs=[row(MLA_Q_RANK), row(MLA_KV_RANK), row(MLA_ROPE), row(MLA_ROPE), full(qn2), full(kvn2),
                  full(wq_aug), full(wkv),
                  pl.BlockSpec((tm, MLA_ROPE), lambda bi, i: (i, 0)), pl.BlockSpec((tm, MLA_ROPE), lambda bi, i: (i, 0))],
        out_specs=[head(MLA_NOPE), head(MLA_ROPE), head(MLA_NOPE), head(MLA_V), head(MLA_ROPE)],
        out_shape=[jax.ShapeDtypeStruct((b, MLA_HEADS, s, w), BF16)
                   for w in (MLA_NOPE, MLA_ROPE, MLA_NOPE, MLA_V, MLA_ROPE)],
        compiler_params=_params("parallel", "parallel"),
        name="mla_up",
    )(c_q, c_kv, k_rope, k_rope_sw, qn2, kvn2, wq_aug, wkv, cs, sn)
    return outs


def _out_kernel(*refs, gated):
    if gated:
        oc_ref, os_ref, ow_ref, gt_ref, ex_ref, ob_ref, x_ref, gm_ref, w_ref, o_ref = refs
        half = oc_ref.shape[-1]
        ge = _dot(jax.nn.sigmoid(gt_ref[...]), ex_ref[...], precision=HI)
        oa = ge[:, :half] * oc_ref[...] + ge[:, half:2 * half] * os_ref[...] + ge[:, 2 * half:] * ow_ref[...]
    else:
        oa_ref, ob_ref, x_ref, gm_ref, w_ref, o_ref = refs
        half = oa_ref.shape[-1]
        oa = oa_ref[...]
    mix = _dot(oa.astype(BF16), w_ref[:half, :]) + _dot(ob_ref[...].astype(BF16), w_ref[half:, :])
    o_ref[...] = x_ref[...] + gm_ref[...] * mix


def out_project(parts, ob, x, gate_m, w_out, gates=None, tm=512):
    b, s, d = x.shape
    gated = gates is not None
    tm = min(tm, s)
    row = lambda a: pl.BlockSpec((None, tm, a.shape[-1]), lambda bi, i: (bi, i, 0))
    full = lambda a: pl.BlockSpec(a.shape, lambda bi, i: (0,) * a.ndim)
    args, in_specs = [], []
    for p in parts:
        args.append(p); in_specs.append(row(p))
    if gated:
        half = parts[0].shape[-1]
        nh = half // HEAD_DIM
        ex = np.zeros((LANES, 3 * half), np.float32)
        for h in range(nh):
            for br in range(3):
                ex[h * 3 + br, br * half + h * HEAD_DIM: br * half + (h + 1) * HEAD_DIM] = 1.0
        gpad = jnp.pad(gates, ((0, 0), (0, 0), (0, LANES - gates.shape[-1])))
        ex = jnp.asarray(ex)
        args += [gpad, ex]; in_specs += [row(gpad), full(ex)]
    gm = gate_m.reshape(b, 1, d)
    wb = w_out.astype(BF16)
    args += [ob, x, gm, wb]
    in_specs += [row(ob), row(x), pl.BlockSpec((None, 1, d), lambda bi, i: (bi, 0, 0)), full(wb)]
    return pl.pallas_call(
        functools.partial(_out_kernel, gated=gated),
        grid=(b, s // tm),
        in_specs=in_specs,
        out_specs=pl.BlockSpec((None, tm, d), lambda bi, i: (bi, i, 0)),
        out_shape=jax.ShapeDtypeStruct((b, s, d), F32),
        compiler_params=_params("parallel", "parallel"),
        name="out_project",
    )(*args)


def _ffn_pre_kernel(x_ref, g_ref, sc_ref, sh_ref, rw_ref, rb_ref, h_ref, cw_ref):
    h = _norm_mod(x_ref[...], g_ref[...], sc_ref[...], sh_ref[...])
    h_ref[...] = h.astype(h_ref.dtype)
    aff = jax.nn.sigmoid(_dot_nt(rw_ref[...], h, precision=HI))
    biased = aff + rb_ref[...]
    epg = EXPERTS_PER_GROUP
    brow = [biased[e:e + 1, :] for e in range(N_EXPERTS)]
    arow = [aff[e:e + 1, :] for e in range(N_EXPERTS)]
    best = gsel = None
    for gi in range(N_GROUPS):
        a, b_, c, d_ = brow[gi * epg:(gi + 1) * epg]
        hi1, lo1, hi2, lo2 = jnp.maximum(a, b_), jnp.minimum(a, b_), jnp.maximum(c, d_), jnp.minimum(c, d_)
        score = jnp.maximum(hi1, hi2) + jnp.maximum(jnp.minimum(hi1, hi2), jnp.maximum(lo1, lo2))
        if gi == 0:
            best, gsel = score, jnp.zeros(score.shape, jnp.int32)
        else:
            better = score > best
            gsel = jnp.where(better, gi, gsel)
            best = jnp.where(better, score, best)

    def in_group(rows, j):
        v = rows[j]
        for gi in range(1, N_GROUPS):
            v = jnp.where(gsel == gi, rows[gi * epg + j], v)
        return v

    bv = [in_group(brow, j) for j in range(epg)]
    av = [in_group(arow, j) for j in range(epg)]

    def argmax_excluding(skip):
        val = idx = None
        for j in range(epg):
            cand = bv[j] if skip is None else jnp.where(skip == j, -jnp.inf, bv[j])
            if j == 0:
                val, idx = cand, jnp.zeros(cand.shape, jnp.int32)
            else:
                better = cand > val
                idx = jnp.where(better, j, idx)
                val = jnp.where(better, cand, val)
        return idx

    first = argmax_excluding(None)
    second = argmax_excluding(first)

    def pick(rows, idx):
        v = rows[0]
        for j in range(1, epg):
            v = jnp.where(idx == j, rows[j], v)
        return v

    a1, a2 = pick(av, first), pick(av, second)
    tot = a1 + a2
    e1, e2 = gsel * epg + first, gsel * epg + second
    eid = _iota(aff.shape, 0)
    cw_ref[...] = jnp.where(eid == e1, a1 / tot, 0.0) + jnp.where(eid == e2, a2 / tot, 0.0)


def ffn_pre(x, g, sc, sh, router_w, router_b, tm=512):
    b, s, d = x.shape
    e = router_w.shape[1]
    tm = min(tm, s)
    return pl.pallas_call(
        _ffn_pre_kernel,
        grid=(b, s // tm),
        in_specs=[pl.BlockSpec((None, tm, d), lambda bi, i: (bi, i, 0)),
                  pl.BlockSpec((1, d), lambda bi, i: (0, 0)),
                  pl.BlockSpec((None, 1, d), lambda bi, i: (bi, 0, 0)),
                  pl.BlockSpec((None, 1, d), lambda bi, i: (bi, 0, 0)),
                  pl.BlockSpec((e, d), lambda bi, i: (0, 0)),
                  pl.BlockSpec((e, 1), lambda bi, i: (0, 0))],
        out_specs=[pl.BlockSpec((None, tm, d), lambda bi, i: (bi, i, 0)),
                   pl.BlockSpec((None, e, tm), lambda bi, i: (bi, 0, i))],
        out_shape=[jax.ShapeDtypeStruct((b, s, d), BF16), jax.ShapeDtypeStruct((b, e, s), F32)],
        compiler_params=_params("parallel", "parallel"),
        name="ffn_pre",
    )(x, g.reshape(1, d), sc.reshape(b, 1, d), sh.reshape(b, 1, d), router_w.T, router_b.reshape(e, 1))


def _moe_kernel(h_ref, cw_ref, x_ref, gf_ref, wg_ref, wu_ref, wd_ref, o_ref, acc_ref):
    e = pl.program_id(2)

    @pl.when(e == 0)
    def _():
        acc_ref[...] = jnp.zeros(acc_ref.shape, F32)

    h = h_ref[...]
    a = _dot(h, wg_ref[...].astype(BF16))
    u = _dot(h, wu_ref[...].astype(BF16))
    cw = cw_ref[...]
    c = jnp.sum(jnp.where(_iota(cw.shape, 1) == e, cw, 0.0), axis=-1, keepdims=True)
    hid = (a * jax.nn.sigmoid(a)) * u * c
    acc_ref[...] += _dot(hid.astype(BF16), wd_ref[...].astype(BF16))

    @pl.when(e == pl.num_programs(2) - 1)
    def _():
        o_ref[...] = x_ref[...] + gf_ref[...] * acc_ref[...]


def moe_dense(h, cw, x, gate_f, w_gate, w_up, w_down, layer, tm=1024):
    b, s, d = x.shape
    _, ne, _, f = w_gate.shape
    tm = min(tm, s)
    return pl.pallas_call(
        _moe_kernel,
        grid=(b, s // tm, ne),
        in_specs=[pl.BlockSpec((None, tm, d), lambda bi, i, e: (bi, i, 0)),
                  pl.BlockSpec((None, tm, ne), lambda bi, i, e: (bi, i, 0)),
                  pl.BlockSpec((None, tm, d), lambda bi, i, e: (bi, i, 0)),
                  pl.BlockSpec((None, 1, d), lambda bi, i, e: (bi, 0, 0)),
                  pl.BlockSpec((None, None, d, f), lambda bi, i, e: (layer, e, 0, 0)),
                  pl.BlockSpec((None, None, d, f), lambda bi, i, e: (layer, e, 0, 0)),
                  pl.BlockSpec((None, None, f, d), lambda bi, i, e: (layer, e, 0, 0))],
        out_specs=pl.BlockSpec((None, tm, d), lambda bi, i, e: (bi, i, 0)),
        out_shape=jax.ShapeDtypeStruct((b, s, d), F32),
        scratch_shapes=[pltpu.VMEM((tm, d), F32)],
        compiler_params=_params("parallel", "parallel", "arbitrary"),
        name="moe_dense",
    )(h, cw, x, gate_f.reshape(b, 1, d), w_gate, w_up, w_down)


def _final_norm_kernel(x_ref, g_ref, o_ref):
    o_ref[...] = _rms(x_ref[...], g_ref[...])


def final_rmsnorm(x, g, tm=512):
    b, s, d = x.shape
    tm = min(tm, s)
    return pl.pallas_call(
        _final_norm_kernel,
        grid=(b, s // tm),
        in_specs=[pl.BlockSpec((None, tm, d), lambda bi, i: (bi, i, 0)), pl.BlockSpec((1, d), lambda bi, i: (0, 0))],
        out_specs=pl.BlockSpec((None, tm, d), lambda bi, i: (bi, i, 0)),
        out_shape=jax.ShapeDtypeStruct((b, s, d), F32),
        compiler_params=_params("parallel", "parallel"),
        name="final_rmsnorm",
    )(x, g.reshape(1, d))


def _heads(x, nh):
    b, s, w = x.shape
    return x.reshape(b, s, nh, w // nh).transpose(0, 2, 1, 3)


ATTN_SCALE = HEAD_DIM ** -0.5


def even_projection(x, g, sc, sh, w_in):
    off = _offsets(EVEN_WIDTHS)
    order = list(range(7)) + [8, 9, 10, 7]
    cols = jnp.concatenate([w_in[:, off[j][0]:off[j][1]] for j in order], axis=1)
    w = jnp.pad(cols, ((0, 0), (0, -cols.shape[1] % LANES))).astype(BF16)
    qw, kvw, hw = NSA_HEADS * HEAD_DIM, NSA_KV_HEADS * HEAD_DIM, MOBA_HEADS * HEAD_DIM
    p = qw + 6 * kvw
    outs = [(0, qw, ATTN_SCALE, F32),
            (0, qw, ATTN_SCALE * LOG2E, BF16),
            (qw, qw + 2 * kvw, 1.0, F32),
            (qw + 2 * kvw, p, 1.0, BF16),
            (p, p + hw, 1.0, F32),
            (p, p + hw, ATTN_SCALE * LOG2E, BF16),
            (p + hw, p + 2 * hw, 1.0, F32),
            (p + hw, p + 3 * hw, 1.0, BF16),
            (p + 3 * hw, p + 3 * hw + LANES, 1.0, F32)]
    return norm_mod_matmul(x, g, sc, sh, w, outs)


def even_mixer(proj, x, gate_m, w_out, pos_k, pos_v, ck_w1, ck_w2, cv_w1, cv_w2, rel_table, banks):
    qa_f32, qa_log2, kcvc, kvsw, qb_f32, qb_log2, kb_f32, kbvb, gates = proj
    b, s, _ = qa_f32.shape
    hk, g, dh = NSA_KV_HEADS, NSA_GROUP, HEAD_DIM
    qa = _heads(qa_f32, NSA_HEADS).reshape(b, hk, g, s, dh)
    qa16_log2 = _heads(qa_log2, NSA_HEADS).reshape(b, hk, g, s, dh)
    (bank_l, far_l, t_l), (bank_w, t_w) = banks["dense"], banks["nsa_window"]

    nc = s // CMP_STRIDE
    kv = _heads(kcvc, 2 * hk).reshape(b, 2, hk, s, dh).transpose(1, 0, 2, 3, 4)
    cmp = nsa_compress(kv.reshape(2, b * hk, s, dh), jnp.stack([ck_w1, cv_w1]), jnp.stack([pos_k, pos_v]),
                       jnp.stack([ck_w2, cv_w2]))
    cmp = cmp.reshape(2, b, hk, nc, dh)
    o_c, sel = nsa_cmp_attention(qa, cmp[0], cmp[1], bias_cmp(rel_table, s, nc, tq=min(256, s)), tq=min(512, s))
    ks, vs, kw, vw = (_heads(kvsw[..., j * hk * dh:(j + 1) * hk * dh], hk) for j in range(4))
    o_s = flash_attention([qa16_log2.reshape(b, 1, hk * g, s, dh)], [ks[:, None]], vs[:, None], bank_l, far_l,
                          sel=sel.reshape(b, 1, hk, sel.shape[-2], s), t=t_l, blk=SLC_BLOCK)
    o_w = window_attention(qa16_log2, kw, vw, bank_w, t_w, NSA_WINDOW)

    hb = MOBA_HEADS
    hw = MOBA_HEADS * dh
    selb = moba_select(_heads(qb_f32, MOBA_HEADS), _heads(kb_f32, MOBA_HEADS), tq=min(2048, s))
    ngb = MOBA_HEADS // hb
    r5 = lambda a: a.reshape(b, ngb, hb, a.shape[-2], a.shape[-1])
    o_b = flash_attention([r5(_heads(qb_log2, MOBA_HEADS))], [r5(_heads(kbvb[..., :hw], MOBA_HEADS))],
                          r5(_heads(kbvb[..., hw:], MOBA_HEADS)), bank_l, far_l, sel=r5(selb), t=t_l, blk=MOBA_BLOCK)

    return out_project([o_c, o_s, o_w], o_b, x, gate_m, w_out, gates=gates)


def odd_projection(x, g, sc, sh, w_in):
    off = _offsets(ODD_WIDTHS)
    half = MLA_ROPE // 2
    rope_lo = off[2][0]
    swapped = jnp.concatenate([w_in[:, rope_lo + half:rope_lo + MLA_ROPE], w_in[:, rope_lo:rope_lo + half]], axis=1)
    w = jnp.concatenate([w_in[:, :off[2][1]], swapped, w_in[:, off[3][0]:]], axis=1).astype(BF16)
    lat = MLA_Q_RANK + MLA_KV_RANK + 2 * MLA_ROPE
    qw, kvw = SWA_HEADS * HEAD_DIM, SWA_KV_HEADS * HEAD_DIM
    outs = [(0, lat, 1.0, F32),
            (lat, lat + qw, ATTN_SCALE * LOG2E, BF16),
            (lat + qw, lat + qw + 2 * kvw, 1.0, BF16)]
    return norm_mod_matmul(x, g, sc, sh, w, outs)


def odd_mixer(proj, x, gate_m, w_out, q_norm, kv_norm, w_q_up, w_kv_up, sinks, banks):
    latents, qd_log2, kdvd = proj
    b, s, _ = latents.shape
    lo = 0
    parts = []
    for wdt in (MLA_Q_RANK, MLA_KV_RANK, MLA_ROPE, MLA_ROPE):
        parts.append(latents[..., lo:lo + wdt])
        lo += wdt
    c_q, c_kv, k_rope, k_rope_sw = parts
    qn, qr, kn, v, kr = mla_up(c_q, c_kv, k_rope, k_rope_sw, q_norm, kv_norm, w_q_up, w_kv_up)
    hb = MLA_HEADS
    r5 = lambda a: a.reshape(b, 1, hb, s, a.shape[-1])
    bank_c, far_c, t_c = banks["causal"]
    lane_pad = jnp.zeros(qn.shape[:-1] + (-(MLA_NOPE + MLA_ROPE) % LANES,), qn.dtype)
    q_full = jnp.concatenate([qn, qr, lane_pad], axis=-1)
    k_full = jnp.concatenate([kn, kr, lane_pad], axis=-1)
    o_c = flash_attention([r5(q_full)], [r5(k_full)], r5(v), bank_c, far_c, t=t_c)

    hk, g, dh = SWA_KV_HEADS, SWA_GROUP, HEAD_DIM
    qd = _heads(qd_log2, SWA_HEADS).reshape(b, hk, g, s, dh)
    bank_s, t_s = banks["swa"]
    o_d = window_attention(qd, _heads(kdvd[..., :hk * dh], hk), _heads(kdvd[..., hk * dh:], hk), bank_s, t_s,
                           SWA_WINDOW, sinks=sinks)
    return out_project([o_c], o_d, x, gate_m, w_out)


def kernel(x, c, rel_table, router_w, router_b, final_norm, norm_mix, norm_ffn, ada_w, ada_b, moe_w_gate, moe_w_up, moe_w_down, ev_w_in, ev_w_out, nsa_pos_k, nsa_pos_v, nsa_ck_w1, nsa_ck_w2, nsa_cv_w1, nsa_cv_w2, od_w_in, od_w_out, mla_q_norm, mla_kv_norm, mla_w_q_up, mla_w_kv_up, swa_sinks):
    b, s, d = x.shape
    depth = ada_w.shape[0]
    mods = ada_all(c, ada_w, ada_b)
    t_dense = min(512, s)
    bank_l = bias_bank(rel_table, t_dense)
    far_l = rel_table[N_BUCKETS - 1] * LOG2E
    future = np.arange(t_dense)[:, None] > np.arange(t_dense)[None, :]
    bank_c = jnp.asarray(np.where(future, NEG, 0.0).astype(np.float32)[None, None])
    t_w = min(256, s)
    banks = {"dense": (bank_l, far_l, t_dense), "causal": (bank_c, jnp.zeros((N_BIAS_HEADS,), F32), t_dense),
             "nsa_window": (bias_bank(rel_table, t_w, window=NSA_WINDOW), t_w),
             "swa": (bias_bank(rel_table, SWA_WINDOW, window=SWA_WINDOW), SWA_WINDOW)}

    for layer in range(depth):
        shift_m, scale_m, gate_m, shift_f, scale_f, gate_f = jnp.split(mods[layer], 6, axis=-1)
        i = layer // 2
        if layer % 2 == 0:
            proj = even_projection(x, norm_mix[layer], scale_m, shift_m, ev_w_in[i])
            x = even_mixer(proj, x, gate_m, ev_w_out[i], nsa_pos_k[i], nsa_pos_v[i], nsa_ck_w1[i], nsa_ck_w2[i],
                           nsa_cv_w1[i], nsa_cv_w2[i], rel_table, banks)
        else:
            proj = odd_projection(x, norm_mix[layer], scale_m, shift_m, od_w_in[i])
            x = odd_mixer(proj, x, gate_m, od_w_out[i], mla_q_norm[i], mla_kv_norm[i], mla_w_q_up[i], mla_w_kv_up[i],
                          swa_sinks[i], banks)
        h, cw = ffn_pre(x, norm_ffn[layer], scale_f, shift_f, router_w, router_b)
        x = moe_dense(h, cw.transpose(0, 2, 1), x, gate_f, moe_w_gate, moe_w_up, moe_w_down, layer)
    return final_rmsnorm(x, final_norm)
```

```python
import functools
import math

import numpy as np
import jax
import jax.numpy as jnp
from jax import lax
from jax.experimental import pallas as pl
from jax.experimental.pallas import tpu as pltpu

F32 = jnp.float32
BF16 = jnp.bfloat16
HI = lax.Precision.HIGHEST

D_MODEL = 1024
HEAD_DIM = 64
NEG = -1e30
BIG = 1e30
M_INIT = -1e9
NORM_EPS = 1e-6
LOG2E = math.log2(math.e)

N_BUCKETS = 32
MAX_DISTANCE = 1024
N_BIAS_HEADS = 8

NSA_HEADS = 8
NSA_KV_HEADS = 2
NSA_GROUP = NSA_HEADS // NSA_KV_HEADS
CMP_LEN = 32
CMP_STRIDE = 16
CMP_HIDDEN = 256
SLC_BLOCK = 64
SLC_TOPN = 16
NSA_WINDOW = 512

MOBA_HEADS = 8
MOBA_BLOCK = 256
MOBA_TOPK = 3

MLA_HEADS = 4
MLA_Q_RANK = 256
MLA_KV_RANK = 128
MLA_NOPE = 128
MLA_ROPE = 64
MLA_V = 128
ROPE_THETA = 10000.0

SWA_HEADS = 8
SWA_KV_HEADS = 2
SWA_GROUP = SWA_HEADS // SWA_KV_HEADS
SWA_WINDOW = 128

N_EXPERTS = 16
N_GROUPS = 4
EXPERTS_PER_GROUP = N_EXPERTS // N_GROUPS
D_EXPERT = 512

EVEN_WIDTHS = (NSA_HEADS * HEAD_DIM,) + (NSA_KV_HEADS * HEAD_DIM,) * 6 + (3 * NSA_HEADS,) + (MOBA_HEADS * HEAD_DIM,) * 3
ODD_WIDTHS = (MLA_Q_RANK, MLA_KV_RANK, MLA_ROPE, SWA_HEADS * HEAD_DIM, SWA_KV_HEADS * HEAD_DIM, SWA_KV_HEADS * HEAD_DIM)

LANES = 128
VMEM_LIMIT = 56 * 1024 * 1024


def _params(*sem):
    return pltpu.CompilerParams(dimension_semantics=sem, vmem_limit_bytes=VMEM_LIMIT)


def _dot(a, b, precision=None):
    return lax.dot_general(a, b, (((1,), (0,)), ((), ())), precision=precision, preferred_element_type=F32)


def _dot_nt(a, b, precision=None):
    return lax.dot_general(a, b, (((1,), (1,)), ((), ())), precision=precision, preferred_element_type=F32)


def _iota(shape, dim):
    return lax.broadcasted_iota(jnp.int32, shape, dim)


def _offsets(widths):
    out, acc = [], 0
    for w in widths:
        out.append((acc, acc + w))
        acc += w
    return out


def _bucket_thresholds():
    d = np.arange(0, 4 * MAX_DISTANCE, dtype=np.int64)
    exact = N_BUCKETS // 2
    x = np.maximum(d, 1).astype(np.float32) / np.float32(exact)
    logp = exact + (np.log(x) / np.float32(math.log(MAX_DISTANCE / exact)) * np.float32(N_BUCKETS - exact)).astype(np.int32)
    bucket = np.where(d < exact, d, np.minimum(logp, N_BUCKETS - 1))
    return [int(np.argmax(bucket >= b)) for b in range(1, N_BUCKETS)]


BUCKET_THR = _bucket_thresholds()
FAR_DIST = BUCKET_THR[-1]


def _ada_kernel(c_ref, w_ref, b_ref, o_ref):
    c = c_ref[...]
    o_ref[...] = _dot(c * jax.nn.sigmoid(c), w_ref[...], precision=HI) + b_ref[...]


def ada_all(c, ada_w, ada_b):
    depth, d, n = ada_w.shape
    rows = 8
    cp = jnp.pad(c, ((0, rows - c.shape[0]), (0, 0)))
    tn = 1536
    out = pl.pallas_call(
        _ada_kernel,
        grid=(depth, n // tn),
        in_specs=[pl.BlockSpec((rows, d), lambda l, j: (0, 0)),
                  pl.BlockSpec((None, d, tn), lambda l, j: (l, 0, j)),
                  pl.BlockSpec((None, 1, tn), lambda l, j: (l, 0, j))],
        out_specs=pl.BlockSpec((None, rows, tn), lambda l, j: (l, 0, j)),
        out_shape=jax.ShapeDtypeStruct((depth, rows, n), F32),
        compiler_params=_params("parallel", "parallel"),
        name="ada",
    )(cp, ada_w, ada_b.reshape(depth, 1, n))
    return out[:, :c.shape[0], :]


def _norm_mod(x, g, sc, sh):
    y = x * lax.rsqrt(jnp.mean(x * x, axis=-1, keepdims=True) + NORM_EPS) * g
    return y * (1.0 + sc) + sh


def _nmm_kernel(x_ref, g_ref, sc_ref, sh_ref, w_ref, *o_refs, outs):
    h = _norm_mod(x_ref[...], g_ref[...], sc_ref[...], sh_ref[...])
    res = _dot(h.astype(BF16), w_ref[...])
    for o_ref, (lo, hi, scale, _) in zip(o_refs, outs):
        part = res[:, lo:hi]
        o_ref[...] = (part if scale == 1.0 else part * scale).astype(o_ref.dtype)


def norm_mod_matmul(x, g, sc, sh, w, outs, tm=512):
    b, s, d = x.shape
    n = w.shape[1]
    tm = min(tm, s)
    return pl.pallas_call(
        functools.partial(_nmm_kernel, outs=outs),
        grid=(b, s // tm),
        in_specs=[pl.BlockSpec((None, tm, d), lambda bi, i: (bi, i, 0)),
                  pl.BlockSpec((1, d), lambda bi, i: (0, 0)),
                  pl.BlockSpec((None, 1, d), lambda bi, i: (bi, 0, 0)),
                  pl.BlockSpec((None, 1, d), lambda bi, i: (bi, 0, 0)),
                  pl.BlockSpec((d, n), lambda bi, i: (0, 0))],
        out_specs=[pl.BlockSpec((None, tm, hi - lo), lambda bi, i: (bi, i, 0)) for lo, hi, _, _ in outs],
        out_shape=[jax.ShapeDtypeStruct((b, s, hi - lo), dt) for lo, hi, _, dt in outs],
        compiler_params=_params("parallel", "parallel"),
        name="norm_mod_matmul",
    )(x, g.reshape(1, d), sc.reshape(b, 1, d), sh.reshape(b, 1, d), w)


def _bias_kernel(tab_ref, o_ref, *, rows, cols, step, cstride, c0, key_major=False, window=None):
    sub = 8
    off = pl.program_id(0) * step - c0
    unit = LOG2E if key_major else 1.0
    if key_major:
        base = _iota((sub, cols), 1) - _iota((sub, cols), 0)
    else:
        base = _iota((sub, cols), 0) - cstride * _iota((sub, cols), 1)

    def body(r, carry):
        dist = off + base + (-r * sub if key_major else r * sub)
        vs = [jnp.full((sub, cols), tab_ref[0, h] * unit, F32) for h in range(N_BIAS_HEADS)]
        for b in range(1, N_BUCKETS):
            ge = dist >= BUCKET_THR[b - 1]
            for h in range(N_BIAS_HEADS):
                vs[h] = jnp.where(ge, tab_ref[b, h] * unit, vs[h])
        masked = dist < 0 if window is None else (dist < 0) | (dist >= window)
        for h in range(N_BIAS_HEADS):
            v = jnp.where(masked, NEG, vs[h]) if key_major else vs[h]
            o_ref[h, pl.ds(pl.multiple_of(r * sub, sub), sub), :] = v
        return carry

    lax.fori_loop(0, rows // sub, body, 0)


def bias_bank(rel_table, t, window=None):
    nd = -(-(FAR_DIST + t - 1) // t) if window is None else -(-window // t) + 1
    kern = functools.partial(_bias_kernel, rows=t, cols=t, step=t, cstride=1, c0=0, key_major=True, window=window)
    bank = pl.pallas_call(
        kern,
        grid=(nd,),
        in_specs=[pl.BlockSpec(memory_space=pltpu.SMEM)],
        out_specs=pl.BlockSpec((N_BIAS_HEADS, None, t, t), lambda i: (0, i, 0, 0)),
        out_shape=jax.ShapeDtypeStruct((N_BIAS_HEADS, nd, t, t), F32),
        compiler_params=_params("parallel"),
        name="bias_bank",
    )(rel_table)
    return bank


def _bias_cmp_kernel(tab_ref, o_ref, *, rows, cols):
    sub = 8
    width = min(LANES, cols)
    t0 = pl.program_id(0) * rows
    base = _iota((sub, width), 0) - CMP_STRIDE * _iota((sub, width), 1)

    def body(r, carry):
        rs = pl.ds(pl.multiple_of(r * sub, sub), sub)
        for c in range(cols // width):
            cs = slice(c * width, (c + 1) * width)
            off = t0 + r * sub - (CMP_LEN - 1) - CMP_STRIDE * width * c
            lo = off - CMP_STRIDE * (width - 1)
            hi = off + sub - 1

            @pl.when((hi >= 0) & (lo < FAR_DIST))
            def _():
                dist = off + base
                vs = [jnp.full((sub, width), tab_ref[0, h], F32) for h in range(N_BIAS_HEADS)]
                for b in range(1, N_BUCKETS):
                    ge = dist >= BUCKET_THR[b - 1]
                    for h in range(N_BIAS_HEADS):
                        vs[h] = jnp.where(ge, tab_ref[b, h], vs[h])
                for h in range(N_BIAS_HEADS):
                    o_ref[h, rs, cs] = vs[h]

            @pl.when(lo >= FAR_DIST)
            def _():
                for h in range(N_BIAS_HEADS):
                    o_ref[h, rs, cs] = jnp.full((sub, width), tab_ref[N_BUCKETS - 1, h], F32)

            @pl.when(hi < 0)
            def _():
                for h in range(N_BIAS_HEADS):
                    o_ref[h, rs, cs] = jnp.full((sub, width), tab_ref[0, h], F32)
        return carry

    lax.fori_loop(0, rows // sub, body, 0)


def bias_cmp(rel_table, s, nc, tq=256):
    kern = functools.partial(_bias_cmp_kernel, rows=tq, cols=nc)
    return pl.pallas_call(
        kern,
        grid=(s // tq,),
        in_specs=[pl.BlockSpec(memory_space=pltpu.SMEM)],
        out_specs=pl.BlockSpec((N_BIAS_HEADS, tq, nc), lambda i: (0, i, 0)),
        out_shape=jax.ShapeDtypeStruct((N_BIAS_HEADS, s, nc), F32),
        compiler_params=_params("parallel"),
        name="bias_cmp",
    )(rel_table)


def _compress_kernel(x_ref, w1_ref, pos_ref, w2_ref, o_ref):
    x = x_ref[...]
    w1 = w1_ref[...]
    half = w1.shape[0]
    hid = w1.shape[1] // 2
    r = _dot(x, w1, precision=HI)
    pos = pos_ref[...]
    pb = _dot(pos[:, :half], w1[:, :hid], precision=HI) + _dot(pos[:, half:], w1[:, hid:], precision=HI)
    nxt = pltpu.roll(r[:, hid:], x.shape[0] - 1, 0)
    pre = r[:, :hid] + nxt + pb[0:1, :]
    o_ref[...] = _dot(jax.nn.gelu(pre), w2_ref[...], precision=HI)


def nsa_compress(kv, w1, pos, w2):
    two, bh, s, dh = kv.shape
    nc = s // CMP_STRIDE
    half = CMP_STRIDE * dh
    hid = w1.shape[2]
    x = kv.reshape(two, bh, nc, half)
    w1cat = jnp.concatenate([w1[:, :half, :], w1[:, half:, :]], axis=2)
    posf = jnp.broadcast_to(pos.reshape(two, 1, CMP_LEN * dh), (two, 8, CMP_LEN * dh))
    return pl.pallas_call(
        _compress_kernel,
        grid=(two, bh),
        in_specs=[pl.BlockSpec((None, None, nc, half), lambda w, i: (w, i, 0, 0)),
                  pl.BlockSpec((None, half, 2 * hid), lambda w, i: (w, 0, 0)),
                  pl.BlockSpec((None, 8, CMP_LEN * dh), lambda w, i: (w, 0, 0)),
                  pl.BlockSpec((None, hid, dh), lambda w, i: (w, 0, 0))],
        out_specs=pl.BlockSpec((None, None, nc, dh), lambda w, i: (w, i, 0, 0)),
        out_shape=jax.ShapeDtypeStruct((two, bh, nc, dh), F32),
        compiler_params=_params("parallel", "parallel"),
        name="nsa_compress",
    )(x, w1cat, posf, w2)


def _topn_mask(score, index, n):
    idx_f = index.astype(F32)
    sel = jnp.zeros(score.shape, jnp.bool_)
    for _ in range(n):
        m = jnp.max(score, axis=0, keepdims=True)
        first = jnp.min(jnp.where(score == m, idx_f, float(score.shape[0])), axis=0, keepdims=True)
        pick = idx_f == first
        sel = sel | pick
        score = jnp.where(pick, -jnp.inf, score)
    return sel


def _cmp_kernel(q_ref, k_ref, v_ref, b_ref, ov_ref, oc_ref, sel_ref, *, tq, nc, nslc, ntop, group):
    qi = pl.program_id(2)
    t = qi * tq + _iota((tq, 1), 0)
    n = _iota((1, nc), 1)
    mask = (n * CMP_STRIDE + (CMP_LEN - 1) <= t) & (n < nc - 1)
    k = k_ref[...]
    v = v_ref[...].astype(BF16)
    dh = k.shape[-1]
    psum = jnp.zeros((tq, nc), F32)
    scores = lambda g: _dot_nt(q_ref[g], k, precision=HI) + b_ref[g]
    s_next = scores(0)
    for g in range(group):
        s, s_next = s_next, (scores(g + 1) if g + 1 < group else None)
        s = jnp.where(mask, s, NEG)
        m = jnp.max(s, axis=-1, keepdims=True)
        p = jnp.where(mask, jnp.exp(s - m), 0.0)
        den = jnp.sum(p, axis=-1, keepdims=True)
        p = p * jnp.where(den > 0.0, 1.0 / den, 0.0)
        oc_ref[:, g * dh:(g + 1) * dh] = _dot(p.astype(BF16), v)
        psum = psum + p
    imp = _dot_nt(ov_ref[...], psum, precision=HI)
    j = _iota((nslc, tq), 0)
    blk = jnp.right_shift(qi * tq + _iota((1, tq), 1), int(math.log2(SLC_BLOCK)))
    forced = (j == 0) | (j == blk) | (j == blk - 1)
    valid = j <= blk
    score = jnp.where(forced, BIG, jnp.where(valid, imp, NEG))
    sel = _topn_mask(score, j, ntop) & valid
    sel_ref[0] = jnp.where(sel, 0.0, NEG)


def nsa_cmp_attention(q, k_cmp, v_cmp, bias_c, tq=256):
    b, hk, g, s, dh = q.shape
    nc = k_cmp.shape[2]
    nslc = s // SLC_BLOCK
    ntop = min(SLC_TOPN, nslc)
    cmp_start = np.arange(nc, dtype=np.int64) * CMP_STRIDE
    slc_lo = np.arange(nslc, dtype=np.int64) * SLC_BLOCK
    overlap = ((cmp_start[:, None] <= slc_lo[None, :] + SLC_BLOCK - 1)
               & (cmp_start[:, None] + CMP_LEN - 1 >= slc_lo[None, :])).astype(np.float32)
    kern = functools.partial(_cmp_kernel, tq=tq, nc=nc, nslc=nslc, ntop=ntop, group=g)
    return pl.pallas_call(
        kern,
        grid=(b, hk, s // tq),
        in_specs=[pl.BlockSpec((None, None, g, tq, dh), lambda bi, h, i: (bi, h, 0, i, 0)),
                  pl.BlockSpec((None, None, nc, dh), lambda bi, h, i: (bi, h, 0, 0)),
                  pl.BlockSpec((None, None, nc, dh), lambda bi, h, i: (bi, h, 0, 0)),
                  pl.BlockSpec((g, tq, nc), lambda bi, h, i: (h, i, 0)),
                  pl.BlockSpec((nslc, nc), lambda bi, h, i: (0, 0))],
        out_specs=[pl.BlockSpec((None, tq, g * dh), lambda bi, h, i: (bi, i, h)),
                   pl.BlockSpec((None, None, 1, nslc, tq), lambda bi, h, i: (bi, h, 0, 0, i))],
        out_shape=[jax.ShapeDtypeStruct((b, s, hk * g * dh), F32),
                   jax.ShapeDtypeStruct((b, hk, 1, nslc, s), F32)],
        compiler_params=_params("parallel", "parallel", "parallel"),
        name="nsa_cmp_attention",
    )(q, k_cmp, v_cmp, bias_c, jnp.asarray(overlap.T))


def _kmean_kernel(k_ref, o_ref, *, nb, blk):
    k = k_ref[...]
    o_ref[...] = jnp.sum(k.reshape(nb, blk, k.shape[-1]), axis=1) * (1.0 / blk)


def _moba_sel_kernel(q_ref, km_ref, sel_ref, *, tq, nb, ntop):
    qi = pl.program_id(2)
    own = jnp.right_shift(qi * tq + _iota((1, tq), 1), int(math.log2(MOBA_BLOCK)))
    j = _iota((nb, tq), 0)
    gate = _dot_nt(km_ref[...], q_ref[...], precision=HI)
    past = j < own
    sel = _topn_mask(jnp.where(past, gate, NEG), j, ntop) & past
    sel_ref[...] = jnp.where(sel | (j == own), 0.0, NEG)


def moba_select(q, k, tq=256):
    b, h, s, dh = q.shape
    nb = s // MOBA_BLOCK
    ntop = min(MOBA_TOPK, nb)
    kmean = pl.pallas_call(
        functools.partial(_kmean_kernel, nb=nb, blk=MOBA_BLOCK),
        grid=(b, h),
        in_specs=[pl.BlockSpec((None, None, s, dh), lambda bi, hi: (bi, hi, 0, 0))],
        out_specs=pl.BlockSpec((None, None, nb, dh), lambda bi, hi: (bi, hi, 0, 0)),
        out_shape=jax.ShapeDtypeStruct((b, h, nb, dh), F32),
        compiler_params=_params("parallel", "parallel"),
        name="moba_kmean",
    )(k)
    return pl.pallas_call(
        functools.partial(_moba_sel_kernel, tq=tq, nb=nb, ntop=ntop),
        grid=(b, h, s // tq),
        in_specs=[pl.BlockSpec((None, None, tq, dh), lambda bi, hi, i: (bi, hi, i, 0)),
                  pl.BlockSpec((None, None, nb, dh), lambda bi, hi, i: (bi, hi, 0, 0))],
        out_specs=pl.BlockSpec((None, None, nb, tq), lambda bi, hi, i: (bi, hi, 0, i)),
        out_shape=jax.ShapeDtypeStruct((b, h, nb, s), F32),
        compiler_params=_params("parallel", "parallel", "parallel"),
        name="moba_select",
    )(q, kmean)


AUG = 64


def _flash_kernel(qt_ref, kt_ref, *refs, hb, kb, sb, bh, t, tq, kc, nparts, use_sel, nbt, nd, ahead):
    pos = 0
    q_refs = refs[pos:pos + nparts]; pos += nparts
    k_refs = refs[pos:pos + nparts]; pos += nparts
    vt_ref = refs[pos]; pos += 1
    sel_ref = None
    if use_sel:
        sel_ref = refs[pos]; pos += 1
    bias_ref, far_ref, o_ref, m_ref, l_ref, acc_ref = refs[pos:pos + 6]

    head0 = pl.program_id(1) * hb
    step = pl.program_id(2)
    qi = qt_ref[step]
    ki = kt_ref[step]

    @pl.when(ki == 0)
    def _():
        m_ref[...] = jnp.full(m_ref.shape, M_INIT, F32)
        l_ref[...] = jnp.zeros(l_ref.shape, F32)
        acc_ref[...] = jnp.zeros(acc_ref.shape, F32)

    nchunk = t // kc
    nlane = t // tq

    def mask_lanes(si, qc):
        rows = sel_ref[si, qc * tq:(qc + 1) * tq, :]
        shift = lax.rem(AUG + LANES - lax.rem(ki * nbt, LANES), LANES)
        lane = _iota((tq, LANES), 1)
        keep = (lane >= AUG) & (lane < AUG + nbt)
        return jnp.where(keep, pltpu.roll(rows, shift, 1), 0.0).astype(BF16)

    def scores(g, qc, near, masks):
        kg = g * kb // hb
        bg = g if bh > 1 else 0
        ql = slice(qc * tq, (qc + 1) * tq)
        qs = [q_ref[g, ql, :] for q_ref in q_refs]
        if use_sel:
            qs[0] = jnp.where(_iota((tq, LANES), 1) < AUG, qs[0], masks[(g * sb // hb, qc)])
        chunks, top = [], None
        for c in range(nchunk):
            kr = slice(c * kc, (c + 1) * kc)
            s = _dot_nt(k_refs[0][kg, kr, :], qs[0])
            for p in range(1, nparts):
                s = s + _dot_nt(k_refs[p][kg, kr, :], qs[p])
            if near:
                s = s + bias_ref[bg, kr, ql]
            chunks.append(s)
            top = s if top is None else jnp.maximum(top, s)
        return chunks, top

    def update(g, qc, near, chunks, top):
        kg = g * kb // hb
        ql = slice(qc * tq, (qc + 1) * tq)
        shift = 0.0 if near else far_ref[head0 + g]
        m = m_ref[g, :, ql]
        m_new = jnp.maximum(m, jnp.max(top, axis=0, keepdims=True) + shift)
        alpha = jnp.exp2(m - m_new)
        base = m_new - shift
        pv = psum = None
        for c in range(nchunk):
            p = jnp.exp2(chunks[c] - base)
            ps = jnp.sum(p, axis=0, keepdims=True)
            pd = _dot(vt_ref[kg, :, c * kc:(c + 1) * kc], p.astype(BF16))
            pv = pd if pv is None else pv + pd
            psum = ps if psum is None else psum + ps
        m_ref[g, :, ql] = m_new
        l_ref[g, :, ql] = alpha * l_ref[g, :, ql] + psum
        acc_ref[g, :, ql] = alpha * acc_ref[g, :, ql] + pv

    def body(near):
        masks = {}
        if use_sel:
            masks = {(si, qc): mask_lanes(si, qc) for si in range(sb) for qc in range(nlane)}
        items = [(g, qc) for g in range(hb) for qc in range(nlane)]
        pending = [scores(*item, near, masks) for item in items[:ahead]]
        for i, item in enumerate(items):
            if i + ahead < len(items):
                pending.append(scores(*items[i + ahead], near, masks))
            update(*item, near, *pending.pop(0))

    @pl.when(qi - ki < nd)
    def _():
        body(True)

    @pl.when(qi - ki >= nd)
    def _():
        body(False)

    @pl.when(ki == qi)
    def _():
        for g in range(hb):
            dv = acc_ref.shape[1]
            o_ref[:, g * dv:(g + 1) * dv] = jnp.transpose(acc_ref[g] / l_ref[g]).astype(o_ref.dtype)


def flash_attention(q_parts, k_parts, v, bias, far, sel=None, t=512, blk=1, tq=256, kc=512, ahead=2):
    b, ng, hb, s, _ = q_parts[0].shape
    kb = v.shape[2]
    dv = v.shape[-1]
    t = min(t, s)
    tq, kc = min(tq, t), min(kc, t)
    nq = s // t
    nd = bias.shape[1]
    pairs = [(i, j) for i in range(nq) for j in range(i + 1)]
    qt = jnp.asarray(np.array([p[0] for p in pairs], np.int32))
    kt = jnp.asarray(np.array([p[1] for p in pairs], np.int32))
    nparts = len(q_parts)
    use_sel = sel is not None
    sb = sel.shape[2] if use_sel else 1
    bh = hb if bias.shape[0] > 1 else 1
    nbt = t // blk
    vt = jnp.swapaxes(v, -1, -2)
    q_parts, k_parts = list(q_parts), list(k_parts)
    if use_sel:
        onehot = (np.arange(s)[:, None] % t // blk == np.arange(LANES - AUG)[None, :]).astype(np.float32)
        k0 = k_parts[0]
        k_parts[0] = jnp.concatenate(
            [k0, jnp.broadcast_to(jnp.asarray(onehot, k0.dtype), k0.shape[:-1] + (LANES - AUG,))], axis=-1)
        q_parts[0] = jnp.pad(q_parts[0], ((0, 0),) * 4 + ((0, LANES - AUG),))
        selq = jnp.swapaxes(sel, -1, -2)
        selq = jnp.pad(selq, ((0, 0),) * 4 + ((0, LANES - selq.shape[-1]),))

    in_specs, args = [], []
    for qp in q_parts:
        in_specs.append(pl.BlockSpec((None, None, hb, t, qp.shape[-1]), lambda bi, n, st, qt, kt: (bi, n, 0, qt[st], 0)))
        args.append(qp)
    for kp in k_parts:
        in_specs.append(pl.BlockSpec((None, None, kb, t, kp.shape[-1]), lambda bi, n, st, qt, kt: (bi, n, 0, kt[st], 0)))
        args.append(kp)
    in_specs.append(pl.BlockSpec((None, None, kb, dv, t), lambda bi, n, st, qt, kt: (bi, n, 0, 0, kt[st])))
    args.append(vt)
    if use_sel:
        in_specs.append(pl.BlockSpec((None, None, sb, t, LANES), lambda bi, n, st, qt, kt: (bi, n, 0, qt[st], 0)))
        args.append(selq)
    near_tile = lambda qt, kt, st: jnp.minimum(qt[st] - kt[st], nd - 1)
    if bh > 1:
        in_specs.append(pl.BlockSpec((bh, None, t, t), lambda bi, n, st, qt, kt: (n, near_tile(qt, kt, st), 0, 0)))
    else:
        in_specs.append(pl.BlockSpec((1, None, t, t), lambda bi, n, st, qt, kt: (0, near_tile(qt, kt, st), 0, 0)))
    in_specs.append(pl.BlockSpec(memory_space=pltpu.SMEM))
    args += [bias, far]
    kern = functools.partial(_flash_kernel, hb=hb, kb=kb, sb=sb, bh=bh, t=t, tq=tq, kc=kc, nparts=nparts,
                             use_sel=use_sel, nbt=nbt, nd=nd, ahead=ahead)
    grid_spec = pltpu.PrefetchScalarGridSpec(
        num_scalar_prefetch=2,
        grid=(b, ng, len(pairs)),
        in_specs=in_specs,
        out_specs=pl.BlockSpec((None, t, hb * dv), lambda bi, n, st, qt, kt: (bi, qt[st], n)),
        scratch_shapes=[pltpu.VMEM((hb, 1, t), F32), pltpu.VMEM((hb, 1, t), F32), pltpu.VMEM((hb, dv, t), F32)])
    return pl.pallas_call(
        kern,
        grid_spec=grid_spec,
        out_shape=jax.ShapeDtypeStruct((b, s, ng * hb * dv), F32),
        compiler_params=_params("parallel", "parallel", "arbitrary"),
        name="flash_attention",
    )(qt, kt, *args)


def _window_kernel(*refs, group, t, ntile, has_sink):
    q_ref = refs[0]
    k_refs = refs[1:1 + ntile]
    vt_refs = refs[1 + ntile:1 + 2 * ntile]
    bias_ref = refs[1 + 2 * ntile]
    sink_ref = refs[2 + 2 * ntile] if has_sink else None
    o_ref = refs[-1]
    kvh = pl.program_id(1)

    def scores(g):
        q = q_ref[g]
        ss = [_dot_nt(k_refs[r][...], q) + bias_ref[g, ntile - 1 - r] for r in range(ntile)]
        top = ss[0]
        for s in ss[1:]:
            top = jnp.maximum(top, s)
        return ss, top

    def finish(g, ss, top):
        m = jnp.max(top, axis=0, keepdims=True)
        den = None
        if has_sink:
            sink = sink_ref[kvh * group + g] * LOG2E
            m = jnp.maximum(m, sink)
            den = jnp.exp2(sink - m)
        acc = None
        for r in range(ntile):
            p = jnp.exp2(ss[r] - m)
            ps = jnp.sum(p, axis=0, keepdims=True)
            pv = _dot(vt_refs[r][...], p.astype(BF16))
            den = ps if den is None else den + ps
            acc = pv if acc is None else acc + pv
        dh = acc.shape[0]
        o_ref[:, g * dh:(g + 1) * dh] = jnp.transpose(acc / den)

    pending = scores(0)
    for g in range(group):
        nxt = scores(g + 1) if g + 1 < group else None
        finish(g, *pending)
        pending = nxt


def window_attention(q, k, v, bank, t, window, sinks=None):
    b, hk, g, s, dh = q.shape
    ntile = -(-window // t) + 1
    front = (ntile - 1) * t
    kp = jnp.pad(k, ((0, 0), (0, 0), (front, 0), (0, 0)))
    vtp = jnp.pad(jnp.swapaxes(v, -1, -2), ((0, 0), (0, 0), (0, 0), (front, 0)))
    has_sink = sinks is not None
    in_specs = [pl.BlockSpec((None, None, g, t, dh), lambda bi, h, i: (bi, h, 0, i, 0))]
    args = [q]
    for r in range(ntile):
        in_specs.append(pl.BlockSpec((None, None, t, dh), lambda bi, h, i, r=r: (bi, h, i + r, 0)))
        args.append(kp)
    for r in range(ntile):
        in_specs.append(pl.BlockSpec((None, None, dh, t), lambda bi, h, i, r=r: (bi, h, 0, i + r)))
        args.append(vtp)
    in_specs.append(pl.BlockSpec((g, ntile, t, t), lambda bi, h, i: (h, 0, 0, 0)))
    args.append(bank)
    if has_sink:
        in_specs.append(pl.BlockSpec(memory_space=pltpu.SMEM))
        args.append(sinks)
    kern = functools.partial(_window_kernel, group=g, t=t, ntile=ntile, has_sink=has_sink)
    return pl.pallas_call(
        kern,
        grid=(b, hk, s // t),
        in_specs=in_specs,
        out_specs=pl.BlockSpec((None, t, g * dh), lambda bi, h, i: (bi, i, h)),
        out_shape=jax.ShapeDtypeStruct((b, s, hk * g * dh), F32),
        compiler_params=_params("parallel", "parallel", "parallel"),
        name="window_attention",
    )(*args)


def _rms(x, g):
    return x * lax.rsqrt(jnp.mean(x * x, axis=-1, keepdims=True) + NORM_EPS) * g


def _mla_up_kernel(cq_ref, ckv_ref, kr_ref, krs_ref, qn_ref, kvn_ref, wq_ref, wkv_ref, cs_ref, sn_ref,
                   qnope_ref, qrope_ref, knope_ref, v_ref, krope_ref, *, scale):
    cs = cs_ref[...]
    sn = sn_ref[...]
    q = _dot(_rms(cq_ref[...], qn_ref[...]).astype(BF16), wq_ref[...]) * scale
    kv = _dot(_rms(ckv_ref[...], kvn_ref[...]).astype(BF16), wkv_ref[...])
    wq_head = MLA_NOPE + 2 * MLA_ROPE
    for h in range(MLA_HEADS):
        base = h * wq_head
        qnope_ref[h] = q[:, base:base + MLA_NOPE].astype(qnope_ref.dtype)
        x = q[:, base + MLA_NOPE:base + MLA_NOPE + MLA_ROPE]
        xs = q[:, base + MLA_NOPE + MLA_ROPE:base + wq_head]
        qrope_ref[h] = (x * cs + xs * sn).astype(qrope_ref.dtype)
        kb = h * (MLA_NOPE + MLA_V)
        knope_ref[h] = kv[:, kb:kb + MLA_NOPE].astype(knope_ref.dtype)
        v_ref[h] = kv[:, kb + MLA_NOPE:kb + MLA_NOPE + MLA_V].astype(v_ref.dtype)
    kr = (kr_ref[...] * cs + krs_ref[...] * sn).astype(krope_ref.dtype)
    for h in range(MLA_HEADS):
        krope_ref[h] = kr


def mla_up(c_q, c_kv, k_rope, k_rope_sw, q_norm, kv_norm, w_q_up, w_kv_up, tm=512):
    b, s, _ = c_q.shape
    half = MLA_ROPE // 2
    inv = ROPE_THETA ** (-np.arange(0, MLA_ROPE, 2, dtype=np.float32) / np.float32(MLA_ROPE))
    ang = np.arange(s, dtype=np.float32)[:, None] * inv[None, :].astype(np.float32)
    cos, sin = np.cos(ang).astype(np.float32), np.sin(ang).astype(np.float32)
    cs = jnp.asarray(np.concatenate([cos, cos], axis=1))
    sn = jnp.asarray(np.concatenate([-sin, sin], axis=1))
    dq = MLA_NOPE + MLA_ROPE
    wq = w_q_up.reshape(MLA_Q_RANK, MLA_HEADS, dq)
    rope_cols = wq[:, :, MLA_NOPE:]
    swapped = jnp.concatenate([rope_cols[:, :, half:], rope_cols[:, :, :half]], axis=2)
    wq_aug = jnp.concatenate([wq, swapped], axis=2).reshape(MLA_Q_RANK, MLA_HEADS * (dq + MLA_ROPE)).astype(BF16)
    wkv = w_kv_up.astype(BF16)
    scale = (MLA_NOPE + MLA_ROPE) ** -0.5 * LOG2E
    tm = min(tm, s)
    row = lambda w: pl.BlockSpec((None, tm, w), lambda bi, i: (bi, i, 0))
    full = lambda a: pl.BlockSpec(a.shape, lambda bi, i: (0,) * a.ndim)
    head = lambda w: pl.BlockSpec((None, MLA_HEADS, tm, w), lambda bi, i: (bi, 0, i, 0))
    qn2, kvn2 = q_norm.reshape(1, -1), kv_norm.reshape(1, -1)
    outs = pl.pallas_call(
        functools.partial(_mla_up_kernel, scale=scale),
        grid=(b, s // tm),
        in_specs=[row(MLA_Q_RANK), row(MLA_KV_RANK), row(MLA_ROPE), row(MLA_ROPE), full(qn2), full(kvn2),
                  full(wq_aug), full(wkv),
                  pl.BlockSpec((tm, MLA_ROPE), lambda bi, i: (i, 0)), pl.BlockSpec((tm, MLA_ROPE), lambda bi, i: (i, 0))],
        out_specs=[head(MLA_NOPE), head(MLA_ROPE), head(MLA_NOPE), head(MLA_V), head(MLA_ROPE)],
        out_shape=[jax.ShapeDtypeStruct((b, MLA_HEADS, s, w), BF16)
                   for w in (MLA_NOPE, MLA_ROPE, MLA_NOPE, MLA_V, MLA_ROPE)],
        compiler_params=_params("parallel", "parallel"),
        name="mla_up",
    )(c_q, c_kv, k_rope, k_rope_sw, qn2, kvn2, wq_aug, wkv, cs, sn)
    return outs


def _out_kernel(*refs, gated):
    if gated:
        oc_ref, os_ref, ow_ref, gt_ref, ex_ref, ob_ref, x_ref, gm_ref, w_ref, o_ref = refs
        half = oc_ref.shape[-1]
        ge = _dot(jax.nn.sigmoid(gt_ref[...]), ex_ref[...], precision=HI)
        oa = ge[:, :half] * oc_ref[...] + ge[:, half:2 * half] * os_ref[...] + ge[:, 2 * half:] * ow_ref[...]
    else:
        oa_ref, ob_ref, x_ref, gm_ref, w_ref, o_ref = refs
        half = oa_ref.shape[-1]
        oa = oa_ref[...]
    mix = _dot(oa.astype(BF16), w_ref[:half, :]) + _dot(ob_ref[...].astype(BF16), w_ref[half:, :])
    o_ref[...] = x_ref[...] + gm_ref[...] * mix


def out_project(parts, ob, x, gate_m, w_out, gates=None, tm=512):
    b, s, d = x.shape
    gated = gates is not None
    tm = min(tm, s)
    row = lambda a: pl.BlockSpec((None, tm, a.shape[-1]), lambda bi, i: (bi, i, 0))
    full = lambda a: pl.BlockSpec(a.shape, lambda bi, i: (0,) * a.ndim)
    args, in_specs = [], []
    for p in parts:
        args.append(p); in_specs.append(row(p))
    if gated:
        half = parts[0].shape[-1]
        nh = half // HEAD_DIM
        ex = np.zeros((LANES, 3 * half), np.float32)
        for h in range(nh):
            for br in range(3):
                ex[h * 3 + br, br * half + h * HEAD_DIM: br * half + (h + 1) * HEAD_DIM] = 1.0
        gpad = jnp.pad(gates, ((0, 0), (0, 0), (0, LANES - gates.shape[-1])))
        ex = jnp.asarray(ex)
        args += [gpad, ex]; in_specs += [row(gpad), full(ex)]
    gm = gate_m.reshape(b, 1, d)
    wb = w_out.astype(BF16)
    args += [ob, x, gm, wb]
    in_specs += [row(ob), row(x), pl.BlockSpec((None, 1, d), lambda bi, i: (bi, 0, 0)), full(wb)]
    return pl.pallas_call(
        functools.partial(_out_kernel, gated=gated),
        grid=(b, s // tm),
        in_specs=in_specs,
        out_specs=pl.BlockSpec((None, tm, d), lambda bi, i: (bi, i, 0)),
        out_shape=jax.ShapeDtypeStruct((b, s, d), F32),
        compiler_params=_params("parallel", "parallel"),
        name="out_project",
    )(*args)


def _ffn_pre_kernel(x_ref, g_ref, sc_ref, sh_ref, rw_ref, rb_ref, h_ref, cw_ref):
    h = _norm_mod(x_ref[...], g_ref[...], sc_ref[...], sh_ref[...])
    h_ref[...] = h.astype(h_ref.dtype)
    aff = jax.nn.sigmoid(_dot_nt(rw_ref[...], h, precision=HI))
    biased = aff + rb_ref[...]
    epg = EXPERTS_PER_GROUP
    brow = [biased[e:e + 1, :] for e in range(N_EXPERTS)]
    arow = [aff[e:e + 1, :] for e in range(N_EXPERTS)]
    best = gsel = None
    for gi in range(N_GROUPS):
        a, b_, c, d_ = brow[gi * epg:(gi + 1) * epg]
        hi1, lo1, hi2, lo2 = jnp.maximum(a, b_), jnp.minimum(a, b_), jnp.maximum(c, d_), jnp.minimum(c, d_)
        score = jnp.maximum(hi1, hi2) + jnp.maximum(jnp.minimum(hi1, hi2), jnp.maximum(lo1, lo2))
        if gi == 0:
            best, gsel = score, jnp.zeros(score.shape, jnp.int32)
        else:
            better = score > best
            gsel = jnp.where(better, gi, gsel)
            best = jnp.where(better, score, best)

    def in_group(rows, j):
        v = rows[j]
        for gi in range(1, N_GROUPS):
            v = jnp.where(gsel == gi, rows[gi * epg + j], v)
        return v

    bv = [in_group(brow, j) for j in range(epg)]
    av = [in_group(arow, j) for j in range(epg)]

    def argmax_excluding(skip):
        val = idx = None
        for j in range(epg):
            cand = bv[j] if skip is None else jnp.where(skip == j, -jnp.inf, bv[j])
            if j == 0:
                val, idx = cand, jnp.zeros(cand.shape, jnp.int32)
            else:
                better = cand > val
                idx = jnp.where(better, j, idx)
                val = jnp.where(better, cand, val)
        return idx

    first = argmax_excluding(None)
    second = argmax_excluding(first)

    def pick(rows, idx):
        v = rows[0]
        for j in range(1, epg):
            v = jnp.where(idx == j, rows[j], v)
        return v

    a1, a2 = pick(av, first), pick(av, second)
    tot = a1 + a2
    e1, e2 = gsel * epg + first, gsel * epg + second
    eid = _iota(aff.shape, 0)
    cw_ref[...] = jnp.where(eid == e1, a1 / tot, 0.0) + jnp.where(eid == e2, a2 / tot, 0.0)


def ffn_pre(x, g, sc, sh, router_w, router_b, tm=512):
    b, s, d = x.shape
    e = router_w.shape[1]
    tm = min(tm, s)
    return pl.pallas_call(
        _ffn_pre_kernel,
        grid=(b, s // tm),
        in_specs=[pl.BlockSpec((None, tm, d), lambda bi, i: (bi, i, 0)),
                  pl.BlockSpec((1, d), lambda bi, i: (0, 0)),
                  pl.BlockSpec((None, 1, d), lambda bi, i: (bi, 0, 0)),
                  pl.BlockSpec((None, 1, d), lambda bi, i: (bi, 0, 0)),
                  pl.BlockSpec((e, d), lambda bi, i: (0, 0)),
                  pl.BlockSpec((e, 1), lambda bi, i: (0, 0))],
        out_specs=[pl.BlockSpec((None, tm, d), lambda bi, i: (bi, i, 0)),
                   pl.BlockSpec((None, e, tm), lambda bi, i: (bi, 0, i))],
        out_shape=[jax.ShapeDtypeStruct((b, s, d), BF16), jax.ShapeDtypeStruct((b, e, s), F32)],
        compiler_params=_params("parallel", "parallel"),
        name="ffn_pre",
    )(x, g.reshape(1, d), sc.reshape(b, 1, d), sh.reshape(b, 1, d), router_w.T, router_b.reshape(e, 1))


def _moe_kernel(h_ref, cw_ref, x_ref, gf_ref, wg_ref, wu_ref, wd_ref, o_ref, acc_ref):
    e = pl.program_id(2)

    @pl.when(e == 0)
    def _():
        acc_ref[...] = jnp.zeros(acc_ref.shape, F32)

    h = h_ref[...]
    a = _dot(h, wg_ref[...].astype(BF16))
    u = _dot(h, wu_ref[...].astype(BF16))
    cw = cw_ref[...]
    c = jnp.sum(jnp.where(_iota(cw.shape, 1) == e, cw, 0.0), axis=-1, keepdims=True)
    hid = (a * jax.nn.sigmoid(a)) * u * c
    acc_ref[...] += _dot(hid.astype(BF16), wd_ref[...].astype(BF16))

    @pl.when(e == pl.num_programs(2) - 1)
    def _():
        o_ref[...] = x_ref[...] + gf_ref[...] * acc_ref[...]


def moe_dense(h, cw, x, gate_f, w_gate, w_up, w_down, layer, tm=1024):
    b, s, d = x.shape
    _, ne, _, f = w_gate.shape
    tm = min(tm, s)
    return pl.pallas_call(
        _moe_kernel,
        grid=(b, s // tm, ne),
        in_specs=[pl.BlockSpec((None, tm, d), lambda bi, i, e: (bi, i, 0)),
                  pl.BlockSpec((None, tm, ne), lambda bi, i, e: (bi, i, 0)),
                  pl.BlockSpec((None, tm, d), lambda bi, i, e: (bi, i, 0)),
                  pl.BlockSpec((None, 1, d), lambda bi, i, e: (bi, 0, 0)),
                  pl.BlockSpec((None, None, d, f), lambda bi, i, e: (layer, e, 0, 0)),
                  pl.BlockSpec((None, None, d, f), lambda bi, i, e: (layer, e, 0, 0)),
                  pl.BlockSpec((None, None, f, d), lambda bi, i, e: (layer, e, 0, 0))],
        out_specs=pl.BlockSpec((None, tm, d), lambda bi, i, e: (bi, i, 0)),
        out_shape=jax.ShapeDtypeStruct((b, s, d), F32),
        scratch_shapes=[pltpu.VMEM((tm, d), F32)],
        compiler_params=_params("parallel", "parallel", "arbitrary"),
        name="moe_dense",
    )(h, cw, x, gate_f.reshape(b, 1, d), w_gate, w_up, w_down)


def _final_norm_kernel(x_ref, g_ref, o_ref):
    o_ref[...] = _rms(x_ref[...], g_ref[...])


def final_rmsnorm(x, g, tm=512):
    b, s, d = x.shape
    tm = min(tm, s)
    return pl.pallas_call(
        _final_norm_kernel,
        grid=(b, s // tm),
        in_specs=[pl.BlockSpec((None, tm, d), lambda bi, i: (bi, i, 0)), pl.BlockSpec((1, d), lambda bi, i: (0, 0))],
        out_specs=pl.BlockSpec((None, tm, d), lambda bi, i: (bi, i, 0)),
        out_shape=jax.ShapeDtypeStruct((b, s, d), F32),
        compiler_params=_params("parallel", "parallel"),
        name="final_rmsnorm",
    )(x, g.reshape(1, d))


def _heads(x, nh):
    b, s, w = x.shape
    return x.reshape(b, s, nh, w // nh).transpose(0, 2, 1, 3)


ATTN_SCALE = HEAD_DIM ** -0.5


def even_projection(x, g, sc, sh, w_in):
    off = _offsets(EVEN_WIDTHS)
    order = list(range(7)) + [8, 9, 10, 7]
    cols = jnp.concatenate([w_in[:, off[j][0]:off[j][1]] for j in order], axis=1)
    w = jnp.pad(cols, ((0, 0), (0, -cols.shape[1] % LANES))).astype(BF16)
    qw, kvw, hw = NSA_HEADS * HEAD_DIM, NSA_KV_HEADS * HEAD_DIM, MOBA_HEADS * HEAD_DIM
    p = qw + 6 * kvw
    outs = [(0, qw, ATTN_SCALE, F32),
            (0, qw, ATTN_SCALE * LOG2E, BF16),
            (qw, qw + 2 * kvw, 1.0, F32),
            (qw + 2 * kvw, p, 1.0, BF16),
            (p, p + hw, 1.0, F32),
            (p, p + hw, ATTN_SCALE * LOG2E, BF16),
            (p + hw, p + 2 * hw, 1.0, F32),
            (p + hw, p + 3 * hw, 1.0, BF16),
            (p + 3 * hw, p + 3 * hw + LANES, 1.0, F32)]
    return norm_mod_matmul(x, g, sc, sh, w, outs)


def even_mixer(proj, x, gate_m, w_out, pos_k, pos_v, ck_w1, ck_w2, cv_w1, cv_w2, rel_table, banks):
    qa_f32, qa_log2, kcvc, kvsw, qb_f32, qb_log2, kb_f32, kbvb, gates = proj
    b, s, _ = qa_f32.shape
    hk, g, dh = NSA_KV_HEADS, NSA_GROUP, HEAD_DIM
    qa = _heads(qa_f32, NSA_HEADS).reshape(b, hk, g, s, dh)
    qa16_log2 = _heads(qa_log2, NSA_HEADS).reshape(b, hk, g, s, dh)
    (bank_l, far_l, t_l), (bank_w, t_w) = banks["dense"], banks["nsa_window"]

    nc = s // CMP_STRIDE
    kv = _heads(kcvc, 2 * hk).reshape(b, 2, hk, s, dh).transpose(1, 0, 2, 3, 4)
    cmp = nsa_compress(kv.reshape(2, b * hk, s, dh), jnp.stack([ck_w1, cv_w1]), jnp.stack([pos_k, pos_v]),
                       jnp.stack([ck_w2, cv_w2]))
    cmp = cmp.reshape(2, b, hk, nc, dh)
    o_c, sel = nsa_cmp_attention(qa, cmp[0], cmp[1], bias_cmp(rel_table, s, nc, tq=min(256, s)), tq=min(1024, s))
    ks, vs, kw, vw = (_heads(kvsw[..., j * hk * dh:(j + 1) * hk * dh], hk) for j in range(4))
    o_s = flash_attention([qa16_log2.reshape(b, 1, hk * g, s, dh)], [ks[:, None]], vs[:, None], bank_l, far_l,
                          sel=sel.reshape(b, 1, hk, sel.shape[-2], s), t=t_l, blk=SLC_BLOCK, tq=t_l)
    o_w = window_attention(qa16_log2, kw, vw, bank_w, t_w, NSA_WINDOW)

    hb = MOBA_HEADS
    hw = MOBA_HEADS * dh
    selb = moba_select(_heads(qb_f32, MOBA_HEADS), _heads(kb_f32, MOBA_HEADS), tq=min(2048, s))
    ngb = MOBA_HEADS // hb
    r5 = lambda a: a.reshape(b, ngb, hb, a.shape[-2], a.shape[-1])
    o_b = flash_attention([r5(_heads(qb_log2, MOBA_HEADS))], [r5(_heads(kbvb[..., :hw], MOBA_HEADS))],
                          r5(_heads(kbvb[..., hw:], MOBA_HEADS)), bank_l, far_l, sel=r5(selb), t=t_l, blk=MOBA_BLOCK,
                          ahead=1)

    return out_project([o_c, o_s, o_w], o_b, x, gate_m, w_out, gates=gates)


def odd_projection(x, g, sc, sh, w_in):
    off = _offsets(ODD_WIDTHS)
    half = MLA_ROPE // 2
    rope_lo = off[2][0]
    swapped = jnp.concatenate([w_in[:, rope_lo + half:rope_lo + MLA_ROPE], w_in[:, rope_lo:rope_lo + half]], axis=1)
    w = jnp.concatenate([w_in[:, :off[2][1]], swapped, w_in[:, off[3][0]:]], axis=1).astype(BF16)
    lat = MLA_Q_RANK + MLA_KV_RANK + 2 * MLA_ROPE
    qw, kvw = SWA_HEADS * HEAD_DIM, SWA_KV_HEADS * HEAD_DIM
    outs = [(0, lat, 1.0, F32),
            (lat, lat + qw, ATTN_SCALE * LOG2E, BF16),
            (lat + qw, lat + qw + 2 * kvw, 1.0, BF16)]
    return norm_mod_matmul(x, g, sc, sh, w, outs)


def odd_mixer(proj, x, gate_m, w_out, q_norm, kv_norm, w_q_up, w_kv_up, sinks, banks):
    latents, qd_log2, kdvd = proj
    b, s, _ = latents.shape
    lo = 0
    parts = []
    for wdt in (MLA_Q_RANK, MLA_KV_RANK, MLA_ROPE, MLA_ROPE):
        parts.append(latents[..., lo:lo + wdt])
        lo += wdt
    c_q, c_kv, k_rope, k_rope_sw = parts
    qn, qr, kn, v, kr = mla_up(c_q, c_kv, k_rope, k_rope_sw, q_norm, kv_norm, w_q_up, w_kv_up)
    hb = MLA_HEADS
    r5 = lambda a: a.reshape(b, 1, hb, s, a.shape[-1])
    bank_c, far_c, t_c = banks["causal"]
    lane_pad = jnp.zeros(qn.shape[:-1] + (-(MLA_NOPE + MLA_ROPE) % LANES,), qn.dtype)
    q_full = jnp.concatenate([qn, qr, lane_pad], axis=-1)
    k_full = jnp.concatenate([kn, kr, lane_pad], axis=-1)
    o_c = flash_attention([r5(q_full)], [r5(k_full)], r5(v), bank_c, far_c, t=t_c, ahead=3)

    hk, g, dh = SWA_KV_HEADS, SWA_GROUP, HEAD_DIM
    qd = _heads(qd_log2, SWA_HEADS).reshape(b, hk, g, s, dh)
    bank_s, t_s = banks["swa"]
    o_d = window_attention(qd, _heads(kdvd[..., :hk * dh], hk), _heads(kdvd[..., hk * dh:], hk), bank_s, t_s,
                           SWA_WINDOW, sinks=sinks)
    return out_project([o_c], o_d, x, gate_m, w_out)


def kernel(x, c, rel_table, router_w, router_b, final_norm, norm_mix, norm_ffn, ada_w, ada_b, moe_w_gate, moe_w_up, moe_w_down, ev_w_in, ev_w_out, nsa_pos_k, nsa_pos_v, nsa_ck_w1, nsa_ck_w2, nsa_cv_w1, nsa_cv_w2, od_w_in, od_w_out, mla_q_norm, mla_kv_norm, mla_w_q_up, mla_w_kv_up, swa_sinks):
    b, s, d = x.shape
    depth = ada_w.shape[0]
    mods = ada_all(c, ada_w, ada_b)
    t_dense = min(512, s)
    bank_l = bias_bank(rel_table, t_dense)
    far_l = rel_table[N_BUCKETS - 1] * LOG2E
    future = np.arange(t_dense)[:, None] > np.arange(t_dense)[None, :]
    bank_c = jnp.asarray(np.where(future, NEG, 0.0).astype(np.float32)[None, None])
    t_w, t_s = min(512, s), min(256, s)
    banks = {"dense": (bank_l, far_l, t_dense), "causal": (bank_c, jnp.zeros((N_BIAS_HEADS,), F32), t_dense),
             "nsa_window": (bias_bank(rel_table, t_w, window=NSA_WINDOW), t_w),
             "swa": (bias_bank(rel_table, t_s, window=SWA_WINDOW), t_s)}

    for layer in range(depth):
        shift_m, scale_m, gate_m, shift_f, scale_f, gate_f = jnp.split(mods[layer], 6, axis=-1)
        i = layer // 2
        if layer % 2 == 0:
            proj = even_projection(x, norm_mix[layer], scale_m, shift_m, ev_w_in[i])
            x = even_mixer(proj, x, gate_m, ev_w_out[i], nsa_pos_k[i], nsa_pos_v[i], nsa_ck_w1[i], nsa_ck_w2[i],
                           nsa_cv_w1[i], nsa_cv_w2[i], rel_table, banks)
        else:
            proj = odd_projection(x, norm_mix[layer], scale_m, shift_m, od_w_in[i])
            x = odd_mixer(proj, x, gate_m, od_w_out[i], mla_q_norm[i], mla_kv_norm[i], mla_w_q_up[i], mla_w_kv_up[i],
                          swa_sinks[i], banks)
        h, cw = ffn_pre(x, norm_ffn[layer], scale_f, shift_f, router_w, router_b)
        x = moe_dense(h, cw.transpose(0, 2, 1), x, gate_f, moe_w_gate, moe_w_up, moe_w_down, layer)
    return final_rmsnorm(x, final_norm)
```

```python
import functools
import math

import numpy as np
import jax
import jax.numpy as jnp
from jax import lax
from jax.experimental import pallas as pl
from jax.experimental.pallas import tpu as pltpu

F32 = jnp.float32
BF16 = jnp.bfloat16
HI = lax.Precision.HIGHEST

D_MODEL = 1024
HEAD_DIM = 64
NEG = -1e30
BIG = 1e30
M_INIT = -1e9
NORM_EPS = 1e-6
LOG2E = math.log2(math.e)

N_BUCKETS = 32
MAX_DISTANCE = 1024
N_BIAS_HEADS = 8

NSA_HEADS = 8
NSA_KV_HEADS = 2
NSA_GROUP = NSA_HEADS // NSA_KV_HEADS
CMP_LEN = 32
CMP_STRIDE = 16
CMP_HIDDEN = 256
SLC_BLOCK = 64
SLC_TOPN = 16
NSA_WINDOW = 512

MOBA_HEADS = 8
MOBA_BLOCK = 256
MOBA_TOPK = 3

MLA_HEADS = 4
MLA_Q_RANK = 256
MLA_KV_RANK = 128
MLA_NOPE = 128
MLA_ROPE = 64
MLA_V = 128
ROPE_THETA = 10000.0

SWA_HEADS = 8
SWA_KV_HEADS = 2
SWA_GROUP = SWA_HEADS // SWA_KV_HEADS
SWA_WINDOW = 128

N_EXPERTS = 16
N_GROUPS = 4
EXPERTS_PER_GROUP = N_EXPERTS // N_GROUPS
D_EXPERT = 512

EVEN_WIDTHS = (NSA_HEADS * HEAD_DIM,) + (NSA_KV_HEADS * HEAD_DIM,) * 6 + (3 * NSA_HEADS,) + (MOBA_HEADS * HEAD_DIM,) * 3
ODD_WIDTHS = (MLA_Q_RANK, MLA_KV_RANK, MLA_ROPE, SWA_HEADS * HEAD_DIM, SWA_KV_HEADS * HEAD_DIM, SWA_KV_HEADS * HEAD_DIM)

LANES = 128
VMEM_LIMIT = 56 * 1024 * 1024


def _params(*sem):
    return pltpu.CompilerParams(dimension_semantics=sem, vmem_limit_bytes=VMEM_LIMIT)


def _dot(a, b, precision=None):
    return lax.dot_general(a, b, (((1,), (0,)), ((), ())), precision=precision, preferred_element_type=F32)


def _dot_nt(a, b, precision=None):
    return lax.dot_general(a, b, (((1,), (1,)), ((), ())), precision=precision, preferred_element_type=F32)


def _iota(shape, dim):
    return lax.broadcasted_iota(jnp.int32, shape, dim)


def _offsets(widths):
    out, acc = [], 0
    for w in widths:
        out.append((acc, acc + w))
        acc += w
    return out


def _bucket_thresholds():
    d = np.arange(0, 4 * MAX_DISTANCE, dtype=np.int64)
    exact = N_BUCKETS // 2
    x = np.maximum(d, 1).astype(np.float32) / np.float32(exact)
    logp = exact + (np.log(x) / np.float32(math.log(MAX_DISTANCE / exact)) * np.float32(N_BUCKETS - exact)).astype(np.int32)
    bucket = np.where(d < exact, d, np.minimum(logp, N_BUCKETS - 1))
    return [int(np.argmax(bucket >= b)) for b in range(1, N_BUCKETS)]


BUCKET_THR = _bucket_thresholds()
FAR_DIST = BUCKET_THR[-1]


def _ada_kernel(c_ref, w_ref, b_ref, o_ref):
    c = c_ref[...]
    o_ref[...] = _dot(c * jax.nn.sigmoid(c), w_ref[...], precision=HI) + b_ref[...]


def ada_all(c, ada_w, ada_b):
    depth, d, n = ada_w.shape
    rows = 8
    cp = jnp.pad(c, ((0, rows - c.shape[0]), (0, 0)))
    tn = 1536
    out = pl.pallas_call(
        _ada_kernel,
        grid=(depth, n // tn),
        in_specs=[pl.BlockSpec((rows, d), lambda l, j: (0, 0)),
                  pl.BlockSpec((None, d, tn), lambda l, j: (l, 0, j)),
                  pl.BlockSpec((None, 1, tn), lambda l, j: (l, 0, j))],
        out_specs=pl.BlockSpec((None, rows, tn), lambda l, j: (l, 0, j)),
        out_shape=jax.ShapeDtypeStruct((depth, rows, n), F32),
        compiler_params=_params("parallel", "parallel"),
        name="ada",
    )(cp, ada_w, ada_b.reshape(depth, 1, n))
    return out[:, :c.shape[0], :]


def _norm_mod(x, g, sc, sh):
    y = x * lax.rsqrt(jnp.mean(x * x, axis=-1, keepdims=True) + NORM_EPS) * g
    return y * (1.0 + sc) + sh


def _nmm_kernel(x_ref, g_ref, sc_ref, sh_ref, w_ref, *o_refs, outs):
    h = _norm_mod(x_ref[...], g_ref[...], sc_ref[...], sh_ref[...])
    res = _dot(h.astype(BF16), w_ref[...])
    for o_ref, (lo, hi, scale, _) in zip(o_refs, outs):
        part = res[:, lo:hi]
        o_ref[...] = (part if scale == 1.0 else part * scale).astype(o_ref.dtype)


def norm_mod_matmul(x, g, sc, sh, w, outs, tm=512):
    b, s, d = x.shape
    n = w.shape[1]
    tm = min(tm, s)
    return pl.pallas_call(
        functools.partial(_nmm_kernel, outs=outs),
        grid=(b, s // tm),
        in_specs=[pl.BlockSpec((None, tm, d), lambda bi, i: (bi, i, 0)),
                  pl.BlockSpec((1, d), lambda bi, i: (0, 0)),
                  pl.BlockSpec((None, 1, d), lambda bi, i: (bi, 0, 0)),
                  pl.BlockSpec((None, 1, d), lambda bi, i: (bi, 0, 0)),
                  pl.BlockSpec((d, n), lambda bi, i: (0, 0))],
        out_specs=[pl.BlockSpec((None, tm, hi - lo), lambda bi, i: (bi, i, 0)) for lo, hi, _, _ in outs],
        out_shape=[jax.ShapeDtypeStruct((b, s, hi - lo), dt) for lo, hi, _, dt in outs],
        compiler_params=_params("parallel", "parallel"),
        name="norm_mod_matmul",
    )(x, g.reshape(1, d), sc.reshape(b, 1, d), sh.reshape(b, 1, d), w)


def _bias_kernel(tab_ref, o_ref, *, rows, cols, step, cstride, c0, key_major=False, window=None):
    sub = 8
    off = pl.program_id(0) * step - c0
    unit = LOG2E if key_major else 1.0
    if key_major:
        base = _iota((sub, cols), 1) - _iota((sub, cols), 0)
    else:
        base = _iota((sub, cols), 0) - cstride * _iota((sub, cols), 1)

    def body(r, carry):
        dist = off + base + (-r * sub if key_major else r * sub)
        vs = [jnp.full((sub, cols), tab_ref[0, h] * unit, F32) for h in range(N_BIAS_HEADS)]
        for b in range(1, N_BUCKETS):
            ge = dist >= BUCKET_THR[b - 1]
            for h in range(N_BIAS_HEADS):
                vs[h] = jnp.where(ge, tab_ref[b, h] * unit, vs[h])
        masked = dist < 0 if window is None else (dist < 0) | (dist >= window)
        for h in range(N_BIAS_HEADS):
            v = jnp.where(masked, NEG, vs[h]) if key_major else vs[h]
            o_ref[h, pl.ds(pl.multiple_of(r * sub, sub), sub), :] = v
        return carry

    lax.fori_loop(0, rows // sub, body, 0)


def bias_bank(rel_table, t, window=None):
    nd = -(-(FAR_DIST + t - 1) // t) if window is None else -(-window // t) + 1
    kern = functools.partial(_bias_kernel, rows=t, cols=t, step=t, cstride=1, c0=0, key_major=True, window=window)
    bank = pl.pallas_call(
        kern,
        grid=(nd,),
        in_specs=[pl.BlockSpec(memory_space=pltpu.SMEM)],
        out_specs=pl.BlockSpec((N_BIAS_HEADS, None, t, t), lambda i: (0, i, 0, 0)),
        out_shape=jax.ShapeDtypeStruct((N_BIAS_HEADS, nd, t, t), F32),
        compiler_params=_params("parallel"),
        name="bias_bank",
    )(rel_table)
    return bank


def _bias_cmp_kernel(tab_ref, o_ref, *, rows, cols):
    sub = 8
    width = min(LANES, cols)
    t0 = pl.program_id(0) * rows
    base = _iota((sub, width), 0) - CMP_STRIDE * _iota((sub, width), 1)

    def body(r, carry):
        rs = pl.ds(pl.multiple_of(r * sub, sub), sub)
        for c in range(cols // width):
            cs = slice(c * width, (c + 1) * width)
            off = t0 + r * sub - (CMP_LEN - 1) - CMP_STRIDE * width * c
            lo = off - CMP_STRIDE * (width - 1)
            hi = off + sub - 1

            @pl.when((hi >= 0) & (lo < FAR_DIST))
            def _():
                dist = off + base
                vs = [jnp.full((sub, width), tab_ref[0, h], F32) for h in range(N_BIAS_HEADS)]
                for b in range(1, N_BUCKETS):
                    ge = dist >= BUCKET_THR[b - 1]
                    for h in range(N_BIAS_HEADS):
                        vs[h] = jnp.where(ge, tab_ref[b, h], vs[h])
                for h in range(N_BIAS_HEADS):
                    o_ref[h, rs, cs] = vs[h]

            @pl.when(lo >= FAR_DIST)
            def _():
                for h in range(N_BIAS_HEADS):
                    o_ref[h, rs, cs] = jnp.full((sub, width), tab_ref[N_BUCKETS - 1, h], F32)

            @pl.when(hi < 0)
            def _():
                for h in range(N_BIAS_HEADS):
                    o_ref[h, rs, cs] = jnp.full((sub, width), tab_ref[0, h], F32)
        return carry

    lax.fori_loop(0, rows // sub, body, 0)


def bias_cmp(rel_table, s, nc, tq=256):
    kern = functools.partial(_bias_cmp_kernel, rows=tq, cols=nc)
    return pl.pallas_call(
        kern,
        grid=(s // tq,),
        in_specs=[pl.BlockSpec(memory_space=pltpu.SMEM)],
        out_specs=pl.BlockSpec((N_BIAS_HEADS, tq, nc), lambda i: (0, i, 0)),
        out_shape=jax.ShapeDtypeStruct((N_BIAS_HEADS, s, nc), F32),
        compiler_params=_params("parallel"),
        name="bias_cmp",
    )(rel_table)


def _compress_kernel(x_ref, w1_ref, pos_ref, w2_ref, o_ref):
    x = x_ref[...]
    w1 = w1_ref[...]
    half = w1.shape[0]
    hid = w1.shape[1] // 2
    r = _dot(x, w1, precision=HI)
    pos = pos_ref[...]
    pb = _dot(pos[:, :half], w1[:, :hid], precision=HI) + _dot(pos[:, half:], w1[:, hid:], precision=HI)
    nxt = pltpu.roll(r[:, hid:], x.shape[0] - 1, 0)
    pre = r[:, :hid] + nxt + pb[0:1, :]
    o_ref[...] = _dot(jax.nn.gelu(pre), w2_ref[...], precision=HI)


def nsa_compress(kv, w1, pos, w2):
    two, bh, s, dh = kv.shape
    nc = s // CMP_STRIDE
    half = CMP_STRIDE * dh
    hid = w1.shape[2]
    x = kv.reshape(two, bh, nc, half)
    w1cat = jnp.concatenate([w1[:, :half, :], w1[:, half:, :]], axis=2)
    posf = jnp.broadcast_to(pos.reshape(two, 1, CMP_LEN * dh), (two, 8, CMP_LEN * dh))
    return pl.pallas_call(
        _compress_kernel,
        grid=(two, bh),
        in_specs=[pl.BlockSpec((None, None, nc, half), lambda w, i: (w, i, 0, 0)),
                  pl.BlockSpec((None, half, 2 * hid), lambda w, i: (w, 0, 0)),
                  pl.BlockSpec((None, 8, CMP_LEN * dh), lambda w, i: (w, 0, 0)),
                  pl.BlockSpec((None, hid, dh), lambda w, i: (w, 0, 0))],
        out_specs=pl.BlockSpec((None, None, nc, dh), lambda w, i: (w, i, 0, 0)),
        out_shape=jax.ShapeDtypeStruct((two, bh, nc, dh), F32),
        compiler_params=_params("parallel", "parallel"),
        name="nsa_compress",
    )(x, w1cat, posf, w2)


def _topn_mask(score, index, n):
    idx_f = index.astype(F32)
    sel = jnp.zeros(score.shape, jnp.bool_)
    for _ in range(n):
        m = jnp.max(score, axis=0, keepdims=True)
        first = jnp.min(jnp.where(score == m, idx_f, float(score.shape[0])), axis=0, keepdims=True)
        pick = idx_f == first
        sel = sel | pick
        score = jnp.where(pick, -jnp.inf, score)
    return sel


def _cmp_kernel(q_ref, k_ref, v_ref, b_ref, ov_ref, oc_ref, sel_ref, *, tq, nc, nslc, ntop, group):
    qi = pl.program_id(2)
    t = qi * tq + _iota((tq, 1), 0)
    n = _iota((1, nc), 1)
    mask = (n * CMP_STRIDE + (CMP_LEN - 1) <= t) & (n < nc - 1)
    k = k_ref[...]
    v = v_ref[...].astype(BF16)
    dh = k.shape[-1]
    psum = jnp.zeros((tq, nc), F32)
    scores = lambda g: _dot_nt(q_ref[g], k, precision=HI) + b_ref[g]
    s_next = scores(0)
    for g in range(group):
        s, s_next = s_next, (scores(g + 1) if g + 1 < group else None)
        s = jnp.where(mask, s, NEG)
        m = jnp.max(s, axis=-1, keepdims=True)
        p = jnp.where(mask, jnp.exp(s - m), 0.0)
        den = jnp.sum(p, axis=-1, keepdims=True)
        p = p * jnp.where(den > 0.0, 1.0 / den, 0.0)
        oc_ref[:, g * dh:(g + 1) * dh] = _dot(p.astype(BF16), v)
        psum = psum + p
    imp = _dot_nt(ov_ref[...], psum, precision=HI)
    j = _iota((nslc, tq), 0)
    blk = jnp.right_shift(qi * tq + _iota((1, tq), 1), int(math.log2(SLC_BLOCK)))
    forced = (j == 0) | (j == blk) | (j == blk - 1)
    valid = j <= blk
    score = jnp.where(forced, BIG, jnp.where(valid, imp, NEG))
    sel = _topn_mask(score, j, ntop) & valid
    sel_ref[0] = jnp.where(sel, 0.0, NEG)


def nsa_cmp_attention(q, k_cmp, v_cmp, bias_c, tq=256):
    b, hk, g, s, dh = q.shape
    nc = k_cmp.shape[2]
    nslc = s // SLC_BLOCK
    ntop = min(SLC_TOPN, nslc)
    cmp_start = np.arange(nc, dtype=np.int64) * CMP_STRIDE
    slc_lo = np.arange(nslc, dtype=np.int64) * SLC_BLOCK
    overlap = ((cmp_start[:, None] <= slc_lo[None, :] + SLC_BLOCK - 1)
               & (cmp_start[:, None] + CMP_LEN - 1 >= slc_lo[None, :])).astype(np.float32)
    kern = functools.partial(_cmp_kernel, tq=tq, nc=nc, nslc=nslc, ntop=ntop, group=g)
    return pl.pallas_call(
        kern,
        grid=(b, hk, s // tq),
        in_specs=[pl.BlockSpec((None, None, g, tq, dh), lambda bi, h, i: (bi, h, 0, i, 0)),
                  pl.BlockSpec((None, None, nc, dh), lambda bi, h, i: (bi, h, 0, 0)),
                  pl.BlockSpec((None, None, nc, dh), lambda bi, h, i: (bi, h, 0, 0)),
                  pl.BlockSpec((g, tq, nc), lambda bi, h, i: (h, i, 0)),
                  pl.BlockSpec((nslc, nc), lambda bi, h, i: (0, 0))],
        out_specs=[pl.BlockSpec((None, tq, g * dh), lambda bi, h, i: (bi, i, h)),
                   pl.BlockSpec((None, None, 1, nslc, tq), lambda bi, h, i: (bi, h, 0, 0, i))],
        out_shape=[jax.ShapeDtypeStruct((b, s, hk * g * dh), F32),
                   jax.ShapeDtypeStruct((b, hk, 1, nslc, s), F32)],
        compiler_params=_params("parallel", "parallel", "parallel"),
        name="nsa_cmp_attention",
    )(q, k_cmp, v_cmp, bias_c, jnp.asarray(overlap.T))


def _kmean_kernel(k_ref, o_ref, *, nb, blk):
    k = k_ref[...]
    o_ref[...] = jnp.sum(k.reshape(nb, blk, k.shape[-1]), axis=1) * (1.0 / blk)


def _moba_sel_kernel(q_ref, km_ref, sel_ref, *, tq, nb, ntop):
    qi = pl.program_id(2)
    own = jnp.right_shift(qi * tq + _iota((1, tq), 1), int(math.log2(MOBA_BLOCK)))
    j = _iota((nb, tq), 0)
    gate = _dot_nt(km_ref[...], q_ref[...], precision=HI)
    past = j < own
    sel = _topn_mask(jnp.where(past, gate, NEG), j, ntop) & past
    sel_ref[...] = jnp.where(sel | (j == own), 0.0, NEG)


def moba_select(q, k, tq=256):
    b, h, s, dh = q.shape
    nb = s // MOBA_BLOCK
    ntop = min(MOBA_TOPK, nb)
    kmean = pl.pallas_call(
        functools.partial(_kmean_kernel, nb=nb, blk=MOBA_BLOCK),
        grid=(b, h),
        in_specs=[pl.BlockSpec((None, None, s, dh), lambda bi, hi: (bi, hi, 0, 0))],
        out_specs=pl.BlockSpec((None, None, nb, dh), lambda bi, hi: (bi, hi, 0, 0)),
        out_shape=jax.ShapeDtypeStruct((b, h, nb, dh), F32),
        compiler_params=_params("parallel", "parallel"),
        name="moba_kmean",
    )(k)
    return pl.pallas_call(
        functools.partial(_moba_sel_kernel, tq=tq, nb=nb, ntop=ntop),
        grid=(b, h, s // tq),
        in_specs=[pl.BlockSpec((None, None, tq, dh), lambda bi, hi, i: (bi, hi, i, 0)),
                  pl.BlockSpec((None, None, nb, dh), lambda bi, hi, i: (bi, hi, 0, 0))],
        out_specs=pl.BlockSpec((None, None, nb, tq), lambda bi, hi, i: (bi, hi, 0, i)),
        out_shape=jax.ShapeDtypeStruct((b, h, nb, s), F32),
        compiler_params=_params("parallel", "parallel", "parallel"),
        name="moba_select",
    )(q, kmean)


AUG = 64


def _flash_kernel(qt_ref, kt_ref, *refs, hb, kb, sb, bh, t, tq, kc, nparts, use_sel, nbt, nd, ahead):
    pos = 0
    q_refs = refs[pos:pos + nparts]; pos += nparts
    k_refs = refs[pos:pos + nparts]; pos += nparts
    vt_ref = refs[pos]; pos += 1
    sel_ref = None
    if use_sel:
        sel_ref = refs[pos]; pos += 1
    bias_ref, far_ref, o_ref, m_ref, l_ref, acc_ref = refs[pos:pos + 6]

    head0 = pl.program_id(1) * hb
    step = pl.program_id(2)
    qi = qt_ref[step]
    ki = kt_ref[step]

    @pl.when(ki == 0)
    def _():
        m_ref[...] = jnp.full(m_ref.shape, M_INIT, F32)
        l_ref[...] = jnp.zeros(l_ref.shape, F32)
        acc_ref[...] = jnp.zeros(acc_ref.shape, F32)

    nchunk = t // kc
    nlane = t // tq

    def mask_lanes(si, qc):
        rows = sel_ref[si, qc * tq:(qc + 1) * tq, :]
        shift = lax.rem(AUG + LANES - lax.rem(ki * nbt, LANES), LANES)
        lane = _iota((tq, LANES), 1)
        keep = (lane >= AUG) & (lane < AUG + nbt)
        return jnp.where(keep, pltpu.roll(rows, shift, 1), 0.0).astype(BF16)

    def scores(g, qc, near, masks):
        kg = g * kb // hb
        bg = g if bh > 1 else 0
        ql = slice(qc * tq, (qc + 1) * tq)
        qs = [q_ref[g, ql, :] for q_ref in q_refs]
        if use_sel:
            qs[0] = jnp.where(_iota((tq, LANES), 1) < AUG, qs[0], masks[(g * sb // hb, qc)])
        chunks, top = [], None
        for c in range(nchunk):
            kr = slice(c * kc, (c + 1) * kc)
            s = _dot_nt(k_refs[0][kg, kr, :], qs[0])
            for p in range(1, nparts):
                s = s + _dot_nt(k_refs[p][kg, kr, :], qs[p])
            if near:
                s = s + bias_ref[bg, kr, ql]
            chunks.append(s)
            top = s if top is None else jnp.maximum(top, s)
        return chunks, top

    def update(g, qc, near, chunks, top):
        kg = g * kb // hb
        ql = slice(qc * tq, (qc + 1) * tq)
        shift = 0.0 if near else far_ref[head0 + g]
        m = m_ref[g, :, ql]
        m_new = jnp.maximum(m, jnp.max(top, axis=0, keepdims=True) + shift)
        alpha = jnp.exp2(m - m_new)
        base = m_new - shift
        pv = psum = None
        for c in range(nchunk):
            p = jnp.exp2(chunks[c] - base)
            ps = jnp.sum(p, axis=0, keepdims=True)
            pd = _dot(vt_ref[kg, :, c * kc:(c + 1) * kc], p.astype(BF16))
            pv = pd if pv is None else pv + pd
            psum = ps if psum is None else psum + ps
        m_ref[g, :, ql] = m_new
        l_ref[g, :, ql] = alpha * l_ref[g, :, ql] + psum
        acc_ref[g, :, ql] = alpha * acc_ref[g, :, ql] + pv

    def body(near):
        masks = {}
        if use_sel:
            masks = {(si, qc): mask_lanes(si, qc) for si in range(sb) for qc in range(nlane)}
        items = [(g, qc) for g in range(hb) for qc in range(nlane)]
        pending = [scores(*item, near, masks) for item in items[:ahead]]
        for i, item in enumerate(items):
            if i + ahead < len(items):
                pending.append(scores(*items[i + ahead], near, masks))
            update(*item, near, *pending.pop(0))

    @pl.when(qi - ki < nd)
    def _():
        body(True)

    @pl.when(qi - ki >= nd)
    def _():
        body(False)

    @pl.when(ki == qi)
    def _():
        for g in range(hb):
            dv = acc_ref.shape[1]
            o_ref[:, g * dv:(g + 1) * dv] = jnp.transpose(acc_ref[g] / l_ref[g]).astype(o_ref.dtype)


def flash_attention(q_parts, k_parts, v, bias, far, sel=None, t=512, blk=1, tq=256, kc=512, ahead=3):
    b, ng, hb, s, _ = q_parts[0].shape
    kb = v.shape[2]
    dv = v.shape[-1]
    t = min(t, s)
    tq, kc = min(tq, t), min(kc, t)
    nq = s // t
    nd = bias.shape[1]
    pairs = [(i, j) for i in range(nq) for j in range(i + 1)]
    qt = jnp.asarray(np.array([p[0] for p in pairs], np.int32))
    kt = jnp.asarray(np.array([p[1] for p in pairs], np.int32))
    nparts = len(q_parts)
    use_sel = sel is not None
    sb = sel.shape[2] if use_sel else 1
    bh = hb if bias.shape[0] > 1 else 1
    nbt = t // blk
    vt = jnp.swapaxes(v, -1, -2)
    q_parts, k_parts = list(q_parts), list(k_parts)
    if use_sel:
        onehot = (np.arange(s)[:, None] % t // blk == np.arange(LANES - AUG)[None, :]).astype(np.float32)
        k0 = k_parts[0]
        k_parts[0] = jnp.concatenate(
            [k0, jnp.broadcast_to(jnp.asarray(onehot, k0.dtype), k0.shape[:-1] + (LANES - AUG,))], axis=-1)
        q_parts[0] = jnp.pad(q_parts[0], ((0, 0),) * 4 + ((0, LANES - AUG),))
        selq = jnp.swapaxes(sel, -1, -2)
        selq = jnp.pad(selq, ((0, 0),) * 4 + ((0, LANES - selq.shape[-1]),))

    in_specs, args = [], []
    for qp in q_parts:
        in_specs.append(pl.BlockSpec((None, None, hb, t, qp.shape[-1]), lambda bi, n, st, qt, kt: (bi, n, 0, qt[st], 0)))
        args.append(qp)
    for kp in k_parts:
        in_specs.append(pl.BlockSpec((None, None, kb, t, kp.shape[-1]), lambda bi, n, st, qt, kt: (bi, n, 0, kt[st], 0)))
        args.append(kp)
    in_specs.append(pl.BlockSpec((None, None, kb, dv, t), lambda bi, n, st, qt, kt: (bi, n, 0, 0, kt[st])))
    args.append(vt)
    if use_sel:
        in_specs.append(pl.BlockSpec((None, None, sb, t, LANES), lambda bi, n, st, qt, kt: (bi, n, 0, qt[st], 0)))
        args.append(selq)
    near_tile = lambda qt, kt, st: jnp.minimum(qt[st] - kt[st], nd - 1)
    if bh > 1:
        in_specs.append(pl.BlockSpec((bh, None, t, t), lambda bi, n, st, qt, kt: (n, near_tile(qt, kt, st), 0, 0)))
    else:
        in_specs.append(pl.BlockSpec((1, None, t, t), lambda bi, n, st, qt, kt: (0, near_tile(qt, kt, st), 0, 0)))
    in_specs.append(pl.BlockSpec(memory_space=pltpu.SMEM))
    args += [bias, far]
    kern = functools.partial(_flash_kernel, hb=hb, kb=kb, sb=sb, bh=bh, t=t, tq=tq, kc=kc, nparts=nparts,
                             use_sel=use_sel, nbt=nbt, nd=nd, ahead=ahead)
    grid_spec = pltpu.PrefetchScalarGridSpec(
        num_scalar_prefetch=2,
        grid=(b, ng, len(pairs)),
        in_specs=in_specs,
        out_specs=pl.BlockSpec((None, t, hb * dv), lambda bi, n, st, qt, kt: (bi, qt[st], n)),
        scratch_shapes=[pltpu.VMEM((hb, 1, t), F32), pltpu.VMEM((hb, 1, t), F32), pltpu.VMEM((hb, dv, t), F32)])
    return pl.pallas_call(
        kern,
        grid_spec=grid_spec,
        out_shape=jax.ShapeDtypeStruct((b, s, ng * hb * dv), F32),
        compiler_params=_params("parallel", "parallel", "arbitrary"),
        name="flash_attention",
    )(qt, kt, *args)


def _window_kernel(*refs, group, t, ntile, has_sink):
    q_ref = refs[0]
    k_refs = refs[1:1 + ntile]
    vt_refs = refs[1 + ntile:1 + 2 * ntile]
    bias_ref = refs[1 + 2 * ntile]
    sink_ref = refs[2 + 2 * ntile] if has_sink else None
    o_ref = refs[-1]
    kvh = pl.program_id(1)

    def scores(g):
        q = q_ref[g]
        ss = [_dot_nt(k_refs[r][...], q) + bias_ref[g, ntile - 1 - r] for r in range(ntile)]
        top = ss[0]
        for s in ss[1:]:
            top = jnp.maximum(top, s)
        return ss, top

    def finish(g, ss, top):
        m = jnp.max(top, axis=0, keepdims=True)
        den = None
        if has_sink:
            sink = sink_ref[kvh * group + g] * LOG2E
            m = jnp.maximum(m, sink)
            den = jnp.exp2(sink - m)
        acc = None
        for r in range(ntile):
            p = jnp.exp2(ss[r] - m)
            ps = jnp.sum(p, axis=0, keepdims=True)
            pv = _dot(vt_refs[r][...], p.astype(BF16))
            den = ps if den is None else den + ps
            acc = pv if acc is None else acc + pv
        dh = acc.shape[0]
        o_ref[:, g * dh:(g + 1) * dh] = jnp.transpose(acc / den)

    pending = scores(0)
    for g in range(group):
        nxt = scores(g + 1) if g + 1 < group else None
        finish(g, *pending)
        pending = nxt


def window_attention(q, k, v, bank, t, window, sinks=None):
    b, hk, g, s, dh = q.shape
    ntile = -(-window // t) + 1
    front = (ntile - 1) * t
    kp = jnp.pad(k, ((0, 0), (0, 0), (front, 0), (0, 0)))
    vtp = jnp.pad(jnp.swapaxes(v, -1, -2), ((0, 0), (0, 0), (0, 0), (front, 0)))
    has_sink = sinks is not None
    in_specs = [pl.BlockSpec((None, None, g, t, dh), lambda bi, h, i: (bi, h, 0, i, 0))]
    args = [q]
    for r in range(ntile):
        in_specs.append(pl.BlockSpec((None, None, t, dh), lambda bi, h, i, r=r: (bi, h, i + r, 0)))
        args.append(kp)
    for r in range(ntile):
        in_specs.append(pl.BlockSpec((None, None, dh, t), lambda bi, h, i, r=r: (bi, h, 0, i + r)))
        args.append(vtp)
    in_specs.append(pl.BlockSpec((g, ntile, t, t), lambda bi, h, i: (h, 0, 0, 0)))
    args.append(bank)
    if has_sink:
        in_specs.append(pl.BlockSpec(memory_space=pltpu.SMEM))
        args.append(sinks)
    kern = functools.partial(_window_kernel, group=g, t=t, ntile=ntile, has_sink=has_sink)
    return pl.pallas_call(
        kern,
        grid=(b, hk, s // t),
        in_specs=in_specs,
        out_specs=pl.BlockSpec((None, t, g * dh), lambda bi, h, i: (bi, i, h)),
        out_shape=jax.ShapeDtypeStruct((b, s, hk * g * dh), F32),
        compiler_params=_params("parallel", "parallel", "parallel"),
        name="window_attention",
    )(*args)


def _rms(x, g):
    return x * lax.rsqrt(jnp.mean(x * x, axis=-1, keepdims=True) + NORM_EPS) * g


def _mla_up_kernel(cq_ref, ckv_ref, kr_ref, krs_ref, qn_ref, kvn_ref, wq_ref, wkv_ref, cs_ref, sn_ref,
                   qnope_ref, qrope_ref, knope_ref, v_ref, krope_ref, *, scale):
    cs = cs_ref[...]
    sn = sn_ref[...]
    q = _dot(_rms(cq_ref[...], qn_ref[...]).astype(BF16), wq_ref[...]) * scale
    kv = _dot(_rms(ckv_ref[...], kvn_ref[...]).astype(BF16), wkv_ref[...])
    wq_head = MLA_NOPE + 2 * MLA_ROPE
    for h in range(MLA_HEADS):
        base = h * wq_head
        qnope_ref[h] = q[:, base:base + MLA_NOPE].astype(qnope_ref.dtype)
        x = q[:, base + MLA_NOPE:base + MLA_NOPE + MLA_ROPE]
        xs = q[:, base + MLA_NOPE + MLA_ROPE:base + wq_head]
        qrope_ref[h] = (x * cs + xs * sn).astype(qrope_ref.dtype)
        kb = h * (MLA_NOPE + MLA_V)
        knope_ref[h] = kv[:, kb:kb + MLA_NOPE].astype(knope_ref.dtype)
        v_ref[h] = kv[:, kb + MLA_NOPE:kb + MLA_NOPE + MLA_V].astype(v_ref.dtype)
    kr = (kr_ref[...] * cs + krs_ref[...] * sn).astype(krope_ref.dtype)
    for h in range(MLA_HEADS):
        krope_ref[h] = kr


def mla_up(c_q, c_kv, k_rope, k_rope_sw, q_norm, kv_norm, w_q_up, w_kv_up, tm=512):
    b, s, _ = c_q.shape
    half = MLA_ROPE // 2
    inv = ROPE_THETA ** (-np.arange(0, MLA_ROPE, 2, dtype=np.float32) / np.float32(MLA_ROPE))
    ang = np.arange(s, dtype=np.float32)[:, None] * inv[None, :].astype(np.float32)
    cos, sin = np.cos(ang).astype(np.float32), np.sin(ang).astype(np.float32)
    cs = jnp.asarray(np.concatenate([cos, cos], axis=1))
    sn = jnp.asarray(np.concatenate([-sin, sin], axis=1))
    dq = MLA_NOPE + MLA_ROPE
    wq = w_q_up.reshape(MLA_Q_RANK, MLA_HEADS, dq)
    rope_cols = wq[:, :, MLA_NOPE:]
    swapped = jnp.concatenate([rope_cols[:, :, half:], rope_cols[:, :, :half]], axis=2)
    wq_aug = jnp.concatenate([wq, swapped], axis=2).reshape(MLA_Q_RANK, MLA_HEADS * (dq + MLA_ROPE)).astype(BF16)
    wkv = w_kv_up.astype(BF16)
    scale = (MLA_NOPE + MLA_ROPE) ** -0.5 * LOG2E
    tm = min(tm, s)
    row = lambda w: pl.BlockSpec((None, tm, w), lambda bi, i: (bi, i, 0))
    full = lambda a: pl.BlockSpec(a.shape, lambda bi, i: (0,) * a.ndim)
    head = lambda w: pl.BlockSpec((None, MLA_HEADS, tm, w), lambda bi, i: (bi, 0, i, 0))
    qn2, kvn2 = q_norm.reshape(1, -1), kv_norm.reshape(1, -1)
    outs = pl.pallas_call(
        functools.partial(_mla_up_kernel, scale=scale),
        grid=(b, s // tm),
        in_specs=[row(MLA_Q_RANK), row(MLA_KV_RANK), row(MLA_ROPE), row(MLA_ROPE), full(qn2), full(kvn2),
                  full(wq_aug), full(wkv),
                  pl.BlockSpec((tm, MLA_ROPE), lambda bi, i: (i, 0)), pl.BlockSpec((tm, MLA_ROPE), lambda bi, i: (i, 0))],
        out_specs=[head(MLA_NOPE), head(MLA_ROPE), head(MLA_NOPE), head(MLA_V), head(MLA_ROPE)],
        out_shape=[jax.ShapeDtypeStruct((b, MLA_HEADS, s, w), BF16)
                   for w in (MLA_NOPE, MLA_ROPE, MLA_NOPE, MLA_V, MLA_ROPE)],
        compiler_params=_params("parallel", "parallel"),
        name="mla_up",
    )(c_q, c_kv, k_rope, k_rope_sw, qn2, kvn2, wq_aug, wkv, cs, sn)
    return outs


def _out_kernel(*refs, gated):
    if gated:
        oc_ref, os_ref, ow_ref, gt_ref, ex_ref, ob_ref, x_ref, gm_ref, w_ref, o_ref = refs
        half = oc_ref.shape[-1]
        ge = _dot(jax.nn.sigmoid(gt_ref[...]), ex_ref[...], precision=HI)
        oa = ge[:, :half] * oc_ref[...] + ge[:, half:2 * half] * os_ref[...] + ge[:, 2 * half:] * ow_ref[...]
    else:
        oa_ref, ob_ref, x_ref, gm_ref, w_ref, o_ref = refs
        half = oa_ref.shape[-1]
        oa = oa_ref[...]
    mix = _dot(oa.astype(BF16), w_ref[:half, :]) + _dot(ob_ref[...].astype(BF16), w_ref[half:, :])
    o_ref[...] = x_ref[...] + gm_ref[...] * mix


def out_project(parts, ob, x, gate_m, w_out, gates=None, tm=512):
    b, s, d = x.shape
    gated = gates is not None
    tm = min(tm, s)
    row = lambda a: pl.BlockSpec((None, tm, a.shape[-1]), lambda bi, i: (bi, i, 0))
    full = lambda a: pl.BlockSpec(a.shape, lambda bi, i: (0,) * a.ndim)
    args, in_specs = [], []
    for p in parts:
        args.append(p); in_specs.append(row(p))
    if gated:
        half = parts[0].shape[-1]
        nh = half // HEAD_DIM
        ex = np.zeros((LANES, 3 * half), np.float32)
        for h in range(nh):
            for br in range(3):
                ex[h * 3 + br, br * half + h * HEAD_DIM: br * half + (h + 1) * HEAD_DIM] = 1.0
        gpad = jnp.pad(gates, ((0, 0), (0, 0), (0, LANES - gates.shape[-1])))
        ex = jnp.asarray(ex)
        args += [gpad, ex]; in_specs += [row(gpad), full(ex)]
    gm = gate_m.reshape(b, 1, d)
    wb = w_out.astype(BF16)
    args += [ob, x, gm, wb]
    in_specs += [row(ob), row(x), pl.BlockSpec((None, 1, d), lambda bi, i: (bi, 0, 0)), full(wb)]
    return pl.pallas_call(
        functools.partial(_out_kernel, gated=gated),
        grid=(b, s // tm),
        in_specs=in_specs,
        out_specs=pl.BlockSpec((None, tm, d), lambda bi, i: (bi, i, 0)),
        out_shape=jax.ShapeDtypeStruct((b, s, d), F32),
        compiler_params=_params("parallel", "parallel"),
        name="out_project",
    )(*args)


def _ffn_pre_kernel(x_ref, g_ref, sc_ref, sh_ref, rw_ref, rb_ref, h_ref, cw_ref):
    h = _norm_mod(x_ref[...], g_ref[...], sc_ref[...], sh_ref[...])
    h_ref[...] = h.astype(h_ref.dtype)
    aff = jax.nn.sigmoid(_dot_nt(rw_ref[...], h, precision=HI))
    biased = aff + rb_ref[...]
    epg = EXPERTS_PER_GROUP
    brow = [biased[e:e + 1, :] for e in range(N_EXPERTS)]
    arow = [aff[e:e + 1, :] for e in range(N_EXPERTS)]
    best = gsel = None
    for gi in range(N_GROUPS):
        a, b_, c, d_ = brow[gi * epg:(gi + 1) * epg]
        hi1, lo1, hi2, lo2 = jnp.maximum(a, b_), jnp.minimum(a, b_), jnp.maximum(c, d_), jnp.minimum(c, d_)
        score = jnp.maximum(hi1, hi2) + jnp.maximum(jnp.minimum(hi1, hi2), jnp.maximum(lo1, lo2))
        if gi == 0:
            best, gsel = score, jnp.zeros(score.shape, jnp.int32)
        else:
            better = score > best
            gsel = jnp.where(better, gi, gsel)
            best = jnp.where(better, score, best)

    def in_group(rows, j):
        v = rows[j]
        for gi in range(1, N_GROUPS):
            v = jnp.where(gsel == gi, rows[gi * epg + j], v)
        return v

    bv = [in_group(brow, j) for j in range(epg)]
    av = [in_group(arow, j) for j in range(epg)]

    def argmax_excluding(skip):
        val = idx = None
        for j in range(epg):
            cand = bv[j] if skip is None else jnp.where(skip == j, -jnp.inf, bv[j])
            if j == 0:
                val, idx = cand, jnp.zeros(cand.shape, jnp.int32)
            else:
                better = cand > val
                idx = jnp.where(better, j, idx)
                val = jnp.where(better, cand, val)
        return idx

    first = argmax_excluding(None)
    second = argmax_excluding(first)

    def pick(rows, idx):
        v = rows[0]
        for j in range(1, epg):
            v = jnp.where(idx == j, rows[j], v)
        return v

    a1, a2 = pick(av, first), pick(av, second)
    tot = a1 + a2
    e1, e2 = gsel * epg + first, gsel * epg + second
    eid = _iota(aff.shape, 0)
    cw_ref[...] = jnp.where(eid == e1, a1 / tot, 0.0) + jnp.where(eid == e2, a2 / tot, 0.0)


def ffn_pre(x, g, sc, sh, router_w, router_b, tm=512):
    b, s, d = x.shape
    e = router_w.shape[1]
    tm = min(tm, s)
    return pl.pallas_call(
        _ffn_pre_kernel,
        grid=(b, s // tm),
        in_specs=[pl.BlockSpec((None, tm, d), lambda bi, i: (bi, i, 0)),
                  pl.BlockSpec((1, d), lambda bi, i: (0, 0)),
                  pl.BlockSpec((None, 1, d), lambda bi, i: (bi, 0, 0)),
                  pl.BlockSpec((None, 1, d), lambda bi, i: (bi, 0, 0)),
                  pl.BlockSpec((e, d), lambda bi, i: (0, 0)),
                  pl.BlockSpec((e, 1), lambda bi, i: (0, 0))],
        out_specs=[pl.BlockSpec((None, tm, d), lambda bi, i: (bi, i, 0)),
                   pl.BlockSpec((None, e, tm), lambda bi, i: (bi, 0, i))],
        out_shape=[jax.ShapeDtypeStruct((b, s, d), BF16), jax.ShapeDtypeStruct((b, e, s), F32)],
        compiler_params=_params("parallel", "parallel"),
        name="ffn_pre",
    )(x, g.reshape(1, d), sc.reshape(b, 1, d), sh.reshape(b, 1, d), router_w.T, router_b.reshape(e, 1))


def _moe_kernel(h_ref, cw_ref, x_ref, gf_ref, wg_ref, wu_ref, wd_ref, o_ref, acc_ref):
    e = pl.program_id(2)

    @pl.when(e == 0)
    def _():
        acc_ref[...] = jnp.zeros(acc_ref.shape, F32)

    h = h_ref[...]
    a = _dot(h, wg_ref[...].astype(BF16))
    u = _dot(h, wu_ref[...].astype(BF16))
    cw = cw_ref[...]
    c = jnp.sum(jnp.where(_iota(cw.shape, 1) == e, cw, 0.0), axis=-1, keepdims=True)
    hid = (a * jax.nn.sigmoid(a)) * u * c
    acc_ref[...] += _dot(hid.astype(BF16), wd_ref[...].astype(BF16))

    @pl.when(e == pl.num_programs(2) - 1)
    def _():
        o_ref[...] = x_ref[...] + gf_ref[...] * acc_ref[...]


def moe_dense(h, cw, x, gate_f, w_gate, w_up, w_down, layer, tm=1024):
    b, s, d = x.shape
    _, ne, _, f = w_gate.shape
    tm = min(tm, s)
    return pl.pallas_call(
        _moe_kernel,
        grid=(b, s // tm, ne),
        in_specs=[pl.BlockSpec((None, tm, d), lambda bi, i, e: (bi, i, 0)),
                  pl.BlockSpec((None, tm, ne), lambda bi, i, e: (bi, i, 0)),
                  pl.BlockSpec((None, tm, d), lambda bi, i, e: (bi, i, 0)),
                  pl.BlockSpec((None, 1, d), lambda bi, i, e: (bi, 0, 0)),
                  pl.BlockSpec((None, None, d, f), lambda bi, i, e: (layer, e, 0, 0)),
                  pl.BlockSpec((None, None, d, f), lambda bi, i, e: (layer, e, 0, 0)),
                  pl.BlockSpec((None, None, f, d), lambda bi, i, e: (layer, e, 0, 0))],
        out_specs=pl.BlockSpec((None, tm, d), lambda bi, i, e: (bi, i, 0)),
        out_shape=jax.ShapeDtypeStruct((b, s, d), F32),
        scratch_shapes=[pltpu.VMEM((tm, d), F32)],
        compiler_params=_params("parallel", "parallel", "arbitrary"),
        name="moe_dense",
    )(h, cw, x, gate_f.reshape(b, 1, d), w_gate, w_up, w_down)


def _final_norm_kernel(x_ref, g_ref, o_ref):
    o_ref[...] = _rms(x_ref[...], g_ref[...])


def final_rmsnorm(x, g, tm=512):
    b, s, d = x.shape
    tm = min(tm, s)
    return pl.pallas_call(
        _final_norm_kernel,
        grid=(b, s // tm),
        in_specs=[pl.BlockSpec((None, tm, d), lambda bi, i: (bi, i, 0)), pl.BlockSpec((1, d), lambda bi, i: (0, 0))],
        out_specs=pl.BlockSpec((None, tm, d), lambda bi, i: (bi, i, 0)),
        out_shape=jax.ShapeDtypeStruct((b, s, d), F32),
        compiler_params=_params("parallel", "parallel"),
        name="final_rmsnorm",
    )(x, g.reshape(1, d))


def _heads(x, nh):
    b, s, w = x.shape
    return x.reshape(b, s, nh, w // nh).transpose(0, 2, 1, 3)


ATTN_SCALE = HEAD_DIM ** -0.5


def even_projection(x, g, sc, sh, w_in):
    off = _offsets(EVEN_WIDTHS)
    order = list(range(7)) + [8, 9, 10, 7]
    cols = jnp.concatenate([w_in[:, off[j][0]:off[j][1]] for j in order], axis=1)
    w = jnp.pad(cols, ((0, 0), (0, -cols.shape[1] % LANES))).astype(BF16)
    qw, kvw, hw = NSA_HEADS * HEAD_DIM, NSA_KV_HEADS * HEAD_DIM, MOBA_HEADS * HEAD_DIM
    p = qw + 6 * kvw
    outs = [(0, qw, ATTN_SCALE, F32),
            (0, qw, ATTN_SCALE * LOG2E, BF16),
            (qw, qw + 2 * kvw, 1.0, F32),
            (qw + 2 * kvw, p, 1.0, BF16),
            (p, p + hw, 1.0, F32),
            (p, p + hw, ATTN_SCALE * LOG2E, BF16),
            (p + hw, p + 2 * hw, 1.0, F32),
            (p + hw, p + 3 * hw, 1.0, BF16),
            (p + 3 * hw, p + 3 * hw + LANES, 1.0, F32)]
    return norm_mod_matmul(x, g, sc, sh, w, outs)


def even_mixer(proj, x, gate_m, w_out, pos_k, pos_v, ck_w1, ck_w2, cv_w1, cv_w2, rel_table, banks):
    qa_f32, qa_log2, kcvc, kvsw, qb_f32, qb_log2, kb_f32, kbvb, gates = proj
    b, s, _ = qa_f32.shape
    hk, g, dh = NSA_KV_HEADS, NSA_GROUP, HEAD_DIM
    qa = _heads(qa_f32, NSA_HEADS).reshape(b, hk, g, s, dh)
    qa16_log2 = _heads(qa_log2, NSA_HEADS).reshape(b, hk, g, s, dh)
    (bank_l, far_l, t_l), (bank_w, t_w) = banks["dense"], banks["nsa_window"]

    nc = s // CMP_STRIDE
    kv = _heads(kcvc, 2 * hk).reshape(b, 2, hk, s, dh).transpose(1, 0, 2, 3, 4)
    cmp = nsa_compress(kv.reshape(2, b * hk, s, dh), jnp.stack([ck_w1, cv_w1]), jnp.stack([pos_k, pos_v]),
                       jnp.stack([ck_w2, cv_w2]))
    cmp = cmp.reshape(2, b, hk, nc, dh)
    o_c, sel = nsa_cmp_attention(qa, cmp[0], cmp[1], bias_cmp(rel_table, s, nc, tq=min(256, s)), tq=min(1024, s))
    ks, vs, kw, vw = (_heads(kvsw[..., j * hk * dh:(j + 1) * hk * dh], hk) for j in range(4))
    o_s = flash_attention([qa16_log2.reshape(b, 1, hk * g, s, dh)], [ks[:, None]], vs[:, None], bank_l, far_l,
                          sel=sel.reshape(b, 1, hk, sel.shape[-2], s), t=t_l, blk=SLC_BLOCK)
    o_w = window_attention(qa16_log2, kw, vw, bank_w, t_w, NSA_WINDOW)

    hb = MOBA_HEADS
    hw = MOBA_HEADS * dh
    selb = moba_select(_heads(qb_f32, MOBA_HEADS), _heads(kb_f32, MOBA_HEADS), tq=min(2048, s))
    ngb = MOBA_HEADS // hb
    r5 = lambda a: a.reshape(b, ngb, hb, a.shape[-2], a.shape[-1])
    o_b = flash_attention([r5(_heads(qb_log2, MOBA_HEADS))], [r5(_heads(kbvb[..., :hw], MOBA_HEADS))],
                          r5(_heads(kbvb[..., hw:], MOBA_HEADS)), bank_l, far_l, sel=r5(selb), t=t_l, blk=MOBA_BLOCK)

    return out_project([o_c, o_s, o_w], o_b, x, gate_m, w_out, gates=gates)


def odd_projection(x, g, sc, sh, w_in):
    off = _offsets(ODD_WIDTHS)
    half = MLA_ROPE // 2
    rope_lo = off[2][0]
    swapped = jnp.concatenate([w_in[:, rope_lo + half:rope_lo + MLA_ROPE], w_in[:, rope_lo:rope_lo + half]], axis=1)
    w = jnp.concatenate([w_in[:, :off[2][1]], swapped, w_in[:, off[3][0]:]], axis=1).astype(BF16)
    lat = MLA_Q_RANK + MLA_KV_RANK + 2 * MLA_ROPE
    qw, kvw = SWA_HEADS * HEAD_DIM, SWA_KV_HEADS * HEAD_DIM
    outs = [(0, lat, 1.0, F32),
            (lat, lat + qw, ATTN_SCALE * LOG2E, BF16),
            (lat + qw, lat + qw + 2 * kvw, 1.0, BF16)]
    return norm_mod_matmul(x, g, sc, sh, w, outs)


def odd_mixer(proj, x, gate_m, w_out, q_norm, kv_norm, w_q_up, w_kv_up, sinks, banks):
    latents, qd_log2, kdvd = proj
    b, s, _ = latents.shape
    lo = 0
    parts = []
    for wdt in (MLA_Q_RANK, MLA_KV_RANK, MLA_ROPE, MLA_ROPE):
        parts.append(latents[..., lo:lo + wdt])
        lo += wdt
    c_q, c_kv, k_rope, k_rope_sw = parts
    qn, qr, kn, v, kr = mla_up(c_q, c_kv, k_rope, k_rope_sw, q_norm, kv_norm, w_q_up, w_kv_up)
    hb = MLA_HEADS
    r5 = lambda a: a.reshape(b, 1, hb, s, a.shape[-1])
    bank_c, far_c, t_c = banks["causal"]
    lane_pad = jnp.zeros(qn.shape[:-1] + (-(MLA_NOPE + MLA_ROPE) % LANES,), qn.dtype)
    q_full = jnp.concatenate([qn, qr, lane_pad], axis=-1)
    k_full = jnp.concatenate([kn, kr, lane_pad], axis=-1)
    o_c = flash_attention([r5(q_full)], [r5(k_full)], r5(v), bank_c, far_c, t=t_c)

    hk, g, dh = SWA_KV_HEADS, SWA_GROUP, HEAD_DIM
    qd = _heads(qd_log2, SWA_HEADS).reshape(b, hk, g, s, dh)
    bank_s, t_s = banks["swa"]
    o_d = window_attention(qd, _heads(kdvd[..., :hk * dh], hk), _heads(kdvd[..., hk * dh:], hk), bank_s, t_s,
                           SWA_WINDOW, sinks=sinks)
    return out_project([o_c], o_d, x, gate_m, w_out)


def kernel(x, c, rel_table, router_w, router_b, final_norm, norm_mix, norm_ffn, ada_w, ada_b, moe_w_gate, moe_w_up, moe_w_down, ev_w_in, ev_w_out, nsa_pos_k, nsa_pos_v, nsa_ck_w1, nsa_ck_w2, nsa_cv_w1, nsa_cv_w2, od_w_in, od_w_out, mla_q_norm, mla_kv_norm, mla_w_q_up, mla_w_kv_up, swa_sinks):
    b, s, d = x.shape
    depth = ada_w.shape[0]
    mods = ada_all(c, ada_w, ada_b)
    t_dense = min(512, s)
    bank_l = bias_bank(rel_table, t_dense)
    far_l = rel_table[N_BUCKETS - 1] * LOG2E
    future = np.arange(t_dense)[:, None] > np.arange(t_dense)[None, :]
    bank_c = jnp.asarray(np.where(future, NEG, 0.0).astype(np.float32)[None, None])
    t_w, t_s = min(256, s), min(256, s)
    banks = {"dense": (bank_l, far_l, t_dense), "causal": (bank_c, jnp.zeros((N_BIAS_HEADS,), F32), t_dense),
             "nsa_window": (bias_bank(rel_table, t_w, window=NSA_WINDOW), t_w),
             "swa": (bias_bank(rel_table, t_s, window=SWA_WINDOW), t_s)}

    for layer in range(depth):
        shift_m, scale_m, gate_m, shift_f, scale_f, gate_f = jnp.split(mods[layer], 6, axis=-1)
        i = layer // 2
        if layer % 2 == 0:
            proj = even_projection(x, norm_mix[layer], scale_m, shift_m, ev_w_in[i])
            x = even_mixer(proj, x, gate_m, ev_w_out[i], nsa_pos_k[i], nsa_pos_v[i], nsa_ck_w1[i], nsa_ck_w2[i],
                           nsa_cv_w1[i], nsa_cv_w2[i], rel_table, banks)
        else:
            proj = odd_projection(x, norm_mix[layer], scale_m, shift_m, od_w_in[i])
            x = odd_mixer(proj, x, gate_m, od_w_out[i], mla_q_norm[i], mla_kv_norm[i], mla_w_q_up[i], mla_w_kv_up[i],
                          swa_sinks[i], banks)
        h, cw = ffn_pre(x, norm_ffn[layer], scale_f, shift_f, router_w, router_b)
        x = moe_dense(h, cw.transpose(0, 2, 1), x, gate_f, moe_w_gate, moe_w_up, moe_w_down, layer)
    return final_rmsnorm(x, final_norm)
```

```python
import functools
import math

import numpy as np
import jax
import jax.numpy as jnp
from jax import lax
from jax.experimental import pallas as pl
from jax.experimental.pallas import tpu as pltpu

F32 = jnp.float32
BF16 = jnp.bfloat16
HI = lax.Precision.HIGHEST

D_MODEL = 1024
HEAD_DIM = 64
NEG = -1e30
BIG = 1e30
M_INIT = -1e9
NORM_EPS = 1e-6
LOG2E = math.log2(math.e)

N_BUCKETS = 32
MAX_DISTANCE = 1024
N_BIAS_HEADS = 8

NSA_HEADS = 8
NSA_KV_HEADS = 2
NSA_GROUP = NSA_HEADS // NSA_KV_HEADS
CMP_LEN = 32
CMP_STRIDE = 16
CMP_HIDDEN = 256
SLC_BLOCK = 64
SLC_TOPN = 16
NSA_WINDOW = 512

MOBA_HEADS = 8
MOBA_BLOCK = 256
MOBA_TOPK = 3

MLA_HEADS = 4
MLA_Q_RANK = 256
MLA_KV_RANK = 128
MLA_NOPE = 128
MLA_ROPE = 64
MLA_V = 128
ROPE_THETA = 10000.0

SWA_HEADS = 8
SWA_KV_HEADS = 2
SWA_GROUP = SWA_HEADS // SWA_KV_HEADS
SWA_WINDOW = 128

N_EXPERTS = 16
N_GROUPS = 4
EXPERTS_PER_GROUP = N_EXPERTS // N_GROUPS
D_EXPERT = 512

EVEN_WIDTHS = (NSA_HEADS * HEAD_DIM,) + (NSA_KV_HEADS * HEAD_DIM,) * 6 + (3 * NSA_HEADS,) + (MOBA_HEADS * HEAD_DIM,) * 3
ODD_WIDTHS = (MLA_Q_RANK, MLA_KV_RANK, MLA_ROPE, SWA_HEADS * HEAD_DIM, SWA_KV_HEADS * HEAD_DIM, SWA_KV_HEADS * HEAD_DIM)

LANES = 128
VMEM_LIMIT = 56 * 1024 * 1024


def _params(*sem):
    return pltpu.CompilerParams(dimension_semantics=sem, vmem_limit_bytes=VMEM_LIMIT)


def _dot(a, b, precision=None):
    return lax.dot_general(a, b, (((1,), (0,)), ((), ())), precision=precision, preferred_element_type=F32)


def _dot_nt(a, b, precision=None):
    return lax.dot_general(a, b, (((1,), (1,)), ((), ())), precision=precision, preferred_element_type=F32)


def _iota(shape, dim):
    return lax.broadcasted_iota(jnp.int32, shape, dim)


def _offsets(widths):
    out, acc = [], 0
    for w in widths:
        out.append((acc, acc + w))
        acc += w
    return out


def _bucket_thresholds():
    d = np.arange(0, 4 * MAX_DISTANCE, dtype=np.int64)
    exact = N_BUCKETS // 2
    x = np.maximum(d, 1).astype(np.float32) / np.float32(exact)
    logp = exact + (np.log(x) / np.float32(math.log(MAX_DISTANCE / exact)) * np.float32(N_BUCKETS - exact)).astype(np.int32)
    bucket = np.where(d < exact, d, np.minimum(logp, N_BUCKETS - 1))
    return [int(np.argmax(bucket >= b)) for b in range(1, N_BUCKETS)]


BUCKET_THR = _bucket_thresholds()
FAR_DIST = BUCKET_THR[-1]


def _ada_kernel(c_ref, w_ref, b_ref, o_ref):
    c = c_ref[...]
    o_ref[...] = _dot(c * jax.nn.sigmoid(c), w_ref[...], precision=HI) + b_ref[...]


def ada_all(c, ada_w, ada_b):
    depth, d, n = ada_w.shape
    rows = 8
    cp = jnp.pad(c, ((0, rows - c.shape[0]), (0, 0)))
    tn = 1536
    out = pl.pallas_call(
        _ada_kernel,
        grid=(depth, n // tn),
        in_specs=[pl.BlockSpec((rows, d), lambda l, j: (0, 0)),
                  pl.BlockSpec((None, d, tn), lambda l, j: (l, 0, j)),
                  pl.BlockSpec((None, 1, tn), lambda l, j: (l, 0, j))],
        out_specs=pl.BlockSpec((None, rows, tn), lambda l, j: (l, 0, j)),
        out_shape=jax.ShapeDtypeStruct((depth, rows, n), F32),
        compiler_params=_params("parallel", "parallel"),
        name="ada",
    )(cp, ada_w, ada_b.reshape(depth, 1, n))
    return out[:, :c.shape[0], :]


def _norm_mod(x, g, sc, sh):
    y = x * lax.rsqrt(jnp.mean(x * x, axis=-1, keepdims=True) + NORM_EPS) * g
    return y * (1.0 + sc) + sh


def _nmm_kernel(x_ref, g_ref, sc_ref, sh_ref, w_ref, *o_refs, outs):
    h = _norm_mod(x_ref[...], g_ref[...], sc_ref[...], sh_ref[...])
    res = _dot(h.astype(BF16), w_ref[...])
    for o_ref, (lo, hi, scale, _) in zip(o_refs, outs):
        part = res[:, lo:hi]
        o_ref[...] = (part if scale == 1.0 else part * scale).astype(o_ref.dtype)


def norm_mod_matmul(x, g, sc, sh, w, outs, tm=512):
    b, s, d = x.shape
    n = w.shape[1]
    tm = min(tm, s)
    return pl.pallas_call(
        functools.partial(_nmm_kernel, outs=outs),
        grid=(b, s // tm),
        in_specs=[pl.BlockSpec((None, tm, d), lambda bi, i: (bi, i, 0)),
                  pl.BlockSpec((1, d), lambda bi, i: (0, 0)),
                  pl.BlockSpec((None, 1, d), lambda bi, i: (bi, 0, 0)),
                  pl.BlockSpec((None, 1, d), lambda bi, i: (bi, 0, 0)),
                  pl.BlockSpec((d, n), lambda bi, i: (0, 0))],
        out_specs=[pl.BlockSpec((None, tm, hi - lo), lambda bi, i: (bi, i, 0)) for lo, hi, _, _ in outs],
        out_shape=[jax.ShapeDtypeStruct((b, s, hi - lo), dt) for lo, hi, _, dt in outs],
        compiler_params=_params("parallel", "parallel"),
        name="norm_mod_matmul",
    )(x, g.reshape(1, d), sc.reshape(b, 1, d), sh.reshape(b, 1, d), w)


def _bias_kernel(tab_ref, o_ref, *, rows, cols, step, cstride, c0, key_major=False, window=None):
    sub = 8
    off = pl.program_id(0) * step - c0
    unit = LOG2E if key_major else 1.0
    if key_major:
        base = _iota((sub, cols), 1) - _iota((sub, cols), 0)
    else:
        base = _iota((sub, cols), 0) - cstride * _iota((sub, cols), 1)

    def body(r, carry):
        dist = off + base + (-r * sub if key_major else r * sub)
        vs = [jnp.full((sub, cols), tab_ref[0, h] * unit, F32) for h in range(N_BIAS_HEADS)]
        for b in range(1, N_BUCKETS):
            ge = dist >= BUCKET_THR[b - 1]
            for h in range(N_BIAS_HEADS):
                vs[h] = jnp.where(ge, tab_ref[b, h] * unit, vs[h])
        masked = dist < 0 if window is None else (dist < 0) | (dist >= window)
        for h in range(N_BIAS_HEADS):
            v = jnp.where(masked, NEG, vs[h]) if key_major else vs[h]
            o_ref[h, pl.ds(pl.multiple_of(r * sub, sub), sub), :] = v
        return carry

    lax.fori_loop(0, rows // sub, body, 0)


def bias_bank(rel_table, t, window=None):
    nd = -(-(FAR_DIST + t - 1) // t) if window is None else -(-window // t) + 1
    kern = functools.partial(_bias_kernel, rows=t, cols=t, step=t, cstride=1, c0=0, key_major=True, window=window)
    bank = pl.pallas_call(
        kern,
        grid=(nd,),
        in_specs=[pl.BlockSpec(memory_space=pltpu.SMEM)],
        out_specs=pl.BlockSpec((N_BIAS_HEADS, None, t, t), lambda i: (0, i, 0, 0)),
        out_shape=jax.ShapeDtypeStruct((N_BIAS_HEADS, nd, t, t), F32),
        compiler_params=_params("parallel"),
        name="bias_bank",
    )(rel_table)
    return bank


def _bias_cmp_kernel(tab_ref, o_ref, *, rows, cols):
    sub = 8
    width = min(LANES, cols)
    t0 = pl.program_id(0) * rows
    base = _iota((sub, width), 0) - CMP_STRIDE * _iota((sub, width), 1)

    def body(r, carry):
        rs = pl.ds(pl.multiple_of(r * sub, sub), sub)
        for c in range(cols // width):
            cs = slice(c * width, (c + 1) * width)
            off = t0 + r * sub - (CMP_LEN - 1) - CMP_STRIDE * width * c
            lo = off - CMP_STRIDE * (width - 1)
            hi = off + sub - 1

            @pl.when((hi >= 0) & (lo < FAR_DIST))
            def _():
                dist = off + base
                vs = [jnp.full((sub, width), tab_ref[0, h], F32) for h in range(N_BIAS_HEADS)]
                for b in range(1, N_BUCKETS):
                    ge = dist >= BUCKET_THR[b - 1]
                    for h in range(N_BIAS_HEADS):
                        vs[h] = jnp.where(ge, tab_ref[b, h], vs[h])
                for h in range(N_BIAS_HEADS):
                    o_ref[h, rs, cs] = vs[h]

            @pl.when(lo >= FAR_DIST)
            def _():
                for h in range(N_BIAS_HEADS):
                    o_ref[h, rs, cs] = jnp.full((sub, width), tab_ref[N_BUCKETS - 1, h], F32)

            @pl.when(hi < 0)
            def _():
                for h in range(N_BIAS_HEADS):
                    o_ref[h, rs, cs] = jnp.full((sub, width), tab_ref[0, h], F32)
        return carry

    lax.fori_loop(0, rows // sub, body, 0)


def bias_cmp(rel_table, s, nc, tq=256):
    kern = functools.partial(_bias_cmp_kernel, rows=tq, cols=nc)
    return pl.pallas_call(
        kern,
        grid=(s // tq,),
        in_specs=[pl.BlockSpec(memory_space=pltpu.SMEM)],
        out_specs=pl.BlockSpec((N_BIAS_HEADS, tq, nc), lambda i: (0, i, 0)),
        out_shape=jax.ShapeDtypeStruct((N_BIAS_HEADS, s, nc), F32),
        compiler_params=_params("parallel"),
        name="bias_cmp",
    )(rel_table)


def _compress_kernel(x_ref, w1_ref, pos_ref, w2_ref, o_ref):
    x = x_ref[...]
    w1 = w1_ref[...]
    half = w1.shape[0]
    hid = w1.shape[1] // 2
    r = _dot(x, w1, precision=HI)
    pos = pos_ref[...]
    pb = _dot(pos[:, :half], w1[:, :hid], precision=HI) + _dot(pos[:, half:], w1[:, hid:], precision=HI)
    nxt = pltpu.roll(r[:, hid:], x.shape[0] - 1, 0)
    pre = r[:, :hid] + nxt + pb[0:1, :]
    o_ref[...] = _dot(jax.nn.gelu(pre), w2_ref[...], precision=HI)


def nsa_compress(kv, w1, pos, w2):
    two, bh, s, dh = kv.shape
    nc = s // CMP_STRIDE
    half = CMP_STRIDE * dh
    hid = w1.shape[2]
    x = kv.reshape(two, bh, nc, half)
    w1cat = jnp.concatenate([w1[:, :half, :], w1[:, half:, :]], axis=2)
    posf = jnp.broadcast_to(pos.reshape(two, 1, CMP_LEN * dh), (two, 8, CMP_LEN * dh))
    return pl.pallas_call(
        _compress_kernel,
        grid=(two, bh),
        in_specs=[pl.BlockSpec((None, None, nc, half), lambda w, i: (w, i, 0, 0)),
                  pl.BlockSpec((None, half, 2 * hid), lambda w, i: (w, 0, 0)),
                  pl.BlockSpec((None, 8, CMP_LEN * dh), lambda w, i: (w, 0, 0)),
                  pl.BlockSpec((None, hid, dh), lambda w, i: (w, 0, 0))],
        out_specs=pl.BlockSpec((None, None, nc, dh), lambda w, i: (w, i, 0, 0)),
        out_shape=jax.ShapeDtypeStruct((two, bh, nc, dh), F32),
        compiler_params=_params("parallel", "parallel"),
        name="nsa_compress",
    )(x, w1cat, posf, w2)


def _topn_mask(score, index, n):
    idx_f = index.astype(F32)
    sel = jnp.zeros(score.shape, jnp.bool_)
    for _ in range(n):
        m = jnp.max(score, axis=0, keepdims=True)
        first = jnp.min(jnp.where(score == m, idx_f, float(score.shape[0])), axis=0, keepdims=True)
        pick = idx_f == first
        sel = sel | pick
        score = jnp.where(pick, -jnp.inf, score)
    return sel


def _cmp_kernel(q_ref, k_ref, v_ref, b_ref, ov_ref, oc_ref, sel_ref, *, tq, nc, nslc, ntop, group):
    qi = pl.program_id(2)
    t = qi * tq + _iota((tq, 1), 0)
    n = _iota((1, nc), 1)
    mask = (n * CMP_STRIDE + (CMP_LEN - 1) <= t) & (n < nc - 1)
    k = k_ref[...]
    v = v_ref[...].astype(BF16)
    dh = k.shape[-1]
    psum = jnp.zeros((tq, nc), F32)
    scores = lambda g: _dot_nt(q_ref[g], k, precision=HI) + b_ref[g]
    s_next = scores(0)
    for g in range(group):
        s, s_next = s_next, (scores(g + 1) if g + 1 < group else None)
        s = jnp.where(mask, s, NEG)
        m = jnp.max(s, axis=-1, keepdims=True)
        p = jnp.where(mask, jnp.exp(s - m), 0.0)
        den = jnp.sum(p, axis=-1, keepdims=True)
        p = p * jnp.where(den > 0.0, 1.0 / den, 0.0)
        oc_ref[:, g * dh:(g + 1) * dh] = _dot(p.astype(BF16), v)
        psum = psum + p
    imp = _dot_nt(ov_ref[...], psum, precision=HI)
    j = _iota((nslc, tq), 0)
    blk = jnp.right_shift(qi * tq + _iota((1, tq), 1), int(math.log2(SLC_BLOCK)))
    forced = (j == 0) | (j == blk) | (j == blk - 1)
    valid = j <= blk
    score = jnp.where(forced, BIG, jnp.where(valid, imp, NEG))
    sel = _topn_mask(score, j, ntop) & valid
    sel_ref[0] = jnp.where(sel, 0.0, NEG)


def nsa_cmp_attention(q, k_cmp, v_cmp, bias_c, tq=256):
    b, hk, g, s, dh = q.shape
    nc = k_cmp.shape[2]
    nslc = s // SLC_BLOCK
    ntop = min(SLC_TOPN, nslc)
    cmp_start = np.arange(nc, dtype=np.int64) * CMP_STRIDE
    slc_lo = np.arange(nslc, dtype=np.int64) * SLC_BLOCK
    overlap = ((cmp_start[:, None] <= slc_lo[None, :] + SLC_BLOCK - 1)
               & (cmp_start[:, None] + CMP_LEN - 1 >= slc_lo[None, :])).astype(np.float32)
    kern = functools.partial(_cmp_kernel, tq=tq, nc=nc, nslc=nslc, ntop=ntop, group=g)
    return pl.pallas_call(
        kern,
        grid=(b, hk, s // tq),
        in_specs=[pl.BlockSpec((None, None, g, tq, dh), lambda bi, h, i: (bi, h, 0, i, 0)),
                  pl.BlockSpec((None, None, nc, dh), lambda bi, h, i: (bi, h, 0, 0)),
                  pl.BlockSpec((None, None, nc, dh), lambda bi, h, i: (bi, h, 0, 0)),
                  pl.BlockSpec((g, tq, nc), lambda bi, h, i: (h, i, 0)),
                  pl.BlockSpec((nslc, nc), lambda bi, h, i: (0, 0))],
        out_specs=[pl.BlockSpec((None, tq, g * dh), lambda bi, h, i: (bi, i, h)),
                   pl.BlockSpec((None, None, 1, nslc, tq), lambda bi, h, i: (bi, h, 0, 0, i))],
        out_shape=[jax.ShapeDtypeStruct((b, s, hk * g * dh), F32),
                   jax.ShapeDtypeStruct((b, hk, 1, nslc, s), F32)],
        compiler_params=_params("parallel", "parallel", "parallel"),
        name="nsa_cmp_attention",
    )(q, k_cmp, v_cmp, bias_c, jnp.asarray(overlap.T))


def _kmean_kernel(k_ref, o_ref, *, nb, blk):
    k = k_ref[...]
    o_ref[...] = jnp.sum(k.reshape(nb, blk, k.shape[-1]), axis=1) * (1.0 / blk)


def _moba_sel_kernel(q_ref, km_ref, sel_ref, *, tq, nb, ntop):
    qi = pl.program_id(2)
    own = jnp.right_shift(qi * tq + _iota((1, tq), 1), int(math.log2(MOBA_BLOCK)))
    j = _iota((nb, tq), 0)
    gate = _dot_nt(km_ref[...], q_ref[...], precision=HI)
    past = j < own
    sel = _topn_mask(jnp.where(past, gate, NEG), j, ntop) & past
    sel_ref[...] = jnp.where(sel | (j == own), 0.0, NEG)


def moba_select(q, k, tq=256):
    b, h, s, dh = q.shape
    nb = s // MOBA_BLOCK
    ntop = min(MOBA_TOPK, nb)
    kmean = pl.pallas_call(
        functools.partial(_kmean_kernel, nb=nb, blk=MOBA_BLOCK),
        grid=(b, h),
        in_specs=[pl.BlockSpec((None, None, s, dh), lambda bi, hi: (bi, hi, 0, 0))],
        out_specs=pl.BlockSpec((None, None, nb, dh), lambda bi, hi: (bi, hi, 0, 0)),
        out_shape=jax.ShapeDtypeStruct((b, h, nb, dh), F32),
        compiler_params=_params("parallel", "parallel"),
        name="moba_kmean",
    )(k)
    return pl.pallas_call(
        functools.partial(_moba_sel_kernel, tq=tq, nb=nb, ntop=ntop),
        grid=(b, h, s // tq),
        in_specs=[pl.BlockSpec((None, None, tq, dh), lambda bi, hi, i: (bi, hi, i, 0)),
                  pl.BlockSpec((None, None, nb, dh), lambda bi, hi, i: (bi, hi, 0, 0))],
        out_specs=pl.BlockSpec((None, None, nb, tq), lambda bi, hi, i: (bi, hi, 0, i)),
        out_shape=jax.ShapeDtypeStruct((b, h, nb, s), F32),
        compiler_params=_params("parallel", "parallel", "parallel"),
        name="moba_select",
    )(q, kmean)


AUG = 64


def _flash_kernel(qt_ref, kt_ref, *refs, hb, kb, sb, bh, t, tq, kc, nparts, use_sel, nbt, nd, ahead):
    pos = 0
    q_refs = refs[pos:pos + nparts]; pos += nparts
    k_refs = refs[pos:pos + nparts]; pos += nparts
    vt_ref = refs[pos]; pos += 1
    sel_ref = None
    if use_sel:
        sel_ref = refs[pos]; pos += 1
    bias_ref, far_ref, o_ref, m_ref, l_ref, acc_ref = refs[pos:pos + 6]

    head0 = pl.program_id(1) * hb
    step = pl.program_id(2)
    qi = qt_ref[step]
    ki = kt_ref[step]

    @pl.when(ki == 0)
    def _():
        m_ref[...] = jnp.full(m_ref.shape, M_INIT, F32)
        l_ref[...] = jnp.zeros(l_ref.shape, F32)
        acc_ref[...] = jnp.zeros(acc_ref.shape, F32)

    nchunk = t // kc
    nlane = t // tq

    def mask_lanes(si, qc):
        rows = sel_ref[si, qc * tq:(qc + 1) * tq, :]
        shift = lax.rem(AUG + LANES - lax.rem(ki * nbt, LANES), LANES)
        lane = _iota((tq, LANES), 1)
        keep = (lane >= AUG) & (lane < AUG + nbt)
        return jnp.where(keep, pltpu.roll(rows, shift, 1), 0.0).astype(BF16)

    def scores(g, qc, near, masks):
        kg = g * kb // hb
        bg = g if bh > 1 else 0
        ql = slice(qc * tq, (qc + 1) * tq)
        qs = [q_ref[g, ql, :] for q_ref in q_refs]
        if use_sel:
            qs[0] = jnp.where(_iota((tq, LANES), 1) < AUG, qs[0], masks[(g * sb // hb, qc)])
        chunks, top = [], None
        for c in range(nchunk):
            kr = slice(c * kc, (c + 1) * kc)
            s = _dot_nt(k_refs[0][kg, kr, :], qs[0])
            for p in range(1, nparts):
                s = s + _dot_nt(k_refs[p][kg, kr, :], qs[p])
            if near:
                s = s + bias_ref[bg, kr, ql]
            chunks.append(s)
            top = s if top is None else jnp.maximum(top, s)
        return chunks, top

    def update(g, qc, near, chunks, top):
        kg = g * kb // hb
        ql = slice(qc * tq, (qc + 1) * tq)
        shift = 0.0 if near else far_ref[head0 + g]
        m = m_ref[g, :, ql]
        m_new = jnp.maximum(m, jnp.max(top, axis=0, keepdims=True) + shift)
        alpha = jnp.exp2(m - m_new)
        base = m_new - shift
        pv = psum = None
        for c in range(nchunk):
            p = jnp.exp2(chunks[c] - base)
            ps = jnp.sum(p, axis=0, keepdims=True)
            pd = _dot(vt_ref[kg, :, c * kc:(c + 1) * kc], p.astype(BF16))
            pv = pd if pv is None else pv + pd
            psum = ps if psum is None else psum + ps
        m_ref[g, :, ql] = m_new
        l_ref[g, :, ql] = alpha * l_ref[g, :, ql] + psum
        acc_ref[g, :, ql] = alpha * acc_ref[g, :, ql] + pv

    def body(near):
        masks = {}
        if use_sel:
            masks = {(si, qc): mask_lanes(si, qc) for si in range(sb) for qc in range(nlane)}
        items = [(g, qc) for g in range(hb) for qc in range(nlane)]
        pending = [scores(*item, near, masks) for item in items[:ahead]]
        for i, item in enumerate(items):
            if i + ahead < len(items):
                pending.append(scores(*items[i + ahead], near, masks))
            update(*item, near, *pending.pop(0))

    @pl.when(qi - ki < nd)
    def _():
        body(True)

    @pl.when(qi - ki >= nd)
    def _():
        body(False)

    @pl.when(ki == qi)
    def _():
        for g in range(hb):
            dv = acc_ref.shape[1]
            o_ref[:, g * dv:(g + 1) * dv] = jnp.transpose(acc_ref[g] / l_ref[g]).astype(o_ref.dtype)


def flash_attention(q_parts, k_parts, v, bias, far, sel=None, t=512, blk=1, tq=256, kc=512, ahead=5):
    b, ng, hb, s, _ = q_parts[0].shape
    kb = v.shape[2]
    dv = v.shape[-1]
    t = min(t, s)
    tq, kc = min(tq, t), min(kc, t)
    nq = s // t
    nd = bias.shape[1]
    pairs = [(i, j) for i in range(nq) for j in range(i + 1)]
    qt = jnp.asarray(np.array([p[0] for p in pairs], np.int32))
    kt = jnp.asarray(np.array([p[1] for p in pairs], np.int32))
    nparts = len(q_parts)
    use_sel = sel is not None
    sb = sel.shape[2] if use_sel else 1
    bh = hb if bias.shape[0] > 1 else 1
    nbt = t // blk
    vt = jnp.swapaxes(v, -1, -2)
    q_parts, k_parts = list(q_parts), list(k_parts)
    if use_sel:
        onehot = (np.arange(s)[:, None] % t // blk == np.arange(LANES - AUG)[None, :]).astype(np.float32)
        k0 = k_parts[0]
        k_parts[0] = jnp.concatenate(
            [k0, jnp.broadcast_to(jnp.asarray(onehot, k0.dtype), k0.shape[:-1] + (LANES - AUG,))], axis=-1)
        q_parts[0] = jnp.pad(q_parts[0], ((0, 0),) * 4 + ((0, LANES - AUG),))
        selq = jnp.swapaxes(sel, -1, -2)
        selq = jnp.pad(selq, ((0, 0),) * 4 + ((0, LANES - selq.shape[-1]),))

    in_specs, args = [], []
    for qp in q_parts:
        in_specs.append(pl.BlockSpec((None, None, hb, t, qp.shape[-1]), lambda bi, n, st, qt, kt: (bi, n, 0, qt[st], 0)))
        args.append(qp)
    for kp in k_parts:
        in_specs.append(pl.BlockSpec((None, None, kb, t, kp.shape[-1]), lambda bi, n, st, qt, kt: (bi, n, 0, kt[st], 0)))
        args.append(kp)
    in_specs.append(pl.BlockSpec((None, None, kb, dv, t), lambda bi, n, st, qt, kt: (bi, n, 0, 0, kt[st])))
    args.append(vt)
    if use_sel:
        in_specs.append(pl.BlockSpec((None, None, sb, t, LANES), lambda bi, n, st, qt, kt: (bi, n, 0, qt[st], 0)))
        args.append(selq)
    near_tile = lambda qt, kt, st: jnp.minimum(qt[st] - kt[st], nd - 1)
    if bh > 1:
        in_specs.append(pl.BlockSpec((bh, None, t, t), lambda bi, n, st, qt, kt: (n, near_tile(qt, kt, st), 0, 0)))
    else:
        in_specs.append(pl.BlockSpec((1, None, t, t), lambda bi, n, st, qt, kt: (0, near_tile(qt, kt, st), 0, 0)))
    in_specs.append(pl.BlockSpec(memory_space=pltpu.SMEM))
    args += [bias, far]
    kern = functools.partial(_flash_kernel, hb=hb, kb=kb, sb=sb, bh=bh, t=t, tq=tq, kc=kc, nparts=nparts,
                             use_sel=use_sel, nbt=nbt, nd=nd, ahead=ahead)
    grid_spec = pltpu.PrefetchScalarGridSpec(
        num_scalar_prefetch=2,
        grid=(b, ng, len(pairs)),
        in_specs=in_specs,
        out_specs=pl.BlockSpec((None, t, hb * dv), lambda bi, n, st, qt, kt: (bi, qt[st], n)),
        scratch_shapes=[pltpu.VMEM((hb, 1, t), F32), pltpu.VMEM((hb, 1, t), F32), pltpu.VMEM((hb, dv, t), F32)])
    return pl.pallas_call(
        kern,
        grid_spec=grid_spec,
        out_shape=jax.ShapeDtypeStruct((b, s, ng * hb * dv), F32),
        compiler_params=_params("parallel", "parallel", "arbitrary"),
        name="flash_attention",
    )(qt, kt, *args)


def _window_kernel(*refs, group, t, ntile, has_sink):
    q_ref = refs[0]
    k_refs = refs[1:1 + ntile]
    vt_refs = refs[1 + ntile:1 + 2 * ntile]
    bias_ref = refs[1 + 2 * ntile]
    sink_ref = refs[2 + 2 * ntile] if has_sink else None
    o_ref = refs[-1]
    kvh = pl.program_id(1)

    def scores(g):
        q = q_ref[g]
        ss = [_dot_nt(k_refs[r][...], q) + bias_ref[g, ntile - 1 - r] for r in range(ntile)]
        top = ss[0]
        for s in ss[1:]:
            top = jnp.maximum(top, s)
        return ss, top

    def finish(g, ss, top):
        m = jnp.max(top, axis=0, keepdims=True)
        den = None
        if has_sink:
            sink = sink_ref[kvh * group + g] * LOG2E
            m = jnp.maximum(m, sink)
            den = jnp.exp2(sink - m)
        acc = None
        for r in range(ntile):
            p = jnp.exp2(ss[r] - m)
            ps = jnp.sum(p, axis=0, keepdims=True)
            pv = _dot(vt_refs[r][...], p.astype(BF16))
            den = ps if den is None else den + ps
            acc = pv if acc is None else acc + pv
        dh = acc.shape[0]
        o_ref[:, g * dh:(g + 1) * dh] = jnp.transpose(acc / den)

    pending = scores(0)
    for g in range(group):
        nxt = scores(g + 1) if g + 1 < group else None
        finish(g, *pending)
        pending = nxt


def window_attention(q, k, v, bank, t, window, sinks=None):
    b, hk, g, s, dh = q.shape
    ntile = -(-window // t) + 1
    front = (ntile - 1) * t
    kp = jnp.pad(k, ((0, 0), (0, 0), (front, 0), (0, 0)))
    vtp = jnp.pad(jnp.swapaxes(v, -1, -2), ((0, 0), (0, 0), (0, 0), (front, 0)))
    has_sink = sinks is not None
    in_specs = [pl.BlockSpec((None, None, g, t, dh), lambda bi, h, i: (bi, h, 0, i, 0))]
    args = [q]
    for r in range(ntile):
        in_specs.append(pl.BlockSpec((None, None, t, dh), lambda bi, h, i, r=r: (bi, h, i + r, 0)))
        args.append(kp)
    for r in range(ntile):
        in_specs.append(pl.BlockSpec((None, None, dh, t), lambda bi, h, i, r=r: (bi, h, 0, i + r)))
        args.append(vtp)
    in_specs.append(pl.BlockSpec((g, ntile, t, t), lambda bi, h, i: (h, 0, 0, 0)))
    args.append(bank)
    if has_sink:
        in_specs.append(pl.BlockSpec(memory_space=pltpu.SMEM))
        args.append(sinks)
    kern = functools.partial(_window_kernel, group=g, t=t, ntile=ntile, has_sink=has_sink)
    return pl.pallas_call(
        kern,
        grid=(b, hk, s // t),
        in_specs=in_specs,
        out_specs=pl.BlockSpec((None, t, g * dh), lambda bi, h, i: (bi, i, h)),
        out_shape=jax.ShapeDtypeStruct((b, s, hk * g * dh), F32),
        compiler_params=_params("parallel", "parallel", "parallel"),
        name="window_attention",
    )(*args)


def _rms(x, g):
    return x * lax.rsqrt(jnp.mean(x * x, axis=-1, keepdims=True) + NORM_EPS) * g


def _mla_up_kernel(cq_ref, ckv_ref, kr_ref, krs_ref, qn_ref, kvn_ref, wq_ref, wkv_ref, cs_ref, sn_ref,
                   qnope_ref, qrope_ref, knope_ref, v_ref, krope_ref, *, scale):
    cs = cs_ref[...]
    sn = sn_ref[...]
    q = _dot(_rms(cq_ref[...], qn_ref[...]).astype(BF16), wq_ref[...]) * scale
    kv = _dot(_rms(ckv_ref[...], kvn_ref[...]).astype(BF16), wkv_ref[...])
    wq_head = MLA_NOPE + 2 * MLA_ROPE
    for h in range(MLA_HEADS):
        base = h * wq_head
        qnope_ref[h] = q[:, base:base + MLA_NOPE].astype(qnope_ref.dtype)
        x = q[:, base + MLA_NOPE:base + MLA_NOPE + MLA_ROPE]
        xs = q[:, base + MLA_NOPE + MLA_ROPE:base + wq_head]
        qrope_ref[h] = (x * cs + xs * sn).astype(qrope_ref.dtype)
        kb = h * (MLA_NOPE + MLA_V)
        knope_ref[h] = kv[:, kb:kb + MLA_NOPE].astype(knope_ref.dtype)
        v_ref[h] = kv[:, kb + MLA_NOPE:kb + MLA_NOPE + MLA_V].astype(v_ref.dtype)
    kr = (kr_ref[...] * cs + krs_ref[...] * sn).astype(krope_ref.dtype)
    for h in range(MLA_HEADS):
        krope_ref[h] = kr


def mla_up(c_q, c_kv, k_rope, k_rope_sw, q_norm, kv_norm, w_q_up, w_kv_up, tm=512):
    b, s, _ = c_q.shape
    half = MLA_ROPE // 2
    inv = ROPE_THETA ** (-np.arange(0, MLA_ROPE, 2, dtype=np.float32) / np.float32(MLA_ROPE))
    ang = np.arange(s, dtype=np.float32)[:, None] * inv[None, :].astype(np.float32)
    cos, sin = np.cos(ang).astype(np.float32), np.sin(ang).astype(np.float32)
    cs = jnp.asarray(np.concatenate([cos, cos], axis=1))
    sn = jnp.asarray(np.concatenate([-sin, sin], axis=1))
    dq = MLA_NOPE + MLA_ROPE
    wq = w_q_up.reshape(MLA_Q_RANK, MLA_HEADS, dq)
    rope_cols = wq[:, :, MLA_NOPE:]
    swapped = jnp.concatenate([rope_cols[:, :, half:], rope_cols[:, :, :half]], axis=2)
    wq_aug = jnp.concatenate([wq, swapped], axis=2).reshape(MLA_Q_RANK, MLA_HEADS * (dq + MLA_ROPE)).astype(BF16)
    wkv = w_kv_up.astype(BF16)
    scale = (MLA_NOPE + MLA_ROPE) ** -0.5 * LOG2E
    tm = min(tm, s)
    row = lambda w: pl.BlockSpec((None, tm, w), lambda bi, i: (bi, i, 0))
    full = lambda a: pl.BlockSpec(a.shape, lambda bi, i: (0,) * a.ndim)
    head = lambda w: pl.BlockSpec((None, MLA_HEADS, tm, w), lambda bi, i: (bi, 0, i, 0))
    qn2, kvn2 = q_norm.reshape(1, -1), kv_norm.reshape(1, -1)
    outs = pl.pallas_call(
        functools.partial(_mla_up_kernel, scale=scale),
        grid=(b, s // tm),
        in_specs=[row(MLA_Q_RANK), row(MLA_KV_RANK), row(MLA_ROPE), row(MLA_ROPE), full(qn2), full(kvn2),
                  full(wq_aug), full(wkv),
                  pl.BlockSpec((tm, MLA_ROPE), lambda bi, i: (i, 0)), pl.BlockSpec((tm, MLA_ROPE), lambda bi, i: (i, 0))],
        out_specs=[head(MLA_NOPE), head(MLA_ROPE), head(MLA_NOPE), head(MLA_V), head(MLA_ROPE)],
        out_shape=[jax.ShapeDtypeStruct((b, MLA_HEADS, s, w), BF16)
                   for w in (MLA_NOPE, MLA_ROPE, MLA_NOPE, MLA_V, MLA_ROPE)],
        compiler_params=_params("parallel", "parallel"),
        name="mla_up",
    )(c_q, c_kv, k_rope, k_rope_sw, qn2, kvn2, wq_aug, wkv, cs, sn)
    return outs


def _out_kernel(*refs, gated):
    if gated:
        oc_ref, os_ref, ow_ref, gt_ref, ex_ref, ob_ref, x_ref, gm_ref, w_ref, o_ref = refs
        half = oc_ref.shape[-1]
        ge = _dot(jax.nn.sigmoid(gt_ref[...]), ex_ref[...], precision=HI)
        oa = ge[:, :half] * oc_ref[...] + ge[:, half:2 * half] * os_ref[...] + ge[:, 2 * half:] * ow_ref[...]
    else:
        oa_ref, ob_ref, x_ref, gm_ref, w_ref, o_ref = refs
        half = oa_ref.shape[-1]
        oa = oa_ref[...]
    mix = _dot(oa.astype(BF16), w_ref[:half, :]) + _dot(ob_ref[...].astype(BF16), w_ref[half:, :])
    o_ref[...] = x_ref[...] + gm_ref[...] * mix


def out_project(parts, ob, x, gate_m, w_out, gates=None, tm=512):
    b, s, d = x.shape
    gated = gates is not None
    tm = min(tm, s)
    row = lambda a: pl.BlockSpec((None, tm, a.shape[-1]), lambda bi, i: (bi, i, 0))
    full = lambda a: pl.BlockSpec(a.shape, lambda bi, i: (0,) * a.ndim)
    args, in_specs = [], []
    for p in parts:
        args.append(p); in_specs.append(row(p))
    if gated:
        half = parts[0].shape[-1]
        nh = half // HEAD_DIM
        ex = np.zeros((LANES, 3 * half), np.float32)
        for h in range(nh):
            for br in range(3):
                ex[h * 3 + br, br * half + h * HEAD_DIM: br * half + (h + 1) * HEAD_DIM] = 1.0
        gpad = jnp.pad(gates, ((0, 0), (0, 0), (0, LANES - gates.shape[-1])))
        ex = jnp.asarray(ex)
        args += [gpad, ex]; in_specs += [row(gpad), full(ex)]
    gm = gate_m.reshape(b, 1, d)
    wb = w_out.astype(BF16)
    args += [ob, x, gm, wb]
    in_specs += [row(ob), row(x), pl.BlockSpec((None, 1, d), lambda bi, i: (bi, 0, 0)), full(wb)]
    return pl.pallas_call(
        functools.partial(_out_kernel, gated=gated),
        grid=(b, s // tm),
        in_specs=in_specs,
        out_specs=pl.BlockSpec((None, tm, d), lambda bi, i: (bi, i, 0)),
        out_shape=jax.ShapeDtypeStruct((b, s, d), F32),
        compiler_params=_params("parallel", "parallel"),
        name="out_project",
    )(*args)


def _ffn_pre_kernel(x_ref, g_ref, sc_ref, sh_ref, rw_ref, rb_ref, h_ref, cw_ref):
    h = _norm_mod(x_ref[...], g_ref[...], sc_ref[...], sh_ref[...])
    h_ref[...] = h.astype(h_ref.dtype)
    aff = jax.nn.sigmoid(_dot_nt(rw_ref[...], h, precision=HI))
    biased = aff + rb_ref[...]
    epg = EXPERTS_PER_GROUP
    brow = [biased[e:e + 1, :] for e in range(N_EXPERTS)]
    arow = [aff[e:e + 1, :] for e in range(N_EXPERTS)]
    best = gsel = None
    for gi in range(N_GROUPS):
        a, b_, c, d_ = brow[gi * epg:(gi + 1) * epg]
        hi1, lo1, hi2, lo2 = jnp.maximum(a, b_), jnp.minimum(a, b_), jnp.maximum(c, d_), jnp.minimum(c, d_)
        score = jnp.maximum(hi1, hi2) + jnp.maximum(jnp.minimum(hi1, hi2), jnp.maximum(lo1, lo2))
        if gi == 0:
            best, gsel = score, jnp.zeros(score.shape, jnp.int32)
        else:
            better = score > best
            gsel = jnp.where(better, gi, gsel)
            best = jnp.where(better, score, best)

    def in_group(rows, j):
        v = rows[j]
        for gi in range(1, N_GROUPS):
            v = jnp.where(gsel == gi, rows[gi * epg + j], v)
        return v

    bv = [in_group(brow, j) for j in range(epg)]
    av = [in_group(arow, j) for j in range(epg)]

    def argmax_excluding(skip):
        val = idx = None
        for j in range(epg):
            cand = bv[j] if skip is None else jnp.where(skip == j, -jnp.inf, bv[j])
            if j == 0:
                val, idx = cand, jnp.zeros(cand.shape, jnp.int32)
            else:
                better = cand > val
                idx = jnp.where(better, j, idx)
                val = jnp.where(better, cand, val)
        return idx

    first = argmax_excluding(None)
    second = argmax_excluding(first)

    def pick(rows, idx):
        v = rows[0]
        for j in range(1, epg):
            v = jnp.where(idx == j, rows[j], v)
        return v

    a1, a2 = pick(av, first), pick(av, second)
    tot = a1 + a2
    e1, e2 = gsel * epg + first, gsel * epg + second
    eid = _iota(aff.shape, 0)
    cw_ref[...] = jnp.where(eid == e1, a1 / tot, 0.0) + jnp.where(eid == e2, a2 / tot, 0.0)


def ffn_pre(x, g, sc, sh, router_w, router_b, tm=512):
    b, s, d = x.shape
    e = router_w.shape[1]
    tm = min(tm, s)
    return pl.pallas_call(
        _ffn_pre_kernel,
        grid=(b, s // tm),
        in_specs=[pl.BlockSpec((None, tm, d), lambda bi, i: (bi, i, 0)),
                  pl.BlockSpec((1, d), lambda bi, i: (0, 0)),
                  pl.BlockSpec((None, 1, d), lambda bi, i: (bi, 0, 0)),
                  pl.BlockSpec((None, 1, d), lambda bi, i: (bi, 0, 0)),
                  pl.BlockSpec((e, d), lambda bi, i: (0, 0)),
                  pl.BlockSpec((e, 1), lambda bi, i: (0, 0))],
        out_specs=[pl.BlockSpec((None, tm, d), lambda bi, i: (bi, i, 0)),
                   pl.BlockSpec((None, e, tm), lambda bi, i: (bi, 0, i))],
        out_shape=[jax.ShapeDtypeStruct((b, s, d), BF16), jax.ShapeDtypeStruct((b, e, s), F32)],
        compiler_params=_params("parallel", "parallel"),
        name="ffn_pre",
    )(x, g.reshape(1, d), sc.reshape(b, 1, d), sh.reshape(b, 1, d), router_w.T, router_b.reshape(e, 1))


def _moe_kernel(h_ref, cw_ref, x_ref, gf_ref, wg_ref, wu_ref, wd_ref, o_ref, acc_ref):
    e = pl.program_id(2)

    @pl.when(e == 0)
    def _():
        acc_ref[...] = jnp.zeros(acc_ref.shape, F32)

    h = h_ref[...]
    a = _dot(h, wg_ref[...].astype(BF16))
    u = _dot(h, wu_ref[...].astype(BF16))
    cw = cw_ref[...]
    c = jnp.sum(jnp.where(_iota(cw.shape, 1) == e, cw, 0.0), axis=-1, keepdims=True)
    hid = (a * jax.nn.sigmoid(a)) * u * c
    acc_ref[...] += _dot(hid.astype(BF16), wd_ref[...].astype(BF16))

    @pl.when(e == pl.num_programs(2) - 1)
    def _():
        o_ref[...] = x_ref[...] + gf_ref[...] * acc_ref[...]


def moe_dense(h, cw, x, gate_f, w_gate, w_up, w_down, layer, tm=1024):
    b, s, d = x.shape
    _, ne, _, f = w_gate.shape
    tm = min(tm, s)
    return pl.pallas_call(
        _moe_kernel,
        grid=(b, s // tm, ne),
        in_specs=[pl.BlockSpec((None, tm, d), lambda bi, i, e: (bi, i, 0)),
                  pl.BlockSpec((None, tm, ne), lambda bi, i, e: (bi, i, 0)),
                  pl.BlockSpec((None, tm, d), lambda bi, i, e: (bi, i, 0)),
                  pl.BlockSpec((None, 1, d), lambda bi, i, e: (bi, 0, 0)),
                  pl.BlockSpec((None, None, d, f), lambda bi, i, e: (layer, e, 0, 0)),
                  pl.BlockSpec((None, None, d, f), lambda bi, i, e: (layer, e, 0, 0)),
                  pl.BlockSpec((None, None, f, d), lambda bi, i, e: (layer, e, 0, 0))],
        out_specs=pl.BlockSpec((None, tm, d), lambda bi, i, e: (bi, i, 0)),
        out_shape=jax.ShapeDtypeStruct((b, s, d), F32),
        scratch_shapes=[pltpu.VMEM((tm, d), F32)],
        compiler_params=_params("parallel", "parallel", "arbitrary"),
        name="moe_dense",
    )(h, cw, x, gate_f.reshape(b, 1, d), w_gate, w_up, w_down)


def _final_norm_kernel(x_ref, g_ref, o_ref):
    o_ref[...] = _rms(x_ref[...], g_ref[...])


def final_rmsnorm(x, g, tm=512):
    b, s, d = x.shape
    tm = min(tm, s)
    return pl.pallas_call(
        _final_norm_kernel,
        grid=(b, s // tm),
        in_specs=[pl.BlockSpec((None, tm, d), lambda bi, i: (bi, i, 0)), pl.BlockSpec((1, d), lambda bi, i: (0, 0))],
        out_specs=pl.BlockSpec((None, tm, d), lambda bi, i: (bi, i, 0)),
        out_shape=jax.ShapeDtypeStruct((b, s, d), F32),
        compiler_params=_params("parallel", "parallel"),
        name="final_rmsnorm",
    )(x, g.reshape(1, d))


def _heads(x, nh):
    b, s, w = x.shape
    return x.reshape(b, s, nh, w // nh).transpose(0, 2, 1, 3)


ATTN_SCALE = HEAD_DIM ** -0.5


def even_projection(x, g, sc, sh, w_in):
    off = _offsets(EVEN_WIDTHS)
    order = list(range(7)) + [8, 9, 10, 7]
    cols = jnp.concatenate([w_in[:, off[j][0]:off[j][1]] for j in order], axis=1)
    w = jnp.pad(cols, ((0, 0), (0, -cols.shape[1] % LANES))).astype(BF16)
    qw, kvw, hw = NSA_HEADS * HEAD_DIM, NSA_KV_HEADS * HEAD_DIM, MOBA_HEADS * HEAD_DIM
    p = qw + 6 * kvw
    outs = [(0, qw, ATTN_SCALE, F32),
            (0, qw, ATTN_SCALE * LOG2E, BF16),
            (qw, qw + 2 * kvw, 1.0, F32),
            (qw + 2 * kvw, p, 1.0, BF16),
            (p, p + hw, 1.0, F32),
            (p, p + hw, ATTN_SCALE * LOG2E, BF16),
            (p + hw, p + 2 * hw, 1.0, F32),
            (p + hw, p + 3 * hw, 1.0, BF16),
            (p + 3 * hw, p + 3 * hw + LANES, 1.0, F32)]
    return norm_mod_matmul(x, g, sc, sh, w, outs)


def even_mixer(proj, x, gate_m, w_out, pos_k, pos_v, ck_w1, ck_w2, cv_w1, cv_w2, rel_table, banks):
    qa_f32, qa_log2, kcvc, kvsw, qb_f32, qb_log2, kb_f32, kbvb, gates = proj
    b, s, _ = qa_f32.shape
    hk, g, dh = NSA_KV_HEADS, NSA_GROUP, HEAD_DIM
    qa = _heads(qa_f32, NSA_HEADS).reshape(b, hk, g, s, dh)
    qa16_log2 = _heads(qa_log2, NSA_HEADS).reshape(b, hk, g, s, dh)
    (bank_l, far_l, t_l), (bank_w, t_w) = banks["dense"], banks["nsa_window"]

    nc = s // CMP_STRIDE
    kv = _heads(kcvc, 2 * hk).reshape(b, 2, hk, s, dh).transpose(1, 0, 2, 3, 4)
    cmp = nsa_compress(kv.reshape(2, b * hk, s, dh), jnp.stack([ck_w1, cv_w1]), jnp.stack([pos_k, pos_v]),
                       jnp.stack([ck_w2, cv_w2]))
    cmp = cmp.reshape(2, b, hk, nc, dh)
    o_c, sel = nsa_cmp_attention(qa, cmp[0], cmp[1], bias_cmp(rel_table, s, nc, tq=min(256, s)), tq=min(1024, s))
    ks, vs, kw, vw = (_heads(kvsw[..., j * hk * dh:(j + 1) * hk * dh], hk) for j in range(4))
    o_s = flash_attention([qa16_log2.reshape(b, 1, hk * g, s, dh)], [ks[:, None]], vs[:, None], bank_l, far_l,
                          sel=sel.reshape(b, 1, hk, sel.shape[-2], s), t=t_l, blk=SLC_BLOCK)
    o_w = window_attention(qa16_log2, kw, vw, bank_w, t_w, NSA_WINDOW)

    hb = MOBA_HEADS
    hw = MOBA_HEADS * dh
    selb = moba_select(_heads(qb_f32, MOBA_HEADS), _heads(kb_f32, MOBA_HEADS), tq=min(2048, s))
    ngb = MOBA_HEADS // hb
    r5 = lambda a: a.reshape(b, ngb, hb, a.shape[-2], a.shape[-1])
    o_b = flash_attention([r5(_heads(qb_log2, MOBA_HEADS))], [r5(_heads(kbvb[..., :hw], MOBA_HEADS))],
                          r5(_heads(kbvb[..., hw:], MOBA_HEADS)), bank_l, far_l, sel=r5(selb), t=t_l, blk=MOBA_BLOCK)

    return out_project([o_c, o_s, o_w], o_b, x, gate_m, w_out, gates=gates)


def odd_projection(x, g, sc, sh, w_in):
    off = _offsets(ODD_WIDTHS)
    half = MLA_ROPE // 2
    rope_lo = off[2][0]
    swapped = jnp.concatenate([w_in[:, rope_lo + half:rope_lo + MLA_ROPE], w_in[:, rope_lo:rope_lo + half]], axis=1)
    w = jnp.concatenate([w_in[:, :off[2][1]], swapped, w_in[:, off[3][0]:]], axis=1).astype(BF16)
    lat = MLA_Q_RANK + MLA_KV_RANK + 2 * MLA_ROPE
    qw, kvw = SWA_HEADS * HEAD_DIM, SWA_KV_HEADS * HEAD_DIM
    outs = [(0, lat, 1.0, F32),
            (lat, lat + qw, ATTN_SCALE * LOG2E, BF16),
            (lat + qw, lat + qw + 2 * kvw, 1.0, BF16)]
    return norm_mod_matmul(x, g, sc, sh, w, outs)


def odd_mixer(proj, x, gate_m, w_out, q_norm, kv_norm, w_q_up, w_kv_up, sinks, banks):
    latents, qd_log2, kdvd = proj
    b, s, _ = latents.shape
    lo = 0
    parts = []
    for wdt in (MLA_Q_RANK, MLA_KV_RANK, MLA_ROPE, MLA_ROPE):
        parts.append(latents[..., lo:lo + wdt])
        lo += wdt
    c_q, c_kv, k_rope, k_rope_sw = parts
    qn, qr, kn, v, kr = mla_up(c_q, c_kv, k_rope, k_rope_sw, q_norm, kv_norm, w_q_up, w_kv_up)
    hb = MLA_HEADS
    r5 = lambda a: a.reshape(b, 1, hb, s, a.shape[-1])
    bank_c, far_c, t_c = banks["causal"]
    lane_pad = jnp.zeros(qn.shape[:-1] + (-(MLA_NOPE + MLA_ROPE) % LANES,), qn.dtype)
    q_full = jnp.concatenate([qn, qr, lane_pad], axis=-1)
    k_full = jnp.concatenate([kn, kr, lane_pad], axis=-1)
    o_c = flash_attention([r5(q_full)], [r5(k_full)], r5(v), bank_c, far_c, t=t_c)

    hk, g, dh = SWA_KV_HEADS, SWA_GROUP, HEAD_DIM
    qd = _heads(qd_log2, SWA_HEADS).reshape(b, hk, g, s, dh)
    bank_s, t_s = banks["swa"]
    o_d = window_attention(qd, _heads(kdvd[..., :hk * dh], hk), _heads(kdvd[..., hk * dh:], hk), bank_s, t_s,
                           SWA_WINDOW, sinks=sinks)
    return out_project([o_c], o_d, x, gate_m, w_out)


def kernel(x, c, rel_table, router_w, router_b, final_norm, norm_mix, norm_ffn, ada_w, ada_b, moe_w_gate, moe_w_up, moe_w_down, ev_w_in, ev_w_out, nsa_pos_k, nsa_pos_v, nsa_ck_w1, nsa_ck_w2, nsa_cv_w1, nsa_cv_w2, od_w_in, od_w_out, mla_q_norm, mla_kv_norm, mla_w_q_up, mla_w_kv_up, swa_sinks):
    b, s, d = x.shape
    depth = ada_w.shape[0]
    mods = ada_all(c, ada_w, ada_b)
    t_dense = min(512, s)
    bank_l = bias_bank(rel_table, t_dense)
    far_l = rel_table[N_BUCKETS - 1] * LOG2E
    future = np.arange(t_dense)[:, None] > np.arange(t_dense)[None, :]
    bank_c = jnp.asarray(np.where(future, NEG, 0.0).astype(np.float32)[None, None])
    t_w, t_s = min(256, s), min(256, s)
    banks = {"dense": (bank_l, far_l, t_dense), "causal": (bank_c, jnp.zeros((N_BIAS_HEADS,), F32), t_dense),
             "nsa_window": (bias_bank(rel_table, t_w, window=NSA_WINDOW), t_w),
             "swa": (bias_bank(rel_table, t_s, window=SWA_WINDOW), t_s)}

    for layer in range(depth):
        shift_m, scale_m, gate_m, shift_f, scale_f, gate_f = jnp.split(mods[layer], 6, axis=-1)
        i = layer // 2
        if layer % 2 == 0:
            proj = even_projection(x, norm_mix[layer], scale_m, shift_m, ev_w_in[i])
            x = even_mixer(proj, x, gate_m, ev_w_out[i], nsa_pos_k[i], nsa_pos_v[i], nsa_ck_w1[i], nsa_ck_w2[i],
                           nsa_cv_w1[i], nsa_cv_w2[i], rel_table, banks)
        else:
            proj = odd_projection(x, norm_mix[layer], scale_m, shift_m, od_w_in[i])
            x = odd_mixer(proj, x, gate_m, od_w_out[i], mla_q_norm[i], mla_kv_norm[i], mla_w_q_up[i], mla_w_kv_up[i],
                          swa_sinks[i], banks)
        h, cw = ffn_pre(x, norm_ffn[layer], scale_f, shift_f, router_w, router_b)
        x = moe_dense(h, cw.transpose(0, 2, 1), x, gate_f, moe_w_gate, moe_w_up, moe_w_down, layer)
    return final_rmsnorm(x, final_norm)
```
